```python
import math
import numpy as np
import jax
import jax.numpy as jnp
from jax import lax

D_MODEL = 1024
BATCH = 2
SEQ = 16384
DEPTH = 4

CTX_LEN = 256
GRID_W = 64
DA_HEADS = 6
DA_DIM = 32
DA_VDIM = 2 * DA_DIM
MLA_HEADS = 6
MLA_NOPE = 64
MLA_ROPE = 32
MLA_VDIM = 64
MLA_Q_RANK = 256
MLA_KV_RANK = 128
RET_HEADS = 4
RET_DIM = 64
RET_CHUNK = 128
DA_WIDTH = DA_HEADS * DA_VDIM
MLA_WIDTH = MLA_HEADS * MLA_VDIM
RET_WIDTH = RET_HEADS * RET_DIM
MIX_WIDTH = DA_WIDTH + MLA_WIDTH + RET_WIDTH
IN_SPLIT_SIZES = (DA_HEADS * 2 * DA_DIM, DA_HEADS * 2 * DA_DIM, DA_HEADS * DA_VDIM,
                  MLA_Q_RANK, MLA_KV_RANK, MLA_ROPE,
                  RET_WIDTH, RET_WIDTH, RET_WIDTH, RET_WIDTH)
IN_COLS = sum(IN_SPLIT_SIZES)
ATTN_ROT_DIM = 32
ROPE_BASE = 10000.0
Q_BLOCK = 128
N_EXPERTS = 16
N_GROUPS = 4
EXPERTS_PER_GROUP = N_EXPERTS // N_GROUPS
TOPK_GROUP = 1
TOP_K = 2
D_EXPERT = 512
ROUTED_SCALE = 1.0
N_MOD = 6
ALPHA = (2 * DEPTH) ** 0.25
BETA = (8 * DEPTH) ** -0.25
LN_EPS = 1e-5
RMS_EPS = 1e-6

kernel_name = 'hybrid_diffattn_mla_retention_grouped_moe'


def layer_norm(x, g, b):
    xf = x.astype(jnp.float32)
    mu = jnp.mean(xf, -1, keepdims=True)
    var = jnp.mean(jnp.square(xf - mu), -1, keepdims=True)
    return ((xf - mu) * lax.rsqrt(var + LN_EPS)).astype(x.dtype) * g + b


def rms_norm(x, g=None):
    xf = x.astype(jnp.float32)
    y = (xf * lax.rsqrt(jnp.mean(jnp.square(xf), -1, keepdims=True) + RMS_EPS)).astype(x.dtype)
    return y if g is None else y * g


def axial_rope(rows, rot_dim):
    r = jnp.repeat(jnp.arange(rows, dtype=jnp.float32), GRID_W)
    col = jnp.tile(jnp.arange(GRID_W, dtype=jnp.float32), rows)
    n_freq = rot_dim // 4
    freqs = ROPE_BASE ** (-jnp.arange(n_freq, dtype=jnp.float32) / n_freq)
    ang = jnp.concatenate([r[:, None] * freqs, col[:, None] * freqs], -1)
    return jnp.cos(ang), jnp.sin(ang)


def apply_rope(x, rope):
    cos, sin = rope
    cos = cos.astype(x.dtype)
    sin = sin.astype(x.dtype)
    x1, x2 = x[..., 0::2], x[..., 1::2]
    return jnp.stack([x1 * cos - x2 * sin, x1 * sin + x2 * cos], -1).reshape(x.shape)


def split_cols(z):
    return jnp.split(z, np.cumsum(IN_SPLIT_SIZES)[:-1].tolist(), axis=-1)


def split_heads(t, n_heads):
    B, L, _ = t.shape
    return t.reshape(B, L, n_heads, -1).transpose(0, 2, 1, 3)


def merge_heads(t):
    B, H, L, d = t.shape
    return t.transpose(0, 2, 1, 3).reshape(B, L, H * d)


def attend_plain(q, k, v, scale):
    s = jnp.einsum('bhmqd,bhmkd->bhmqk', q, k).astype(jnp.float32) * scale
    p = jax.nn.softmax(s, -1).astype(v.dtype)
    return jnp.einsum('bhmqk,bhkd->bhmqd', p, v)


def attend_latent(q_rot, q_free, k_lat, k_ctx, v_lat, v_ctx, scale):
    B, H, M, S, _ = q_rot.shape
    n_lat = k_lat.shape[3]

    def block(i):
        start = i * Q_BLOCK
        qr = lax.dynamic_slice_in_dim(q_rot, start, Q_BLOCK, axis=3)
        qf = lax.dynamic_slice_in_dim(q_free, start, Q_BLOCK, axis=3)
        s = jnp.concatenate([jnp.einsum('bhmqd,bhmkd->bhmqk', qr, k_lat),
                             jnp.einsum('bhmqd,bhmkd->bhmqk', qf, k_ctx)], -1)
        p = jax.nn.softmax(s.astype(jnp.float32) * scale, -1).astype(v_lat.dtype)
        return (jnp.einsum('bhmqk,bhkd->bhmqd', p[..., :n_lat], v_lat)
                + jnp.einsum('bhmqk,bhkd->bhmqd', p[..., n_lat:], v_ctx))

    out = lax.map(block, jnp.arange(S // Q_BLOCK))
    return jnp.moveaxis(out, 0, 3).reshape(B, H, M, S, v_lat.shape[-1])


def diff_attention(lat, ctx_in, lam_vecs, subln_g, layer_idx, rope, last):
    def shape_qk(z):
        B, L, _ = z.shape
        return z.reshape(B, L, DA_HEADS, 2, DA_DIM).transpose(0, 2, 3, 1, 4)
    q, k, v = shape_qk(lat[0]), shape_qk(lat[1]), split_heads(lat[2], DA_HEADS)
    qc, kc, vc = shape_qk(ctx_in[0]), shape_qk(ctx_in[1]), split_heads(ctx_in[2], DA_HEADS)
    lam_init = 0.8 - 0.6 * math.exp(-0.3 * layer_idx)
    lf = lam_vecs.astype(jnp.float32)
    lam = jnp.exp(jnp.sum(lf[0] * lf[1])) - jnp.exp(jnp.sum(lf[2] * lf[3])) + lam_init
    scale = DA_DIM ** -0.5

    def combine(o):
        o = o[:, :, 0] - lam.astype(o.dtype) * o[:, :, 1]
        return merge_heads(rms_norm(o, subln_g) * (1.0 - lam_init))

    y = combine(attend_latent(apply_rope(q, rope), q, apply_rope(k, rope), kc, v, vc, scale))
    yc = None if last else combine(attend_plain(qc, kc, vc, scale))
    return y, yc


def mla_attention(lat, ctx_in, q_norm_g, w_uq, kv_norm_g, w_ukv, rope, last):
    def project(cq, ckv, kpe):
        B, L, _ = cq.shape
        q = (rms_norm(cq, q_norm_g) @ w_uq).reshape(B, L, MLA_HEADS, MLA_NOPE + MLA_ROPE)
        q = q.transpose(0, 2, 1, 3)[:, :, None]
        kv = (rms_norm(ckv, kv_norm_g) @ w_ukv).reshape(B, L, MLA_HEADS, MLA_NOPE + MLA_VDIM)
        kv = kv.transpose(0, 2, 1, 3)
        return q, kv[:, :, None, :, :MLA_NOPE], kpe[:, None, None], kv[..., MLA_NOPE:]
    q, k_nope, k_pe, v = project(*lat)
    qc, kc_nope, kc_pe, vc = project(*ctx_in)

    def full_k(kn, kp):
        return jnp.concatenate([kn, jnp.broadcast_to(kp, kn.shape[:-1] + (MLA_ROPE,))], -1)
    q_rot = jnp.concatenate([q[..., :MLA_NOPE], apply_rope(q[..., MLA_NOPE:], rope)], -1)
    k_lat = full_k(k_nope, apply_rope(k_pe, rope))
    k_ctx = full_k(kc_nope, kc_pe)
    scale = (MLA_NOPE + MLA_ROPE) ** -0.5
    y = merge_heads(attend_latent(q_rot, q, k_lat, k_ctx, v, vc, scale)[:, :, 0])
    yc = None if last else merge_heads(attend_plain(qc, k_ctx, vc, scale)[:, :, 0])
    return y, yc


def chunk_retention(q, k, v, log_gamma, state0, strict, with_output=True):
    B, H, L, dk = q.shape
    dv = v.shape[-1]
    n_chunks = L // RET_CHUNK
    qb = q.reshape(B, H, n_chunks, RET_CHUNK, dk)
    kb = k.reshape(B, H, n_chunks, RET_CHUNK, dk)
    vb = v.reshape(B, H, n_chunks, RET_CHUNK, dv)
    pos = jnp.arange(RET_CHUNK, dtype=jnp.float32)
    lg = log_gamma[:, None]
    zeta = jnp.exp(lg * (RET_CHUNK - 1 - pos)).astype(q.dtype)
    chunk_decay = jnp.exp(log_gamma * RET_CHUNK).astype(q.dtype)[None, :, None, None]
    upd = jnp.einsum('bhncd,bhnce->bhnde', kb * zeta[None, :, None, :, None], vb)

    def step(state, u):
        return state * chunk_decay + u, state
    final, prev = lax.scan(step, state0, jnp.moveaxis(upd, 2, 0))
    if not with_output:
        return None, final
    diff = pos[:, None] - pos[None, :]
    mask = diff > 0 if strict else diff >= 0
    dmask = jnp.where(mask[None], jnp.exp(lg[..., None] * jnp.where(mask, diff, 0.0)[None]), 0.0).astype(q.dtype)
    xi = jnp.exp(lg * (pos + 1)).astype(q.dtype)
    a = jnp.einsum('bhncd,bhnmd->bhncm', qb, kb) * dmask[None, :, None]
    inner = jnp.einsum('bhncm,bhnme->bhnce', a, vb)
    cross = jnp.einsum('bhncd,nbhde->bhnce', qb * xi[None, :, None, :, None], prev)
    return (inner + cross).reshape(B, H, L, dv), final


def retention(lat, ctx_in, decay_f, decay_b, rope, last):
    zq, zk, zv, zg = lat
    zqc, zkc, zvc, zgc = ctx_in
    kscale = RET_DIM ** -0.5
    q = apply_rope(split_heads(zq, RET_HEADS), rope)
    k = apply_rope(split_heads(zk, RET_HEADS), rope) * kscale
    v = split_heads(zv, RET_HEADS)
    qc = split_heads(zqc, RET_HEADS)
    kc = split_heads(zkc, RET_HEADS) * kscale
    vc = split_heads(zvc, RET_HEADS)
    lg_f = jax.nn.log_sigmoid(decay_f.astype(jnp.float32))
    lg_b = jax.nn.log_sigmoid(decay_b.astype(jnp.float32))
    zero = jnp.zeros((q.shape[0], RET_HEADS, RET_DIM, RET_DIM), q.dtype)

    def flip(t):
        return jnp.flip(t, axis=2)
    oc_f, st_f = chunk_retention(qc, kc, vc, lg_f, zero, False, with_output=not last)
    oc_b, st_b = chunk_retention(flip(qc), flip(kc), flip(vc), lg_b, zero, True, with_output=not last)
    o_f, _ = chunk_retention(q, k, v, lg_f, st_f, False)
    o_b, _ = chunk_retention(flip(q), flip(k), flip(v), lg_b, st_b, True)

    def finish(o, zgate):
        return jax.nn.silu(zgate) * merge_heads(rms_norm(o))
    y = finish(o_f + flip(o_b), zg)
    yc = None if last else finish(oc_f + flip(oc_b), zgc)
    return y, yc


def moe(tokens, router_w, router_b, w1, w3, w2):
    T = tokens.shape[0]
    scores = jax.nn.sigmoid((tokens @ router_w).astype(jnp.float32))
    choice = scores + router_b.astype(jnp.float32)
    group_score = lax.top_k(choice.reshape(T, N_GROUPS, EXPERTS_PER_GROUP), 2)[0].sum(-1)
    _, gidx = lax.top_k(group_score, TOPK_GROUP)
    gmask = jax.nn.one_hot(gidx, N_GROUPS, dtype=jnp.float32).sum(1)
    emask = jnp.repeat(gmask, EXPERTS_PER_GROUP, axis=-1)
    _, eidx = lax.top_k(jnp.where(emask > 0, choice, -jnp.inf), TOP_K)
    wsel = jnp.take_along_axis(scores, eidx, -1)
    wsel = wsel / jnp.sum(wsel, -1, keepdims=True) * ROUTED_SCALE
    gates = jnp.sum(jax.nn.one_hot(eidx, N_EXPERTS, dtype=jnp.float32) * wsel[..., None], 1).astype(tokens.dtype)

    def expert_step(acc, p):
        w1e, w3e, w2e, ge = p
        y = (jax.nn.silu(tokens @ w1e) * (tokens @ w3e)) @ w2e
        return acc + ge[:, None] * y, None
    out, _ = lax.scan(expert_step, jnp.zeros_like(tokens), (w1, w3, w2, gates.T))
    return out


def setup_inputs(seed: int = 0) -> dict:
    key = jax.random.key(seed)
    ks = jax.random.split(key, 25)
    f32 = jnp.float32

    def nrm(k, shape, s):
        return jax.random.normal(k, shape, f32) * s

    def gain(k, shape):
        return 1.0 + nrm(k, shape, 0.02)
    decay_init = jnp.log(2.0 ** (5.0 + jnp.arange(RET_HEADS, dtype=f32)) - 1.0)
    return {
        'x': nrm(ks[0], (BATCH, SEQ, D_MODEL), 1.0),
        'c': nrm(ks[1], (BATCH, D_MODEL), 1.0),
        'ctx': nrm(ks[2], (BATCH, CTX_LEN, D_MODEL), 1.0),
        'c_ctx': nrm(ks[3], (D_MODEL,), 1.0),
        'w_ada': nrm(ks[4], (DEPTH, D_MODEL, N_MOD * D_MODEL), 0.5 * D_MODEL ** -0.5),
        'b_ada': nrm(ks[5], (DEPTH, N_MOD * D_MODEL), 0.02),
        'w_in': nrm(ks[6], (DEPTH, D_MODEL, IN_COLS), D_MODEL ** -0.5),
        'da_lambda': nrm(ks[7], (DEPTH, 4, DA_DIM), 0.1),
        'da_subln': gain(ks[8], (DEPTH, DA_VDIM)),
        'mla_q_norm': gain(ks[9], (DEPTH, MLA_Q_RANK)),
        'mla_w_uq': nrm(ks[10], (DEPTH, MLA_Q_RANK, MLA_HEADS * (MLA_NOPE + MLA_ROPE)), MLA_Q_RANK ** -0.5),
        'mla_kv_norm': gain(ks[11], (DEPTH, MLA_KV_RANK)),
        'mla_w_ukv': nrm(ks[12], (DEPTH, MLA_KV_RANK, MLA_HEADS * (MLA_NOPE + MLA_VDIM)), MLA_KV_RANK ** -0.5),
        'ret_decay_f': decay_init + nrm(ks[13], (DEPTH, RET_HEADS), 0.1),
        'ret_decay_b': decay_init + nrm(ks[14], (DEPTH, RET_HEADS), 0.1),
        'w_out': nrm(ks[15], (DEPTH, MIX_WIDTH, D_MODEL), BETA * MIX_WIDTH ** -0.5),
        'ln1_g': gain(ks[16], (DEPTH, D_MODEL)),
        'ln1_b': nrm(ks[17], (DEPTH, D_MODEL), 0.02),
        'router_w': nrm(ks[18], (D_MODEL, N_EXPERTS), D_MODEL ** -0.5),
        'router_b': nrm(ks[19], (N_EXPERTS,), 0.01),
        'exp_w1': nrm(ks[20], (DEPTH, N_EXPERTS, D_MODEL, D_EXPERT), D_MODEL ** -0.5),
        'exp_w3': nrm(ks[21], (DEPTH, N_EXPERTS, D_MODEL, D_EXPERT), D_MODEL ** -0.5),
        'exp_w2': nrm(ks[22], (DEPTH, N_EXPERTS, D_EXPERT, D_MODEL), BETA * D_EXPERT ** -0.5),
        'ln2_g': gain(ks[23], (DEPTH, D_MODEL)),
        'ln2_b': nrm(ks[24], (DEPTH, D_MODEL), 0.02),
    }


def reference(x, c, ctx, c_ctx, w_ada, b_ada, w_in, da_lambda, da_subln, mla_q_norm, mla_w_uq,
              mla_kv_norm, mla_w_ukv, ret_decay_f, ret_decay_b, w_out, ln1_g, ln1_b,
              router_w, router_b, exp_w1, exp_w3, exp_w2, ln2_g, ln2_b):
    B, S, D = x.shape
    C = ctx.shape[1]
    ROWS = S // GRID_W
    rope_attn = axial_rope(ROWS, ATTN_ROT_DIM)
    rope_ret = axial_rope(ROWS, RET_DIM)
    silu_c = jax.nn.silu(c)
    silu_cc = jax.nn.silu(c_ctx)
    for l in range(DEPTH):
        last = l == DEPTH - 1
        mod = (silu_c @ w_ada[l] + b_ada[l]).reshape(B, N_MOD, 1, D)
        modc = (silu_cc @ w_ada[l] + b_ada[l]).reshape(N_MOD, D)
        h = x * (1 + mod[:, 1]) + mod[:, 0]
        hc = ctx * (1 + modc[1]) + modc[0]
        parts = split_cols(h @ w_in[l])
        parts_c = split_cols(hc @ w_in[l])
        y_da, yc_da = diff_attention(parts[0:3], parts_c[0:3], da_lambda[l], da_subln[l], l, rope_attn, last)
        y_mla, yc_mla = mla_attention(parts[3:6], parts_c[3:6], mla_q_norm[l], mla_w_uq[l],
                                      mla_kv_norm[l], mla_w_ukv[l], rope_attn, last)
        y_ret, yc_ret = retention(parts[6:10], parts_c[6:10], ret_decay_f[l], ret_decay_b[l], rope_ret, last)
        y = jnp.concatenate([y_da, y_mla, y_ret], -1) @ w_out[l]
        x = layer_norm(ALPHA * x + mod[:, 2] * y, ln1_g[l], ln1_b[l])
        if not last:
            yc = jnp.concatenate([yc_da, yc_mla, yc_ret], -1) @ w_out[l]
            ctx = layer_norm(ALPHA * ctx + modc[2] * yc, ln1_g[l], ln1_b[l])
        h = x * (1 + mod[:, 4]) + mod[:, 3]
        tokens = h.reshape(B * S, D)
        if not last:
            hc = ctx * (1 + modc[4]) + modc[3]
            tokens = jnp.concatenate([tokens, hc.reshape(B * C, D)], 0)
        f = moe(tokens, router_w, router_b, exp_w1[l], exp_w3[l], exp_w2[l])
        x = layer_norm(ALPHA * x + mod[:, 5] * f[:B * S].reshape(B, S, D), ln2_g[l], ln2_b[l])
        if not last:
            ctx = layer_norm(ALPHA * ctx + modc[5] * f[B * S:].reshape(B, C, D), ln2_g[l], ln2_b[l])
    return x
```

```python
import functools
import math

import jax
import jax.numpy as jnp
import numpy as np
from jax import lax
from jax.experimental import pallas as pl
from jax.experimental.pallas import tpu as pltpu

F32 = jnp.float32
BF16 = jnp.bfloat16

GRID_W = 64
DA_HEADS, DA_DIM, DA_VDIM = 6, 32, 64
MLA_HEADS, MLA_NOPE, MLA_ROPE, MLA_VDIM = 6, 64, 32, 64
MLA_Q_RANK, MLA_KV_RANK = 256, 128
RET_HEADS, RET_DIM, RET_CHUNK = 4, 64, 128
ROPE_BASE = 10000.0
N_EXPERTS, N_GROUPS, D_EXPERT = 16, 4, 512
ROUTED_SCALE = 1.0
N_MOD = 6
LN_EPS = 1e-5
RMS_EPS = 1e-6
LOG2E = math.log2(math.e)

DA_W = DA_HEADS * 2 * DA_DIM
DA_VW = DA_HEADS * DA_VDIM
RET_W = RET_HEADS * RET_DIM
MLA_QW = MLA_HEADS * (MLA_NOPE + MLA_ROPE)
MLA_VW = MLA_HEADS * MLA_VDIM
IN_SPLIT = (DA_W, DA_W, DA_VW, MLA_Q_RANK, MLA_KV_RANK, MLA_ROPE, RET_W, RET_W, RET_W, RET_W)

LANES = 128
V_ROWS = 80
TM = 256
TQ = 256
TK = 512
TM_MOE = 640

NT_DIMS = (((1,), (1,)), ((), ()))
TN_DIMS = (((0,), (0,)), ((), ()))


def _sigmoid(x):
    return 1.0 / (1.0 + jnp.exp(-x))


def _mod_kernel(c_ref, w_ref, b_ref, o_ref):
    c = c_ref[...]
    sc = (c * _sigmoid(c)).astype(BF16)
    o_ref[0] = jnp.dot(sc, w_ref[0].astype(BF16), preferred_element_type=F32) + b_ref[0]


def _modulation(cvec, w_ada, b_ada):
    depth, d, n = w_ada.shape
    tn = 1536
    return pl.pallas_call(
        _mod_kernel,
        grid=(depth, n // tn),
        in_specs=[
            pl.BlockSpec((8, d), lambda l, j: (0, 0)),
            pl.BlockSpec((1, d, tn), lambda l, j: (l, 0, j)),
            pl.BlockSpec((1, 1, tn), lambda l, j: (l, 0, j)),
        ],
        out_specs=pl.BlockSpec((1, 8, tn), lambda l, j: (l, 0, j)),
        out_shape=jax.ShapeDtypeStruct((depth, 8, n), F32),
        name="adaln_mod",
    )(cvec, w_ada, b_ada.reshape(depth, 1, n))


_O_DAQ, _O_DAQS, _O_DAK, _O_DAKS, _O_DAV = 0, 384, 768, 1152, 1536
_O_CQ, _O_CKV, _O_KPE, _O_KPES = 1920, 2176, 2304, 2336
_O_RQ, _O_RQS, _O_RK, _O_RKS, _O_RV, _O_RG = 2368, 2624, 2880, 3136, 3392, 3648
N_ZROWS = 3904


def _rms_rows(x):
    return x * lax.rsqrt(jnp.mean(x * x, axis=0, keepdims=True) + RMS_EPS)


def _in_kernel(x_ref, sh_ref, sc_ref, wT_ref, cosA_ref, sinA_ref, cosR_ref, sinR_ref,
               gq_ref, wuqT_ref, gkv_ref, wvT_ref, wkT_ref,
               daqr, daqf, dak, dav, mqr, mqf, mk, mv, rq, rk, rv, rg, z_ref, *, cq_da, cq_mla, kscale):
    tm = x_ref.shape[1]
    x = x_ref[0]
    h = (x * (1.0 + sc_ref[0, 0, 0]) + sh_ref[0, 0, 0]).astype(BF16)
    z_ref[...] = lax.dot_general(wT_ref[0], h, NT_DIMS, preferred_element_type=F32)
    cosA = cosA_ref[...]
    sinA = sinA_ref[...]
    cosR = cosR_ref[...]
    sinR = sinR_ref[...]
    ones_rows = (lax.broadcasted_iota(jnp.int32, (DA_HEADS, V_ROWS - 64, tm), 1) == 0).astype(BF16)

    def rows(off, n):
        return z_ref[off:off + n, :]

    q = rows(_O_DAQ, DA_W).reshape(12, DA_DIM, tm)
    qs = rows(_O_DAQS, DA_W).reshape(12, DA_DIM, tm)
    daqr[0] = ((q * cosA + qs * sinA) * cq_da).astype(BF16)
    daqf[0] = (q * cq_da).astype(BF16)
    k = rows(_O_DAK, DA_W).reshape(12, DA_DIM, tm)
    ks = rows(_O_DAKS, DA_W).reshape(12, DA_DIM, tm)
    kr = (k * cosA + ks * sinA).reshape(DA_HEADS, 2 * DA_DIM, tm)
    kp = jnp.concatenate([kr, jnp.zeros_like(kr)], axis=1).reshape(DA_HEADS * LANES, tm)
    dak[0] = kp.T.astype(BF16)
    dav[0, :, 0:64, :] = rows(_O_DAV, DA_VW).reshape(DA_HEADS, DA_VDIM, tm).astype(BF16)
    dav[0, :, 64:V_ROWS, :] = ones_rows

    cqn = (_rms_rows(rows(_O_CQ, MLA_Q_RANK)) * gq_ref[0]).astype(BF16)
    qm = jnp.dot(wuqT_ref[0], cqn, preferred_element_type=F32)
    qh = qm[0:MLA_QW].reshape(MLA_HEADS, MLA_NOPE + MLA_ROPE, tm)
    qsw = qm[MLA_QW:MLA_QW + MLA_HEADS * MLA_ROPE].reshape(MLA_HEADS, MLA_ROPE, tm)
    rot = qh[:, MLA_NOPE:, :] * cosA + qsw * sinA
    mqr[0, :, 0:MLA_NOPE, :] = (qh[:, 0:MLA_NOPE, :] * cq_mla).astype(BF16)
    mqr[0, :, MLA_NOPE:, :] = (rot * cq_mla).astype(BF16)
    mqf[0] = (qh * cq_mla).astype(BF16)
    ckvn = (_rms_rows(rows(_O_CKV, MLA_KV_RANK)) * gkv_ref[0]).astype(BF16)
    vm = jnp.dot(wvT_ref[0], ckvn, preferred_element_type=F32)
    mv[0, :, 0:64, :] = vm.reshape(MLA_HEADS, MLA_VDIM, tm).astype(BF16)
    mv[0, :, 64:V_ROWS, :] = ones_rows
    kper = rows(_O_KPE, MLA_ROPE) * cosA + rows(_O_KPES, MLA_ROPE) * sinA
    kin = jnp.concatenate([ckvn, kper.astype(BF16)], axis=0)
    kmT = jnp.dot(wkT_ref[0], kin, preferred_element_type=F32)
    mk[0] = kmT.T.astype(BF16)

    r_q = rows(_O_RQ, RET_W).reshape(RET_HEADS, RET_DIM, tm)
    r_qs = rows(_O_RQS, RET_W).reshape(RET_HEADS, RET_DIM, tm)
    rq[0] = (r_q * cosR + r_qs * sinR).astype(BF16)
    r_k = rows(_O_RK, RET_W).reshape(RET_HEADS, RET_DIM, tm)
    r_ks = rows(_O_RKS, RET_W).reshape(RET_HEADS, RET_DIM, tm)
    rkr = (r_k * cosR + r_ks * sinR) * kscale
    rkp = jnp.concatenate([rkr, jnp.zeros_like(rkr)], axis=1).reshape(RET_HEADS * LANES, tm)
    rk[0] = rkp.T.astype(BF16)
    rv[0] = rows(_O_RV, RET_W).reshape(RET_HEADS, RET_DIM, tm).astype(BF16)
    rg[0] = rows(_O_RG, RET_W)


def _in_proj(l, xc, modr, wT_all, tabs, gq, wuqT, gkv, wvT, wkT, S):
    B, Sp, D = xc.shape
    nt = Sp // TM
    n_lat_tiles = S // TM
    cosA, sinA, cosR, sinR = tabs

    def mod_spec(j):
        return pl.BlockSpec((1, 1, 1, 1, D), lambda b, t: (l, jnp.where(t >= n_lat_tiles, 2, b), j, 0, 0))

    def wspec(a):
        return pl.BlockSpec((1,) + a.shape[1:], lambda b, t: (l,) + (0,) * (a.ndim - 1))

    def fm4(h, d):
        return pl.BlockSpec((1, h, d, TM), lambda b, t: (b, 0, 0, t))

    def tmaj(w):
        return pl.BlockSpec((1, TM, w), lambda b, t: (b, t, 0))

    kern = functools.partial(
        _in_kernel,
        cq_da=float(DA_DIM ** -0.5 * LOG2E),
        cq_mla=float((MLA_NOPE + MLA_ROPE) ** -0.5 * LOG2E),
        kscale=float(RET_DIM ** -0.5),
    )
    sds = jax.ShapeDtypeStruct
    return pl.pallas_call(
        kern,
        grid=(B, nt),
        in_specs=[
            pl.BlockSpec((1, TM, D), lambda b, t: (b, t, 0)),
            mod_spec(0), mod_spec(1),
            wspec(wT_all),
            pl.BlockSpec((DA_DIM, TM), lambda b, t: (0, t)),
            pl.BlockSpec((DA_DIM, TM), lambda b, t: (0, t)),
            pl.BlockSpec((RET_DIM, TM), lambda b, t: (0, t)),
            pl.BlockSpec((RET_DIM, TM), lambda b, t: (0, t)),
            wspec(gq), wspec(wuqT), wspec(gkv), wspec(wvT), wspec(wkT),
        ],
        out_specs=[
            fm4(12, DA_DIM), fm4(12, DA_DIM), tmaj(DA_HEADS * LANES), fm4(DA_HEADS, V_ROWS),
            fm4(MLA_HEADS, 96), fm4(MLA_HEADS, 96), tmaj(MLA_HEADS * LANES), fm4(MLA_HEADS, V_ROWS),
            fm4(RET_HEADS, RET_DIM), tmaj(RET_HEADS * LANES), fm4(RET_HEADS, RET_DIM),
            pl.BlockSpec((1, RET_W, TM), lambda b, t: (b, 0, t)),
        ],
        out_shape=[
            sds((B, 12, DA_DIM, Sp), BF16), sds((B, 12, DA_DIM, Sp), BF16),
            sds((B, Sp, DA_HEADS * LANES), BF16), sds((B, DA_HEADS, V_ROWS, Sp), BF16),
            sds((B, MLA_HEADS, 96, Sp), BF16), sds((B, MLA_HEADS, 96, Sp), BF16),
            sds((B, Sp, MLA_HEADS * LANES), BF16), sds((B, MLA_HEADS, V_ROWS, Sp), BF16),
            sds((B, RET_HEADS, RET_DIM, Sp), BF16), sds((B, Sp, RET_HEADS * LANES), BF16),
            sds((B, RET_HEADS, RET_DIM, Sp), BF16), sds((B, RET_W, Sp), F32),
        ],
        scratch_shapes=[pltpu.VMEM((N_ZROWS, TM), F32)],
        compiler_params=pltpu.CompilerParams(
            dimension_semantics=("arbitrary", "arbitrary"), vmem_limit_bytes=56 * 1024 * 1024),
        name="in_proj",
    )(xc, modr, modr, wT_all, cosA, sinA, cosR, sinR, gq, wuqT, gkv, wvT, wkT)


def _attn_kernel(*refs, mode, S, C, lam_init):
    if mode == "da":
        qr_ref, qf_ref, k_ref, v_ref, lam_ref, g_ref, o_ref, m_sc, acc_sc = refs
    else:
        qr_ref, qf_ref, k_ref, v_ref, o_ref, m_sc, acc_sc = refs
    tq = o_ref.shape[2]
    is_ctx = pl.program_id(2) == S // tq

    def qpad(ref):
        if mode == "da":
            z = jnp.zeros((DA_DIM, tq), BF16)
            c1 = jnp.concatenate([ref[0, 0], z, z, z], axis=0)
            c2 = jnp.concatenate([z, ref[0, 1], z, z], axis=0)
            return jnp.concatenate([c1, c2], axis=1)
        return jnp.concatenate([ref[0, 0], jnp.zeros((LANES - 96, tq), BF16)], axis=0)

    m_sc[...] = jnp.full(m_sc.shape, -jnp.inf, F32)
    acc_sc[...] = jnp.zeros(acc_sc.shape, F32)

    def step(kc, vc, q):
        s = jnp.dot(kc, q, preferred_element_type=F32)
        m_old = m_sc[...]
        m_new = jnp.maximum(m_old, jnp.max(s, axis=0, keepdims=True))
        p = jnp.exp2(s - m_new).astype(BF16)
        acc_sc[...] = acc_sc[...] * jnp.exp2(m_old - m_new) + jnp.dot(vc, p, preferred_element_type=F32)
        m_sc[...] = m_new

    q_rot = qpad(qr_ref)

    def body(j, carry):
        off = pl.multiple_of(j * TK, TK)
        step(k_ref[0, pl.ds(off, TK), :], v_ref[0, 0, :, pl.ds(off, TK)], q_rot)
        return carry

    lax.fori_loop(0, jnp.where(is_ctx, 0, S // TK), body, 0)
    step(k_ref[0, S:S + C, :], v_ref[0, 0, :, S:S + C], qpad(qf_ref))

    acc = acc_sc[...]
    o = acc[0:64, :] / acc[64:65, :]
    if mode == "da":
        lf = lam_ref[0]
        lam = (jnp.exp(jnp.sum(lf[0:1] * lf[1:2], axis=1, keepdims=True))
               - jnp.exp(jnp.sum(lf[2:3] * lf[3:4], axis=1, keepdims=True)) + lam_init)
        od = o[:, 0:tq] - lam * o[:, tq:2 * tq]
        y = _rms_rows(od) * g_ref[0]
        o_ref[0] = (y * (1.0 - lam_init)).astype(BF16)
    else:
        o_ref[0] = o.astype(BF16)


def _attention(mode, l, qr, qf, k, v, S, C, lam=None, g=None):
    B, nmaps, d, Sp = qr.shape
    H = v.shape[1]
    nm = nmaps // H
    lam_init = 0.8 - 0.6 * math.exp(-0.3 * l)
    kern = functools.partial(_attn_kernel, mode=mode, S=S, C=C, lam_init=lam_init)
    in_specs = [
        pl.BlockSpec((1, nm, d, TQ), lambda b, h, i: (b, h, 0, i)),
        pl.BlockSpec((1, nm, d, TQ), lambda b, h, i: (b, h, 0, i)),
        pl.BlockSpec((1, Sp, LANES), lambda b, h, i: (b, 0, h)),
        pl.BlockSpec((1, 1, V_ROWS, Sp), lambda b, h, i: (b, h, 0, 0)),
    ]
    args = [qr, qf, k, v]
    if mode == "da":
        in_specs += [pl.BlockSpec((1,) + lam.shape[1:], lambda b, h, i: (l, 0, 0)),
                     pl.BlockSpec((1,) + g.shape[1:], lambda b, h, i: (l, 0, 0))]
        args += [lam, g]
    return pl.pallas_call(
        kern,
        grid=(B, H, Sp // TQ),
        in_specs=in_specs,
        out_specs=pl.BlockSpec((1, 64, TQ), lambda b, h, i: (b, h, i)),
        out_shape=jax.ShapeDtypeStruct((B, H * 64, Sp), BF16),
        scratch_shapes=[pltpu.VMEM((1, nm * TQ), F32), pltpu.VMEM((V_ROWS, nm * TQ), F32)],
        compiler_params=pltpu.CompilerParams(
            dimension_semantics=("arbitrary", "arbitrary", "arbitrary"), vmem_limit_bytes=48 * 1024 * 1024),
        name="attn_" + mode,
    )(*args)


def _ret_kernel(q_ref, k_ref, v_ref, g_ref, dec_ref, o_ref, sb_sc, *, S, C):
    CH = RET_CHUNK
    nl, nc = S // CH, C // CH

    def log_sigmoid(x):
        return jnp.minimum(x, 0.0) - jnp.log1p(jnp.exp(-jnp.abs(x)))

    lgf = log_sigmoid(dec_ref[0, 0][0:1, :])
    lgb = log_sigmoid(dec_ref[1, 0][0:1, :])
    ii = lax.broadcasted_iota(jnp.int32, (CH, CH), 1).astype(F32)
    jj = lax.broadcasted_iota(jnp.int32, (CH, CH), 0).astype(F32)
    dd = ii - jj
    fwd = dd >= 0
    mt = (jnp.where(fwd, jnp.exp(lgf * jnp.where(fwd, dd, 0.0)), 0.0)
          + jnp.where(fwd, 0.0, jnp.exp(lgb * jnp.where(fwd, 0.0, -dd))))
    xi_f = jnp.exp(lgf * (ii + 1.0))
    xi_b = jnp.exp(lgb * (CH - ii))
    zeta_f = jnp.exp(lgf * (CH - 1.0 - jj))
    zeta_b = jnp.exp(lgb * jj)
    cdf = jnp.exp(lgf * CH)
    cdb = jnp.exp(lgb * CH)
    zq = jnp.zeros((RET_DIM, CH), BF16)

    def sl(c):
        return pl.ds(pl.multiple_of(c * CH, CH), CH)

    def state_update(c, zeta):
        kz = (k_ref[0, sl(c), :].astype(F32) * zeta).astype(BF16)
        return jnp.dot(v_ref[0, 0, :, sl(c)], kz, preferred_element_type=F32)

    def bwd_step(c, sb):
        sb_sc[c] = sb
        return sb * cdb + state_update(c, zeta_b)

    sb = jnp.zeros((RET_DIM, CH), F32)
    for c in range(nl + nc - 1, nl - 1, -1):
        sb = bwd_step(c, sb)
    lax.fori_loop(0, nl, lambda n, s: bwd_step(nl - 1 - n, s), sb)

    def fwd_step(c, sf):
        qc = q_ref[0, 0, :, sl(c)]
        qp = jnp.concatenate([qc, zq], axis=0)
        kc = k_ref[0, sl(c), :]
        vc = v_ref[0, 0, :, sl(c)]
        at = jnp.dot(kc, qp, preferred_element_type=F32)
        qpf = qp.astype(F32)
        rhs = jnp.concatenate([(at * mt).astype(BF16), (qpf * xi_f).astype(BF16), (qpf * xi_b).astype(BF16)], axis=0)
        lhs = jnp.concatenate([vc, sf.astype(BF16), sb_sc[c].astype(BF16)], axis=1)
        o = jnp.dot(lhs, rhs, preferred_element_type=F32)
        gch = g_ref[0, :, sl(c)]
        o_ref[0, :, sl(c)] = (gch * _sigmoid(gch) * _rms_rows(o)).astype(BF16)
        return sf * cdf + state_update(c, zeta_f)

    sf = jnp.zeros((RET_DIM, CH), F32)
    for c in range(nl, nl + nc):
        sf = fwd_step(c, sf)
    lax.fori_loop(0, nl, fwd_step, sf)


def _retention(rq, rk, rv, rg, dec, S, C):
    B, H, d, Sp = rq.shape
    kern = functools.partial(_ret_kernel, S=S, C=C)
    return pl.pallas_call(
        kern,
        grid=(B, H),
        in_specs=[
            pl.BlockSpec((1, 1, d, Sp), lambda b, h: (b, h, 0, 0)),
            pl.BlockSpec((1, Sp, LANES), lambda b, h: (b, 0, h)),
            pl.BlockSpec((1, 1, d, Sp), lambda b, h: (b, h, 0, 0)),
            pl.BlockSpec((1, d, Sp), lambda b, h: (b, h, 0)),
            pl.BlockSpec((2, 1, 8, LANES), lambda b, h: (0, h, 0, 0)),
        ],
        out_specs=pl.BlockSpec((1, d, Sp), lambda b, h: (b, h, 0)),
        out_shape=jax.ShapeDtypeStruct((B, H * d, Sp), BF16),
        scratch_shapes=[pltpu.VMEM((Sp // RET_CHUNK, RET_DIM, RET_CHUNK), F32)],
        compiler_params=pltpu.CompilerParams(
            dimension_semantics=("arbitrary", "arbitrary"), vmem_limit_bytes=56 * 1024 * 1024),
        name="retention",
    )(rq, rk, rv, rg, dec)


def _layer_norm(t, g, b):
    mu = jnp.mean(t, axis=-1, keepdims=True)
    tc = t - mu
    var = jnp.mean(tc * tc, axis=-1, keepdims=True)
    return tc * lax.rsqrt(var + LN_EPS) * g + b


def _out_kernel(yda_ref, ymla_ref, yret_ref, w_ref, x_ref, gate_ref, g1_ref, b1_ref, sh2_ref, sc2_ref, rw_ref,
                x1_ref, h2_ref, lg_ref, *, alpha):
    proj = (lax.dot_general(yda_ref[0], w_ref[0, 0:DA_VW, :], TN_DIMS, preferred_element_type=F32)
            + lax.dot_general(ymla_ref[0], w_ref[0, DA_VW:DA_VW + MLA_VW, :], TN_DIMS, preferred_element_type=F32)
            + lax.dot_general(yret_ref[0], w_ref[0, DA_VW + MLA_VW:, :], TN_DIMS, preferred_element_type=F32))
    x1 = _layer_norm(alpha * x_ref[0] + gate_ref[0, 0, 0] * proj, g1_ref[0], b1_ref[0])
    x1_ref[0] = x1
    h2 = x1 * (1.0 + sc2_ref[0, 0, 0]) + sh2_ref[0, 0, 0]
    h2_ref[0] = h2.astype(BF16)
    lg_ref[0] = jnp.dot(h2, rw_ref[...], preferred_element_type=F32, precision=lax.Precision.HIGHEST)


def _out_proj(l, yda, ymla, yret, w_out, xc, modr, ln_g, ln_b, rw_pad, S, alpha):
    B, Sp, D = xc.shape
    n_lat_tiles = S // TM

    def mod_spec(j):
        return pl.BlockSpec((1, 1, 1, 1, D), lambda b, t: (l, jnp.where(t >= n_lat_tiles, 2, b), j, 0, 0))

    def fm(a):
        return pl.BlockSpec((1, a.shape[1], TM), lambda b, t: (b, 0, t))

    def wspec(a):
        return pl.BlockSpec((1,) + a.shape[1:], lambda b, t: (l,) + (0,) * (a.ndim - 1))

    row = pl.BlockSpec((1, TM, D), lambda b, t: (b, t, 0))
    sds = jax.ShapeDtypeStruct
    return pl.pallas_call(
        functools.partial(_out_kernel, alpha=alpha),
        grid=(B, Sp // TM),
        in_specs=[fm(yda), fm(ymla), fm(yret), wspec(w_out), row, mod_spec(2), wspec(ln_g), wspec(ln_b),
                  mod_spec(3), mod_spec(4), pl.BlockSpec(rw_pad.shape, lambda b, t: (0, 0))],
        out_specs=[row, row, pl.BlockSpec((1, TM, LANES), lambda b, t: (b, t, 0))],
        out_shape=[sds((B, Sp, D), F32), sds((B, Sp, D), BF16), sds((B, Sp, LANES), F32)],
        compiler_params=pltpu.CompilerParams(dimension_semantics=("arbitrary", "arbitrary")),
        name="out_proj",
    )(yda, ymla, yret, w_out, xc, modr, ln_g, ln_b, modr, modr, rw_pad)


def _gates_T(logits_T, rb):
    s = _sigmoid(logits_T)
    ch = s + rb
    srow = [s[e:e + 1] for e in range(N_EXPERTS)]
    crow = [ch[e:e + 1] for e in range(N_EXPERTS)]
    per = N_EXPERTS // N_GROUPS
    gs = []
    for g in range(N_GROUPS):
        a, b, c, d = crow[per * g: per * g + per]
        m1, n1, m2, n2 = jnp.maximum(a, b), jnp.minimum(a, b), jnp.maximum(c, d), jnp.minimum(c, d)
        gs.append(jnp.maximum(m1, m2) + jnp.maximum(jnp.minimum(m1, m2), jnp.maximum(n1, n2)))
    gmax = functools.reduce(jnp.maximum, gs)
    taken = jnp.zeros(gmax.shape, jnp.bool_)
    gsel = []
    for g in range(N_GROUPS):
        sg = jnp.logical_and(gs[g] == gmax, jnp.logical_not(taken))
        gsel.append(sg)
        taken = jnp.logical_or(taken, sg)
    neg = jnp.full(gmax.shape, -jnp.inf, F32)
    mc = [jnp.where(gsel[e // per], crow[e], neg) for e in range(N_EXPERTS)]
    sel = [jnp.zeros(gmax.shape, jnp.bool_) for _ in range(N_EXPERTS)]
    for _ in range(2):
        top = functools.reduce(jnp.maximum, mc)
        taken = jnp.zeros(gmax.shape, jnp.bool_)
        for e in range(N_EXPERTS):
            hit = jnp.logical_and(mc[e] == top, jnp.logical_not(taken))
            taken = jnp.logical_or(taken, hit)
            sel[e] = jnp.logical_or(sel[e], hit)
            mc[e] = jnp.where(hit, neg, mc[e])
    w = [jnp.where(sel[e], srow[e], 0.0) for e in range(N_EXPERTS)]
    wsum = functools.reduce(lambda a, b: a + b, w)
    return jnp.concatenate([we / wsum * ROUTED_SCALE for we in w], axis=0)


def _moe_kernel(h_ref, lg_ref, rb_ref, w13_ref, w2_ref, x1_ref, gb_ref, gc_ref, g2_ref, b2_ref, o_ref,
                gates_sc, acc_sc, *, S, alpha):
    e = pl.program_id(2)
    tm = h_ref.shape[1]

    @pl.when(e == 0)
    def _():
        gT = _gates_T(lg_ref[0].T[0:N_EXPERTS, :], rb_ref[...])
        gates_sc[...] = jnp.concatenate([gT, jnp.zeros((LANES - N_EXPERTS, tm), F32)], axis=0).T
        acc_sc[...] = jnp.zeros(acc_sc.shape, F32)

    ab = jnp.dot(h_ref[0], w13_ref[0, 0], preferred_element_type=F32)
    a = ab[:, 0:D_EXPERT]
    hid = (a * _sigmoid(a) * ab[:, D_EXPERT:]).astype(BF16)
    y = jnp.dot(hid, w2_ref[0, 0], preferred_element_type=F32)
    lane = lax.broadcasted_iota(jnp.int32, (tm, LANES), 1)
    ge = jnp.sum(jnp.where(lane == e, gates_sc[...], 0.0), axis=1, keepdims=True)
    acc_sc[...] += ge * y

    @pl.when(e == N_EXPERTS - 1)
    def _():
        n_lat = S - pl.program_id(1) * tm
        rowi = lax.broadcasted_iota(jnp.int32, (tm, 1), 0)
        gate = jnp.where(rowi < n_lat, gb_ref[0, 0, 0], gc_ref[0, 0, 0])
        o_ref[0] = _layer_norm(alpha * x1_ref[0] + gate * acc_sc[...], g2_ref[0], b2_ref[0])


def _moe(l, h2, logits, rb, w13, w2, x1, modr, ln_g, ln_b, S, alpha):
    B, Sp, D = x1.shape
    tm = TM_MOE

    def wspec(a):
        return pl.BlockSpec((1,) + a.shape[1:], lambda b, t, e: (l,) + (0,) * (a.ndim - 1))

    row = lambda b, t, e: (b, t, 0)
    return pl.pallas_call(
        functools.partial(_moe_kernel, S=S, alpha=alpha),
        grid=(B, Sp // tm, N_EXPERTS),
        in_specs=[
            pl.BlockSpec((1, tm, D), row),
            pl.BlockSpec((1, tm, LANES), row),
            pl.BlockSpec(rb.shape, lambda b, t, e: (0, 0)),
            pl.BlockSpec((1, 1, D, 2 * D_EXPERT), lambda b, t, e: (l, e, 0, 0)),
            pl.BlockSpec((1, 1, D_EXPERT, D), lambda b, t, e: (l, e, 0, 0)),
            pl.BlockSpec((1, tm, D), row),
            pl.BlockSpec((1, 1, 1, 1, D), lambda b, t, e: (l, b, 5, 0, 0)),
            pl.BlockSpec((1, 1, 1, 1, D), lambda b, t, e: (l, 2, 5, 0, 0)),
            wspec(ln_g), wspec(ln_b),
        ],
        out_specs=pl.BlockSpec((1, tm, D), row),
        out_shape=jax.ShapeDtypeStruct((B, Sp, D), F32),
        scratch_shapes=[pltpu.VMEM((tm, LANES), F32), pltpu.VMEM((tm, D), F32)],
        compiler_params=pltpu.CompilerParams(
            dimension_semantics=("arbitrary", "arbitrary", "arbitrary"), vmem_limit_bytes=56 * 1024 * 1024),
        name="moe",
    )(h2, logits, rb, w13, w2, x1, modr, modr, ln_g, ln_b)


def _pairswap(w):
    n = w.shape[-1]
    return w.reshape(w.shape[:-1] + (n // 2, 2))[..., ::-1].reshape(w.shape)


def _rope_tables(S, C, rot_dim):
    rows = S // GRID_W
    r = jnp.repeat(jnp.arange(rows, dtype=F32), GRID_W)
    col = jnp.tile(jnp.arange(GRID_W, dtype=F32), rows)
    n_freq = rot_dim // 4
    freqs = ROPE_BASE ** (-jnp.arange(n_freq, dtype=F32) / n_freq)
    ang = jnp.concatenate([r[:, None] * freqs, col[:, None] * freqs], -1)
    cos, sin = jnp.cos(ang), jnp.sin(ang)
    cos_rep = jnp.repeat(cos, 2, axis=-1)
    sin_alt = jnp.stack([-sin, sin], -1).reshape(S, rot_dim)
    cos_rep = jnp.concatenate([cos_rep, jnp.ones((C, rot_dim), F32)], 0)
    sin_alt = jnp.concatenate([sin_alt, jnp.zeros((C, rot_dim), F32)], 0)
    return cos_rep.T, sin_alt.T


def kernel(x, c, ctx, c_ctx, w_ada, b_ada, w_in, da_lambda, da_subln, mla_q_norm, mla_w_uq, mla_kv_norm, mla_w_ukv, ret_decay_f, ret_decay_b, w_out, ln1_g, ln1_b, router_w, router_b, exp_w1, exp_w3, exp_w2, ln2_g, ln2_b):
    B, S, D = x.shape
    C = ctx.shape[1]
    depth = w_in.shape[0]
    alpha = float((2 * depth) ** 0.25)

    cvec = jnp.concatenate([c, c_ctx[None, :], jnp.zeros((8 - B - 1, D), F32)], 0)
    mod = _modulation(cvec, w_ada, b_ada)
    modr = mod.reshape(depth, 8, N_MOD, 1, D)

    offs = np.cumsum((0,) + IN_SPLIT)
    blk = [w_in[:, :, offs[i]:offs[i + 1]] for i in range(len(IN_SPLIT))]
    w_daq, w_dak, w_dav, w_cq, w_ckv, w_kpe, w_rq, w_rk, w_rv, w_rg = blk
    w_all = jnp.concatenate([w_daq, _pairswap(w_daq), w_dak, _pairswap(w_dak), w_dav, w_cq, w_ckv,
                             w_kpe, _pairswap(w_kpe), w_rq, _pairswap(w_rq), w_rk, _pairswap(w_rk), w_rv, w_rg], -1)
    wT_all = jnp.swapaxes(w_all, 1, 2).astype(BF16)
    uq = mla_w_uq.reshape(depth, MLA_Q_RANK, MLA_HEADS, MLA_NOPE + MLA_ROPE)
    uq_sw = _pairswap(uq[..., MLA_NOPE:]).reshape(depth, MLA_Q_RANK, MLA_HEADS * MLA_ROPE)
    wuqT = jnp.swapaxes(jnp.concatenate([mla_w_uq, uq_sw], -1), 1, 2).astype(BF16)
    ukv = mla_w_ukv.reshape(depth, MLA_KV_RANK, MLA_HEADS, MLA_NOPE + MLA_VDIM)
    wvT = jnp.swapaxes(ukv[..., MLA_NOPE:].reshape(depth, MLA_KV_RANK, MLA_VW), 1, 2).astype(BF16)
    nopeT = jnp.transpose(ukv[..., :MLA_NOPE], (0, 2, 3, 1))
    top = jnp.concatenate([nopeT, jnp.zeros((depth, MLA_HEADS, MLA_NOPE, MLA_ROPE), F32)], -1)
    mid = jnp.concatenate([jnp.zeros((MLA_ROPE, MLA_KV_RANK), F32), jnp.eye(MLA_ROPE, dtype=F32)], -1)
    mid = jnp.broadcast_to(mid, (depth, MLA_HEADS, MLA_ROPE, MLA_KV_RANK + MLA_ROPE))
    bot = jnp.zeros((depth, MLA_HEADS, LANES - MLA_NOPE - MLA_ROPE, MLA_KV_RANK + MLA_ROPE), F32)
    wkT = jnp.concatenate([top, mid, bot], 2).reshape(depth, MLA_HEADS * LANES, MLA_KV_RANK + MLA_ROPE).astype(BF16)
    gq = mla_q_norm[:, :, None]
    gkv = mla_kv_norm[:, :, None]
    subln = da_subln[:, :, None]
    w_out_b = w_out.astype(BF16)
    w13 = jnp.concatenate([exp_w1, exp_w3], -1).astype(BF16)
    w2 = exp_w2.astype(BF16)
    rw_pad = jnp.concatenate([router_w, jnp.zeros((D, LANES - N_EXPERTS), F32)], -1)
    rb = router_b[:, None]
    dec = jnp.broadcast_to(jnp.stack([ret_decay_f, ret_decay_b], 1)[:, :, :, None, None],
                           (depth, 2, RET_HEADS, 8, LANES))
    tabs = _rope_tables(S, C, DA_DIM) + _rope_tables(S, C, RET_DIM)
    ln1g, ln1b, ln2g, ln2b = (a[:, None, :] for a in (ln1_g, ln1_b, ln2_g, ln2_b))

    xc = jnp.concatenate([x, ctx], 1)
    for l in range(depth):
        (daqr, daqf, dak, dav, mqr, mqf, mk, mv, rq, rk, rv, rg) = _in_proj(
            l, xc, modr, wT_all, tabs, gq, wuqT, gkv, wvT, wkT, S)
        yda = _attention("da", l, daqr, daqf, dak, dav, S, C, lam=da_lambda, g=subln)
        ymla = _attention("mla", l, mqr, mqf, mk, mv, S, C)
        yret = _retention(rq, rk, rv, rg, dec[l], S, C)
        x1, h2, logits = _out_proj(l, yda, ymla, yret, w_out_b, xc, modr, ln1g, ln1b, rw_pad, S, alpha)
        xc = _moe(l, h2, logits, rb, w13, w2, x1, modr, ln2g, ln2b, S, alpha)
    return xc[:, :S, :]
```

```python
import functools
import math

import jax
import jax.numpy as jnp
import numpy as np
from jax import lax
from jax.experimental import pallas as pl
from jax.experimental.pallas import tpu as pltpu

F32 = jnp.float32
BF16 = jnp.bfloat16

GRID_W = 64
DA_HEADS, DA_DIM, DA_VDIM = 6, 32, 64
MLA_HEADS, MLA_NOPE, MLA_ROPE, MLA_VDIM = 6, 64, 32, 64
MLA_Q_RANK, MLA_KV_RANK = 256, 128
RET_HEADS, RET_DIM, RET_CHUNK = 4, 64, 128
ROPE_BASE = 10000.0
N_EXPERTS, N_GROUPS, D_EXPERT = 16, 4, 512
ROUTED_SCALE = 1.0
N_MOD = 6
LN_EPS = 1e-5
RMS_EPS = 1e-6
LOG2E = math.log2(math.e)

DA_W = DA_HEADS * 2 * DA_DIM
DA_VW = DA_HEADS * DA_VDIM
RET_W = RET_HEADS * RET_DIM
MLA_QW = MLA_HEADS * (MLA_NOPE + MLA_ROPE)
MLA_VW = MLA_HEADS * MLA_VDIM
IN_SPLIT = (DA_W, DA_W, DA_VW, MLA_Q_RANK, MLA_KV_RANK, MLA_ROPE, RET_W, RET_W, RET_W, RET_W)

LANES = 128
V_ROWS = 80
TM = 256
ATTN_N = 512
TK = 512
ATTN_NBUF = 2
ATTN_UNROLL = 4
TM_MOE = 640

NT_DIMS = (((1,), (1,)), ((), ()))
TN_DIMS = (((0,), (0,)), ((), ()))


def _sigmoid(x):
    return 1.0 / (1.0 + jnp.exp(-x))


def _mod_kernel(c_ref, w_ref, b_ref, o_ref):
    c = c_ref[...]
    sc = (c * _sigmoid(c)).astype(BF16)
    o_ref[0] = jnp.dot(sc, w_ref[0].astype(BF16), preferred_element_type=F32) + b_ref[0]


def _modulation(cvec, w_ada, b_ada):
    depth, d, n = w_ada.shape
    tn = 1536
    return pl.pallas_call(
        _mod_kernel,
        grid=(depth, n // tn),
        in_specs=[
            pl.BlockSpec((8, d), lambda l, j: (0, 0)),
            pl.BlockSpec((1, d, tn), lambda l, j: (l, 0, j)),
            pl.BlockSpec((1, 1, tn), lambda l, j: (l, 0, j)),
        ],
        out_specs=pl.BlockSpec((1, 8, tn), lambda l, j: (l, 0, j)),
        out_shape=jax.ShapeDtypeStruct((depth, 8, n), F32),
        name="adaln_mod",
    )(cvec, w_ada, b_ada.reshape(depth, 1, n))


_O_DAQ, _O_DAQS, _O_DAK, _O_DAKS, _O_DAV = 0, 384, 768, 1152, 1536
_O_CQ, _O_CKV, _O_KPE, _O_KPES = 1920, 2176, 2304, 2336
_O_RQ, _O_RQS, _O_RK, _O_RKS, _O_RV, _O_RG = 2368, 2624, 2880, 3136, 3392, 3648
N_ZROWS = 3904


def _rms_rows(x):
    return x * lax.rsqrt(jnp.mean(x * x, axis=0, keepdims=True) + RMS_EPS)


def _in_kernel(x_ref, sh_ref, sc_ref, wT_ref, cosA_ref, sinA_ref, cosR_ref, sinR_ref,
               gq_ref, wuqT_ref, gkv_ref, wvT_ref, wkT_ref,
               daqr, daqf, dak, dav, mqr, mqf, mk, mv, rq, rk, rv, rg, z_ref, *, cq_da, cq_mla, kscale):
    tm = x_ref.shape[1]
    x = x_ref[0]
    h = (x * (1.0 + sc_ref[0, 0, 0]) + sh_ref[0, 0, 0]).astype(BF16)
    z_ref[...] = lax.dot_general(wT_ref[0], h, NT_DIMS, preferred_element_type=F32)
    cosA = cosA_ref[...]
    sinA = sinA_ref[...]
    cosR = cosR_ref[...]
    sinR = sinR_ref[...]
    ones_rows = (lax.broadcasted_iota(jnp.int32, (DA_HEADS, V_ROWS - 64, tm), 1) == 0).astype(BF16)

    def rows(off, n):
        return z_ref[off:off + n, :]

    q = rows(_O_DAQ, DA_W).reshape(12, DA_DIM, tm)
    qs = rows(_O_DAQS, DA_W).reshape(12, DA_DIM, tm)
    daqr[0] = ((q * cosA + qs * sinA) * cq_da).astype(BF16)
    daqf[0] = (q * cq_da).astype(BF16)
    k = rows(_O_DAK, DA_W).reshape(12, DA_DIM, tm)
    ks = rows(_O_DAKS, DA_W).reshape(12, DA_DIM, tm)
    kr = (k * cosA + ks * sinA).reshape(DA_HEADS, 2 * DA_DIM, tm)
    kp = jnp.concatenate([kr, jnp.zeros_like(kr)], axis=1).reshape(DA_HEADS * LANES, tm)
    dak[0] = kp.T.astype(BF16)
    dav[0, :, 0:64, :] = rows(_O_DAV, DA_VW).reshape(DA_HEADS, DA_VDIM, tm).astype(BF16)
    dav[0, :, 64:V_ROWS, :] = ones_rows

    cqn = (_rms_rows(rows(_O_CQ, MLA_Q_RANK)) * gq_ref[0]).astype(BF16)
    qm = jnp.dot(wuqT_ref[0], cqn, preferred_element_type=F32)
    qh = qm[0:MLA_QW].reshape(MLA_HEADS, MLA_NOPE + MLA_ROPE, tm)
    qsw = qm[MLA_QW:MLA_QW + MLA_HEADS * MLA_ROPE].reshape(MLA_HEADS, MLA_ROPE, tm)
    rot = qh[:, MLA_NOPE:, :] * cosA + qsw * sinA
    mqr[0, :, 0:MLA_NOPE, :] = (qh[:, 0:MLA_NOPE, :] * cq_mla).astype(BF16)
    mqr[0, :, MLA_NOPE:, :] = (rot * cq_mla).astype(BF16)
    mqf[0] = (qh * cq_mla).astype(BF16)
    ckvn = (_rms_rows(rows(_O_CKV, MLA_KV_RANK)) * gkv_ref[0]).astype(BF16)
    vm = jnp.dot(wvT_ref[0], ckvn, preferred_element_type=F32)
    mv[0, :, 0:64, :] = vm.reshape(MLA_HEADS, MLA_VDIM, tm).astype(BF16)
    mv[0, :, 64:V_ROWS, :] = ones_rows
    kper = rows(_O_KPE, MLA_ROPE) * cosA + rows(_O_KPES, MLA_ROPE) * sinA
    kin = jnp.concatenate([ckvn, kper.astype(BF16)], axis=0)
    kmT = jnp.dot(wkT_ref[0], kin, preferred_element_type=F32)
    mk[0] = kmT.T.astype(BF16)

    r_q = rows(_O_RQ, RET_W).reshape(RET_HEADS, RET_DIM, tm)
    r_qs = rows(_O_RQS, RET_W).reshape(RET_HEADS, RET_DIM, tm)
    rq[0] = (r_q * cosR + r_qs * sinR).astype(BF16)
    r_k = rows(_O_RK, RET_W).reshape(RET_HEADS, RET_DIM, tm)
    r_ks = rows(_O_RKS, RET_W).reshape(RET_HEADS, RET_DIM, tm)
    rkr = (r_k * cosR + r_ks * sinR) * kscale
    rkp = jnp.concatenate([rkr, jnp.zeros_like(rkr)], axis=1).reshape(RET_HEADS * LANES, tm)
    rk[0] = rkp.T.astype(BF16)
    rv[0] = rows(_O_RV, RET_W).reshape(RET_HEADS, RET_DIM, tm).astype(BF16)
    rg[0] = rows(_O_RG, RET_W)


def _in_proj(l, xc, modr, wT_all, tabs, gq, wuqT, gkv, wvT, wkT, S):
    B, Sp, D = xc.shape
    nt = Sp // TM
    n_lat_tiles = S // TM
    cosA, sinA, cosR, sinR = tabs

    def mod_spec(j):
        return pl.BlockSpec((1, 1, 1, 1, D), lambda b, t: (l, jnp.where(t >= n_lat_tiles, B, b), j, 0, 0))

    def wspec(a):
        return pl.BlockSpec((1,) + a.shape[1:], lambda b, t: (l,) + (0,) * (a.ndim - 1))

    def fm4(h, d):
        return pl.BlockSpec((1, h, d, TM), lambda b, t: (b, 0, 0, t))

    def tmaj(w):
        return pl.BlockSpec((1, TM, w), lambda b, t: (b, t, 0))

    kern = functools.partial(
        _in_kernel,
        cq_da=float(DA_DIM ** -0.5 * LOG2E),
        cq_mla=float((MLA_NOPE + MLA_ROPE) ** -0.5 * LOG2E),
        kscale=float(RET_DIM ** -0.5),
    )
    sds = jax.ShapeDtypeStruct
    return pl.pallas_call(
        kern,
        grid=(B, nt),
        in_specs=[
            pl.BlockSpec((1, TM, D), lambda b, t: (b, t, 0)),
            mod_spec(0), mod_spec(1),
            wspec(wT_all),
            pl.BlockSpec((DA_DIM, TM), lambda b, t: (0, t)),
            pl.BlockSpec((DA_DIM, TM), lambda b, t: (0, t)),
            pl.BlockSpec((RET_DIM, TM), lambda b, t: (0, t)),
            pl.BlockSpec((RET_DIM, TM), lambda b, t: (0, t)),
            wspec(gq), wspec(wuqT), wspec(gkv), wspec(wvT), wspec(wkT),
        ],
        out_specs=[
            fm4(12, DA_DIM), fm4(12, DA_DIM), tmaj(DA_HEADS * LANES), fm4(DA_HEADS, V_ROWS),
            fm4(MLA_HEADS, 96), fm4(MLA_HEADS, 96), tmaj(MLA_HEADS * LANES), fm4(MLA_HEADS, V_ROWS),
            fm4(RET_HEADS, RET_DIM), tmaj(RET_HEADS * LANES), fm4(RET_HEADS, RET_DIM),
            pl.BlockSpec((1, RET_W, TM), lambda b, t: (b, 0, t)),
        ],
        out_shape=[
            sds((B, 12, DA_DIM, Sp), BF16), sds((B, 12, DA_DIM, Sp), BF16),
            sds((B, Sp, DA_HEADS * LANES), BF16), sds((B, DA_HEADS, V_ROWS, Sp), BF16),
            sds((B, MLA_HEADS, 96, Sp), BF16), sds((B, MLA_HEADS, 96, Sp), BF16),
            sds((B, Sp, MLA_HEADS * LANES), BF16), sds((B, MLA_HEADS, V_ROWS, Sp), BF16),
            sds((B, RET_HEADS, RET_DIM, Sp), BF16), sds((B, Sp, RET_HEADS * LANES), BF16),
            sds((B, RET_HEADS, RET_DIM, Sp), BF16), sds((B, RET_W, Sp), F32),
        ],
        scratch_shapes=[pltpu.VMEM((N_ZROWS, TM), F32)],
        compiler_params=pltpu.CompilerParams(
            dimension_semantics=("arbitrary", "arbitrary"), vmem_limit_bytes=56 * 1024 * 1024),
        name="in_proj",
    )(xc, modr, modr, wT_all, cosA, sinA, cosR, sinR, gq, wuqT, gkv, wvT, wkT)


def _attn_kernel(*refs, mode, S, C, lam_init, ctx_only):
    refs = list(refs)
    qr_ref, qf_ref, k_ref, v_ref = refs[:4]
    del refs[:4]
    if mode == "da":
        lam_ref, g_ref = refs[:2]
        del refs[:2]
    if ctx_only:
        del refs[:1]
    o_ref, m_sc, acc_sc, q_sc = refs[:4]
    nb = ATTN_NBUF
    sbuf, pbuf, albuf = (refs[4 + i * nb: 4 + (i + 1) * nb] for i in range(3))
    tq = o_ref.shape[2]
    n_lat = S // TK

    def qpad(ref):
        if mode == "da":
            z = jnp.zeros((DA_DIM, tq), BF16)
            c1 = jnp.concatenate([ref[0, 0], z, z, z], axis=0)
            c2 = jnp.concatenate([z, ref[0, 1], z, z], axis=0)
            return jnp.concatenate([c1, c2], axis=1)
        return jnp.concatenate([ref[0, 0], jnp.zeros((LANES - 96, tq), BF16)], axis=0)

    s = jnp.dot(k_ref[0, S:S + C, :], qpad(qf_ref), preferred_element_type=F32)
    m_c = jnp.max(s, axis=0, keepdims=True)
    m_sc[...] = m_c
    acc_sc[...] = jnp.dot(v_ref[0, 0, :, S:S + C], jnp.exp2(s - m_c).astype(BF16), preferred_element_type=F32)

    if not ctx_only:
        q_sc[...] = qpad(qr_ref)

        def chunk(j):
            return pl.ds(j * TK if isinstance(j, int) else pl.multiple_of(j * TK, TK), TK)

        def scores(j, par):
            sbuf[par][...] = jnp.dot(k_ref[0, chunk(j), :], q_sc[...], preferred_element_type=F32)

        def softmax(par):
            m_old = m_sc[...]
            sj = sbuf[par][...]
            m_new = jnp.maximum(m_old, jnp.max(sj, axis=0, keepdims=True))
            pbuf[par][...] = jnp.exp2(sj - m_new).astype(BF16)
            albuf[par][...] = jnp.exp2(m_old - m_new)
            m_sc[...] = m_new

        def accumulate(j, par):
            acc_sc[...] = acc_sc[...] * albuf[par][...] + jnp.dot(
                v_ref[0, 0, :, chunk(j)], pbuf[par][...], preferred_element_type=F32)

        def step(j, r):
            accumulate(j, r)
            softmax((r + 1) % nb)
            scores(j + 2, (r + 2) % nb)

        scores(0, 0)
        softmax(0)
        scores(1, 1)
        n_steps = n_lat - 2
        trips = n_steps // ATTN_UNROLL

        def body(t, carry):
            for u in range(ATTN_UNROLL):
                step(t * ATTN_UNROLL + u, u % nb)
            return carry

        lax.fori_loop(0, trips, body, 0)
        for j in range(trips * ATTN_UNROLL, n_steps):
            step(j, j % nb)
        accumulate(n_lat - 2, (n_lat - 2) % nb)
        softmax((n_lat - 1) % nb)
        accumulate(n_lat - 1, (n_lat - 1) % nb)

    acc = acc_sc[...]
    o = acc[0:64, :] / acc[64:65, :]
    if mode == "da":
        lf = lam_ref[0]
        lam = (jnp.exp(jnp.sum(lf[0:1] * lf[1:2], axis=1, keepdims=True))
               - jnp.exp(jnp.sum(lf[2:3] * lf[3:4], axis=1, keepdims=True)) + lam_init)
        od = o[:, 0:tq] - lam * o[:, tq:2 * tq]
        y = _rms_rows(od) * g_ref[0]
        o_ref[0] = (y * (1.0 - lam_init)).astype(BF16)
    else:
        o_ref[0] = o.astype(BF16)


def _attention_call(mode, l, qr, qf, k, v, S, C, lam, g, y_latent):
    B, nmaps, d, Sp = qr.shape
    H = v.shape[1]
    nm = nmaps // H
    ctx_only = y_latent is not None
    tq = C if ctx_only else ATTN_N // nm
    n = nm * tq
    q0 = S // tq if ctx_only else 0
    lam_init = 0.8 - 0.6 * math.exp(-0.3 * l)
    kern = functools.partial(_attn_kernel, mode=mode, S=S, C=C, lam_init=lam_init, ctx_only=ctx_only)
    in_specs = [
        pl.BlockSpec((1, nm, d, tq), lambda b, h, i: (b, h, 0, q0 + i)),
        pl.BlockSpec((1, nm, d, tq), lambda b, h, i: (b, h, 0, q0 + i)),
        pl.BlockSpec((1, Sp, LANES), lambda b, h, i: (b, 0, h)),
        pl.BlockSpec((1, 1, V_ROWS, Sp), lambda b, h, i: (b, h, 0, 0)),
    ]
    args = [qr, qf, k, v]
    if mode == "da":
        in_specs += [pl.BlockSpec((1,) + lam.shape[1:], lambda b, h, i: (l, 0, 0)),
                     pl.BlockSpec((1,) + g.shape[1:], lambda b, h, i: (l, 0, 0))]
        args += [lam, g]
    aliases = {}
    if ctx_only:
        in_specs.append(pl.BlockSpec(memory_space=pl.ANY))
        args.append(y_latent)
        aliases = {len(args) - 1: 0}
    tk = C if ctx_only else TK
    return pl.pallas_call(
        kern,
        grid=(B, H, 1 if ctx_only else S // tq),
        in_specs=in_specs,
        out_specs=pl.BlockSpec((1, 64, tq), lambda b, h, i: (b, h, q0 + i)),
        out_shape=jax.ShapeDtypeStruct((B, H * 64, Sp), BF16),
        input_output_aliases=aliases,
        scratch_shapes=(
            [pltpu.VMEM((1, n), F32), pltpu.VMEM((V_ROWS, n), F32), pltpu.VMEM((LANES, n), BF16)]
            + [pltpu.VMEM((tk, n), F32)] * ATTN_NBUF
            + [pltpu.VMEM((tk, n), BF16)] * ATTN_NBUF + [pltpu.VMEM((1, n), F32)] * ATTN_NBUF),
        compiler_params=pltpu.CompilerParams(
            dimension_semantics=("arbitrary", "arbitrary", "arbitrary"), vmem_limit_bytes=48 * 1024 * 1024),
        name="attn_" + mode + ("_ctx" if ctx_only else ""),
    )(*args)


def _attention(mode, l, qr, qf, k, v, S, C, lam=None, g=None):
    y = _attention_call(mode, l, qr, qf, k, v, S, C, lam, g, None)
    return _attention_call(mode, l, qr, qf, k, v, S, C, lam, g, y)


def _ret_kernel(q_ref, k_ref, v_ref, g_ref, dec_ref, o_ref, sb_sc, *, S, C):
    CH = RET_CHUNK
    nl, nc = S // CH, C // CH

    def log_sigmoid(x):
        return jnp.minimum(x, 0.0) - jnp.log1p(jnp.exp(-jnp.abs(x)))

    lgf = log_sigmoid(dec_ref[0, 0][0:1, :])
    lgb = log_sigmoid(dec_ref[1, 0][0:1, :])
    ii = lax.broadcasted_iota(jnp.int32, (CH, CH), 1).astype(F32)
    jj = lax.broadcasted_iota(jnp.int32, (CH, CH), 0).astype(F32)
    dd = ii - jj
    fwd = dd >= 0
    mt = (jnp.where(fwd, jnp.exp(lgf * jnp.where(fwd, dd, 0.0)), 0.0)
          + jnp.where(fwd, 0.0, jnp.exp(lgb * jnp.where(fwd, 0.0, -dd))))
    xi_f = jnp.exp(lgf * (ii + 1.0))
    xi_b = jnp.exp(lgb * (CH - ii))
    zeta_f = jnp.exp(lgf * (CH - 1.0 - jj))
    zeta_b = jnp.exp(lgb * jj)
    cdf = jnp.exp(lgf * CH)
    cdb = jnp.exp(lgb * CH)
    zq = jnp.zeros((RET_DIM, CH), BF16)

    def sl(c):
        return pl.ds(pl.multiple_of(c * CH, CH), CH)

    def state_update(c, zeta):
        kz = (k_ref[0, sl(c), :].astype(F32) * zeta).astype(BF16)
        return jnp.dot(v_ref[0, 0, :, sl(c)], kz, preferred_element_type=F32)

    def bwd_step(c, sb):
        sb_sc[c] = sb
        return sb * cdb + state_update(c, zeta_b)

    sb = jnp.zeros((RET_DIM, CH), F32)
    for c in range(nl + nc - 1, nl - 1, -1):
        sb = bwd_step(c, sb)
    lax.fori_loop(0, nl, lambda n, s: bwd_step(nl - 1 - n, s), sb)

    def fwd_step(c, sf):
        qc = q_ref[0, 0, :, sl(c)]
        qp = jnp.concatenate([qc, zq], axis=0)
        kc = k_ref[0, sl(c), :]
        vc = v_ref[0, 0, :, sl(c)]
        at = jnp.dot(kc, qp, preferred_element_type=F32)
        qpf = qp.astype(F32)
        rhs = jnp.concatenate([(at * mt).astype(BF16), (qpf * xi_f).astype(BF16), (qpf * xi_b).astype(BF16)], axis=0)
        lhs = jnp.concatenate([vc, sf.astype(BF16), sb_sc[c].astype(BF16)], axis=1)
        o = jnp.dot(lhs, rhs, preferred_element_type=F32)
        gch = g_ref[0, :, sl(c)]
        o_ref[0, :, sl(c)] = (gch * _sigmoid(gch) * _rms_rows(o)).astype(BF16)
        return sf * cdf + state_update(c, zeta_f)

    sf = jnp.zeros((RET_DIM, CH), F32)
    for c in range(nl, nl + nc):
        sf = fwd_step(c, sf)
    lax.fori_loop(0, nl, fwd_step, sf)


def _retention(rq, rk, rv, rg, dec, S, C):
    B, H, d, Sp = rq.shape
    kern = functools.partial(_ret_kernel, S=S, C=C)
    return pl.pallas_call(
        kern,
        grid=(B, H),
        in_specs=[
            pl.BlockSpec((1, 1, d, Sp), lambda b, h: (b, h, 0, 0)),
            pl.BlockSpec((1, Sp, LANES), lambda b, h: (b, 0, h)),
            pl.BlockSpec((1, 1, d, Sp), lambda b, h: (b, h, 0, 0)),
            pl.BlockSpec((1, d, Sp), lambda b, h: (b, h, 0)),
            pl.BlockSpec((2, 1, 8, LANES), lambda b, h: (0, h, 0, 0)),
        ],
        out_specs=pl.BlockSpec((1, d, Sp), lambda b, h: (b, h, 0)),
        out_shape=jax.ShapeDtypeStruct((B, H * d, Sp), BF16),
        scratch_shapes=[pltpu.VMEM((Sp // RET_CHUNK, RET_DIM, RET_CHUNK), F32)],
        compiler_params=pltpu.CompilerParams(
            dimension_semantics=("arbitrary", "arbitrary"), vmem_limit_bytes=56 * 1024 * 1024),
        name="retention",
    )(rq, rk, rv, rg, dec)


def _layer_norm(t, g, b):
    mu = jnp.mean(t, axis=-1, keepdims=True)
    tc = t - mu
    var = jnp.mean(tc * tc, axis=-1, keepdims=True)
    return tc * lax.rsqrt(var + LN_EPS) * g + b


def _out_kernel(yda_ref, ymla_ref, yret_ref, w_ref, x_ref, gate_ref, g1_ref, b1_ref, sh2_ref, sc2_ref, rw_ref,
                x1_ref, h2_ref, lg_ref, *, alpha):
    proj = (lax.dot_general(yda_ref[0], w_ref[0, 0:DA_VW, :], TN_DIMS, preferred_element_type=F32)
            + lax.dot_general(ymla_ref[0], w_ref[0, DA_VW:DA_VW + MLA_VW, :], TN_DIMS, preferred_element_type=F32)
            + lax.dot_general(yret_ref[0], w_ref[0, DA_VW + MLA_VW:, :], TN_DIMS, preferred_element_type=F32))
    x1 = _layer_norm(alpha * x_ref[0] + gate_ref[0, 0, 0] * proj, g1_ref[0], b1_ref[0])
    x1_ref[0] = x1
    h2 = x1 * (1.0 + sc2_ref[0, 0, 0]) + sh2_ref[0, 0, 0]
    h2_ref[0] = h2.astype(BF16)
    lg_ref[0] = jnp.dot(h2, rw_ref[...], preferred_element_type=F32, precision=lax.Precision.HIGHEST)


def _out_proj(l, yda, ymla, yret, w_out, xc, modr, ln_g, ln_b, rw_pad, S, alpha):
    B, Sp, D = xc.shape
    n_lat_tiles = S // TM

    def mod_spec(j):
        return pl.BlockSpec((1, 1, 1, 1, D), lambda b, t: (l, jnp.where(t >= n_lat_tiles, B, b), j, 0, 0))

    def fm(a):
        return pl.BlockSpec((1, a.shape[1], TM), lambda b, t: (b, 0, t))

    def wspec(a):
        return pl.BlockSpec((1,) + a.shape[1:], lambda b, t: (l,) + (0,) * (a.ndim - 1))

    row = pl.BlockSpec((1, TM, D), lambda b, t: (b, t, 0))
    sds = jax.ShapeDtypeStruct
    return pl.pallas_call(
        functools.partial(_out_kernel, alpha=alpha),
        grid=(B, Sp // TM),
        in_specs=[fm(yda), fm(ymla), fm(yret), wspec(w_out), row, mod_spec(2), wspec(ln_g), wspec(ln_b),
                  mod_spec(3), mod_spec(4), pl.BlockSpec(rw_pad.shape, lambda b, t: (0, 0))],
        out_specs=[row, row, pl.BlockSpec((1, TM, LANES), lambda b, t: (b, t, 0))],
        out_shape=[sds((B, Sp, D), F32), sds((B, Sp, D), BF16), sds((B, Sp, LANES), F32)],
        compiler_params=pltpu.CompilerParams(dimension_semantics=("arbitrary", "arbitrary")),
        name="out_proj",
    )(yda, ymla, yret, w_out, xc, modr, ln_g, ln_b, modr, modr, rw_pad)


def _gates_T(logits_T, rb):
    s = _sigmoid(logits_T)
    ch = s + rb
    srow = [s[e:e + 1] for e in range(N_EXPERTS)]
    crow = [ch[e:e + 1] for e in range(N_EXPERTS)]
    per = N_EXPERTS // N_GROUPS
    gs = []
    for g in range(N_GROUPS):
        a, b, c, d = crow[per * g: per * g + per]
        m1, n1, m2, n2 = jnp.maximum(a, b), jnp.minimum(a, b), jnp.maximum(c, d), jnp.minimum(c, d)
        gs.append(jnp.maximum(m1, m2) + jnp.maximum(jnp.minimum(m1, m2), jnp.maximum(n1, n2)))
    gmax = functools.reduce(jnp.maximum, gs)
    taken = jnp.zeros(gmax.shape, jnp.bool_)
    gsel = []
    for g in range(N_GROUPS):
        sg = jnp.logical_and(gs[g] == gmax, jnp.logical_not(taken))
        gsel.append(sg)
        taken = jnp.logical_or(taken, sg)
    neg = jnp.full(gmax.shape, -jnp.inf, F32)
    mc = [jnp.where(gsel[e // per], crow[e], neg) for e in range(N_EXPERTS)]
    sel = [jnp.zeros(gmax.shape, jnp.bool_) for _ in range(N_EXPERTS)]
    for _ in range(2):
        top = functools.reduce(jnp.maximum, mc)
        taken = jnp.zeros(gmax.shape, jnp.bool_)
        for e in range(N_EXPERTS):
            hit = jnp.logical_and(mc[e] == top, jnp.logical_not(taken))
            taken = jnp.logical_or(taken, hit)
            sel[e] = jnp.logical_or(sel[e], hit)
            mc[e] = jnp.where(hit, neg, mc[e])
    w = [jnp.where(sel[e], srow[e], 0.0) for e in range(N_EXPERTS)]
    wsum = functools.reduce(lambda a, b: a + b, w)
    return jnp.concatenate([we / wsum * ROUTED_SCALE for we in w], axis=0)


def _moe_kernel(h_ref, lg_ref, rb_ref, w13_ref, w2_ref, x1_ref, gb_ref, gc_ref, g2_ref, b2_ref, o_ref,
                gates_sc, acc_sc, *, S, alpha):
    e = pl.program_id(2)
    tm = h_ref.shape[1]

    @pl.when(e == 0)
    def _():
        gT = _gates_T(lg_ref[0].T[0:N_EXPERTS, :], rb_ref[...])
        gates_sc[...] = jnp.concatenate([gT, jnp.zeros((LANES - N_EXPERTS, tm), F32)], axis=0).T
        acc_sc[...] = jnp.zeros(acc_sc.shape, F32)

    ab = jnp.dot(h_ref[0], w13_ref[0, 0], preferred_element_type=F32)
    a = ab[:, 0:D_EXPERT]
    hid = (a * _sigmoid(a) * ab[:, D_EXPERT:]).astype(BF16)
    y = jnp.dot(hid, w2_ref[0, 0], preferred_element_type=F32)
    lane = lax.broadcasted_iota(jnp.int32, (tm, LANES), 1)
    ge = jnp.sum(jnp.where(lane == e, gates_sc[...], 0.0), axis=1, keepdims=True)
    acc_sc[...] += ge * y

    @pl.when(e == N_EXPERTS - 1)
    def _():
        n_lat = S - pl.program_id(1) * tm
        rowi = lax.broadcasted_iota(jnp.int32, (tm, 1), 0)
        gate = jnp.where(rowi < n_lat, gb_ref[0, 0, 0], gc_ref[0, 0, 0])
        o_ref[0] = _layer_norm(alpha * x1_ref[0] + gate * acc_sc[...], g2_ref[0], b2_ref[0])


def _moe(l, h2, logits, rb, w13, w2, x1, modr, ln_g, ln_b, S, alpha):
    B, Sp, D = x1.shape
    tm = TM_MOE

    def wspec(a):
        return pl.BlockSpec((1,) + a.shape[1:], lambda b, t, e: (l,) + (0,) * (a.ndim - 1))

    row = lambda b, t, e: (b, t, 0)
    return pl.pallas_call(
        functools.partial(_moe_kernel, S=S, alpha=alpha),
        grid=(B, Sp // tm, N_EXPERTS),
        in_specs=[
            pl.BlockSpec((1, tm, D), row),
            pl.BlockSpec((1, tm, LANES), row),
            pl.BlockSpec(rb.shape, lambda b, t, e: (0, 0)),
            pl.BlockSpec((1, 1, D, 2 * D_EXPERT), lambda b, t, e: (l, e, 0, 0)),
            pl.BlockSpec((1, 1, D_EXPERT, D), lambda b, t, e: (l, e, 0, 0)),
            pl.BlockSpec((1, tm, D), row),
            pl.BlockSpec((1, 1, 1, 1, D), lambda b, t, e: (l, b, 5, 0, 0)),
            pl.BlockSpec((1, 1, 1, 1, D), lambda b, t, e: (l, B, 5, 0, 0)),
            wspec(ln_g), wspec(ln_b),
        ],
        out_specs=pl.BlockSpec((1, tm, D), row),
        out_shape=jax.ShapeDtypeStruct((B, Sp, D), F32),
        scratch_shapes=[pltpu.VMEM((tm, LANES), F32), pltpu.VMEM((tm, D), F32)],
        compiler_params=pltpu.CompilerParams(
            dimension_semantics=("arbitrary", "arbitrary", "arbitrary"), vmem_limit_bytes=56 * 1024 * 1024),
        name="moe",
    )(h2, logits, rb, w13, w2, x1, modr, modr, ln_g, ln_b)


def _pairswap(w):
    n = w.shape[-1]
    return w.reshape(w.shape[:-1] + (n // 2, 2))[..., ::-1].reshape(w.shape)


def _rope_tables(S, C, rot_dim):
    rows = S // GRID_W
    r = jnp.repeat(jnp.arange(rows, dtype=F32), GRID_W)
    col = jnp.tile(jnp.arange(GRID_W, dtype=F32), rows)
    n_freq = rot_dim // 4
    freqs = ROPE_BASE ** (-jnp.arange(n_freq, dtype=F32) / n_freq)
    ang = jnp.concatenate([r[:, None] * freqs, col[:, None] * freqs], -1)
    cos, sin = jnp.cos(ang), jnp.sin(ang)
    cos_rep = jnp.repeat(cos, 2, axis=-1)
    sin_alt = jnp.stack([-sin, sin], -1).reshape(S, rot_dim)
    cos_rep = jnp.concatenate([cos_rep, jnp.ones((C, rot_dim), F32)], 0)
    sin_alt = jnp.concatenate([sin_alt, jnp.zeros((C, rot_dim), F32)], 0)
    return cos_rep.T, sin_alt.T


def kernel(x, c, ctx, c_ctx, w_ada, b_ada, w_in, da_lambda, da_subln, mla_q_norm, mla_w_uq, mla_kv_norm, mla_w_ukv, ret_decay_f, ret_decay_b, w_out, ln1_g, ln1_b, router_w, router_b, exp_w1, exp_w3, exp_w2, ln2_g, ln2_b):
    B, S, D = x.shape
    C = ctx.shape[1]
    depth = w_in.shape[0]
    alpha = float((2 * depth) ** 0.25)

    cvec = jnp.concatenate([c, c_ctx[None, :], jnp.zeros((8 - B - 1, D), F32)], 0)
    mod = _modulation(cvec, w_ada, b_ada)
    modr = mod.reshape(depth, 8, N_MOD, 1, D)

    offs = np.cumsum((0,) + IN_SPLIT)
    blk = [w_in[:, :, offs[i]:offs[i + 1]] for i in range(len(IN_SPLIT))]
    w_daq, w_dak, w_dav, w_cq, w_ckv, w_kpe, w_rq, w_rk, w_rv, w_rg = blk
    w_all = jnp.concatenate([w_daq, _pairswap(w_daq), w_dak, _pairswap(w_dak), w_dav, w_cq, w_ckv,
                             w_kpe, _pairswap(w_kpe), w_rq, _pairswap(w_rq), w_rk, _pairswap(w_rk), w_rv, w_rg], -1)
    wT_all = jnp.swapaxes(w_all, 1, 2).astype(BF16)
    uq = mla_w_uq.reshape(depth, MLA_Q_RANK, MLA_HEADS, MLA_NOPE + MLA_ROPE)
    uq_sw = _pairswap(uq[..., MLA_NOPE:]).reshape(depth, MLA_Q_RANK, MLA_HEADS * MLA_ROPE)
    wuqT = jnp.swapaxes(jnp.concatenate([mla_w_uq, uq_sw], -1), 1, 2).astype(BF16)
    ukv = mla_w_ukv.reshape(depth, MLA_KV_RANK, MLA_HEADS, MLA_NOPE + MLA_VDIM)
    wvT = jnp.swapaxes(ukv[..., MLA_NOPE:].reshape(depth, MLA_KV_RANK, MLA_VW), 1, 2).astype(BF16)
    nopeT = jnp.transpose(ukv[..., :MLA_NOPE], (0, 2, 3, 1))
    top = jnp.concatenate([nopeT, jnp.zeros((depth, MLA_HEADS, MLA_NOPE, MLA_ROPE), F32)], -1)
    mid = jnp.concatenate([jnp.zeros((MLA_ROPE, MLA_KV_RANK), F32), jnp.eye(MLA_ROPE, dtype=F32)], -1)
    mid = jnp.broadcast_to(mid, (depth, MLA_HEADS, MLA_ROPE, MLA_KV_RANK + MLA_ROPE))
    bot = jnp.zeros((depth, MLA_HEADS, LANES - MLA_NOPE - MLA_ROPE, MLA_KV_RANK + MLA_ROPE), F32)
    wkT = jnp.concatenate([top, mid, bot], 2).reshape(depth, MLA_HEADS * LANES, MLA_KV_RANK + MLA_ROPE).astype(BF16)
    gq = mla_q_norm[:, :, None]
    gkv = mla_kv_norm[:, :, None]
    subln = da_subln[:, :, None]
    w_out_b = w_out.astype(BF16)
    w13 = jnp.concatenate([exp_w1, exp_w3], -1).astype(BF16)
    w2 = exp_w2.astype(BF16)
    rw_pad = jnp.concatenate([router_w, jnp.zeros((D, LANES - N_EXPERTS), F32)], -1)
    rb = router_b[:, None]
    dec = jnp.broadcast_to(jnp.stack([ret_decay_f, ret_decay_b], 1)[:, :, :, None, None],
                           (depth, 2, RET_HEADS, 8, LANES))
    tabs = _rope_tables(S, C, DA_DIM) + _rope_tables(S, C, RET_DIM)
    ln1g, ln1b, ln2g, ln2b = (a[:, None, :] for a in (ln1_g, ln1_b, ln2_g, ln2_b))

    xc = jnp.concatenate([x, ctx], 1)
    for l in range(depth):
        (daqr, daqf, dak, dav, mqr, mqf, mk, mv, rq, rk, rv, rg) = _in_proj(
            l, xc, modr, wT_all, tabs, gq, wuqT, gkv, wvT, wkT, S)
        yda = _attention("da", l, daqr, daqf, dak, dav, S, C, lam=da_lambda, g=subln)
        ymla = _attention("mla", l, mqr, mqf, mk, mv, S, C)
        yret = _retention(rq, rk, rv, rg, dec[l], S, C)
        x1, h2, logits = _out_proj(l, yda, ymla, yret, w_out_b, xc, modr, ln1g, ln1b, rw_pad, S, alpha)
        xc = _moe(l, h2, logits, rb, w13, w2, x1, modr, ln2g, ln2b, S, alpha)
    return xc[:, :S, :]
```

```python
import functools
import math

import jax
import jax.numpy as jnp
import numpy as np
from jax import lax
from jax.experimental import pallas as pl
from jax.experimental.pallas import tpu as pltpu

F32 = jnp.float32
BF16 = jnp.bfloat16

GRID_W = 64
DA_HEADS, DA_DIM, DA_VDIM = 6, 32, 64
MLA_HEADS, MLA_NOPE, MLA_ROPE, MLA_VDIM = 6, 64, 32, 64
MLA_Q_RANK, MLA_KV_RANK = 256, 128
RET_HEADS, RET_DIM, RET_CHUNK = 4, 64, 128
ROPE_BASE = 10000.0
N_EXPERTS, N_GROUPS, D_EXPERT = 16, 4, 512
ROUTED_SCALE = 1.0
N_MOD = 6
LN_EPS = 1e-5
RMS_EPS = 1e-6
LOG2E = math.log2(math.e)

DA_W = DA_HEADS * 2 * DA_DIM
DA_VW = DA_HEADS * DA_VDIM
RET_W = RET_HEADS * RET_DIM
MLA_QW = MLA_HEADS * (MLA_NOPE + MLA_ROPE)
MLA_VW = MLA_HEADS * MLA_VDIM
IN_SPLIT = (DA_W, DA_W, DA_VW, MLA_Q_RANK, MLA_KV_RANK, MLA_ROPE, RET_W, RET_W, RET_W, RET_W)

LANES = 128
V_ROWS = 80
TM = 256
ATTN_N = 512
TK = 2048
ATTN_NBUF = 2
ATTN_MAX_JUMP = 32.0
ATTN_UNROLL = 2
TM_MOE = 640

NT_DIMS = (((1,), (1,)), ((), ()))
TN_DIMS = (((0,), (0,)), ((), ()))


def _sigmoid(x):
    return 1.0 / (1.0 + jnp.exp(-x))


def _mod_kernel(c_ref, w_ref, b_ref, o_ref):
    c = c_ref[...]
    sc = (c * _sigmoid(c)).astype(BF16)
    o_ref[0] = jnp.dot(sc, w_ref[0].astype(BF16), preferred_element_type=F32) + b_ref[0]


def _modulation(cvec, w_ada, b_ada):
    depth, d, n = w_ada.shape
    tn = 1536
    return pl.pallas_call(
        _mod_kernel,
        grid=(depth, n // tn),
        in_specs=[
            pl.BlockSpec((8, d), lambda l, j: (0, 0)),
            pl.BlockSpec((1, d, tn), lambda l, j: (l, 0, j)),
            pl.BlockSpec((1, 1, tn), lambda l, j: (l, 0, j)),
        ],
        out_specs=pl.BlockSpec((1, 8, tn), lambda l, j: (l, 0, j)),
        out_shape=jax.ShapeDtypeStruct((depth, 8, n), F32),
        name="adaln_mod",
    )(cvec, w_ada, b_ada.reshape(depth, 1, n))


_O_DAQ, _O_DAQS, _O_DAK, _O_DAKS, _O_DAV = 0, 384, 768, 1152, 1536
_O_CQ, _O_CKV, _O_KPE, _O_KPES = 1920, 2176, 2304, 2336
_O_RQ, _O_RQS, _O_RK, _O_RKS, _O_RV, _O_RG = 2368, 2624, 2880, 3136, 3392, 3648
N_ZROWS = 3904


def _rms_rows(x):
    return x * lax.rsqrt(jnp.mean(x * x, axis=0, keepdims=True) + RMS_EPS)


def _in_kernel(x_ref, sh_ref, sc_ref, wT_ref, cosA_ref, sinA_ref, cosR_ref, sinR_ref,
               gq_ref, wuqT_ref, gkv_ref, wvT_ref, wkT_ref,
               daqr, daqf, dak, dav, mqr, mqf, mk, mv, rq, rk, rv, rg, z_ref, *, cq_da, cq_mla, kscale):
    tm = x_ref.shape[1]
    x = x_ref[0]
    h = (x * (1.0 + sc_ref[0, 0, 0]) + sh_ref[0, 0, 0]).astype(BF16)
    z_ref[...] = lax.dot_general(wT_ref[0], h, NT_DIMS, preferred_element_type=F32)
    cosA = cosA_ref[...]
    sinA = sinA_ref[...]
    cosR = cosR_ref[...]
    sinR = sinR_ref[...]
    ones_rows = (lax.broadcasted_iota(jnp.int32, (DA_HEADS, V_ROWS - 64, tm), 1) == 0).astype(BF16)

    def rows(off, n):
        return z_ref[off:off + n, :]

    q = rows(_O_DAQ, DA_W).reshape(12, DA_DIM, tm)
    qs = rows(_O_DAQS, DA_W).reshape(12, DA_DIM, tm)
    daqr[0] = ((q * cosA + qs * sinA) * cq_da).astype(BF16)
    daqf[0] = (q * cq_da).astype(BF16)
    k = rows(_O_DAK, DA_W).reshape(12, DA_DIM, tm)
    ks = rows(_O_DAKS, DA_W).reshape(12, DA_DIM, tm)
    kr = (k * cosA + ks * sinA).reshape(DA_HEADS, 2 * DA_DIM, tm)
    kp = jnp.concatenate([kr, jnp.zeros_like(kr)], axis=1).reshape(DA_HEADS * LANES, tm)
    dak[0] = kp.T.astype(BF16)
    dav[0, :, 0:64, :] = rows(_O_DAV, DA_VW).reshape(DA_HEADS, DA_VDIM, tm).astype(BF16)
    dav[0, :, 64:V_ROWS, :] = ones_rows

    cqn = (_rms_rows(rows(_O_CQ, MLA_Q_RANK)) * gq_ref[0]).astype(BF16)
    qm = jnp.dot(wuqT_ref[0], cqn, preferred_element_type=F32)
    qh = qm[0:MLA_QW].reshape(MLA_HEADS, MLA_NOPE + MLA_ROPE, tm)
    qsw = qm[MLA_QW:MLA_QW + MLA_HEADS * MLA_ROPE].reshape(MLA_HEADS, MLA_ROPE, tm)
    rot = qh[:, MLA_NOPE:, :] * cosA + qsw * sinA
    mqr[0, :, 0:MLA_NOPE, :] = (qh[:, 0:MLA_NOPE, :] * cq_mla).astype(BF16)
    mqr[0, :, MLA_NOPE:, :] = (rot * cq_mla).astype(BF16)
    mqf[0] = (qh * cq_mla).astype(BF16)
    ckvn = (_rms_rows(rows(_O_CKV, MLA_KV_RANK)) * gkv_ref[0]).astype(BF16)
    vm = jnp.dot(wvT_ref[0], ckvn, preferred_element_type=F32)
    mv[0, :, 0:64, :] = vm.reshape(MLA_HEADS, MLA_VDIM, tm).astype(BF16)
    mv[0, :, 64:V_ROWS, :] = ones_rows
    kper = rows(_O_KPE, MLA_ROPE) * cosA + rows(_O_KPES, MLA_ROPE) * sinA
    kin = jnp.concatenate([ckvn, kper.astype(BF16)], axis=0)
    kmT = jnp.dot(wkT_ref[0], kin, preferred_element_type=F32)
    mk[0] = kmT.T.astype(BF16)

    r_q = rows(_O_RQ, RET_W).reshape(RET_HEADS, RET_DIM, tm)
    r_qs = rows(_O_RQS, RET_W).reshape(RET_HEADS, RET_DIM, tm)
    rq[0] = (r_q * cosR + r_qs * sinR).astype(BF16)
    r_k = rows(_O_RK, RET_W).reshape(RET_HEADS, RET_DIM, tm)
    r_ks = rows(_O_RKS, RET_W).reshape(RET_HEADS, RET_DIM, tm)
    rkr = (r_k * cosR + r_ks * sinR) * kscale
    rkp = jnp.concatenate([rkr, jnp.zeros_like(rkr)], axis=1).reshape(RET_HEADS * LANES, tm)
    rk[0] = rkp.T.astype(BF16)
    rv[0] = rows(_O_RV, RET_W).reshape(RET_HEADS, RET_DIM, tm).astype(BF16)
    rg[0] = rows(_O_RG, RET_W)


def _in_proj(l, xc, modr, wT_all, tabs, gq, wuqT, gkv, wvT, wkT, S):
    B, Sp, D = xc.shape
    nt = Sp // TM
    n_lat_tiles = S // TM
    cosA, sinA, cosR, sinR = tabs

    def mod_spec(j):
        return pl.BlockSpec((1, 1, 1, 1, D), lambda b, t: (l, jnp.where(t >= n_lat_tiles, B, b), j, 0, 0))

    def wspec(a):
        return pl.BlockSpec((1,) + a.shape[1:], lambda b, t: (l,) + (0,) * (a.ndim - 1))

    def fm4(h, d):
        return pl.BlockSpec((1, h, d, TM), lambda b, t: (b, 0, 0, t))

    def tmaj(w):
        return pl.BlockSpec((1, TM, w), lambda b, t: (b, t, 0))

    kern = functools.partial(
        _in_kernel,
        cq_da=float(DA_DIM ** -0.5 * LOG2E),
        cq_mla=float((MLA_NOPE + MLA_ROPE) ** -0.5 * LOG2E),
        kscale=float(RET_DIM ** -0.5),
    )
    sds = jax.ShapeDtypeStruct
    return pl.pallas_call(
        kern,
        grid=(B, nt),
        in_specs=[
            pl.BlockSpec((1, TM, D), lambda b, t: (b, t, 0)),
            mod_spec(0), mod_spec(1),
            wspec(wT_all),
            pl.BlockSpec((DA_DIM, TM), lambda b, t: (0, t)),
            pl.BlockSpec((DA_DIM, TM), lambda b, t: (0, t)),
            pl.BlockSpec((RET_DIM, TM), lambda b, t: (0, t)),
            pl.BlockSpec((RET_DIM, TM), lambda b, t: (0, t)),
            wspec(gq), wspec(wuqT), wspec(gkv), wspec(wvT), wspec(wkT),
        ],
        out_specs=[
            fm4(12, DA_DIM), fm4(12, DA_DIM), tmaj(DA_HEADS * LANES), fm4(DA_HEADS, V_ROWS),
            fm4(MLA_HEADS, 96), fm4(MLA_HEADS, 96), tmaj(MLA_HEADS * LANES), fm4(MLA_HEADS, V_ROWS),
            fm4(RET_HEADS, RET_DIM), tmaj(RET_HEADS * LANES), fm4(RET_HEADS, RET_DIM),
            pl.BlockSpec((1, RET_W, TM), lambda b, t: (b, 0, t)),
        ],
        out_shape=[
            sds((B, 12, DA_DIM, Sp), BF16), sds((B, 12, DA_DIM, Sp), BF16),
            sds((B, Sp, DA_HEADS * LANES), BF16), sds((B, DA_HEADS, V_ROWS, Sp), BF16),
            sds((B, MLA_HEADS, 96, Sp), BF16), sds((B, MLA_HEADS, 96, Sp), BF16),
            sds((B, Sp, MLA_HEADS * LANES), BF16), sds((B, MLA_HEADS, V_ROWS, Sp), BF16),
            sds((B, RET_HEADS, RET_DIM, Sp), BF16), sds((B, Sp, RET_HEADS * LANES), BF16),
            sds((B, RET_HEADS, RET_DIM, Sp), BF16), sds((B, RET_W, Sp), F32),
        ],
        scratch_shapes=[pltpu.VMEM((N_ZROWS, TM), F32)],
        compiler_params=pltpu.CompilerParams(
            dimension_semantics=("arbitrary", "arbitrary"), vmem_limit_bytes=56 * 1024 * 1024),
        name="in_proj",
    )(xc, modr, modr, wT_all, cosA, sinA, cosR, sinR, gq, wuqT, gkv, wvT, wkT)


def _attn_kernel(*refs, mode, S, C, lam_init, ctx_only):
    refs = list(refs)
    qr_ref, qf_ref, k_ref, v_ref = refs[:4]
    del refs[:4]
    if mode == "da":
        lam_ref, g_ref = refs[:2]
        del refs[:2]
    if ctx_only:
        del refs[:1]
    o_ref, m_sc, acc_sc, q_sc, jump_sc = refs[:5]
    nb = ATTN_NBUF
    pbuf, albuf = (refs[5 + i * nb: 5 + (i + 1) * nb] for i in range(2))
    tq = o_ref.shape[2]
    n_lat = S // TK

    def qpad(ref):
        if mode == "da":
            z = jnp.zeros((DA_DIM, tq), BF16)
            c1 = jnp.concatenate([ref[0, 0], z, z, z], axis=0)
            c2 = jnp.concatenate([z, ref[0, 1], z, z], axis=0)
            return jnp.concatenate([c1, c2], axis=1)
        return jnp.concatenate([ref[0, 0], jnp.zeros((LANES - 96, tq), BF16)], axis=0)

    def chunk(j):
        return pl.ds(j * TK if isinstance(j, int) else pl.multiple_of(j * TK, TK), TK)

    def context_keys():
        s = jnp.dot(k_ref[0, S:S + C, :], qpad(qf_ref), preferred_element_type=F32)
        m_c = jnp.max(s, axis=0, keepdims=True)
        m_sc[...] = m_c
        acc_sc[...] = jnp.dot(v_ref[0, 0, :, S:S + C], jnp.exp2(s - m_c).astype(BF16), preferred_element_type=F32)

    context_keys()

    if not ctx_only:
        q_sc[...] = qpad(qr_ref)

        jump_sc[...] = jnp.full(jump_sc.shape, -jnp.inf, F32)

        def probs(j, r):
            s = jnp.dot(k_ref[0, chunk(j), :], q_sc[...], preferred_element_type=F32)
            m_prev = m_sc[...]
            pbuf[r][...] = jnp.exp2(s - m_prev).astype(BF16)
            m_chunk = jnp.max(s, axis=0, keepdims=True)
            m_new = jnp.maximum(m_prev, m_chunk)
            jump_sc[...] = jnp.maximum(jump_sc[...], m_chunk - m_prev)
            albuf[r][...] = jnp.exp2(m_prev - m_new)
            m_sc[...] = m_new

        def accumulate(j, r):
            pv = jnp.dot(v_ref[0, 0, :, chunk(j)], pbuf[r][...], preferred_element_type=F32)
            acc_sc[...] = (acc_sc[...] + pv) * albuf[r][...]

        def step(j, r):
            accumulate(j, r)
            probs(j + 1, (r + 1) % nb)

        probs(0, 0)
        n_steps = n_lat - 1
        trips = n_steps // ATTN_UNROLL

        def body(t, carry):
            for u in range(ATTN_UNROLL):
                step(t * ATTN_UNROLL + u, u % nb)
            return carry

        lax.fori_loop(0, trips, body, 0)
        for j in range(trips * ATTN_UNROLL, n_steps):
            step(j, j % nb)
        accumulate(n_lat - 1, (n_lat - 1) % nb)

        @pl.when(jnp.max(jump_sc[...]) > ATTN_MAX_JUMP)
        def _():
            context_keys()

            def exact_step(j, carry):
                s = jnp.dot(k_ref[0, chunk(j), :], q_sc[...], preferred_element_type=F32)
                m_old = m_sc[...]
                m_new = jnp.maximum(m_old, jnp.max(s, axis=0, keepdims=True))
                pv = jnp.dot(v_ref[0, 0, :, chunk(j)], jnp.exp2(s - m_new).astype(BF16), preferred_element_type=F32)
                acc_sc[...] = acc_sc[...] * jnp.exp2(m_old - m_new) + pv
                m_sc[...] = m_new
                return carry

            lax.fori_loop(0, n_lat, exact_step, 0)

    acc = acc_sc[...]
    o = acc[0:64, :] / acc[64:65, :]
    if mode == "da":
        lf = lam_ref[0]
        lam = (jnp.exp(jnp.sum(lf[0:1] * lf[1:2], axis=1, keepdims=True))
               - jnp.exp(jnp.sum(lf[2:3] * lf[3:4], axis=1, keepdims=True)) + lam_init)
        od = o[:, 0:tq] - lam * o[:, tq:2 * tq]
        y = _rms_rows(od) * g_ref[0]
        o_ref[0] = (y * (1.0 - lam_init)).astype(BF16)
    else:
        o_ref[0] = o.astype(BF16)


def _attention_call(mode, l, qr, qf, k, v, S, C, lam, g, y_latent):
    B, nmaps, d, Sp = qr.shape
    H = v.shape[1]
    nm = nmaps // H
    ctx_only = y_latent is not None
    tq = C if ctx_only else ATTN_N // nm
    n = nm * tq
    q0 = S // tq if ctx_only else 0
    lam_init = 0.8 - 0.6 * math.exp(-0.3 * l)
    kern = functools.partial(_attn_kernel, mode=mode, S=S, C=C, lam_init=lam_init, ctx_only=ctx_only)
    in_specs = [
        pl.BlockSpec((1, nm, d, tq), lambda b, h, i: (b, h, 0, q0 + i)),
        pl.BlockSpec((1, nm, d, tq), lambda b, h, i: (b, h, 0, q0 + i)),
        pl.BlockSpec((1, Sp, LANES), lambda b, h, i: (b, 0, h)),
        pl.BlockSpec((1, 1, V_ROWS, Sp), lambda b, h, i: (b, h, 0, 0)),
    ]
    args = [qr, qf, k, v]
    if mode == "da":
        in_specs += [pl.BlockSpec((1,) + lam.shape[1:], lambda b, h, i: (l, 0, 0)),
                     pl.BlockSpec((1,) + g.shape[1:], lambda b, h, i: (l, 0, 0))]
        args += [lam, g]
    aliases = {}
    if ctx_only:
        in_specs.append(pl.BlockSpec(memory_space=pl.ANY))
        args.append(y_latent)
        aliases = {len(args) - 1: 0}
    tk = C if ctx_only else TK
    return pl.pallas_call(
        kern,
        grid=(B, H, 1 if ctx_only else S // tq),
        in_specs=in_specs,
        out_specs=pl.BlockSpec((1, 64, tq), lambda b, h, i: (b, h, q0 + i)),
        out_shape=jax.ShapeDtypeStruct((B, H * 64, Sp), BF16),
        input_output_aliases=aliases,
        scratch_shapes=(
            [pltpu.VMEM((1, n), F32), pltpu.VMEM((V_ROWS, n), F32), pltpu.VMEM((LANES, n), BF16),
             pltpu.VMEM((1, n), F32)]
            + [pltpu.VMEM((tk, n), BF16)] * ATTN_NBUF + [pltpu.VMEM((1, n), F32)] * ATTN_NBUF),
        compiler_params=pltpu.CompilerParams(
            dimension_semantics=("arbitrary", "arbitrary", "arbitrary"), vmem_limit_bytes=48 * 1024 * 1024),
        name="attn_" + mode + ("_ctx" if ctx_only else ""),
    )(*args)


def _attention(mode, l, qr, qf, k, v, S, C, lam=None, g=None):
    y = _attention_call(mode, l, qr, qf, k, v, S, C, lam, g, None)
    return _attention_call(mode, l, qr, qf, k, v, S, C, lam, g, y)


def _ret_kernel(q_ref, k_ref, v_ref, g_ref, dec_ref, o_ref, sb_sc, *, S, C):
    CH = RET_CHUNK
    nl, nc = S // CH, C // CH

    def log_sigmoid(x):
        return jnp.minimum(x, 0.0) - jnp.log1p(jnp.exp(-jnp.abs(x)))

    lgf = log_sigmoid(dec_ref[0, 0][0:1, :])
    lgb = log_sigmoid(dec_ref[1, 0][0:1, :])
    ii = lax.broadcasted_iota(jnp.int32, (CH, CH), 1).astype(F32)
    jj = lax.broadcasted_iota(jnp.int32, (CH, CH), 0).astype(F32)
    dd = ii - jj
    fwd = dd >= 0
    mt = (jnp.where(fwd, jnp.exp(lgf * jnp.where(fwd, dd, 0.0)), 0.0)
          + jnp.where(fwd, 0.0, jnp.exp(lgb * jnp.where(fwd, 0.0, -dd))))
    xi_f = jnp.exp(lgf * (ii + 1.0))
    xi_b = jnp.exp(lgb * (CH - ii))
    zeta_f = jnp.exp(lgf * (CH - 1.0 - jj))
    zeta_b = jnp.exp(lgb * jj)
    cdf = jnp.exp(lgf * CH)
    cdb = jnp.exp(lgb * CH)
    zq = jnp.zeros((RET_DIM, CH), BF16)

    def sl(c):
        return pl.ds(pl.multiple_of(c * CH, CH), CH)

    def state_update(c, zeta):
        kz = (k_ref[0, sl(c), :].astype(F32) * zeta).astype(BF16)
        return jnp.dot(v_ref[0, 0, :, sl(c)], kz, preferred_element_type=F32)

    def bwd_step(c, sb):
        sb_sc[c] = sb
        return sb * cdb + state_update(c, zeta_b)

    sb = jnp.zeros((RET_DIM, CH), F32)
    for c in range(nl + nc - 1, nl - 1, -1):
        sb = bwd_step(c, sb)
    lax.fori_loop(0, nl, lambda n, s: bwd_step(nl - 1 - n, s), sb)

    def fwd_step(c, sf):
        qc = q_ref[0, 0, :, sl(c)]
        qp = jnp.concatenate([qc, zq], axis=0)
        kc = k_ref[0, sl(c), :]
        vc = v_ref[0, 0, :, sl(c)]
        at = jnp.dot(kc, qp, preferred_element_type=F32)
        qpf = qp.astype(F32)
        rhs = jnp.concatenate([(at * mt).astype(BF16), (qpf * xi_f).astype(BF16), (qpf * xi_b).astype(BF16)], axis=0)
        lhs = jnp.concatenate([vc, sf.astype(BF16), sb_sc[c].astype(BF16)], axis=1)
        o = jnp.dot(lhs, rhs, preferred_element_type=F32)
        gch = g_ref[0, :, sl(c)]
        o_ref[0, :, sl(c)] = (gch * _sigmoid(gch) * _rms_rows(o)).astype(BF16)
        return sf * cdf + state_update(c, zeta_f)

    sf = jnp.zeros((RET_DIM, CH), F32)
    for c in range(nl, nl + nc):
        sf = fwd_step(c, sf)
    lax.fori_loop(0, nl, fwd_step, sf)


def _retention(rq, rk, rv, rg, dec, S, C):
    B, H, d, Sp = rq.shape
    kern = functools.partial(_ret_kernel, S=S, C=C)
    return pl.pallas_call(
        kern,
        grid=(B, H),
        in_specs=[
            pl.BlockSpec((1, 1, d, Sp), lambda b, h: (b, h, 0, 0)),
            pl.BlockSpec((1, Sp, LANES), lambda b, h: (b, 0, h)),
            pl.BlockSpec((1, 1, d, Sp), lambda b, h: (b, h, 0, 0)),
            pl.BlockSpec((1, d, Sp), lambda b, h: (b, h, 0)),
            pl.BlockSpec((2, 1, 8, LANES), lambda b, h: (0, h, 0, 0)),
        ],
        out_specs=pl.BlockSpec((1, d, Sp), lambda b, h: (b, h, 0)),
        out_shape=jax.ShapeDtypeStruct((B, H * d, Sp), BF16),
        scratch_shapes=[pltpu.VMEM((Sp // RET_CHUNK, RET_DIM, RET_CHUNK), F32)],
        compiler_params=pltpu.CompilerParams(
            dimension_semantics=("arbitrary", "arbitrary"), vmem_limit_bytes=56 * 1024 * 1024),
        name="retention",
    )(rq, rk, rv, rg, dec)


def _layer_norm(t, g, b):
    mu = jnp.mean(t, axis=-1, keepdims=True)
    tc = t - mu
    var = jnp.mean(tc * tc, axis=-1, keepdims=True)
    return tc * lax.rsqrt(var + LN_EPS) * g + b


def _out_kernel(yda_ref, ymla_ref, yret_ref, w_ref, x_ref, gate_ref, g1_ref, b1_ref, sh2_ref, sc2_ref, rw_ref,
                x1_ref, h2_ref, lg_ref, *, alpha):
    proj = (lax.dot_general(yda_ref[0], w_ref[0, 0:DA_VW, :], TN_DIMS, preferred_element_type=F32)
            + lax.dot_general(ymla_ref[0], w_ref[0, DA_VW:DA_VW + MLA_VW, :], TN_DIMS, preferred_element_type=F32)
            + lax.dot_general(yret_ref[0], w_ref[0, DA_VW + MLA_VW:, :], TN_DIMS, preferred_element_type=F32))
    x1 = _layer_norm(alpha * x_ref[0] + gate_ref[0, 0, 0] * proj, g1_ref[0], b1_ref[0])
    x1_ref[0] = x1
    h2 = x1 * (1.0 + sc2_ref[0, 0, 0]) + sh2_ref[0, 0, 0]
    h2_ref[0] = h2.astype(BF16)
    lg_ref[0] = jnp.dot(h2, rw_ref[...], preferred_element_type=F32, precision=lax.Precision.HIGHEST)


def _out_proj(l, yda, ymla, yret, w_out, xc, modr, ln_g, ln_b, rw_pad, S, alpha):
    B, Sp, D = xc.shape
    n_lat_tiles = S // TM

    def mod_spec(j):
        return pl.BlockSpec((1, 1, 1, 1, D), lambda b, t: (l, jnp.where(t >= n_lat_tiles, B, b), j, 0, 0))

    def fm(a):
        return pl.BlockSpec((1, a.shape[1], TM), lambda b, t: (b, 0, t))

    def wspec(a):
        return pl.BlockSpec((1,) + a.shape[1:], lambda b, t: (l,) + (0,) * (a.ndim - 1))

    row = pl.BlockSpec((1, TM, D), lambda b, t: (b, t, 0))
    sds = jax.ShapeDtypeStruct
    return pl.pallas_call(
        functools.partial(_out_kernel, alpha=alpha),
        grid=(B, Sp // TM),
        in_specs=[fm(yda), fm(ymla), fm(yret), wspec(w_out), row, mod_spec(2), wspec(ln_g), wspec(ln_b),
                  mod_spec(3), mod_spec(4), pl.BlockSpec(rw_pad.shape, lambda b, t: (0, 0))],
        out_specs=[row, row, pl.BlockSpec((1, TM, LANES), lambda b, t: (b, t, 0))],
        out_shape=[sds((B, Sp, D), F32), sds((B, Sp, D), BF16), sds((B, Sp, LANES), F32)],
        compiler_params=pltpu.CompilerParams(dimension_semantics=("arbitrary", "arbitrary")),
        name="out_proj",
    )(yda, ymla, yret, w_out, xc, modr, ln_g, ln_b, modr, modr, rw_pad)


def _gates_T(logits_T, rb):
    s = _sigmoid(logits_T)
    ch = s + rb
    srow = [s[e:e + 1] for e in range(N_EXPERTS)]
    crow = [ch[e:e + 1] for e in range(N_EXPERTS)]
    per = N_EXPERTS // N_GROUPS
    gs = []
    for g in range(N_GROUPS):
        a, b, c, d = crow[per * g: per * g + per]
        m1, n1, m2, n2 = jnp.maximum(a, b), jnp.minimum(a, b), jnp.maximum(c, d), jnp.minimum(c, d)
        gs.append(jnp.maximum(m1, m2) + jnp.maximum(jnp.minimum(m1, m2), jnp.maximum(n1, n2)))
    gmax = functools.reduce(jnp.maximum, gs)
    taken = jnp.zeros(gmax.shape, jnp.bool_)
    gsel = []
    for g in range(N_GROUPS):
        sg = jnp.logical_and(gs[g] == gmax, jnp.logical_not(taken))
        gsel.append(sg)
        taken = jnp.logical_or(taken, sg)
    neg = jnp.full(gmax.shape, -jnp.inf, F32)
    mc = [jnp.where(gsel[e // per], crow[e], neg) for e in range(N_EXPERTS)]
    sel = [jnp.zeros(gmax.shape, jnp.bool_) for _ in range(N_EXPERTS)]
    for _ in range(2):
        top = functools.reduce(jnp.maximum, mc)
        taken = jnp.zeros(gmax.shape, jnp.bool_)
        for e in range(N_EXPERTS):
            hit = jnp.logical_and(mc[e] == top, jnp.logical_not(taken))
            taken = jnp.logical_or(taken, hit)
            sel[e] = jnp.logical_or(sel[e], hit)
            mc[e] = jnp.where(hit, neg, mc[e])
    w = [jnp.where(sel[e], srow[e], 0.0) for e in range(N_EXPERTS)]
    wsum = functools.reduce(lambda a, b: a + b, w)
    return jnp.concatenate([we / wsum * ROUTED_SCALE for we in w], axis=0)


def _moe_kernel(h_ref, lg_ref, rb_ref, w13_ref, w2_ref, x1_ref, gb_ref, gc_ref, g2_ref, b2_ref, o_ref,
                gates_sc, acc_sc, *, S, alpha):
    e = pl.program_id(2)
    tm = h_ref.shape[1]

    @pl.when(e == 0)
    def _():
        gT = _gates_T(lg_ref[0].T[0:N_EXPERTS, :], rb_ref[...])
        gates_sc[...] = jnp.concatenate([gT, jnp.zeros((LANES - N_EXPERTS, tm), F32)], axis=0).T
        acc_sc[...] = jnp.zeros(acc_sc.shape, F32)

    ab = jnp.dot(h_ref[0], w13_ref[0, 0], preferred_element_type=F32)
    a = ab[:, 0:D_EXPERT]
    hid = (a * _sigmoid(a) * ab[:, D_EXPERT:]).astype(BF16)
    y = jnp.dot(hid, w2_ref[0, 0], preferred_element_type=F32)
    lane = lax.broadcasted_iota(jnp.int32, (tm, LANES), 1)
    ge = jnp.sum(jnp.where(lane == e, gates_sc[...], 0.0), axis=1, keepdims=True)
    acc_sc[...] += ge * y

    @pl.when(e == N_EXPERTS - 1)
    def _():
        n_lat = S - pl.program_id(1) * tm
        rowi = lax.broadcasted_iota(jnp.int32, (tm, 1), 0)
        gate = jnp.where(rowi < n_lat, gb_ref[0, 0, 0], gc_ref[0, 0, 0])
        o_ref[0] = _layer_norm(alpha * x1_ref[0] + gate * acc_sc[...], g2_ref[0], b2_ref[0])


def _moe(l, h2, logits, rb, w13, w2, x1, modr, ln_g, ln_b, S, alpha):
    B, Sp, D = x1.shape
    tm = TM_MOE

    def wspec(a):
        return pl.BlockSpec((1,) + a.shape[1:], lambda b, t, e: (l,) + (0,) * (a.ndim - 1))

    row = lambda b, t, e: (b, t, 0)
    return pl.pallas_call(
        functools.partial(_moe_kernel, S=S, alpha=alpha),
        grid=(B, Sp // tm, N_EXPERTS),
        in_specs=[
            pl.BlockSpec((1, tm, D), row),
            pl.BlockSpec((1, tm, LANES), row),
            pl.BlockSpec(rb.shape, lambda b, t, e: (0, 0)),
            pl.BlockSpec((1, 1, D, 2 * D_EXPERT), lambda b, t, e: (l, e, 0, 0)),
            pl.BlockSpec((1, 1, D_EXPERT, D), lambda b, t, e: (l, e, 0, 0)),
            pl.BlockSpec((1, tm, D), row),
            pl.BlockSpec((1, 1, 1, 1, D), lambda b, t, e: (l, b, 5, 0, 0)),
            pl.BlockSpec((1, 1, 1, 1, D), lambda b, t, e: (l, B, 5, 0, 0)),
            wspec(ln_g), wspec(ln_b),
        ],
        out_specs=pl.BlockSpec((1, tm, D), row),
        out_shape=jax.ShapeDtypeStruct((B, Sp, D), F32),
        scratch_shapes=[pltpu.VMEM((tm, LANES), F32), pltpu.VMEM((tm, D), F32)],
        compiler_params=pltpu.CompilerParams(
            dimension_semantics=("arbitrary", "arbitrary", "arbitrary"), vmem_limit_bytes=56 * 1024 * 1024),
        name="moe",
    )(h2, logits, rb, w13, w2, x1, modr, modr, ln_g, ln_b)


def _pairswap(w):
    n = w.shape[-1]
    return w.reshape(w.shape[:-1] + (n // 2, 2))[..., ::-1].reshape(w.shape)


def _rope_tables(S, C, rot_dim):
    rows = S // GRID_W
    r = jnp.repeat(jnp.arange(rows, dtype=F32), GRID_W)
    col = jnp.tile(jnp.arange(GRID_W, dtype=F32), rows)
    n_freq = rot_dim // 4
    freqs = ROPE_BASE ** (-jnp.arange(n_freq, dtype=F32) / n_freq)
    ang = jnp.concatenate([r[:, None] * freqs, col[:, None] * freqs], -1)
    cos, sin = jnp.cos(ang), jnp.sin(ang)
    cos_rep = jnp.repeat(cos, 2, axis=-1)
    sin_alt = jnp.stack([-sin, sin], -1).reshape(S, rot_dim)
    cos_rep = jnp.concatenate([cos_rep, jnp.ones((C, rot_dim), F32)], 0)
    sin_alt = jnp.concatenate([sin_alt, jnp.zeros((C, rot_dim), F32)], 0)
    return cos_rep.T, sin_alt.T


def kernel(x, c, ctx, c_ctx, w_ada, b_ada, w_in, da_lambda, da_subln, mla_q_norm, mla_w_uq, mla_kv_norm, mla_w_ukv, ret_decay_f, ret_decay_b, w_out, ln1_g, ln1_b, router_w, router_b, exp_w1, exp_w3, exp_w2, ln2_g, ln2_b):
    B, S, D = x.shape
    C = ctx.shape[1]
    depth = w_in.shape[0]
    alpha = float((2 * depth) ** 0.25)

    cvec = jnp.concatenate([c, c_ctx[None, :], jnp.zeros((8 - B - 1, D), F32)], 0)
    mod = _modulation(cvec, w_ada, b_ada)
    modr = mod.reshape(depth, 8, N_MOD, 1, D)

    offs = np.cumsum((0,) + IN_SPLIT)
    blk = [w_in[:, :, offs[i]:offs[i + 1]] for i in range(len(IN_SPLIT))]
    w_daq, w_dak, w_dav, w_cq, w_ckv, w_kpe, w_rq, w_rk, w_rv, w_rg = blk
    w_all = jnp.concatenate([w_daq, _pairswap(w_daq), w_dak, _pairswap(w_dak), w_dav, w_cq, w_ckv,
                             w_kpe, _pairswap(w_kpe), w_rq, _pairswap(w_rq), w_rk, _pairswap(w_rk), w_rv, w_rg], -1)
    wT_all = jnp.swapaxes(w_all, 1, 2).astype(BF16)
    uq = mla_w_uq.reshape(depth, MLA_Q_RANK, MLA_HEADS, MLA_NOPE + MLA_ROPE)
    uq_sw = _pairswap(uq[..., MLA_NOPE:]).reshape(depth, MLA_Q_RANK, MLA_HEADS * MLA_ROPE)
    wuqT = jnp.swapaxes(jnp.concatenate([mla_w_uq, uq_sw], -1), 1, 2).astype(BF16)
    ukv = mla_w_ukv.reshape(depth, MLA_KV_RANK, MLA_HEADS, MLA_NOPE + MLA_VDIM)
    wvT = jnp.swapaxes(ukv[..., MLA_NOPE:].reshape(depth, MLA_KV_RANK, MLA_VW), 1, 2).astype(BF16)
    nopeT = jnp.transpose(ukv[..., :MLA_NOPE], (0, 2, 3, 1))
    top = jnp.concatenate([nopeT, jnp.zeros((depth, MLA_HEADS, MLA_NOPE, MLA_ROPE), F32)], -1)
    mid = jnp.concatenate([jnp.zeros((MLA_ROPE, MLA_KV_RANK), F32), jnp.eye(MLA_ROPE, dtype=F32)], -1)
    mid = jnp.broadcast_to(mid, (depth, MLA_HEADS, MLA_ROPE, MLA_KV_RANK + MLA_ROPE))
    bot = jnp.zeros((depth, MLA_HEADS, LANES - MLA_NOPE - MLA_ROPE, MLA_KV_RANK + MLA_ROPE), F32)
    wkT = jnp.concatenate([top, mid, bot], 2).reshape(depth, MLA_HEADS * LANES, MLA_KV_RANK + MLA_ROPE).astype(BF16)
    gq = mla_q_norm[:, :, None]
    gkv = mla_kv_norm[:, :, None]
    subln = da_subln[:, :, None]
    w_out_b = w_out.astype(BF16)
    w13 = jnp.concatenate([exp_w1, exp_w3], -1).astype(BF16)
    w2 = exp_w2.astype(BF16)
    rw_pad = jnp.concatenate([router_w, jnp.zeros((D, LANES - N_EXPERTS), F32)], -1)
    rb = router_b[:, None]
    dec = jnp.broadcast_to(jnp.stack([ret_decay_f, ret_decay_b], 1)[:, :, :, None, None],
                           (depth, 2, RET_HEADS, 8, LANES))
    tabs = _rope_tables(S, C, DA_DIM) + _rope_tables(S, C, RET_DIM)
    ln1g, ln1b, ln2g, ln2b = (a[:, None, :] for a in (ln1_g, ln1_b, ln2_g, ln2_b))

    xc = jnp.concatenate([x, ctx], 1)
    for l in range(depth):
        (daqr, daqf, dak, dav, mqr, mqf, mk, mv, rq, rk, rv, rg) = _in_proj(
            l, xc, modr, wT_all, tabs, gq, wuqT, gkv, wvT, wkT, S)
        yda = _attention("da", l, daqr, daqf, dak, dav, S, C, lam=da_lambda, g=subln)
        ymla = _attention("mla", l, mqr, mqf, mk, mv, S, C)
        yret = _retention(rq, rk, rv, rg, dec[l], S, C)
        x1, h2, logits = _out_proj(l, yda, ymla, yret, w_out_b, xc, modr, ln1g, ln1b, rw_pad, S, alpha)
        xc = _moe(l, h2, logits, rb, w13, w2, x1, modr, ln2g, ln2b, S, alpha)
    return xc[:, :S, :]
```

```python
import functools
import math

import jax
import jax.numpy as jnp
import numpy as np
from jax import lax
from jax.experimental import pallas as pl
from jax.experimental.pallas import tpu as pltpu

F32 = jnp.float32
BF16 = jnp.bfloat16

GRID_W = 64
DA_HEADS, DA_DIM, DA_VDIM = 6, 32, 64
MLA_HEADS, MLA_NOPE, MLA_ROPE, MLA_VDIM = 6, 64, 32, 64
MLA_Q_RANK, MLA_KV_RANK = 256, 128
RET_HEADS, RET_DIM, RET_CHUNK = 4, 64, 128
ROPE_BASE = 10000.0
N_EXPERTS, N_GROUPS, D_EXPERT = 16, 4, 512
ROUTED_SCALE = 1.0
N_MOD = 6
LN_EPS = 1e-5
RMS_EPS = 1e-6
LOG2E = math.log2(math.e)

DA_W = DA_HEADS * 2 * DA_DIM
DA_VW = DA_HEADS * DA_VDIM
RET_W = RET_HEADS * RET_DIM
MLA_QW = MLA_HEADS * (MLA_NOPE + MLA_ROPE)
MLA_VW = MLA_HEADS * MLA_VDIM
IN_SPLIT = (DA_W, DA_W, DA_VW, MLA_Q_RANK, MLA_KV_RANK, MLA_ROPE, RET_W, RET_W, RET_W, RET_W)

LANES = 128
V_ROWS = 80
TM = 256
ATTN_N = 512
TK = 2048
ATTN_SUBTILES = 2
ATTN_NBUF = 2
ATTN_MAX_JUMP = 32.0
ATTN_UNROLL = 2
TM_MOE = 640
RET_UNROLL = 8

NT_DIMS = (((1,), (1,)), ((), ()))
TN_DIMS = (((0,), (0,)), ((), ()))


def _sigmoid(x):
    return 1.0 / (1.0 + jnp.exp(-x))


def _mod_kernel(c_ref, w_ref, b_ref, o_ref):
    c = c_ref[...]
    sc = (c * _sigmoid(c)).astype(BF16)
    o_ref[0] = jnp.dot(sc, w_ref[0].astype(BF16), preferred_element_type=F32) + b_ref[0]


def _modulation(cvec, w_ada, b_ada):
    depth, d, n = w_ada.shape
    tn = 1536
    return pl.pallas_call(
        _mod_kernel,
        grid=(depth, n // tn),
        in_specs=[
            pl.BlockSpec((8, d), lambda l, j: (0, 0)),
            pl.BlockSpec((1, d, tn), lambda l, j: (l, 0, j)),
            pl.BlockSpec((1, 1, tn), lambda l, j: (l, 0, j)),
        ],
        out_specs=pl.BlockSpec((1, 8, tn), lambda l, j: (l, 0, j)),
        out_shape=jax.ShapeDtypeStruct((depth, 8, n), F32),
        name="adaln_mod",
    )(cvec, w_ada, b_ada.reshape(depth, 1, n))


_O_DAQ, _O_DAQS, _O_DAK, _O_DAKS, _O_DAV = 0, 384, 768, 1152, 1536
_O_CQ, _O_CKV, _O_KPE, _O_KPES = 1920, 2176, 2304, 2336
_O_RQ, _O_RQS, _O_RK, _O_RKS, _O_RV, _O_RG = 2368, 2624, 2880, 3136, 3392, 3648
N_ZROWS = 3904


def _rms_rows(x):
    return x * lax.rsqrt(jnp.mean(x * x, axis=0, keepdims=True) + RMS_EPS)


def _in_kernel(x_ref, sh_ref, sc_ref, wT_ref, cosA_ref, sinA_ref, cosR_ref, sinR_ref,
               gq_ref, wuqT_ref, gkv_ref, wvT_ref, wkT_ref,
               daqr, daqf, dak, dav, mqr, mqf, mk, mv, rq, rk, rv, rg, z_ref, *, cq_da, cq_mla, kscale):
    tm = x_ref.shape[1]
    x = x_ref[0]
    h = (x * (1.0 + sc_ref[0, 0, 0]) + sh_ref[0, 0, 0]).astype(BF16)
    z_ref[...] = lax.dot_general(wT_ref[0], h, NT_DIMS, preferred_element_type=F32)
    cosA = cosA_ref[...]
    sinA = sinA_ref[...]
    cosR = cosR_ref[...]
    sinR = sinR_ref[...]
    ones_rows = (lax.broadcasted_iota(jnp.int32, (DA_HEADS, V_ROWS - 64, tm), 1) == 0).astype(BF16)

    def rows(off, n):
        return z_ref[off:off + n, :]

    q = rows(_O_DAQ, DA_W).reshape(12, DA_DIM, tm)
    qs = rows(_O_DAQS, DA_W).reshape(12, DA_DIM, tm)
    daqr[0] = ((q * cosA + qs * sinA) * cq_da).astype(BF16)
    daqf[0] = (q * cq_da).astype(BF16)
    k = rows(_O_DAK, DA_W).reshape(12, DA_DIM, tm)
    ks = rows(_O_DAKS, DA_W).reshape(12, DA_DIM, tm)
    kr = (k * cosA + ks * sinA).reshape(DA_HEADS, 2 * DA_DIM, tm)
    kp = jnp.concatenate([kr, jnp.zeros_like(kr)], axis=1).reshape(DA_HEADS * LANES, tm)
    dak[0] = kp.T.astype(BF16)
    dav[0, :, 0:64, :] = rows(_O_DAV, DA_VW).reshape(DA_HEADS, DA_VDIM, tm).astype(BF16)
    dav[0, :, 64:V_ROWS, :] = ones_rows

    cqn = (_rms_rows(rows(_O_CQ, MLA_Q_RANK)) * gq_ref[0]).astype(BF16)
    qm = jnp.dot(wuqT_ref[0], cqn, preferred_element_type=F32)
    qh = qm[0:MLA_QW].reshape(MLA_HEADS, MLA_NOPE + MLA_ROPE, tm)
    qsw = qm[MLA_QW:MLA_QW + MLA_HEADS * MLA_ROPE].reshape(MLA_HEADS, MLA_ROPE, tm)
    rot = qh[:, MLA_NOPE:, :] * cosA + qsw * sinA
    mqr[0, :, 0:MLA_NOPE, :] = (qh[:, 0:MLA_NOPE, :] * cq_mla).astype(BF16)
    mqr[0, :, MLA_NOPE:, :] = (rot * cq_mla).astype(BF16)
    mqf[0] = (qh * cq_mla).astype(BF16)
    ckvn = (_rms_rows(rows(_O_CKV, MLA_KV_RANK)) * gkv_ref[0]).astype(BF16)
    vm = jnp.dot(wvT_ref[0], ckvn, preferred_element_type=F32)
    mv[0, :, 0:64, :] = vm.reshape(MLA_HEADS, MLA_VDIM, tm).astype(BF16)
    mv[0, :, 64:V_ROWS, :] = ones_rows
    kper = rows(_O_KPE, MLA_ROPE) * cosA + rows(_O_KPES, MLA_ROPE) * sinA
    kin = jnp.concatenate([ckvn, kper.astype(BF16)], axis=0)
    kmT = jnp.dot(wkT_ref[0], kin, preferred_element_type=F32)
    mk[0] = kmT.T.astype(BF16)

    r_q = rows(_O_RQ, RET_W).reshape(RET_HEADS, RET_DIM, tm)
    r_qs = rows(_O_RQS, RET_W).reshape(RET_HEADS, RET_DIM, tm)
    rq[0] = (r_q * cosR + r_qs * sinR).astype(BF16)
    r_k = rows(_O_RK, RET_W).reshape(RET_HEADS, RET_DIM, tm)
    r_ks = rows(_O_RKS, RET_W).reshape(RET_HEADS, RET_DIM, tm)
    rkr = (r_k * cosR + r_ks * sinR) * kscale
    rkp = jnp.concatenate([rkr, jnp.zeros_like(rkr)], axis=1).reshape(RET_HEADS * LANES, tm)
    rk[0] = rkp.T.astype(BF16)
    rv[0] = rows(_O_RV, RET_W).reshape(RET_HEADS, RET_DIM, tm).astype(BF16)
    rg[0] = rows(_O_RG, RET_W)


def _in_proj(l, xc, modr, wT_all, tabs, gq, wuqT, gkv, wvT, wkT, S):
    B, Sp, D = xc.shape
    nt = Sp // TM
    n_lat_tiles = S // TM
    cosA, sinA, cosR, sinR = tabs

    def mod_spec(j):
        return pl.BlockSpec((1, 1, 1, 1, D), lambda b, t: (l, jnp.where(t >= n_lat_tiles, B, b), j, 0, 0))

    def wspec(a):
        return pl.BlockSpec((1,) + a.shape[1:], lambda b, t: (l,) + (0,) * (a.ndim - 1))

    def fm4(h, d):
        return pl.BlockSpec((1, h, d, TM), lambda b, t: (b, 0, 0, t))

    def tmaj(w):
        return pl.BlockSpec((1, TM, w), lambda b, t: (b, t, 0))

    kern = functools.partial(
        _in_kernel,
        cq_da=float(DA_DIM ** -0.5 * LOG2E),
        cq_mla=float((MLA_NOPE + MLA_ROPE) ** -0.5 * LOG2E),
        kscale=float(RET_DIM ** -0.5),
    )
    sds = jax.ShapeDtypeStruct
    return pl.pallas_call(
        kern,
        grid=(B, nt),
        in_specs=[
            pl.BlockSpec((1, TM, D), lambda b, t: (b, t, 0)),
            mod_spec(0), mod_spec(1),
            wspec(wT_all),
            pl.BlockSpec((DA_DIM, TM), lambda b, t: (0, t)),
            pl.BlockSpec((DA_DIM, TM), lambda b, t: (0, t)),
            pl.BlockSpec((RET_DIM, TM), lambda b, t: (0, t)),
            pl.BlockSpec((RET_DIM, TM), lambda b, t: (0, t)),
            wspec(gq), wspec(wuqT), wspec(gkv), wspec(wvT), wspec(wkT),
        ],
        out_specs=[
            fm4(12, DA_DIM), fm4(12, DA_DIM), tmaj(DA_HEADS * LANES), fm4(DA_HEADS, V_ROWS),
            fm4(MLA_HEADS, 96), fm4(MLA_HEADS, 96), tmaj(MLA_HEADS * LANES), fm4(MLA_HEADS, V_ROWS),
            fm4(RET_HEADS, RET_DIM), tmaj(RET_HEADS * LANES), fm4(RET_HEADS, RET_DIM),
            pl.BlockSpec((1, RET_W, TM), lambda b, t: (b, 0, t)),
        ],
        out_shape=[
            sds((B, 12, DA_DIM, Sp), BF16), sds((B, 12, DA_DIM, Sp), BF16),
            sds((B, Sp, DA_HEADS * LANES), BF16), sds((B, DA_HEADS, V_ROWS, Sp), BF16),
            sds((B, MLA_HEADS, 96, Sp), BF16), sds((B, MLA_HEADS, 96, Sp), BF16),
            sds((B, Sp, MLA_HEADS * LANES), BF16), sds((B, MLA_HEADS, V_ROWS, Sp), BF16),
            sds((B, RET_HEADS, RET_DIM, Sp), BF16), sds((B, Sp, RET_HEADS * LANES), BF16),
            sds((B, RET_HEADS, RET_DIM, Sp), BF16), sds((B, RET_W, Sp), F32),
        ],
        scratch_shapes=[pltpu.VMEM((N_ZROWS, TM), F32)],
        compiler_params=pltpu.CompilerParams(
            dimension_semantics=("arbitrary", "arbitrary"), vmem_limit_bytes=56 * 1024 * 1024),
        name="in_proj",
    )(xc, modr, modr, wT_all, cosA, sinA, cosR, sinR, gq, wuqT, gkv, wvT, wkT)


def _attn_kernel(*refs, mode, S, C, lam_init, ctx_only, nsub):
    refs = list(refs)
    qr_ref, qf_ref, k_ref, v_ref = refs[:4]
    del refs[:4]
    if mode == "da":
        lam_ref, g_ref = refs[:2]
        del refs[:2]
    if ctx_only:
        del refs[:1]
    o_ref, m_sc, acc_sc, q_sc, jump_sc = refs[:5]
    nb = ATTN_NBUF
    pbuf, albuf = (refs[5 + i * nb: 5 + (i + 1) * nb] for i in range(2))
    tq = o_ref.shape[2] // nsub
    n_lat = S // TK

    def qpad(ref, t):
        cols = slice(t * tq, (t + 1) * tq)
        if mode == "da":
            z = jnp.zeros((DA_DIM, tq), BF16)
            c1 = jnp.concatenate([ref[0, 0, :, cols], z, z, z], axis=0)
            c2 = jnp.concatenate([z, ref[0, 1, :, cols], z, z], axis=0)
            return jnp.concatenate([c1, c2], axis=1)
        return jnp.concatenate([ref[0, 0, :, cols], jnp.zeros((LANES - 96, tq), BF16)], axis=0)

    def chunk(j):
        return pl.ds(j * TK if isinstance(j, int) else pl.multiple_of(j * TK, TK), TK)

    def context_keys(t):
        s = jnp.dot(k_ref[0, S:S + C, :], qpad(qf_ref, t), preferred_element_type=F32)
        m_c = jnp.max(s, axis=0, keepdims=True)
        m_sc[t] = m_c
        acc_sc[t] = jnp.dot(v_ref[0, 0, :, S:S + C], jnp.exp2(s - m_c).astype(BF16), preferred_element_type=F32)

    for t in range(nsub):
        context_keys(t)

    if not ctx_only:
        for t in range(nsub):
            q_sc[t] = qpad(qr_ref, t)
        jump_sc[...] = jnp.full(jump_sc.shape, -jnp.inf, F32)

        def probs(t, j, r):
            s = jnp.dot(k_ref[0, chunk(j), :], q_sc[t], preferred_element_type=F32)
            m_prev = m_sc[t]
            pbuf[r][t] = jnp.exp2(s - m_prev).astype(BF16)
            m_chunk = jnp.max(s, axis=0, keepdims=True)
            m_new = jnp.maximum(m_prev, m_chunk)
            jump_sc[t] = jnp.maximum(jump_sc[t], m_chunk - m_prev)
            albuf[r][t] = jnp.exp2(m_prev - m_new)
            m_sc[t] = m_new

        def accumulate(t, j, r):
            pv = jnp.dot(v_ref[0, 0, :, chunk(j)], pbuf[r][t], preferred_element_type=F32)
            acc_sc[t] = (acc_sc[t] + pv) * albuf[r][t]

        def step(j, r):
            for t in range(nsub):
                accumulate(t, j, r)
                probs(t, j + 1, (r + 1) % nb)

        for t in range(nsub):
            probs(t, 0, 0)
        n_steps = n_lat - 1
        trips = n_steps // ATTN_UNROLL

        def body(i, carry):
            for u in range(ATTN_UNROLL):
                step(i * ATTN_UNROLL + u, u % nb)
            return carry

        lax.fori_loop(0, trips, body, 0)
        for j in range(trips * ATTN_UNROLL, n_steps):
            step(j, j % nb)
        for t in range(nsub):
            accumulate(t, n_lat - 1, (n_lat - 1) % nb)

        @pl.when(jnp.max(jump_sc[...]) > ATTN_MAX_JUMP)
        def _():
            for t in range(nsub):
                context_keys(t)

                def exact_step(j, carry, t=t):
                    s = jnp.dot(k_ref[0, chunk(j), :], q_sc[t], preferred_element_type=F32)
                    m_old = m_sc[t]
                    m_new = jnp.maximum(m_old, jnp.max(s, axis=0, keepdims=True))
                    pv = jnp.dot(v_ref[0, 0, :, chunk(j)], jnp.exp2(s - m_new).astype(BF16),
                                 preferred_element_type=F32)
                    acc_sc[t] = acc_sc[t] * jnp.exp2(m_old - m_new) + pv
                    m_sc[t] = m_new
                    return carry

                lax.fori_loop(0, n_lat, exact_step, 0)

    for t in range(nsub):
        acc = acc_sc[t]
        o = acc[0:64, :] / acc[64:65, :]
        cols = slice(t * tq, (t + 1) * tq)
        if mode == "da":
            lf = lam_ref[0]
            lam = (jnp.exp(jnp.sum(lf[0:1] * lf[1:2], axis=1, keepdims=True))
                   - jnp.exp(jnp.sum(lf[2:3] * lf[3:4], axis=1, keepdims=True)) + lam_init)
            od = o[:, 0:tq] - lam * o[:, tq:2 * tq]
            y = _rms_rows(od) * g_ref[0]
            o_ref[0, :, cols] = (y * (1.0 - lam_init)).astype(BF16)
        else:
            o_ref[0, :, cols] = o.astype(BF16)


def _attention_call(mode, l, qr, qf, k, v, S, C, lam, g, y_latent):
    B, nmaps, d, Sp = qr.shape
    H = v.shape[1]
    nm = nmaps // H
    ctx_only = y_latent is not None
    nsub = 1 if ctx_only else ATTN_SUBTILES
    tq = C if ctx_only else ATTN_N // nm
    n = nm * tq
    tqb = nsub * tq
    q0 = S // tqb if ctx_only else 0
    lam_init = 0.8 - 0.6 * math.exp(-0.3 * l)
    kern = functools.partial(_attn_kernel, mode=mode, S=S, C=C, lam_init=lam_init, ctx_only=ctx_only, nsub=nsub)
    in_specs = [
        pl.BlockSpec((1, nm, d, tqb), lambda b, h, i: (b, h, 0, q0 + i)),
        pl.BlockSpec((1, nm, d, tqb), lambda b, h, i: (b, h, 0, q0 + i)),
        pl.BlockSpec((1, Sp, LANES), lambda b, h, i: (b, 0, h)),
        pl.BlockSpec((1, 1, V_ROWS, Sp), lambda b, h, i: (b, h, 0, 0)),
    ]
    args = [qr, qf, k, v]
    if mode == "da":
        in_specs += [pl.BlockSpec((1,) + lam.shape[1:], lambda b, h, i: (l, 0, 0)),
                     pl.BlockSpec((1,) + g.shape[1:], lambda b, h, i: (l, 0, 0))]
        args += [lam, g]
    aliases = {}
    if ctx_only:
        in_specs.append(pl.BlockSpec(memory_space=pl.ANY))
        args.append(y_latent)
        aliases = {len(args) - 1: 0}
    tk = C if ctx_only else TK
    return pl.pallas_call(
        kern,
        grid=(B, H, 1 if ctx_only else S // tqb),
        in_specs=in_specs,
        out_specs=pl.BlockSpec((1, 64, tqb), lambda b, h, i: (b, h, q0 + i)),
        out_shape=jax.ShapeDtypeStruct((B, H * 64, Sp), BF16),
        input_output_aliases=aliases,
        scratch_shapes=(
            [pltpu.VMEM((nsub, 1, n), F32), pltpu.VMEM((nsub, V_ROWS, n), F32), pltpu.VMEM((nsub, LANES, n), BF16),
             pltpu.VMEM((nsub, 1, n), F32)]
            + [pltpu.VMEM((nsub, tk, n), BF16)] * ATTN_NBUF + [pltpu.VMEM((nsub, 1, n), F32)] * ATTN_NBUF),
        compiler_params=pltpu.CompilerParams(
            dimension_semantics=("arbitrary", "arbitrary", "arbitrary"), vmem_limit_bytes=48 * 1024 * 1024),
        name="attn_" + mode + ("_ctx" if ctx_only else ""),
    )(*args)


def _attention(mode, l, qr, qf, k, v, S, C, lam=None, g=None):
    y = _attention_call(mode, l, qr, qf, k, v, S, C, lam, g, None)
    return _attention_call(mode, l, qr, qf, k, v, S, C, lam, g, y)


def _ret_kernel(q_ref, k_ref, v_ref, g_ref, dec_ref, o_ref, sb_sc, *, S, C):
    CH = RET_CHUNK
    nl, nc = S // CH, C // CH

    def log_sigmoid(x):
        return jnp.minimum(x, 0.0) - jnp.log1p(jnp.exp(-jnp.abs(x)))

    lgf = log_sigmoid(dec_ref[0, 0][0:1, :])
    lgb = log_sigmoid(dec_ref[1, 0][0:1, :])
    ii = lax.broadcasted_iota(jnp.int32, (CH, CH), 1).astype(F32)
    jj = lax.broadcasted_iota(jnp.int32, (CH, CH), 0).astype(F32)
    dd = ii - jj
    fwd = dd >= 0
    mt = (jnp.where(fwd, jnp.exp(lgf * jnp.where(fwd, dd, 0.0)), 0.0)
          + jnp.where(fwd, 0.0, jnp.exp(lgb * jnp.where(fwd, 0.0, -dd))))
    xi_f = jnp.exp(lgf * (ii + 1.0))
    xi_b = jnp.exp(lgb * (CH - ii))
    zeta_f = jnp.exp(lgf * (CH - 1.0 - jj))
    zeta_b = jnp.exp(lgb * jj)
    cdf = jnp.exp(lgf * CH)
    cdb = jnp.exp(lgb * CH)
    zq = jnp.zeros((RET_DIM, CH), BF16)

    def sl(c):
        return pl.ds(pl.multiple_of(c * CH, CH), CH)

    def state_update(c, zeta):
        kz = (k_ref[0, sl(c), :].astype(F32) * zeta).astype(BF16)
        return jnp.dot(v_ref[0, 0, :, sl(c)], kz, preferred_element_type=F32)

    def bwd_step(c, sb):
        sb_sc[c] = sb
        return sb * cdb + state_update(c, zeta_b)

    sb = jnp.zeros((RET_DIM, CH), F32)
    for c in range(nl + nc - 1, nl - 1, -1):
        sb = bwd_step(c, sb)
    lax.fori_loop(0, nl, lambda n, s: bwd_step(nl - 1 - n, s), sb, unroll=RET_UNROLL)

    def fwd_step(c, sf):
        qc = q_ref[0, 0, :, sl(c)]
        qp = jnp.concatenate([qc, zq], axis=0)
        kc = k_ref[0, sl(c), :]
        vc = v_ref[0, 0, :, sl(c)]
        at = jnp.dot(kc, qp, preferred_element_type=F32)
        qpf = qp.astype(F32)
        rhs = jnp.concatenate([(at * mt).astype(BF16), (qpf * xi_f).astype(BF16), (qpf * xi_b).astype(BF16)], axis=0)
        lhs = jnp.concatenate([vc, sf.astype(BF16), sb_sc[c].astype(BF16)], axis=1)
        o = jnp.dot(lhs, rhs, preferred_element_type=F32)
        gch = g_ref[0, :, sl(c)]
        o_ref[0, :, sl(c)] = (gch * _sigmoid(gch) * _rms_rows(o)).astype(BF16)
        return sf * cdf + state_update(c, zeta_f)

    sf = jnp.zeros((RET_DIM, CH), F32)
    for c in range(nl, nl + nc):
        sf = fwd_step(c, sf)
    lax.fori_loop(0, nl, fwd_step, sf, unroll=RET_UNROLL)


def _retention(rq, rk, rv, rg, dec, S, C):
    B, H, d, Sp = rq.shape
    kern = functools.partial(_ret_kernel, S=S, C=C)
    return pl.pallas_call(
        kern,
        grid=(B, H),
        in_specs=[
            pl.BlockSpec((1, 1, d, Sp), lambda b, h: (b, h, 0, 0)),
            pl.BlockSpec((1, Sp, LANES), lambda b, h: (b, 0, h)),
            pl.BlockSpec((1, 1, d, Sp), lambda b, h: (b, h, 0, 0)),
            pl.BlockSpec((1, d, Sp), lambda b, h: (b, h, 0)),
            pl.BlockSpec((2, 1, 8, LANES), lambda b, h: (0, h, 0, 0)),
        ],
        out_specs=pl.BlockSpec((1, d, Sp), lambda b, h: (b, h, 0)),
        out_shape=jax.ShapeDtypeStruct((B, H * d, Sp), BF16),
        scratch_shapes=[pltpu.VMEM((Sp // RET_CHUNK, RET_DIM, RET_CHUNK), F32)],
        compiler_params=pltpu.CompilerParams(
            dimension_semantics=("arbitrary", "arbitrary"), vmem_limit_bytes=56 * 1024 * 1024),
        name="retention",
    )(rq, rk, rv, rg, dec)


def _layer_norm(t, g, b):
    mu = jnp.mean(t, axis=-1, keepdims=True)
    tc = t - mu
    var = jnp.mean(tc * tc, axis=-1, keepdims=True)
    return tc * lax.rsqrt(var + LN_EPS) * g + b


def _out_kernel(yda_ref, ymla_ref, yret_ref, w_ref, x_ref, gate_ref, g1_ref, b1_ref, sh2_ref, sc2_ref, rw_ref,
                x1_ref, h2_ref, lg_ref, *, alpha):
    proj = (lax.dot_general(yda_ref[0], w_ref[0, 0:DA_VW, :], TN_DIMS, preferred_element_type=F32)
            + lax.dot_general(ymla_ref[0], w_ref[0, DA_VW:DA_VW + MLA_VW, :], TN_DIMS, preferred_element_type=F32)
            + lax.dot_general(yret_ref[0], w_ref[0, DA_VW + MLA_VW:, :], TN_DIMS, preferred_element_type=F32))
    x1 = _layer_norm(alpha * x_ref[0] + gate_ref[0, 0, 0] * proj, g1_ref[0], b1_ref[0])
    x1_ref[0] = x1
    h2 = x1 * (1.0 + sc2_ref[0, 0, 0]) + sh2_ref[0, 0, 0]
    h2_ref[0] = h2.astype(BF16)
    h2_hi = h2.astype(BF16)
    h2_lo = (h2 - h2_hi.astype(F32)).astype(BF16)
    hi = jnp.dot(h2_hi, rw_ref[...], preferred_element_type=F32)
    lo = jnp.dot(h2_lo, rw_ref[:, 0:LANES], preferred_element_type=F32)
    lg_ref[0] = hi[:, 0:LANES] + (hi[:, LANES:] + lo)


def _out_proj(l, yda, ymla, yret, w_out, xc, modr, ln_g, ln_b, rw_pad, S, alpha):
    B, Sp, D = xc.shape
    n_lat_tiles = S // TM

    def mod_spec(j):
        return pl.BlockSpec((1, 1, 1, 1, D), lambda b, t: (l, jnp.where(t >= n_lat_tiles, B, b), j, 0, 0))

    def fm(a):
        return pl.BlockSpec((1, a.shape[1], TM), lambda b, t: (b, 0, t))

    def wspec(a):
        return pl.BlockSpec((1,) + a.shape[1:], lambda b, t: (l,) + (0,) * (a.ndim - 1))

    row = pl.BlockSpec((1, TM, D), lambda b, t: (b, t, 0))
    sds = jax.ShapeDtypeStruct
    return pl.pallas_call(
        functools.partial(_out_kernel, alpha=alpha),
        grid=(B, Sp // TM),
        in_specs=[fm(yda), fm(ymla), fm(yret), wspec(w_out), row, mod_spec(2), wspec(ln_g), wspec(ln_b),
                  mod_spec(3), mod_spec(4), pl.BlockSpec(rw_pad.shape, lambda b, t: (0, 0))],
        out_specs=[row, row, pl.BlockSpec((1, TM, LANES), lambda b, t: (b, t, 0))],
        out_shape=[sds((B, Sp, D), F32), sds((B, Sp, D), BF16), sds((B, Sp, LANES), F32)],
        compiler_params=pltpu.CompilerParams(dimension_semantics=("arbitrary", "arbitrary")),
        name="out_proj",
    )(yda, ymla, yret, w_out, xc, modr, ln_g, ln_b, modr, modr, rw_pad)


def _gates_T(logits_T, rb):
    s = _sigmoid(logits_T)
    ch = s + rb
    srow = [s[e:e + 1] for e in range(N_EXPERTS)]
    crow = [ch[e:e + 1] for e in range(N_EXPERTS)]
    per = N_EXPERTS // N_GROUPS
    gs = []
    for g in range(N_GROUPS):
        a, b, c, d = crow[per * g: per * g + per]
        m1, n1, m2, n2 = jnp.maximum(a, b), jnp.minimum(a, b), jnp.maximum(c, d), jnp.minimum(c, d)
        gs.append(jnp.maximum(m1, m2) + jnp.maximum(jnp.minimum(m1, m2), jnp.maximum(n1, n2)))
    gmax = functools.reduce(jnp.maximum, gs)
    taken = jnp.zeros(gmax.shape, jnp.bool_)
    gsel = []
    for g in range(N_GROUPS):
        sg = jnp.logical_and(gs[g] == gmax, jnp.logical_not(taken))
        gsel.append(sg)
        taken = jnp.logical_or(taken, sg)
    neg = jnp.full(gmax.shape, -jnp.inf, F32)
    mc = [jnp.where(gsel[e // per], crow[e], neg) for e in range(N_EXPERTS)]
    sel = [jnp.zeros(gmax.shape, jnp.bool_) for _ in range(N_EXPERTS)]
    for _ in range(2):
        top = functools.reduce(jnp.maximum, mc)
        taken = jnp.zeros(gmax.shape, jnp.bool_)
        for e in range(N_EXPERTS):
            hit = jnp.logical_and(mc[e] == top, jnp.logical_not(taken))
            taken = jnp.logical_or(taken, hit)
            sel[e] = jnp.logical_or(sel[e], hit)
            mc[e] = jnp.where(hit, neg, mc[e])
    w = [jnp.where(sel[e], srow[e], 0.0) for e in range(N_EXPERTS)]
    wsum = functools.reduce(lambda a, b: a + b, w)
    return jnp.concatenate([we / wsum * ROUTED_SCALE for we in w], axis=0)


def _moe_kernel(h_ref, lg_ref, rb_ref, w13_ref, w2_ref, x1_ref, gb_ref, gc_ref, g2_ref, b2_ref, o_ref,
                gates_sc, acc_sc, *, S, alpha):
    e = pl.program_id(2)
    tm = h_ref.shape[1]

    @pl.when(e == 0)
    def _():
        gT = _gates_T(lg_ref[0].T[0:N_EXPERTS, :], rb_ref[...])
        gates_sc[...] = jnp.concatenate([gT, jnp.zeros((LANES - N_EXPERTS, tm), F32)], axis=0).T
        acc_sc[...] = jnp.zeros(acc_sc.shape, F32)

    ab = jnp.dot(h_ref[0], w13_ref[0, 0], preferred_element_type=F32)
    a = ab[:, 0:D_EXPERT]
    hid = (a * _sigmoid(a) * ab[:, D_EXPERT:]).astype(BF16)
    y = jnp.dot(hid, w2_ref[0, 0], preferred_element_type=F32)
    lane = lax.broadcasted_iota(jnp.int32, (tm, LANES), 1)
    ge = jnp.sum(jnp.where(lane == e, gates_sc[...], 0.0), axis=1, keepdims=True)
    acc_sc[...] += ge * y

    @pl.when(e == N_EXPERTS - 1)
    def _():
        n_lat = S - pl.program_id(1) * tm
        rowi = lax.broadcasted_iota(jnp.int32, (tm, 1), 0)
        gate = jnp.where(rowi < n_lat, gb_ref[0, 0, 0], gc_ref[0, 0, 0])
        o_ref[0] = _layer_norm(alpha * x1_ref[0] + gate * acc_sc[...], g2_ref[0], b2_ref[0])


def _moe(l, h2, logits, rb, w13, w2, x1, modr, ln_g, ln_b, S, alpha):
    B, Sp, D = x1.shape
    tm = TM_MOE

    def wspec(a):
        return pl.BlockSpec((1,) + a.shape[1:], lambda b, t, e: (l,) + (0,) * (a.ndim - 1))

    row = lambda b, t, e: (b, t, 0)
    return pl.pallas_call(
        functools.partial(_moe_kernel, S=S, alpha=alpha),
        grid=(B, Sp // tm, N_EXPERTS),
        in_specs=[
            pl.BlockSpec((1, tm, D), row),
            pl.BlockSpec((1, tm, LANES), row),
            pl.BlockSpec(rb.shape, lambda b, t, e: (0, 0)),
            pl.BlockSpec((1, 1, D, 2 * D_EXPERT), lambda b, t, e: (l, e, 0, 0)),
            pl.BlockSpec((1, 1, D_EXPERT, D), lambda b, t, e: (l, e, 0, 0)),
            pl.BlockSpec((1, tm, D), row),
            pl.BlockSpec((1, 1, 1, 1, D), lambda b, t, e: (l, b, 5, 0, 0)),
            pl.BlockSpec((1, 1, 1, 1, D), lambda b, t, e: (l, B, 5, 0, 0)),
            wspec(ln_g), wspec(ln_b),
        ],
        out_specs=pl.BlockSpec((1, tm, D), row),
        out_shape=jax.ShapeDtypeStruct((B, Sp, D), F32),
        scratch_shapes=[pltpu.VMEM((tm, LANES), F32), pltpu.VMEM((tm, D), F32)],
        compiler_params=pltpu.CompilerParams(
            dimension_semantics=("arbitrary", "arbitrary", "arbitrary"), vmem_limit_bytes=56 * 1024 * 1024),
        name="moe",
    )(h2, logits, rb, w13, w2, x1, modr, modr, ln_g, ln_b)


def _pairswap(w):
    n = w.shape[-1]
    return w.reshape(w.shape[:-1] + (n // 2, 2))[..., ::-1].reshape(w.shape)


def _rope_tables(S, C, rot_dim):
    rows = S // GRID_W
    r = jnp.repeat(jnp.arange(rows, dtype=F32), GRID_W)
    col = jnp.tile(jnp.arange(GRID_W, dtype=F32), rows)
    n_freq = rot_dim // 4
    freqs = ROPE_BASE ** (-jnp.arange(n_freq, dtype=F32) / n_freq)
    ang = jnp.concatenate([r[:, None] * freqs, col[:, None] * freqs], -1)
    cos, sin = jnp.cos(ang), jnp.sin(ang)
    cos_rep = jnp.repeat(cos, 2, axis=-1)
    sin_alt = jnp.stack([-sin, sin], -1).reshape(S, rot_dim)
    cos_rep = jnp.concatenate([cos_rep, jnp.ones((C, rot_dim), F32)], 0)
    sin_alt = jnp.concatenate([sin_alt, jnp.zeros((C, rot_dim), F32)], 0)
    return cos_rep.T, sin_alt.T


def kernel(x, c, ctx, c_ctx, w_ada, b_ada, w_in, da_lambda, da_subln, mla_q_norm, mla_w_uq, mla_kv_norm, mla_w_ukv, ret_decay_f, ret_decay_b, w_out, ln1_g, ln1_b, router_w, router_b, exp_w1, exp_w3, exp_w2, ln2_g, ln2_b):
    B, S, D = x.shape
    C = ctx.shape[1]
    depth = w_in.shape[0]
    alpha = float((2 * depth) ** 0.25)

    cvec = jnp.concatenate([c, c_ctx[None, :], jnp.zeros((8 - B - 1, D), F32)], 0)
    mod = _modulation(cvec, w_ada, b_ada)
    modr = mod.reshape(depth, 8, N_MOD, 1, D)

    offs = np.cumsum((0,) + IN_SPLIT)
    blk = [w_in[:, :, offs[i]:offs[i + 1]] for i in range(len(IN_SPLIT))]
    w_daq, w_dak, w_dav, w_cq, w_ckv, w_kpe, w_rq, w_rk, w_rv, w_rg = blk
    w_all = jnp.concatenate([w_daq, _pairswap(w_daq), w_dak, _pairswap(w_dak), w_dav, w_cq, w_ckv,
                             w_kpe, _pairswap(w_kpe), w_rq, _pairswap(w_rq), w_rk, _pairswap(w_rk), w_rv, w_rg], -1)
    wT_all = jnp.swapaxes(w_all, 1, 2).astype(BF16)
    uq = mla_w_uq.reshape(depth, MLA_Q_RANK, MLA_HEADS, MLA_NOPE + MLA_ROPE)
    uq_sw = _pairswap(uq[..., MLA_NOPE:]).reshape(depth, MLA_Q_RANK, MLA_HEADS * MLA_ROPE)
    wuqT = jnp.swapaxes(jnp.concatenate([mla_w_uq, uq_sw], -1), 1, 2).astype(BF16)
    ukv = mla_w_ukv.reshape(depth, MLA_KV_RANK, MLA_HEADS, MLA_NOPE + MLA_VDIM)
    wvT = jnp.swapaxes(ukv[..., MLA_NOPE:].reshape(depth, MLA_KV_RANK, MLA_VW), 1, 2).astype(BF16)
    nopeT = jnp.transpose(ukv[..., :MLA_NOPE], (0, 2, 3, 1))
    top = jnp.concatenate([nopeT, jnp.zeros((depth, MLA_HEADS, MLA_NOPE, MLA_ROPE), F32)], -1)
    mid = jnp.concatenate([jnp.zeros((MLA_ROPE, MLA_KV_RANK), F32), jnp.eye(MLA_ROPE, dtype=F32)], -1)
    mid = jnp.broadcast_to(mid, (depth, MLA_HEADS, MLA_ROPE, MLA_KV_RANK + MLA_ROPE))
    bot = jnp.zeros((depth, MLA_HEADS, LANES - MLA_NOPE - MLA_ROPE, MLA_KV_RANK + MLA_ROPE), F32)
    wkT = jnp.concatenate([top, mid, bot], 2).reshape(depth, MLA_HEADS * LANES, MLA_KV_RANK + MLA_ROPE).astype(BF16)
    gq = mla_q_norm[:, :, None]
    gkv = mla_kv_norm[:, :, None]
    subln = da_subln[:, :, None]
    w_out_b = w_out.astype(BF16)
    w13 = jnp.concatenate([exp_w1, exp_w3], -1).astype(BF16)
    w2 = exp_w2.astype(BF16)
    rw_f = jnp.concatenate([router_w, jnp.zeros((D, LANES - N_EXPERTS), F32)], -1)
    rw_hi = rw_f.astype(BF16)
    rw_pad = jnp.concatenate([rw_hi, (rw_f - rw_hi.astype(F32)).astype(BF16)], -1)
    rb = router_b[:, None]
    dec = jnp.broadcast_to(jnp.stack([ret_decay_f, ret_decay_b], 1)[:, :, :, None, None],
                           (depth, 2, RET_HEADS, 8, LANES))
    tabs = _rope_tables(S, C, DA_DIM) + _rope_tables(S, C, RET_DIM)
    ln1g, ln1b, ln2g, ln2b = (a[:, None, :] for a in (ln1_g, ln1_b, ln2_g, ln2_b))

    xc = jnp.concatenate([x, ctx], 1)
    for l in range(depth):
        (daqr, daqf, dak, dav, mqr, mqf, mk, mv, rq, rk, rv, rg) = _in_proj(
            l, xc, modr, wT_all, tabs, gq, wuqT, gkv, wvT, wkT, S)
        yda = _attention("da", l, daqr, daqf, dak, dav, S, C, lam=da_lambda, g=subln)
        ymla = _attention("mla", l, mqr, mqf, mk, mv, S, C)
        yret = _retention(rq, rk, rv, rg, dec[l], S, C)
        x1, h2, logits = _out_proj(l, yda, ymla, yret, w_out_b, xc, modr, ln1g, ln1b, rw_pad, S, alpha)
        xc = _moe(l, h2, logits, rb, w13, w2, x1, modr, ln2g, ln2b, S, alpha)
    return xc[:, :S, :]
```

```python
import functools
import math

import jax
import jax.numpy as jnp
import numpy as np
from jax import lax
from jax.experimental import pallas as pl
from jax.experimental.pallas import tpu as pltpu

F32 = jnp.float32
BF16 = jnp.bfloat16

GRID_W = 64
DA_HEADS, DA_DIM, DA_VDIM = 6, 32, 64
MLA_HEADS, MLA_NOPE, MLA_ROPE, MLA_VDIM = 6, 64, 32, 64
MLA_Q_RANK, MLA_KV_RANK = 256, 128
RET_HEADS, RET_DIM, RET_CHUNK = 4, 64, 128
ROPE_BASE = 10000.0
N_EXPERTS, N_GROUPS, D_EXPERT = 16, 4, 512
ROUTED_SCALE = 1.0
N_MOD = 6
LN_EPS = 1e-5
RMS_EPS = 1e-6
LOG2E = math.log2(math.e)

DA_W = DA_HEADS * 2 * DA_DIM
DA_VW = DA_HEADS * DA_VDIM
RET_W = RET_HEADS * RET_DIM
MLA_QW = MLA_HEADS * (MLA_NOPE + MLA_ROPE)
MLA_VW = MLA_HEADS * MLA_VDIM
IN_SPLIT = (DA_W, DA_W, DA_VW, MLA_Q_RANK, MLA_KV_RANK, MLA_ROPE, RET_W, RET_W, RET_W, RET_W)

LANES = 128
V_ROWS = 80
TM = 256
ATTN_N = 512
TK = 2048
ATTN_SUBTILES = 2
ATTN_NBUF = 2
ATTN_MAX_JUMP = 32.0
ATTN_UNROLL = 2
TM_MOE = 640
TM_MOE_LAST = 1024
RET_UNROLL = 8

NT_DIMS = (((1,), (1,)), ((), ()))
TN_DIMS = (((0,), (0,)), ((), ()))


def _sigmoid(x):
    return 1.0 / (1.0 + jnp.exp(-x))


def _mod_kernel(c_ref, w_ref, b_ref, o_ref):
    c = c_ref[...]
    sc = (c * _sigmoid(c)).astype(BF16)
    o_ref[0] = jnp.dot(sc, w_ref[0].astype(BF16), preferred_element_type=F32) + b_ref[0]


def _modulation(cvec, w_ada, b_ada):
    depth, d, n = w_ada.shape
    tn = 1536
    return pl.pallas_call(
        _mod_kernel,
        grid=(depth, n // tn),
        in_specs=[
            pl.BlockSpec((8, d), lambda l, j: (0, 0)),
            pl.BlockSpec((1, d, tn), lambda l, j: (l, 0, j)),
            pl.BlockSpec((1, 1, tn), lambda l, j: (l, 0, j)),
        ],
        out_specs=pl.BlockSpec((1, 8, tn), lambda l, j: (l, 0, j)),
        out_shape=jax.ShapeDtypeStruct((depth, 8, n), F32),
        name="adaln_mod",
    )(cvec, w_ada, b_ada.reshape(depth, 1, n))


_O_DAQ, _O_DAK, _O_DAV, _O_CQ, _O_CKV, _O_KPE, _O_RQ, _O_RK, _O_RV, _O_RG = (
    int(v) for v in np.cumsum((0,) + IN_SPLIT)[:-1])
N_ZROWS = sum(IN_SPLIT)


def _rms_rows(x):
    return x * lax.rsqrt(jnp.mean(x * x, axis=0, keepdims=True) + RMS_EPS)


def _pairswap_rows(x):
    n = x.shape[0]
    even = (lax.broadcasted_iota(jnp.int32, x.shape, 0) & 1) == 0
    return jnp.where(even, pltpu.roll(x, n - 1, 0), pltpu.roll(x, 1, 0))


def _in_kernel(x_ref, sh_ref, sc_ref, wT_ref, cosA_ref, sinA_ref, cosR_ref, sinR_ref,
               gq_ref, wuqT_ref, gkv_ref, wvT_ref, wkT_ref,
               daqr, daqf, dak, dav, mqr, mqf, mk, mv, rq, rk, rv, rg, z_ref, *, cq_da, cq_mla, kscale):
    tm = x_ref.shape[1]
    x = x_ref[0]
    h = (x * (1.0 + sc_ref[0, 0, 0]) + sh_ref[0, 0, 0]).astype(BF16)
    z_ref[...] = lax.dot_general(wT_ref[0], h, NT_DIMS, preferred_element_type=F32)
    cosA = cosA_ref[...]
    sinA = sinA_ref[...]
    cosR = cosR_ref[...]
    sinR = sinR_ref[...]
    ones_rows = (lax.broadcasted_iota(jnp.int32, (DA_HEADS, V_ROWS - 64, tm), 1) == 0).astype(BF16)

    def rows(off, n):
        return z_ref[off:off + n, :]

    q2 = rows(_O_DAQ, DA_W)
    q = q2.reshape(12, DA_DIM, tm)
    qs = _pairswap_rows(q2).reshape(12, DA_DIM, tm)
    daqr[0] = ((q * cosA + qs * sinA) * cq_da).astype(BF16)
    daqf[0] = (q * cq_da).astype(BF16)
    k2 = rows(_O_DAK, DA_W)
    k = k2.reshape(12, DA_DIM, tm)
    ks = _pairswap_rows(k2).reshape(12, DA_DIM, tm)
    kr = (k * cosA + ks * sinA).reshape(DA_HEADS, 2 * DA_DIM, tm)
    kp = jnp.concatenate([kr, jnp.zeros_like(kr)], axis=1).reshape(DA_HEADS * LANES, tm)
    dak[0] = kp.T.astype(BF16)
    dav[0, :, 0:64, :] = rows(_O_DAV, DA_VW).reshape(DA_HEADS, DA_VDIM, tm).astype(BF16)
    dav[0, :, 64:V_ROWS, :] = ones_rows

    cqn = (_rms_rows(rows(_O_CQ, MLA_Q_RANK)) * gq_ref[0]).astype(BF16)
    qm = jnp.dot(wuqT_ref[0], cqn, preferred_element_type=F32)
    qh = qm.reshape(MLA_HEADS, MLA_NOPE + MLA_ROPE, tm)
    qsw = _pairswap_rows(qm).reshape(MLA_HEADS, MLA_NOPE + MLA_ROPE, tm)[:, MLA_NOPE:, :]
    rot = qh[:, MLA_NOPE:, :] * cosA + qsw * sinA
    mqr[0, :, 0:MLA_NOPE, :] = (qh[:, 0:MLA_NOPE, :] * cq_mla).astype(BF16)
    mqr[0, :, MLA_NOPE:, :] = (rot * cq_mla).astype(BF16)
    mqf[0] = (qh * cq_mla).astype(BF16)
    ckvn = (_rms_rows(rows(_O_CKV, MLA_KV_RANK)) * gkv_ref[0]).astype(BF16)
    vm = jnp.dot(wvT_ref[0], ckvn, preferred_element_type=F32)
    mv[0, :, 0:64, :] = vm.reshape(MLA_HEADS, MLA_VDIM, tm).astype(BF16)
    mv[0, :, 64:V_ROWS, :] = ones_rows
    kpe = rows(_O_KPE, MLA_ROPE)
    kper = kpe * cosA + _pairswap_rows(kpe) * sinA
    kin = jnp.concatenate([ckvn, kper.astype(BF16)], axis=0)
    kmT = jnp.dot(wkT_ref[0], kin, preferred_element_type=F32)
    mk[0] = kmT.T.astype(BF16)

    r_q2 = rows(_O_RQ, RET_W)
    r_q = r_q2.reshape(RET_HEADS, RET_DIM, tm)
    r_qs = _pairswap_rows(r_q2).reshape(RET_HEADS, RET_DIM, tm)
    rq[0] = (r_q * cosR + r_qs * sinR).astype(BF16)
    r_k2 = rows(_O_RK, RET_W)
    r_k = r_k2.reshape(RET_HEADS, RET_DIM, tm)
    r_ks = _pairswap_rows(r_k2).reshape(RET_HEADS, RET_DIM, tm)
    rkr = (r_k * cosR + r_ks * sinR) * kscale
    rkp = jnp.concatenate([rkr, jnp.zeros_like(rkr)], axis=1).reshape(RET_HEADS * LANES, tm)
    rk[0] = rkp.T.astype(BF16)
    rv[0] = rows(_O_RV, RET_W).reshape(RET_HEADS, RET_DIM, tm).astype(BF16)
    rg[0] = rows(_O_RG, RET_W)


def _in_proj(l, xc, modr, wT_all, tabs, gq, wuqT, gkv, wvT, wkT, S):
    B, Sp, D = xc.shape
    nt = Sp // TM
    n_lat_tiles = S // TM
    cosA, sinA, cosR, sinR = tabs

    def mod_spec(j):
        return pl.BlockSpec((1, 1, 1, 1, D), lambda b, t: (l, jnp.where(t >= n_lat_tiles, B, b), j, 0, 0))

    def wspec(a):
        return pl.BlockSpec((1,) + a.shape[1:], lambda b, t: (l,) + (0,) * (a.ndim - 1))

    def fm4(h, d):
        return pl.BlockSpec((1, h, d, TM), lambda b, t: (b, 0, 0, t))

    def tmaj(w):
        return pl.BlockSpec((1, TM, w), lambda b, t: (b, t, 0))

    kern = functools.partial(
        _in_kernel,
        cq_da=float(DA_DIM ** -0.5 * LOG2E),
        cq_mla=float((MLA_NOPE + MLA_ROPE) ** -0.5 * LOG2E),
        kscale=float(RET_DIM ** -0.5),
    )
    sds = jax.ShapeDtypeStruct
    return pl.pallas_call(
        kern,
        grid=(B, nt),
        in_specs=[
            pl.BlockSpec((1, TM, D), lambda b, t: (b, t, 0)),
            mod_spec(0), mod_spec(1),
            wspec(wT_all),
            pl.BlockSpec((DA_DIM, TM), lambda b, t: (0, t)),
            pl.BlockSpec((DA_DIM, TM), lambda b, t: (0, t)),
            pl.BlockSpec((RET_DIM, TM), lambda b, t: (0, t)),
            pl.BlockSpec((RET_DIM, TM), lambda b, t: (0, t)),
            wspec(gq), wspec(wuqT), wspec(gkv), wspec(wvT), wspec(wkT),
        ],
        out_specs=[
            fm4(12, DA_DIM), fm4(12, DA_DIM), tmaj(DA_HEADS * LANES), fm4(DA_HEADS, V_ROWS),
            fm4(MLA_HEADS, 96), fm4(MLA_HEADS, 96), tmaj(MLA_HEADS * LANES), fm4(MLA_HEADS, V_ROWS),
            fm4(RET_HEADS, RET_DIM), tmaj(RET_HEADS * LANES), fm4(RET_HEADS, RET_DIM),
            pl.BlockSpec((1, RET_W, TM), lambda b, t: (b, 0, t)),
        ],
        out_shape=[
            sds((B, 12, DA_DIM, Sp), BF16), sds((B, 12, DA_DIM, Sp), BF16),
            sds((B, Sp, DA_HEADS * LANES), BF16), sds((B, DA_HEADS, V_ROWS, Sp), BF16),
            sds((B, MLA_HEADS, 96, Sp), BF16), sds((B, MLA_HEADS, 96, Sp), BF16),
            sds((B, Sp, MLA_HEADS * LANES), BF16), sds((B, MLA_HEADS, V_ROWS, Sp), BF16),
            sds((B, RET_HEADS, RET_DIM, Sp), BF16), sds((B, Sp, RET_HEADS * LANES), BF16),
            sds((B, RET_HEADS, RET_DIM, Sp), BF16), sds((B, RET_W, Sp), F32),
        ],
        scratch_shapes=[pltpu.VMEM((N_ZROWS, TM), F32)],
        compiler_params=pltpu.CompilerParams(
            dimension_semantics=("arbitrary", "arbitrary"), vmem_limit_bytes=56 * 1024 * 1024),
        name="in_proj",
    )(xc, modr, modr, wT_all, cosA, sinA, cosR, sinR, gq, wuqT, gkv, wvT, wkT)


def _attn_kernel(*refs, mode, S, C, lam_init, ctx_only, nsub):
    refs = list(refs)
    qr_ref, qf_ref, k_ref, v_ref = refs[:4]
    del refs[:4]
    if mode == "da":
        lam_ref, g_ref = refs[:2]
        del refs[:2]
    if ctx_only:
        del refs[:1]
    o_ref, m_sc, acc_sc, q_sc, jump_sc = refs[:5]
    nb = ATTN_NBUF
    pbuf, albuf = (refs[5 + i * nb: 5 + (i + 1) * nb] for i in range(2))
    tq = o_ref.shape[2] // nsub
    n_lat = S // TK

    def qpad(ref, t):
        cols = slice(t * tq, (t + 1) * tq)
        if mode == "da":
            z = jnp.zeros((DA_DIM, tq), BF16)
            c1 = jnp.concatenate([ref[0, 0, :, cols], z, z, z], axis=0)
            c2 = jnp.concatenate([z, ref[0, 1, :, cols], z, z], axis=0)
            return jnp.concatenate([c1, c2], axis=1)
        return jnp.concatenate([ref[0, 0, :, cols], jnp.zeros((LANES - 96, tq), BF16)], axis=0)

    def chunk(j):
        return pl.ds(j * TK if isinstance(j, int) else pl.multiple_of(j * TK, TK), TK)

    def context_keys(t):
        c0 = 0 if ctx_only else S
        s = jnp.dot(k_ref[0, c0:c0 + C, :], qpad(qf_ref, t), preferred_element_type=F32)
        m_c = jnp.max(s, axis=0, keepdims=True)
        m_sc[t] = m_c
        acc_sc[t] = jnp.dot(v_ref[0, 0, :, c0:c0 + C], jnp.exp2(s - m_c).astype(BF16), preferred_element_type=F32)

    for t in range(nsub):
        context_keys(t)

    if not ctx_only:
        for t in range(nsub):
            q_sc[t] = qpad(qr_ref, t)
        jump_sc[...] = jnp.full(jump_sc.shape, -jnp.inf, F32)

        def probs(t, j, r):
            s = jnp.dot(k_ref[0, chunk(j), :], q_sc[t], preferred_element_type=F32)
            m_prev = m_sc[t]
            pbuf[r][t] = jnp.exp2(s - m_prev).astype(BF16)
            m_chunk = jnp.max(s, axis=0, keepdims=True)
            m_new = jnp.maximum(m_prev, m_chunk)
            jump_sc[t] = jnp.maximum(jump_sc[t], m_chunk - m_prev)
            albuf[r][t] = jnp.exp2(m_prev - m_new)
            m_sc[t] = m_new

        def accumulate(t, j, r):
            pv = jnp.dot(v_ref[0, 0, :, chunk(j)], pbuf[r][t], preferred_element_type=F32)
            acc_sc[t] = (acc_sc[t] + pv) * albuf[r][t]

        def step(j, r):
            for t in range(nsub):
                accumulate(t, j, r)
                probs(t, j + 1, (r + 1) % nb)

        for t in range(nsub):
            probs(t, 0, 0)
        n_steps = n_lat - 1
        trips = n_steps // ATTN_UNROLL

        def body(i, carry):
            for u in range(ATTN_UNROLL):
                step(i * ATTN_UNROLL + u, u % nb)
            return carry

        lax.fori_loop(0, trips, body, 0)
        for j in range(trips * ATTN_UNROLL, n_steps):
            step(j, j % nb)
        for t in range(nsub):
            accumulate(t, n_lat - 1, (n_lat - 1) % nb)

        @pl.when(jnp.max(jump_sc[...]) > ATTN_MAX_JUMP)
        def _():
            for t in range(nsub):
                context_keys(t)

                def exact_step(j, carry, t=t):
                    s = jnp.dot(k_ref[0, chunk(j), :], q_sc[t], preferred_element_type=F32)
                    m_old = m_sc[t]
                    m_new = jnp.maximum(m_old, jnp.max(s, axis=0, keepdims=True))
                    pv = jnp.dot(v_ref[0, 0, :, chunk(j)], jnp.exp2(s - m_new).astype(BF16),
                                 preferred_element_type=F32)
                    acc_sc[t] = acc_sc[t] * jnp.exp2(m_old - m_new) + pv
                    m_sc[t] = m_new
                    return carry

                lax.fori_loop(0, n_lat, exact_step, 0)

    for t in range(nsub):
        acc = acc_sc[t]
        o = acc[0:64, :] / acc[64:65, :]
        cols = slice(t * tq, (t + 1) * tq)
        if mode == "da":
            lf = lam_ref[0]
            lam = (jnp.exp(jnp.sum(lf[0:1] * lf[1:2], axis=1, keepdims=True))
                   - jnp.exp(jnp.sum(lf[2:3] * lf[3:4], axis=1, keepdims=True)) + lam_init)
            od = o[:, 0:tq] - lam * o[:, tq:2 * tq]
            y = _rms_rows(od) * g_ref[0]
            o_ref[0, :, cols] = (y * (1.0 - lam_init)).astype(BF16)
        else:
            o_ref[0, :, cols] = o.astype(BF16)


def _attention_call(mode, l, qr, qf, k, v, S, C, lam, g, y_latent):
    B, nmaps, d, Sp = qr.shape
    H = v.shape[1]
    nm = nmaps // H
    ctx_only = y_latent is not None
    nsub = 1 if ctx_only else ATTN_SUBTILES
    tq = C if ctx_only else ATTN_N // nm
    n = nm * tq
    tqb = nsub * tq
    q0 = S // tqb if ctx_only else 0
    lam_init = 0.8 - 0.6 * math.exp(-0.3 * l)
    kern = functools.partial(_attn_kernel, mode=mode, S=S, C=C, lam_init=lam_init, ctx_only=ctx_only, nsub=nsub)
    in_specs = [
        pl.BlockSpec((1, nm, d, tqb), lambda b, h, i: (b, h, 0, q0 + i)),
        pl.BlockSpec((1, nm, d, tqb), lambda b, h, i: (b, h, 0, q0 + i)),
        (pl.BlockSpec((1, C, LANES), lambda b, h, i: (b, S // C, h)) if ctx_only
         else pl.BlockSpec((1, Sp, LANES), lambda b, h, i: (b, 0, h))),
        (pl.BlockSpec((1, 1, V_ROWS, C), lambda b, h, i: (b, h, 0, S // C)) if ctx_only
         else pl.BlockSpec((1, 1, V_ROWS, Sp), lambda b, h, i: (b, h, 0, 0))),
    ]
    args = [qr, qf, k, v]
    if mode == "da":
        in_specs += [pl.BlockSpec((1,) + lam.shape[1:], lambda b, h, i: (l, 0, 0)),
                     pl.BlockSpec((1,) + g.shape[1:], lambda b, h, i: (l, 0, 0))]
        args += [lam, g]
    aliases = {}
    if ctx_only:
        in_specs.append(pl.BlockSpec(memory_space=pl.ANY))
        args.append(y_latent)
        aliases = {len(args) - 1: 0}
    tk = C if ctx_only else TK
    return pl.pallas_call(
        kern,
        grid=(B, H, 1 if ctx_only else S // tqb),
        in_specs=in_specs,
        out_specs=pl.BlockSpec((1, 64, tqb), lambda b, h, i: (b, h, q0 + i)),
        out_shape=jax.ShapeDtypeStruct((B, H * 64, Sp), BF16),
        input_output_aliases=aliases,
        scratch_shapes=(
            [pltpu.VMEM((nsub, 1, n), F32), pltpu.VMEM((nsub, V_ROWS, n), F32), pltpu.VMEM((nsub, LANES, n), BF16),
             pltpu.VMEM((nsub, 1, n), F32)]
            + [pltpu.VMEM((nsub, tk, n), BF16)] * ATTN_NBUF + [pltpu.VMEM((nsub, 1, n), F32)] * ATTN_NBUF),
        compiler_params=pltpu.CompilerParams(
            dimension_semantics=("arbitrary", "arbitrary", "arbitrary"), vmem_limit_bytes=48 * 1024 * 1024),
        name="attn_" + mode + ("_ctx" if ctx_only else ""),
    )(*args)


def _attention(mode, l, qr, qf, k, v, S, C, with_ctx, lam=None, g=None):
    y = _attention_call(mode, l, qr, qf, k, v, S, C, lam, g, None)
    return _attention_call(mode, l, qr, qf, k, v, S, C, lam, g, y) if with_ctx else y


def _ret_kernel(q_ref, k_ref, v_ref, g_ref, dec_ref, o_ref, sb_sc, *, S, C):
    CH = RET_CHUNK
    nl, nc = S // CH, C // CH

    def log_sigmoid(x):
        return jnp.minimum(x, 0.0) - jnp.log1p(jnp.exp(-jnp.abs(x)))

    lgf = log_sigmoid(dec_ref[0, 0][0:1, :])
    lgb = log_sigmoid(dec_ref[1, 0][0:1, :])
    ii = lax.broadcasted_iota(jnp.int32, (CH, CH), 1).astype(F32)
    jj = lax.broadcasted_iota(jnp.int32, (CH, CH), 0).astype(F32)
    dd = ii - jj
    fwd = dd >= 0
    mt = (jnp.where(fwd, jnp.exp(lgf * jnp.where(fwd, dd, 0.0)), 0.0)
          + jnp.where(fwd, 0.0, jnp.exp(lgb * jnp.where(fwd, 0.0, -dd))))
    xi_f = jnp.exp(lgf * (ii + 1.0))
    xi_b = jnp.exp(lgb * (CH - ii))
    zeta_f = jnp.exp(lgf * (CH - 1.0 - jj))
    zeta_b = jnp.exp(lgb * jj)
    cdf = jnp.exp(lgf * CH)
    cdb = jnp.exp(lgb * CH)
    zq = jnp.zeros((RET_DIM, CH), BF16)

    def sl(c):
        return pl.ds(pl.multiple_of(c * CH, CH), CH)

    def state_update(c, zeta):
        kz = (k_ref[0, sl(c), :].astype(F32) * zeta).astype(BF16)
        return jnp.dot(v_ref[0, 0, :, sl(c)], kz, preferred_element_type=F32)

    def bwd_step(c, sb):
        sb_sc[c] = sb
        return sb * cdb + state_update(c, zeta_b)

    sb = jnp.zeros((RET_DIM, CH), F32)
    for c in range(nl + nc - 1, nl - 1, -1):
        sb = bwd_step(c, sb)
    lax.fori_loop(0, nl, lambda n, s: bwd_step(nl - 1 - n, s), sb, unroll=RET_UNROLL)

    def fwd_step(c, sf):
        qc = q_ref[0, 0, :, sl(c)]
        qp = jnp.concatenate([qc, zq], axis=0)
        kc = k_ref[0, sl(c), :]
        vc = v_ref[0, 0, :, sl(c)]
        at = jnp.dot(kc, qp, preferred_element_type=F32)
        qpf = qp.astype(F32)
        rhs = jnp.concatenate([(at * mt).astype(BF16), (qpf * xi_f).astype(BF16), (qpf * xi_b).astype(BF16)], axis=0)
        lhs = jnp.concatenate([vc, sf.astype(BF16), sb_sc[c].astype(BF16)], axis=1)
        o = jnp.dot(lhs, rhs, preferred_element_type=F32)
        gch = g_ref[0, :, sl(c)]
        o_ref[0, :, sl(c)] = (gch * _sigmoid(gch) * _rms_rows(o)).astype(BF16)
        return sf * cdf + state_update(c, zeta_f)

    sf = jnp.zeros((RET_DIM, CH), F32)
    for c in range(nl, nl + nc):
        sf = fwd_step(c, sf)
    lax.fori_loop(0, nl, fwd_step, sf, unroll=RET_UNROLL)


def _retention(rq, rk, rv, rg, dec, S, C):
    B, H, d, Sp = rq.shape
    kern = functools.partial(_ret_kernel, S=S, C=C)
    return pl.pallas_call(
        kern,
        grid=(B, H),
        in_specs=[
            pl.BlockSpec((1, 1, d, Sp), lambda b, h: (b, h, 0, 0)),
            pl.BlockSpec((1, Sp, LANES), lambda b, h: (b, 0, h)),
            pl.BlockSpec((1, 1, d, Sp), lambda b, h: (b, h, 0, 0)),
            pl.BlockSpec((1, d, Sp), lambda b, h: (b, h, 0)),
            pl.BlockSpec((2, 1, 8, LANES), lambda b, h: (0, h, 0, 0)),
        ],
        out_specs=pl.BlockSpec((1, d, Sp), lambda b, h: (b, h, 0)),
        out_shape=jax.ShapeDtypeStruct((B, H * d, Sp), BF16),
        scratch_shapes=[pltpu.VMEM((Sp // RET_CHUNK, RET_DIM, RET_CHUNK), F32)],
        compiler_params=pltpu.CompilerParams(
            dimension_semantics=("arbitrary", "arbitrary"), vmem_limit_bytes=56 * 1024 * 1024),
        name="retention",
    )(rq, rk, rv, rg, dec)


def _layer_norm(t, g, b):
    mu = jnp.mean(t, axis=-1, keepdims=True)
    tc = t - mu
    var = jnp.mean(tc * tc, axis=-1, keepdims=True)
    return tc * lax.rsqrt(var + LN_EPS) * g + b


def _out_kernel(yda_ref, ymla_ref, yret_ref, w_ref, x_ref, gate_ref, g1_ref, b1_ref, sh2_ref, sc2_ref, rw_ref,
                x1_ref, h2_ref, lg_ref, *, alpha):
    proj = (lax.dot_general(yda_ref[0], w_ref[0, 0:DA_VW, :], TN_DIMS, preferred_element_type=F32)
            + lax.dot_general(ymla_ref[0], w_ref[0, DA_VW:DA_VW + MLA_VW, :], TN_DIMS, preferred_element_type=F32)
            + lax.dot_general(yret_ref[0], w_ref[0, DA_VW + MLA_VW:, :], TN_DIMS, preferred_element_type=F32))
    x1 = _layer_norm(alpha * x_ref[0] + gate_ref[0, 0, 0] * proj, g1_ref[0], b1_ref[0])
    x1_ref[0] = x1
    h2 = x1 * (1.0 + sc2_ref[0, 0, 0]) + sh2_ref[0, 0, 0]
    h2_ref[0] = h2.astype(BF16)
    h2_hi = h2.astype(BF16)
    h2_lo = (h2 - h2_hi.astype(F32)).astype(BF16)
    hi = jnp.dot(h2_hi, rw_ref[...], preferred_element_type=F32)
    lo = jnp.dot(h2_lo, rw_ref[:, 0:LANES], preferred_element_type=F32)
    lg_ref[0] = hi[:, 0:LANES] + (hi[:, LANES:] + lo)


def _out_proj(l, yda, ymla, yret, w_out, xc, modr, ln_g, ln_b, rw_pad, S, alpha, latents_only):
    B, Sp, D = xc.shape
    n_lat_tiles = S // TM
    n_rows = S if latents_only else Sp

    def mod_spec(j):
        return pl.BlockSpec((1, 1, 1, 1, D), lambda b, t: (l, jnp.where(t >= n_lat_tiles, B, b), j, 0, 0))

    def fm(a):
        return pl.BlockSpec((1, a.shape[1], TM), lambda b, t: (b, 0, t))

    def wspec(a):
        return pl.BlockSpec((1,) + a.shape[1:], lambda b, t: (l,) + (0,) * (a.ndim - 1))

    row = pl.BlockSpec((1, TM, D), lambda b, t: (b, t, 0))
    sds = jax.ShapeDtypeStruct
    return pl.pallas_call(
        functools.partial(_out_kernel, alpha=alpha),
        grid=(B, n_rows // TM),
        in_specs=[fm(yda), fm(ymla), fm(yret), wspec(w_out), row, mod_spec(2), wspec(ln_g), wspec(ln_b),
                  mod_spec(3), mod_spec(4), pl.BlockSpec(rw_pad.shape, lambda b, t: (0, 0))],
        out_specs=[row, row, pl.BlockSpec((1, TM, LANES), lambda b, t: (b, t, 0))],
        out_shape=[sds((B, n_rows, D), F32), sds((B, n_rows, D), BF16), sds((B, n_rows, LANES), F32)],
        compiler_params=pltpu.CompilerParams(dimension_semantics=("arbitrary", "arbitrary")),
        name="out_proj",
    )(yda, ymla, yret, w_out, xc, modr, ln_g, ln_b, modr, modr, rw_pad)


def _gates_T(logits_T, rb):
    s = _sigmoid(logits_T)
    ch = s + rb
    srow = [s[e:e + 1] for e in range(N_EXPERTS)]
    crow = [ch[e:e + 1] for e in range(N_EXPERTS)]
    per = N_EXPERTS // N_GROUPS
    gs = []
    for g in range(N_GROUPS):
        a, b, c, d = crow[per * g: per * g + per]
        m1, n1, m2, n2 = jnp.maximum(a, b), jnp.minimum(a, b), jnp.maximum(c, d), jnp.minimum(c, d)
        gs.append(jnp.maximum(m1, m2) + jnp.maximum(jnp.minimum(m1, m2), jnp.maximum(n1, n2)))
    gmax = functools.reduce(jnp.maximum, gs)
    taken = jnp.zeros(gmax.shape, jnp.bool_)
    gsel = []
    for g in range(N_GROUPS):
        sg = jnp.logical_and(gs[g] == gmax, jnp.logical_not(taken))
        gsel.append(sg)
        taken = jnp.logical_or(taken, sg)
    neg = jnp.full(gmax.shape, -jnp.inf, F32)
    mc = [jnp.where(gsel[e // per], crow[e], neg) for e in range(N_EXPERTS)]
    sel = [jnp.zeros(gmax.shape, jnp.bool_) for _ in range(N_EXPERTS)]
    for _ in range(2):
        top = functools.reduce(jnp.maximum, mc)
        taken = jnp.zeros(gmax.shape, jnp.bool_)
        for e in range(N_EXPERTS):
            hit = jnp.logical_and(mc[e] == top, jnp.logical_not(taken))
            taken = jnp.logical_or(taken, hit)
            sel[e] = jnp.logical_or(sel[e], hit)
            mc[e] = jnp.where(hit, neg, mc[e])
    w = [jnp.where(sel[e], srow[e], 0.0) for e in range(N_EXPERTS)]
    wsum = functools.reduce(lambda a, b: a + b, w)
    return jnp.concatenate([we / wsum * ROUTED_SCALE for we in w], axis=0)


def _moe_kernel(h_ref, lg_ref, rb_ref, w13_ref, w2_ref, x1_ref, gb_ref, gc_ref, g2_ref, b2_ref, o_ref,
                gates_sc, acc_sc, *, S, alpha):
    e = pl.program_id(2)
    tm = h_ref.shape[1]

    @pl.when(e == 0)
    def _():
        gT = _gates_T(lg_ref[0].T[0:N_EXPERTS, :], rb_ref[...])
        gates_sc[...] = jnp.concatenate([gT, jnp.zeros((LANES - N_EXPERTS, tm), F32)], axis=0).T
        acc_sc[...] = jnp.zeros(acc_sc.shape, F32)

    ab = jnp.dot(h_ref[0], w13_ref[0, 0], preferred_element_type=F32)
    a = ab[:, 0:D_EXPERT]
    hid = (a * _sigmoid(a) * ab[:, D_EXPERT:]).astype(BF16)
    y = jnp.dot(hid, w2_ref[0, 0], preferred_element_type=F32)
    lane = lax.broadcasted_iota(jnp.int32, (tm, LANES), 1)
    ge = jnp.sum(jnp.where(lane == e, gates_sc[...], 0.0), axis=1, keepdims=True)
    acc_sc[...] += ge * y

    @pl.when(e == N_EXPERTS - 1)
    def _():
        n_lat = S - pl.program_id(1) * tm
        rowi = lax.broadcasted_iota(jnp.int32, (tm, 1), 0)
        gate = jnp.where(rowi < n_lat, gb_ref[0, 0, 0], gc_ref[0, 0, 0])
        o_ref[0] = _layer_norm(alpha * x1_ref[0] + gate * acc_sc[...], g2_ref[0], b2_ref[0])


def _moe(l, h2, logits, rb, w13, w2, x1, modr, ln_g, ln_b, S, alpha, latents_only):
    B, Sp, D = x1.shape
    tm = TM_MOE_LAST if latents_only else TM_MOE
    n_rows = S if latents_only else Sp

    def wspec(a):
        return pl.BlockSpec((1,) + a.shape[1:], lambda b, t, e: (l,) + (0,) * (a.ndim - 1))

    row = lambda b, t, e: (b, t, 0)
    return pl.pallas_call(
        functools.partial(_moe_kernel, S=S, alpha=alpha),
        grid=(B, n_rows // tm, N_EXPERTS),
        in_specs=[
            pl.BlockSpec((1, tm, D), row),
            pl.BlockSpec((1, tm, LANES), row),
            pl.BlockSpec(rb.shape, lambda b, t, e: (0, 0)),
            pl.BlockSpec((1, 1, D, 2 * D_EXPERT), lambda b, t, e: (l, e, 0, 0)),
            pl.BlockSpec((1, 1, D_EXPERT, D), lambda b, t, e: (l, e, 0, 0)),
            pl.BlockSpec((1, tm, D), row),
            pl.BlockSpec((1, 1, 1, 1, D), lambda b, t, e: (l, b, 5, 0, 0)),
            pl.BlockSpec((1, 1, 1, 1, D), lambda b, t, e: (l, B, 5, 0, 0)),
            wspec(ln_g), wspec(ln_b),
        ],
        out_specs=pl.BlockSpec((1, tm, D), row),
        out_shape=jax.ShapeDtypeStruct((B, n_rows, D), F32),
        scratch_shapes=[pltpu.VMEM((tm, LANES), F32), pltpu.VMEM((tm, D), F32)],
        compiler_params=pltpu.CompilerParams(
            dimension_semantics=("arbitrary", "arbitrary", "arbitrary"), vmem_limit_bytes=56 * 1024 * 1024),
        name="moe",
    )(h2, logits, rb, w13, w2, x1, modr, modr, ln_g, ln_b)


def _rope_tables(S, C, rot_dim):
    rows = S // GRID_W
    r = jnp.repeat(jnp.arange(rows, dtype=F32), GRID_W)
    col = jnp.tile(jnp.arange(GRID_W, dtype=F32), rows)
    n_freq = rot_dim // 4
    freqs = ROPE_BASE ** (-jnp.arange(n_freq, dtype=F32) / n_freq)
    ang = jnp.concatenate([r[:, None] * freqs, col[:, None] * freqs], -1)
    cos, sin = jnp.cos(ang), jnp.sin(ang)
    cos_rep = jnp.repeat(cos, 2, axis=-1)
    sin_alt = jnp.stack([-sin, sin], -1).reshape(S, rot_dim)
    cos_rep = jnp.concatenate([cos_rep, jnp.ones((C, rot_dim), F32)], 0)
    sin_alt = jnp.concatenate([sin_alt, jnp.zeros((C, rot_dim), F32)], 0)
    return cos_rep.T, sin_alt.T


def kernel(x, c, ctx, c_ctx, w_ada, b_ada, w_in, da_lambda, da_subln, mla_q_norm, mla_w_uq, mla_kv_norm, mla_w_ukv, ret_decay_f, ret_decay_b, w_out, ln1_g, ln1_b, router_w, router_b, exp_w1, exp_w3, exp_w2, ln2_g, ln2_b):
    B, S, D = x.shape
    C = ctx.shape[1]
    depth = w_in.shape[0]
    alpha = float((2 * depth) ** 0.25)

    cvec = jnp.concatenate([c, c_ctx[None, :], jnp.zeros((8 - B - 1, D), F32)], 0)
    mod = _modulation(cvec, w_ada, b_ada)
    modr = mod.reshape(depth, 8, N_MOD, 1, D)

    wT_all = jnp.swapaxes(w_in, 1, 2).astype(BF16)
    wuqT = jnp.swapaxes(mla_w_uq, 1, 2).astype(BF16)
    ukv = mla_w_ukv.reshape(depth, MLA_KV_RANK, MLA_HEADS, MLA_NOPE + MLA_VDIM)
    wvT = jnp.swapaxes(ukv[..., MLA_NOPE:].reshape(depth, MLA_KV_RANK, MLA_VW), 1, 2).astype(BF16)
    nopeT = jnp.transpose(ukv[..., :MLA_NOPE], (0, 2, 3, 1))
    top = jnp.concatenate([nopeT, jnp.zeros((depth, MLA_HEADS, MLA_NOPE, MLA_ROPE), F32)], -1)
    mid = jnp.concatenate([jnp.zeros((MLA_ROPE, MLA_KV_RANK), F32), jnp.eye(MLA_ROPE, dtype=F32)], -1)
    mid = jnp.broadcast_to(mid, (depth, MLA_HEADS, MLA_ROPE, MLA_KV_RANK + MLA_ROPE))
    bot = jnp.zeros((depth, MLA_HEADS, LANES - MLA_NOPE - MLA_ROPE, MLA_KV_RANK + MLA_ROPE), F32)
    wkT = jnp.concatenate([top, mid, bot], 2).reshape(depth, MLA_HEADS * LANES, MLA_KV_RANK + MLA_ROPE).astype(BF16)
    gq = mla_q_norm[:, :, None]
    gkv = mla_kv_norm[:, :, None]
    subln = da_subln[:, :, None]
    w_out_b = w_out.astype(BF16)
    w13 = jnp.concatenate([exp_w1, exp_w3], -1).astype(BF16)
    w2 = exp_w2.astype(BF16)
    rw_f = jnp.concatenate([router_w, jnp.zeros((D, LANES - N_EXPERTS), F32)], -1)
    rw_hi = rw_f.astype(BF16)
    rw_pad = jnp.concatenate([rw_hi, (rw_f - rw_hi.astype(F32)).astype(BF16)], -1)
    rb = router_b[:, None]
    dec = jnp.broadcast_to(jnp.stack([ret_decay_f, ret_decay_b], 1)[:, :, :, None, None],
                           (depth, 2, RET_HEADS, 8, LANES))
    tabs = _rope_tables(S, C, DA_DIM) + _rope_tables(S, C, RET_DIM)
    ln1g, ln1b, ln2g, ln2b = (a[:, None, :] for a in (ln1_g, ln1_b, ln2_g, ln2_b))

    xc = jnp.concatenate([x, ctx], 1)
    for l in range(depth):
        last = l == depth - 1
        (daqr, daqf, dak, dav, mqr, mqf, mk, mv, rq, rk, rv, rg) = _in_proj(
            l, xc, modr, wT_all, tabs, gq, wuqT, gkv, wvT, wkT, S)
        yda = _attention("da", l, daqr, daqf, dak, dav, S, C, not last, lam=da_lambda, g=subln)
        ymla = _attention("mla", l, mqr, mqf, mk, mv, S, C, not last)
        yret = _retention(rq, rk, rv, rg, dec[l], S, C)
        x1, h2, logits = _out_proj(l, yda, ymla, yret, w_out_b, xc, modr, ln1g, ln1b, rw_pad, S, alpha, last)
        xc = _moe(l, h2, logits, rb, w13, w2, x1, modr, ln2g, ln2b, S, alpha, last)
    return xc
```

```python
import functools
import math

import jax
import jax.numpy as jnp
import numpy as np
from jax import lax
from jax.experimental import pallas as pl
from jax.experimental.pallas import tpu as pltpu

F32 = jnp.float32
BF16 = jnp.bfloat16

GRID_W = 64
DA_HEADS, DA_DIM, DA_VDIM = 6, 32, 64
MLA_HEADS, MLA_NOPE, MLA_ROPE, MLA_VDIM = 6, 64, 32, 64
MLA_Q_RANK, MLA_KV_RANK = 256, 128
RET_HEADS, RET_DIM, RET_CHUNK = 4, 64, 128
ROPE_BASE = 10000.0
N_EXPERTS, N_GROUPS, D_EXPERT = 16, 4, 512
ROUTED_SCALE = 1.0
N_MOD = 6
LN_EPS = 1e-5
RMS_EPS = 1e-6
LOG2E = math.log2(math.e)

DA_W = DA_HEADS * 2 * DA_DIM
DA_VW = DA_HEADS * DA_VDIM
RET_W = RET_HEADS * RET_DIM
MLA_QW = MLA_HEADS * (MLA_NOPE + MLA_ROPE)
MLA_VW = MLA_HEADS * MLA_VDIM
IN_SPLIT = (DA_W, DA_W, DA_VW, MLA_Q_RANK, MLA_KV_RANK, MLA_ROPE, RET_W, RET_W, RET_W, RET_W)

LANES = 128
V_ROWS = 80
TM = 256
ATTN_N = 512
TK = 2048
ATTN_SUBTILES = 2
ATTN_NBUF = 2
ATTN_MAX_JUMP = 32.0
ATTN_UNROLL = 2
TM_MOE = 640
TM_MOE_LAST = 512
RET_UNROLL = 8

NT_DIMS = (((1,), (1,)), ((), ()))
TN_DIMS = (((0,), (0,)), ((), ()))


def _sigmoid(x):
    return 1.0 / (1.0 + jnp.exp(-x))


def _mod_kernel(c_ref, w_ref, b_ref, o_ref):
    c = c_ref[...]
    sc = (c * _sigmoid(c)).astype(BF16)
    o_ref[0] = jnp.dot(sc, w_ref[0].astype(BF16), preferred_element_type=F32) + b_ref[0]


def _modulation(cvec, w_ada, b_ada):
    depth, d, n = w_ada.shape
    tn = 1536
    return pl.pallas_call(
        _mod_kernel,
        grid=(depth, n // tn),
        in_specs=[
            pl.BlockSpec((8, d), lambda l, j: (0, 0)),
            pl.BlockSpec((1, d, tn), lambda l, j: (l, 0, j)),
            pl.BlockSpec((1, 1, tn), lambda l, j: (l, 0, j)),
        ],
        out_specs=pl.BlockSpec((1, 8, tn), lambda l, j: (l, 0, j)),
        out_shape=jax.ShapeDtypeStruct((depth, 8, n), F32),
        name="adaln_mod",
    )(cvec, w_ada, b_ada.reshape(depth, 1, n))


_O_DAQ, _O_DAK, _O_DAV, _O_CQ, _O_CKV, _O_KPE, _O_RQ, _O_RK, _O_RV, _O_RG = (
    int(v) for v in np.cumsum((0,) + IN_SPLIT)[:-1])
N_ZROWS = sum(IN_SPLIT)


def _rms_rows(x):
    return x * lax.rsqrt(jnp.mean(x * x, axis=0, keepdims=True) + RMS_EPS)


def _pairswap_rows(x):
    n = x.shape[0]
    even = (lax.broadcasted_iota(jnp.int32, x.shape, 0) & 1) == 0
    return jnp.where(even, pltpu.roll(x, n - 1, 0), pltpu.roll(x, 1, 0))


def _in_kernel(x_ref, sh_ref, sc_ref, wT_ref, cosA_ref, sinA_ref, cosR_ref, sinR_ref,
               gq_ref, wuqT_ref, gkv_ref, wvT_ref, wkT_ref,
               daqr, daqf, dak, dav, mqr, mqf, mk, mv, rq, rk, rv, rg, z_ref, *, cq_da, cq_mla, kscale):
    tm = x_ref.shape[1]
    x = x_ref[0]
    h = (x * (1.0 + sc_ref[0, 0, 0]) + sh_ref[0, 0, 0]).astype(BF16)
    z_ref[...] = lax.dot_general(wT_ref[0], h, NT_DIMS, preferred_element_type=F32)
    cosA = cosA_ref[...]
    sinA = sinA_ref[...]
    cosR = cosR_ref[...]
    sinR = sinR_ref[...]
    ones_rows = (lax.broadcasted_iota(jnp.int32, (DA_HEADS, V_ROWS - 64, tm), 1) == 0).astype(BF16)

    def rows(off, n):
        return z_ref[off:off + n, :]

    q2 = rows(_O_DAQ, DA_W)
    q = q2.reshape(12, DA_DIM, tm)
    qs = _pairswap_rows(q2).reshape(12, DA_DIM, tm)
    daqr[0] = ((q * cosA + qs * sinA) * cq_da).astype(BF16)
    daqf[0] = (q * cq_da).astype(BF16)
    k2 = rows(_O_DAK, DA_W)
    k = k2.reshape(12, DA_DIM, tm)
    ks = _pairswap_rows(k2).reshape(12, DA_DIM, tm)
    kr = (k * cosA + ks * sinA).reshape(DA_HEADS, 2 * DA_DIM, tm)
    kp = jnp.concatenate([kr, jnp.zeros_like(kr)], axis=1).reshape(DA_HEADS * LANES, tm)
    dak[0] = kp.T.astype(BF16)
    dav[0, :, 0:64, :] = rows(_O_DAV, DA_VW).reshape(DA_HEADS, DA_VDIM, tm).astype(BF16)
    dav[0, :, 64:V_ROWS, :] = ones_rows

    cqn = (_rms_rows(rows(_O_CQ, MLA_Q_RANK)) * gq_ref[0]).astype(BF16)
    qm = jnp.dot(wuqT_ref[0], cqn, preferred_element_type=F32)
    qh = qm.reshape(MLA_HEADS, MLA_NOPE + MLA_ROPE, tm)
    qsw = _pairswap_rows(qm).reshape(MLA_HEADS, MLA_NOPE + MLA_ROPE, tm)[:, MLA_NOPE:, :]
    rot = qh[:, MLA_NOPE:, :] * cosA + qsw * sinA
    mqr[0, :, 0:MLA_NOPE, :] = (qh[:, 0:MLA_NOPE, :] * cq_mla).astype(BF16)
    mqr[0, :, MLA_NOPE:, :] = (rot * cq_mla).astype(BF16)
    mqf[0] = (qh * cq_mla).astype(BF16)
    ckvn = (_rms_rows(rows(_O_CKV, MLA_KV_RANK)) * gkv_ref[0]).astype(BF16)
    vm = jnp.dot(wvT_ref[0], ckvn, preferred_element_type=F32)
    mv[0, :, 0:64, :] = vm.reshape(MLA_HEADS, MLA_VDIM, tm).astype(BF16)
    mv[0, :, 64:V_ROWS, :] = ones_rows
    kpe = rows(_O_KPE, MLA_ROPE)
    kper = kpe * cosA + _pairswap_rows(kpe) * sinA
    kin = jnp.concatenate([ckvn, kper.astype(BF16)], axis=0)
    kmT = jnp.dot(wkT_ref[0], kin, preferred_element_type=F32)
    mk[0] = kmT.T.astype(BF16)

    r_q2 = rows(_O_RQ, RET_W)
    r_q = r_q2.reshape(RET_HEADS, RET_DIM, tm)
    r_qs = _pairswap_rows(r_q2).reshape(RET_HEADS, RET_DIM, tm)
    rq[0] = (r_q * cosR + r_qs * sinR).astype(BF16)
    r_k2 = rows(_O_RK, RET_W)
    r_k = r_k2.reshape(RET_HEADS, RET_DIM, tm)
    r_ks = _pairswap_rows(r_k2).reshape(RET_HEADS, RET_DIM, tm)
    rkr = (r_k * cosR + r_ks * sinR) * kscale
    rkp = jnp.concatenate([rkr, jnp.zeros_like(rkr)], axis=1).reshape(RET_HEADS * LANES, tm)
    rk[0] = rkp.T.astype(BF16)
    rv[0] = rows(_O_RV, RET_W).reshape(RET_HEADS, RET_DIM, tm).astype(BF16)
    rg[0] = rows(_O_RG, RET_W)


def _in_proj(l, xc, modr, wT_all, tabs, gq, wuqT, gkv, wvT, wkT, S):
    B, Sp, D = xc.shape
    nt = Sp // TM
    n_lat_tiles = S // TM
    cosA, sinA, cosR, sinR = tabs

    def mod_spec(j):
        return pl.BlockSpec((1, 1, 1, 1, D), lambda b, t: (l, jnp.where(t >= n_lat_tiles, B, b), j, 0, 0))

    def wspec(a):
        return pl.BlockSpec((1,) + a.shape[1:], lambda b, t: (l,) + (0,) * (a.ndim - 1))

    def fm4(h, d):
        return pl.BlockSpec((1, h, d, TM), lambda b, t: (b, 0, 0, t))

    def tmaj(w):
        return pl.BlockSpec((1, TM, w), lambda b, t: (b, t, 0))

    kern = functools.partial(
        _in_kernel,
        cq_da=float(DA_DIM ** -0.5 * LOG2E),
        cq_mla=float((MLA_NOPE + MLA_ROPE) ** -0.5 * LOG2E),
        kscale=float(RET_DIM ** -0.5),
    )
    sds = jax.ShapeDtypeStruct
    return pl.pallas_call(
        kern,
        grid=(B, nt),
        in_specs=[
            pl.BlockSpec((1, TM, D), lambda b, t: (b, t, 0)),
            mod_spec(0), mod_spec(1),
            wspec(wT_all),
            pl.BlockSpec((DA_DIM, TM), lambda b, t: (0, t)),
            pl.BlockSpec((DA_DIM, TM), lambda b, t: (0, t)),
            pl.BlockSpec((RET_DIM, TM), lambda b, t: (0, t)),
            pl.BlockSpec((RET_DIM, TM), lambda b, t: (0, t)),
            wspec(gq), wspec(wuqT), wspec(gkv), wspec(wvT), wspec(wkT),
        ],
        out_specs=[
            fm4(12, DA_DIM), fm4(12, DA_DIM), tmaj(DA_HEADS * LANES), fm4(DA_HEADS, V_ROWS),
            fm4(MLA_HEADS, 96), fm4(MLA_HEADS, 96), tmaj(MLA_HEADS * LANES), fm4(MLA_HEADS, V_ROWS),
            fm4(RET_HEADS, RET_DIM), tmaj(RET_HEADS * LANES), fm4(RET_HEADS, RET_DIM),
            pl.BlockSpec((1, RET_W, TM), lambda b, t: (b, 0, t)),
        ],
        out_shape=[
            sds((B, 12, DA_DIM, Sp), BF16), sds((B, 12, DA_DIM, Sp), BF16),
            sds((B, Sp, DA_HEADS * LANES), BF16), sds((B, DA_HEADS, V_ROWS, Sp), BF16),
            sds((B, MLA_HEADS, 96, Sp), BF16), sds((B, MLA_HEADS, 96, Sp), BF16),
            sds((B, Sp, MLA_HEADS * LANES), BF16), sds((B, MLA_HEADS, V_ROWS, Sp), BF16),
            sds((B, RET_HEADS, RET_DIM, Sp), BF16), sds((B, Sp, RET_HEADS * LANES), BF16),
            sds((B, RET_HEADS, RET_DIM, Sp), BF16), sds((B, RET_W, Sp), F32),
        ],
        scratch_shapes=[pltpu.VMEM((N_ZROWS, TM), F32)],
        compiler_params=pltpu.CompilerParams(
            dimension_semantics=("arbitrary", "arbitrary"), vmem_limit_bytes=56 * 1024 * 1024),
        name="in_proj",
    )(xc, modr, modr, wT_all, cosA, sinA, cosR, sinR, gq, wuqT, gkv, wvT, wkT)


def _attn_kernel(*refs, mode, S, C, lam_init, ctx_only, nsub):
    refs = list(refs)
    qr_ref, qf_ref, k_ref, v_ref = refs[:4]
    del refs[:4]
    if mode == "da":
        lam_ref, g_ref = refs[:2]
        del refs[:2]
    if ctx_only:
        del refs[:1]
    o_ref, m_sc, acc_sc, q_sc, jump_sc = refs[:5]
    nb = ATTN_NBUF
    pbuf, albuf = (refs[5 + i * nb: 5 + (i + 1) * nb] for i in range(2))
    tq = o_ref.shape[2] // nsub
    n_lat = S // TK

    def qpad(ref, t):
        cols = slice(t * tq, (t + 1) * tq)
        if mode == "da":
            z = jnp.zeros((DA_DIM, tq), BF16)
            c1 = jnp.concatenate([ref[0, 0, :, cols], z, z, z], axis=0)
            c2 = jnp.concatenate([z, ref[0, 1, :, cols], z, z], axis=0)
            return jnp.concatenate([c1, c2], axis=1)
        return jnp.concatenate([ref[0, 0, :, cols], jnp.zeros((LANES - 96, tq), BF16)], axis=0)

    def chunk(j):
        return pl.ds(j * TK if isinstance(j, int) else pl.multiple_of(j * TK, TK), TK)

    def context_keys(t):
        c0 = 0 if ctx_only else S
        s = jnp.dot(k_ref[0, c0:c0 + C, :], qpad(qf_ref, t), preferred_element_type=F32)
        m_c = jnp.max(s, axis=0, keepdims=True)
        m_sc[t] = m_c
        acc_sc[t] = jnp.dot(v_ref[0, 0, :, c0:c0 + C], jnp.exp2(s - m_c).astype(BF16), preferred_element_type=F32)

    for t in range(nsub):
        context_keys(t)

    if not ctx_only:
        for t in range(nsub):
            q_sc[t] = qpad(qr_ref, t)
        jump_sc[...] = jnp.full(jump_sc.shape, -jnp.inf, F32)

        def probs(t, j, r):
            s = jnp.dot(k_ref[0, chunk(j), :], q_sc[t], preferred_element_type=F32)
            m_prev = m_sc[t]
            pbuf[r][t] = jnp.exp2(s - m_prev).astype(BF16)
            m_chunk = jnp.max(s, axis=0, keepdims=True)
            m_new = jnp.maximum(m_prev, m_chunk)
            jump_sc[t] = jnp.maximum(jump_sc[t], m_chunk - m_prev)
            albuf[r][t] = jnp.exp2(m_prev - m_new)
            m_sc[t] = m_new

        def accumulate(t, j, r):
            pv = jnp.dot(v_ref[0, 0, :, chunk(j)], pbuf[r][t], preferred_element_type=F32)
            acc_sc[t] = (acc_sc[t] + pv) * albuf[r][t]

        def step(j, r):
            for t in range(nsub):
                accumulate(t, j, r)
                probs(t, j + 1, (r + 1) % nb)

        for t in range(nsub):
            probs(t, 0, 0)
        n_steps = n_lat - 1
        trips = n_steps // ATTN_UNROLL

        def body(i, carry):
            for u in range(ATTN_UNROLL):
                step(i * ATTN_UNROLL + u, u % nb)
            return carry

        lax.fori_loop(0, trips, body, 0)
        for j in range(trips * ATTN_UNROLL, n_steps):
            step(j, j % nb)
        for t in range(nsub):
            accumulate(t, n_lat - 1, (n_lat - 1) % nb)

        @pl.when(jnp.max(jump_sc[...]) > ATTN_MAX_JUMP)
        def _():
            for t in range(nsub):
                context_keys(t)

                def exact_step(j, carry, t=t):
                    s = jnp.dot(k_ref[0, chunk(j), :], q_sc[t], preferred_element_type=F32)
                    m_old = m_sc[t]
                    m_new = jnp.maximum(m_old, jnp.max(s, axis=0, keepdims=True))
                    pv = jnp.dot(v_ref[0, 0, :, chunk(j)], jnp.exp2(s - m_new).astype(BF16),
                                 preferred_element_type=F32)
                    acc_sc[t] = acc_sc[t] * jnp.exp2(m_old - m_new) + pv
                    m_sc[t] = m_new
                    return carry

                lax.fori_loop(0, n_lat, exact_step, 0)

    for t in range(nsub):
        acc = acc_sc[t]
        o = acc[0:64, :] / acc[64:65, :]
        cols = slice(t * tq, (t + 1) * tq)
        if mode == "da":
            lf = lam_ref[0]
            lam = (jnp.exp(jnp.sum(lf[0:1] * lf[1:2], axis=1, keepdims=True))
                   - jnp.exp(jnp.sum(lf[2:3] * lf[3:4], axis=1, keepdims=True)) + lam_init)
            od = o[:, 0:tq] - lam * o[:, tq:2 * tq]
            y = _rms_rows(od) * g_ref[0]
            o_ref[0, :, cols] = (y * (1.0 - lam_init)).astype(BF16)
        else:
            o_ref[0, :, cols] = o.astype(BF16)


def _attention_call(mode, l, qr, qf, k, v, S, C, lam, g, y_latent):
    B, nmaps, d, Sp = qr.shape
    H = v.shape[1]
    nm = nmaps // H
    ctx_only = y_latent is not None
    nsub = 1 if ctx_only else ATTN_SUBTILES
    tq = C if ctx_only else ATTN_N // nm
    n = nm * tq
    tqb = nsub * tq
    q0 = S // tqb if ctx_only else 0
    lam_init = 0.8 - 0.6 * math.exp(-0.3 * l)
    kern = functools.partial(_attn_kernel, mode=mode, S=S, C=C, lam_init=lam_init, ctx_only=ctx_only, nsub=nsub)
    in_specs = [
        pl.BlockSpec((1, nm, d, tqb), lambda b, h, i: (b, h, 0, q0 + i)),
        pl.BlockSpec((1, nm, d, tqb), lambda b, h, i: (b, h, 0, q0 + i)),
        (pl.BlockSpec((1, C, LANES), lambda b, h, i: (b, S // C, h)) if ctx_only
         else pl.BlockSpec((1, Sp, LANES), lambda b, h, i: (b, 0, h))),
        (pl.BlockSpec((1, 1, V_ROWS, C), lambda b, h, i: (b, h, 0, S // C)) if ctx_only
         else pl.BlockSpec((1, 1, V_ROWS, Sp), lambda b, h, i: (b, h, 0, 0))),
    ]
    args = [qr, qf, k, v]
    if mode == "da":
        in_specs += [pl.BlockSpec((1,) + lam.shape[1:], lambda b, h, i: (l, 0, 0)),
                     pl.BlockSpec((1,) + g.shape[1:], lambda b, h, i: (l, 0, 0))]
        args += [lam, g]
    aliases = {}
    if ctx_only:
        in_specs.append(pl.BlockSpec(memory_space=pl.ANY))
        args.append(y_latent)
        aliases = {len(args) - 1: 0}
    tk = C if ctx_only else TK
    return pl.pallas_call(
        kern,
        grid=(B, H, 1 if ctx_only else S // tqb),
        in_specs=in_specs,
        out_specs=pl.BlockSpec((1, 64, tqb), lambda b, h, i: (b, h, q0 + i)),
        out_shape=jax.ShapeDtypeStruct((B, H * 64, Sp), BF16),
        input_output_aliases=aliases,
        scratch_shapes=(
            [pltpu.VMEM((nsub, 1, n), F32), pltpu.VMEM((nsub, V_ROWS, n), F32), pltpu.VMEM((nsub, LANES, n), BF16),
             pltpu.VMEM((nsub, 1, n), F32)]
            + [pltpu.VMEM((nsub, tk, n), BF16)] * ATTN_NBUF + [pltpu.VMEM((nsub, 1, n), F32)] * ATTN_NBUF),
        compiler_params=pltpu.CompilerParams(
            dimension_semantics=("arbitrary", "arbitrary", "arbitrary"), vmem_limit_bytes=48 * 1024 * 1024),
        name="attn_" + mode + ("_ctx" if ctx_only else ""),
    )(*args)


def _attention(mode, l, qr, qf, k, v, S, C, with_ctx, lam=None, g=None):
    y = _attention_call(mode, l, qr, qf, k, v, S, C, lam, g, None)
    return _attention_call(mode, l, qr, qf, k, v, S, C, lam, g, y) if with_ctx else y


def _ret_kernel(q_ref, k_ref, v_ref, g_ref, dec_ref, o_ref, sb_sc, *, S, C):
    CH = RET_CHUNK
    nl, nc = S // CH, C // CH

    def log_sigmoid(x):
        return jnp.minimum(x, 0.0) - jnp.log1p(jnp.exp(-jnp.abs(x)))

    lgf = log_sigmoid(dec_ref[0, 0][0:1, :])
    lgb = log_sigmoid(dec_ref[1, 0][0:1, :])
    ii = lax.broadcasted_iota(jnp.int32, (CH, CH), 1).astype(F32)
    jj = lax.broadcasted_iota(jnp.int32, (CH, CH), 0).astype(F32)
    dd = ii - jj
    fwd = dd >= 0
    mt = (jnp.where(fwd, jnp.exp(lgf * jnp.where(fwd, dd, 0.0)), 0.0)
          + jnp.where(fwd, 0.0, jnp.exp(lgb * jnp.where(fwd, 0.0, -dd))))
    xi_f = jnp.exp(lgf * (ii + 1.0))
    xi_b = jnp.exp(lgb * (CH - ii))
    zeta_f = jnp.exp(lgf * (CH - 1.0 - jj))
    zeta_b = jnp.exp(lgb * jj)
    cdf = jnp.exp(lgf * CH)
    cdb = jnp.exp(lgb * CH)
    zq = jnp.zeros((RET_DIM, CH), BF16)

    def sl(c):
        return pl.ds(pl.multiple_of(c * CH, CH), CH)

    def state_update(c, zeta):
        kz = (k_ref[0, sl(c), :].astype(F32) * zeta).astype(BF16)
        return jnp.dot(v_ref[0, 0, :, sl(c)], kz, preferred_element_type=F32)

    def bwd_step(c, sb):
        sb_sc[c] = sb
        return sb * cdb + state_update(c, zeta_b)

    sb = jnp.zeros((RET_DIM, CH), F32)
    for c in range(nl + nc - 1, nl - 1, -1):
        sb = bwd_step(c, sb)
    lax.fori_loop(0, nl, lambda n, s: bwd_step(nl - 1 - n, s), sb, unroll=RET_UNROLL)

    def fwd_step(c, sf):
        qc = q_ref[0, 0, :, sl(c)]
        qp = jnp.concatenate([qc, zq], axis=0)
        kc = k_ref[0, sl(c), :]
        vc = v_ref[0, 0, :, sl(c)]
        at = jnp.dot(kc, qp, preferred_element_type=F32)
        qpf = qp.astype(F32)
        rhs = jnp.concatenate([(at * mt).astype(BF16), (qpf * xi_f).astype(BF16), (qpf * xi_b).astype(BF16)], axis=0)
        lhs = jnp.concatenate([vc, sf.astype(BF16), sb_sc[c].astype(BF16)], axis=1)
        o = jnp.dot(lhs, rhs, preferred_element_type=F32)
        gch = g_ref[0, :, sl(c)]
        o_ref[0, :, sl(c)] = (gch * _sigmoid(gch) * _rms_rows(o)).astype(BF16)
        return sf * cdf + state_update(c, zeta_f)

    sf = jnp.zeros((RET_DIM, CH), F32)
    for c in range(nl, nl + nc):
        sf = fwd_step(c, sf)
    lax.fori_loop(0, nl, fwd_step, sf, unroll=RET_UNROLL)


def _retention(rq, rk, rv, rg, dec, S, C):
    B, H, d, Sp = rq.shape
    kern = functools.partial(_ret_kernel, S=S, C=C)
    return pl.pallas_call(
        kern,
        grid=(B, H),
        in_specs=[
            pl.BlockSpec((1, 1, d, Sp), lambda b, h: (b, h, 0, 0)),
            pl.BlockSpec((1, Sp, LANES), lambda b, h: (b, 0, h)),
            pl.BlockSpec((1, 1, d, Sp), lambda b, h: (b, h, 0, 0)),
            pl.BlockSpec((1, d, Sp), lambda b, h: (b, h, 0)),
            pl.BlockSpec((2, 1, 8, LANES), lambda b, h: (0, h, 0, 0)),
        ],
        out_specs=pl.BlockSpec((1, d, Sp), lambda b, h: (b, h, 0)),
        out_shape=jax.ShapeDtypeStruct((B, H * d, Sp), BF16),
        scratch_shapes=[pltpu.VMEM((Sp // RET_CHUNK, RET_DIM, RET_CHUNK), F32)],
        compiler_params=pltpu.CompilerParams(
            dimension_semantics=("arbitrary", "arbitrary"), vmem_limit_bytes=56 * 1024 * 1024),
        name="retention",
    )(rq, rk, rv, rg, dec)


def _layer_norm(t, g, b):
    mu = jnp.mean(t, axis=-1, keepdims=True)
    tc = t - mu
    var = jnp.mean(tc * tc, axis=-1, keepdims=True)
    return tc * lax.rsqrt(var + LN_EPS) * g + b


def _out_kernel(yda_ref, ymla_ref, yret_ref, w_ref, x_ref, gate_ref, g1_ref, b1_ref, sh2_ref, sc2_ref, rw_ref,
                x1_ref, h2_ref, lg_ref, *, alpha):
    proj = (lax.dot_general(yda_ref[0], w_ref[0, 0:DA_VW, :], TN_DIMS, preferred_element_type=F32)
            + lax.dot_general(ymla_ref[0], w_ref[0, DA_VW:DA_VW + MLA_VW, :], TN_DIMS, preferred_element_type=F32)
            + lax.dot_general(yret_ref[0], w_ref[0, DA_VW + MLA_VW:, :], TN_DIMS, preferred_element_type=F32))
    x1 = _layer_norm(alpha * x_ref[0] + gate_ref[0, 0, 0] * proj, g1_ref[0], b1_ref[0])
    x1_ref[0] = x1
    h2 = x1 * (1.0 + sc2_ref[0, 0, 0]) + sh2_ref[0, 0, 0]
    h2_ref[0] = h2.astype(BF16)
    h2_hi = h2.astype(BF16)
    h2_lo = (h2 - h2_hi.astype(F32)).astype(BF16)
    hi = jnp.dot(h2_hi, rw_ref[...], preferred_element_type=F32)
    lo = jnp.dot(h2_lo, rw_ref[:, 0:LANES], preferred_element_type=F32)
    lg_ref[0] = hi[:, 0:LANES] + (hi[:, LANES:] + lo)


def _out_proj(l, yda, ymla, yret, w_out, xc, modr, ln_g, ln_b, rw_pad, S, alpha, latents_only):
    B, Sp, D = xc.shape
    n_lat_tiles = S // TM
    n_rows = S if latents_only else Sp

    def mod_spec(j):
        return pl.BlockSpec((1, 1, 1, 1, D), lambda b, t: (l, jnp.where(t >= n_lat_tiles, B, b), j, 0, 0))

    def fm(a):
        return pl.BlockSpec((1, a.shape[1], TM), lambda b, t: (b, 0, t))

    def wspec(a):
        return pl.BlockSpec((1,) + a.shape[1:], lambda b, t: (l,) + (0,) * (a.ndim - 1))

    row = pl.BlockSpec((1, TM, D), lambda b, t: (b, t, 0))
    sds = jax.ShapeDtypeStruct
    return pl.pallas_call(
        functools.partial(_out_kernel, alpha=alpha),
        grid=(B, n_rows // TM),
        in_specs=[fm(yda), fm(ymla), fm(yret), wspec(w_out), row, mod_spec(2), wspec(ln_g), wspec(ln_b),
                  mod_spec(3), mod_spec(4), pl.BlockSpec(rw_pad.shape, lambda b, t: (0, 0))],
        out_specs=[row, row, pl.BlockSpec((1, TM, LANES), lambda b, t: (b, t, 0))],
        out_shape=[sds((B, n_rows, D), F32), sds((B, n_rows, D), BF16), sds((B, n_rows, LANES), F32)],
        compiler_params=pltpu.CompilerParams(dimension_semantics=("arbitrary", "arbitrary")),
        name="out_proj",
    )(yda, ymla, yret, w_out, xc, modr, ln_g, ln_b, modr, modr, rw_pad)


def _gates_T(logits_T, rb):
    s = _sigmoid(logits_T)
    ch = s + rb
    srow = [s[e:e + 1] for e in range(N_EXPERTS)]
    crow = [ch[e:e + 1] for e in range(N_EXPERTS)]
    per = N_EXPERTS // N_GROUPS
    gs = []
    for g in range(N_GROUPS):
        a, b, c, d = crow[per * g: per * g + per]
        m1, n1, m2, n2 = jnp.maximum(a, b), jnp.minimum(a, b), jnp.maximum(c, d), jnp.minimum(c, d)
        gs.append(jnp.maximum(m1, m2) + jnp.maximum(jnp.minimum(m1, m2), jnp.maximum(n1, n2)))
    gmax = functools.reduce(jnp.maximum, gs)
    taken = jnp.zeros(gmax.shape, jnp.bool_)
    gsel = []
    for g in range(N_GROUPS):
        sg = jnp.logical_and(gs[g] == gmax, jnp.logical_not(taken))
        gsel.append(sg)
        taken = jnp.logical_or(taken, sg)
    neg = jnp.full(gmax.shape, -jnp.inf, F32)
    mc = [jnp.where(gsel[e // per], crow[e], neg) for e in range(N_EXPERTS)]
    sel = [jnp.zeros(gmax.shape, jnp.bool_) for _ in range(N_EXPERTS)]
    for _ in range(2):
        top = functools.reduce(jnp.maximum, mc)
        taken = jnp.zeros(gmax.shape, jnp.bool_)
        for e in range(N_EXPERTS):
            hit = jnp.logical_and(mc[e] == top, jnp.logical_not(taken))
            taken = jnp.logical_or(taken, hit)
            sel[e] = jnp.logical_or(sel[e], hit)
            mc[e] = jnp.where(hit, neg, mc[e])
    w = [jnp.where(sel[e], srow[e], 0.0) for e in range(N_EXPERTS)]
    wsum = functools.reduce(lambda a, b: a + b, w)
    gates = jnp.concatenate([we / wsum * ROUTED_SCALE for we in w], axis=0)
    return gates, [jnp.where(sg, 1.0, 0.0) for sg in gsel]


def _moe_capacity(tm):
    return -(-int(tm / N_GROUPS + 6.0 * math.sqrt(tm * (N_GROUPS - 1)) / N_GROUPS) // 32) * 32


def _moe_kernel(h_ref, lg_ref, rb_ref, w13_ref, w2_ref, x1_ref, gb_ref, gc_ref, g2_ref, b2_ref, o_ref,
                gates_sc, acc_sc, dense_sc, perm_sc, hs_sc, gs_sc, accs_sc, *, S, alpha, cap):
    e = pl.program_id(2)
    tm = h_ref.shape[1]
    per = N_EXPERTS // N_GROUPS
    lane = lax.broadcasted_iota(jnp.int32, (1, LANES), 1)

    def expert(x):
        ab = jnp.dot(x, w13_ref[0, 0], preferred_element_type=F32)
        a = ab[:, 0:D_EXPERT]
        hid = (a * _sigmoid(a) * ab[:, D_EXPERT:]).astype(BF16)
        return jnp.dot(hid, w2_ref[0, 0], preferred_element_type=F32)

    def gate_col(g):
        return jnp.sum(jnp.where(lane == e, g, 0.0), axis=1, keepdims=True)

    @pl.when(e == 0)
    def _():
        gT, gsel = _gates_T(lg_ref[0].T[0:N_EXPERTS, :], rb_ref[...])
        gates = jnp.concatenate([gT, jnp.zeros((LANES - N_EXPERTS, tm), F32)], axis=0).T
        gates_sc[...] = gates
        acc_sc[...] = jnp.zeros(acc_sc.shape, F32)
        member = jnp.concatenate(gsel + [jnp.zeros((16 - N_GROUPS, tm), F32)], axis=0)
        earlier = (lax.broadcasted_iota(jnp.int32, (tm, tm), 0) < lax.broadcasted_iota(jnp.int32, (tm, tm), 1))
        rank = jnp.dot(member.astype(BF16), jnp.where(earlier, 1.0, 0.0).astype(BF16), preferred_element_type=F32)
        slot = functools.reduce(lambda a, b: a + b,
                                [gsel[g] * (rank[g:g + 1] + float(g * cap)) for g in range(N_GROUPS)])
        count = jnp.sum(member, axis=1, keepdims=True)
        dense_sc[0] = jnp.where(jnp.max(count) > float(cap), 1, 0)
        perm = jnp.where(lax.broadcasted_iota(jnp.int32, (N_GROUPS * cap, tm), 0) == slot.astype(jnp.int32), 1.0, 0.0)
        perm = perm.astype(BF16)
        perm_sc[...] = perm
        hs_sc[...] = jnp.dot(perm, h_ref[0], preferred_element_type=F32).astype(BF16)
        g_hi = gates.astype(BF16)
        g_lo = (gates - g_hi.astype(F32)).astype(BF16)
        gs_sc[...] = (jnp.dot(perm, g_hi, preferred_element_type=F32)
                      + jnp.dot(perm, g_lo, preferred_element_type=F32))
        accs_sc[...] = jnp.zeros(accs_sc.shape, F32)

    @pl.when(dense_sc[0] == 0)
    def _():
        rows = pl.ds(pl.multiple_of((e // per) * cap, 16), cap)
        accs_sc[rows, :] += gate_col(gs_sc[rows, :]) * expert(hs_sc[rows, :])

    @pl.when(dense_sc[0] != 0)
    def _():
        acc_sc[...] += gate_col(gates_sc[...]) * expert(h_ref[0])

    @pl.when(e == N_EXPERTS - 1)
    def _():
        @pl.when(dense_sc[0] == 0)
        def _():
            a = accs_sc[...]
            a_hi = a.astype(BF16)
            a_lo = (a - a_hi.astype(F32)).astype(BF16)
            acc_sc[...] = (lax.dot_general(perm_sc[...], a_hi, TN_DIMS, preferred_element_type=F32)
                           + lax.dot_general(perm_sc[...], a_lo, TN_DIMS, preferred_element_type=F32))

        n_lat = S - pl.program_id(1) * tm
        rowi = lax.broadcasted_iota(jnp.int32, (tm, 1), 0)
        gate = jnp.where(rowi < n_lat, gb_ref[0, 0, 0], gc_ref[0, 0, 0])
        o_ref[0] = _layer_norm(alpha * x1_ref[0] + gate * acc_sc[...], g2_ref[0], b2_ref[0])


def _moe(l, h2, logits, rb, w13, w2, x1, modr, ln_g, ln_b, S, alpha, latents_only):
    B, Sp, D = x1.shape
    tm = TM_MOE_LAST if latents_only else TM_MOE
    n_rows = S if latents_only else Sp

    def wspec(a):
        return pl.BlockSpec((1,) + a.shape[1:], lambda b, t, e: (l,) + (0,) * (a.ndim - 1))

    row = lambda b, t, e: (b, t, 0)
    cap = _moe_capacity(tm)
    return pl.pallas_call(
        functools.partial(_moe_kernel, S=S, alpha=alpha, cap=cap),
        grid=(B, n_rows // tm, N_EXPERTS),
        in_specs=[
            pl.BlockSpec((1, tm, D), row),
            pl.BlockSpec((1, tm, LANES), row),
            pl.BlockSpec(rb.shape, lambda b, t, e: (0, 0)),
            pl.BlockSpec((1, 1, D, 2 * D_EXPERT), lambda b, t, e: (l, e, 0, 0)),
            pl.BlockSpec((1, 1, D_EXPERT, D), lambda b, t, e: (l, e, 0, 0)),
            pl.BlockSpec((1, tm, D), row),
            pl.BlockSpec((1, 1, 1, 1, D), lambda b, t, e: (l, b, 5, 0, 0)),
            pl.BlockSpec((1, 1, 1, 1, D), lambda b, t, e: (l, B, 5, 0, 0)),
            wspec(ln_g), wspec(ln_b),
        ],
        out_specs=pl.BlockSpec((1, tm, D), row),
        out_shape=jax.ShapeDtypeStruct((B, n_rows, D), F32),
        scratch_shapes=[
            pltpu.VMEM((tm, LANES), F32), pltpu.VMEM((tm, D), F32), pltpu.SMEM((1,), jnp.int32),
            pltpu.VMEM((N_GROUPS * cap, tm), BF16), pltpu.VMEM((N_GROUPS * cap, D), BF16),
            pltpu.VMEM((N_GROUPS * cap, LANES), F32), pltpu.VMEM((N_GROUPS * cap, D), F32)],
        compiler_params=pltpu.CompilerParams(
            dimension_semantics=("arbitrary", "arbitrary", "arbitrary"), vmem_limit_bytes=56 * 1024 * 1024),
        name="moe",
    )(h2, logits, rb, w13, w2, x1, modr, modr, ln_g, ln_b)


def _rope_tables(S, C, rot_dim):
    rows = S // GRID_W
    r = jnp.repeat(jnp.arange(rows, dtype=F32), GRID_W)
    col = jnp.tile(jnp.arange(GRID_W, dtype=F32), rows)
    n_freq = rot_dim // 4
    freqs = ROPE_BASE ** (-jnp.arange(n_freq, dtype=F32) / n_freq)
    ang = jnp.concatenate([r[:, None] * freqs, col[:, None] * freqs], -1)
    cos, sin = jnp.cos(ang), jnp.sin(ang)
    cos_rep = jnp.repeat(cos, 2, axis=-1)
    sin_alt = jnp.stack([-sin, sin], -1).reshape(S, rot_dim)
    cos_rep = jnp.concatenate([cos_rep, jnp.ones((C, rot_dim), F32)], 0)
    sin_alt = jnp.concatenate([sin_alt, jnp.zeros((C, rot_dim), F32)], 0)
    return cos_rep.T, sin_alt.T


def kernel(x, c, ctx, c_ctx, w_ada, b_ada, w_in, da_lambda, da_subln, mla_q_norm, mla_w_uq, mla_kv_norm, mla_w_ukv, ret_decay_f, ret_decay_b, w_out, ln1_g, ln1_b, router_w, router_b, exp_w1, exp_w3, exp_w2, ln2_g, ln2_b):
    B, S, D = x.shape
    C = ctx.shape[1]
    depth = w_in.shape[0]
    alpha = float((2 * depth) ** 0.25)

    cvec = jnp.concatenate([c, c_ctx[None, :], jnp.zeros((8 - B - 1, D), F32)], 0)
    mod = _modulation(cvec, w_ada, b_ada)
    modr = mod.reshape(depth, 8, N_MOD, 1, D)

    wT_all = jnp.swapaxes(w_in, 1, 2).astype(BF16)
    wuqT = jnp.swapaxes(mla_w_uq, 1, 2).astype(BF16)
    ukv = mla_w_ukv.reshape(depth, MLA_KV_RANK, MLA_HEADS, MLA_NOPE + MLA_VDIM)
    wvT = jnp.swapaxes(ukv[..., MLA_NOPE:].reshape(depth, MLA_KV_RANK, MLA_VW), 1, 2).astype(BF16)
    nopeT = jnp.transpose(ukv[..., :MLA_NOPE], (0, 2, 3, 1))
    top = jnp.concatenate([nopeT, jnp.zeros((depth, MLA_HEADS, MLA_NOPE, MLA_ROPE), F32)], -1)
    mid = jnp.concatenate([jnp.zeros((MLA_ROPE, MLA_KV_RANK), F32), jnp.eye(MLA_ROPE, dtype=F32)], -1)
    mid = jnp.broadcast_to(mid, (depth, MLA_HEADS, MLA_ROPE, MLA_KV_RANK + MLA_ROPE))
    bot = jnp.zeros((depth, MLA_HEADS, LANES - MLA_NOPE - MLA_ROPE, MLA_KV_RANK + MLA_ROPE), F32)
    wkT = jnp.concatenate([top, mid, bot], 2).reshape(depth, MLA_HEADS * LANES, MLA_KV_RANK + MLA_ROPE).astype(BF16)
    gq = mla_q_norm[:, :, None]
    gkv = mla_kv_norm[:, :, None]
    subln = da_subln[:, :, None]
    w_out_b = w_out.astype(BF16)
    w13 = jnp.concatenate([exp_w1, exp_w3], -1).astype(BF16)
    w2 = exp_w2.astype(BF16)
    rw_f = jnp.concatenate([router_w, jnp.zeros((D, LANES - N_EXPERTS), F32)], -1)
    rw_hi = rw_f.astype(BF16)
    rw_pad = jnp.concatenate([rw_hi, (rw_f - rw_hi.astype(F32)).astype(BF16)], -1)
    rb = router_b[:, None]
    dec = jnp.broadcast_to(jnp.stack([ret_decay_f, ret_decay_b], 1)[:, :, :, None, None],
                           (depth, 2, RET_HEADS, 8, LANES))
    tabs = _rope_tables(S, C, DA_DIM) + _rope_tables(S, C, RET_DIM)
    ln1g, ln1b, ln2g, ln2b = (a[:, None, :] for a in (ln1_g, ln1_b, ln2_g, ln2_b))

    xc = jnp.concatenate([x, ctx], 1)
    for l in range(depth):
        last = l == depth - 1
        (daqr, daqf, dak, dav, mqr, mqf, mk, mv, rq, rk, rv, rg) = _in_proj(
            l, xc, modr, wT_all, tabs, gq, wuqT, gkv, wvT, wkT, S)
        yda = _attention("da", l, daqr, daqf, dak, dav, S, C, not last, lam=da_lambda, g=subln)
        ymla = _attention("mla", l, mqr, mqf, mk, mv, S, C, not last)
        yret = _retention(rq, rk, rv, rg, dec[l], S, C)
        x1, h2, logits = _out_proj(l, yda, ymla, yret, w_out_b, xc, modr, ln1g, ln1b, rw_pad, S, alpha, last)
        xc = _moe(l, h2, logits, rb, w13, w2, x1, modr, ln2g, ln2b, S, alpha, last)
    return xc
```

```python
import functools
import math

import jax
import jax.numpy as jnp
import numpy as np
from jax import lax
from jax.experimental import pallas as pl
from jax.experimental.pallas import tpu as pltpu

F32 = jnp.float32
BF16 = jnp.bfloat16

GRID_W = 64
DA_HEADS, DA_DIM, DA_VDIM = 6, 32, 64
MLA_HEADS, MLA_NOPE, MLA_ROPE, MLA_VDIM = 6, 64, 32, 64
MLA_Q_RANK, MLA_KV_RANK = 256, 128
RET_HEADS, RET_DIM, RET_CHUNK = 4, 64, 128
ROPE_BASE = 10000.0
N_EXPERTS, N_GROUPS, D_EXPERT = 16, 4, 512
ROUTED_SCALE = 1.0
N_MOD = 6
LN_EPS = 1e-5
RMS_EPS = 1e-6
LOG2E = math.log2(math.e)

DA_W = DA_HEADS * 2 * DA_DIM
DA_VW = DA_HEADS * DA_VDIM
RET_W = RET_HEADS * RET_DIM
MLA_QW = MLA_HEADS * (MLA_NOPE + MLA_ROPE)
MLA_VW = MLA_HEADS * MLA_VDIM
IN_SPLIT = (DA_W, DA_W, DA_VW, MLA_Q_RANK, MLA_KV_RANK, MLA_ROPE, RET_W, RET_W, RET_W, RET_W)

LANES = 128
V_ROWS = 80
TM = 256
ATTN_N = 512
TK = 2048
ATTN_SUBTILES = 2
ATTN_NBUF = 2
ATTN_MAX_JUMP = 32.0
ATTN_UNROLL = 2
TM_MOE = 640
TM_MOE_LAST = 512
RET_UNROLL = 8

NT_DIMS = (((1,), (1,)), ((), ()))
TN_DIMS = (((0,), (0,)), ((), ()))


def _sigmoid(x):
    return 1.0 / (1.0 + jnp.exp(-x))


def _mod_kernel(c_ref, w_ref, b_ref, o_ref):
    c = c_ref[...]
    sc = (c * _sigmoid(c)).astype(BF16)
    o_ref[0] = jnp.dot(sc, w_ref[0].astype(BF16), preferred_element_type=F32) + b_ref[0]


def _modulation(cvec, w_ada, b_ada):
    depth, d, n = w_ada.shape
    tn = 1536
    return pl.pallas_call(
        _mod_kernel,
        grid=(depth, n // tn),
        in_specs=[
            pl.BlockSpec((8, d), lambda l, j: (0, 0)),
            pl.BlockSpec((1, d, tn), lambda l, j: (l, 0, j)),
            pl.BlockSpec((1, 1, tn), lambda l, j: (l, 0, j)),
        ],
        out_specs=pl.BlockSpec((1, 8, tn), lambda l, j: (l, 0, j)),
        out_shape=jax.ShapeDtypeStruct((depth, 8, n), F32),
        name="adaln_mod",
    )(cvec, w_ada, b_ada.reshape(depth, 1, n))


_O_DAQ, _O_DAK, _O_DAV, _O_CQ, _O_CKV, _O_KPE, _O_RQ, _O_RK, _O_RV, _O_RG = (
    int(v) for v in np.cumsum((0,) + IN_SPLIT)[:-1])
N_ZROWS = sum(IN_SPLIT)


def _rms_rows(x):
    return x * lax.rsqrt(jnp.mean(x * x, axis=0, keepdims=True) + RMS_EPS)


def _pairswap_rows(x):
    n = x.shape[0]
    even = (lax.broadcasted_iota(jnp.int32, x.shape, 0) & 1) == 0
    return jnp.where(even, pltpu.roll(x, n - 1, 0), pltpu.roll(x, 1, 0))


def _in_kernel(x_ref, sh_ref, sc_ref, wT_ref, cosA_ref, sinA_ref, cosR_ref, sinR_ref,
               gq_ref, wuqT_ref, gkv_ref, wvT_ref, wkT_ref,
               daqr, daqf, dak, dav, mqr, mqf, mk, mv, rq, rk, rv, rg, z_ref, *, cq_da, cq_mla, kscale):
    tm = x_ref.shape[1]
    x = x_ref[0]
    h = (x * (1.0 + sc_ref[0, 0, 0]) + sh_ref[0, 0, 0]).astype(BF16)
    z_ref[...] = lax.dot_general(wT_ref[0], h, NT_DIMS, preferred_element_type=F32)
    cosA = cosA_ref[...]
    sinA = sinA_ref[...]
    cosR = cosR_ref[...]
    sinR = sinR_ref[...]
    ones_rows = (lax.broadcasted_iota(jnp.int32, (DA_HEADS, V_ROWS - 64, tm), 1) == 0).astype(BF16)

    def rows(off, n):
        return z_ref[off:off + n, :]

    q2 = rows(_O_DAQ, DA_W)
    q = q2.reshape(12, DA_DIM, tm)
    qs = _pairswap_rows(q2).reshape(12, DA_DIM, tm)
    daqr[0] = ((q * cosA + qs * sinA) * cq_da).astype(BF16)
    daqf[0] = (q * cq_da).astype(BF16)
    k2 = rows(_O_DAK, DA_W)
    k = k2.reshape(12, DA_DIM, tm)
    ks = _pairswap_rows(k2).reshape(12, DA_DIM, tm)
    kr = (k * cosA + ks * sinA).reshape(DA_HEADS, 2 * DA_DIM, tm)
    kp = jnp.concatenate([kr, jnp.zeros_like(kr)], axis=1).reshape(DA_HEADS * LANES, tm)
    dak[0] = kp.T.astype(BF16)
    dav[0, :, 0:64, :] = rows(_O_DAV, DA_VW).reshape(DA_HEADS, DA_VDIM, tm).astype(BF16)
    dav[0, :, 64:V_ROWS, :] = ones_rows

    cqn = (_rms_rows(rows(_O_CQ, MLA_Q_RANK)) * gq_ref[0]).astype(BF16)
    qm = jnp.dot(wuqT_ref[0], cqn, preferred_element_type=F32)
    qh = qm.reshape(MLA_HEADS, MLA_NOPE + MLA_ROPE, tm)
    qsw = _pairswap_rows(qm).reshape(MLA_HEADS, MLA_NOPE + MLA_ROPE, tm)[:, MLA_NOPE:, :]
    rot = qh[:, MLA_NOPE:, :] * cosA + qsw * sinA
    mqr[0, :, 0:MLA_NOPE, :] = (qh[:, 0:MLA_NOPE, :] * cq_mla).astype(BF16)
    mqr[0, :, MLA_NOPE:, :] = (rot * cq_mla).astype(BF16)
    mqf[0] = (qh * cq_mla).astype(BF16)
    ckvn = (_rms_rows(rows(_O_CKV, MLA_KV_RANK)) * gkv_ref[0]).astype(BF16)
    vm = jnp.dot(wvT_ref[0], ckvn, preferred_element_type=F32)
    mv[0, :, 0:64, :] = vm.reshape(MLA_HEADS, MLA_VDIM, tm).astype(BF16)
    mv[0, :, 64:V_ROWS, :] = ones_rows
    kpe = rows(_O_KPE, MLA_ROPE)
    kper = kpe * cosA + _pairswap_rows(kpe) * sinA
    kin = jnp.concatenate([ckvn, kper.astype(BF16)], axis=0)
    kmT = jnp.dot(wkT_ref[0], kin, preferred_element_type=F32)
    mk[0] = kmT.T.astype(BF16)

    r_q2 = rows(_O_RQ, RET_W)
    r_q = r_q2.reshape(RET_HEADS, RET_DIM, tm)
    r_qs = _pairswap_rows(r_q2).reshape(RET_HEADS, RET_DIM, tm)
    rq[0] = (r_q * cosR + r_qs * sinR).astype(BF16)
    r_k2 = rows(_O_RK, RET_W)
    r_k = r_k2.reshape(RET_HEADS, RET_DIM, tm)
    r_ks = _pairswap_rows(r_k2).reshape(RET_HEADS, RET_DIM, tm)
    rkr = (r_k * cosR + r_ks * sinR) * kscale
    rkp = jnp.concatenate([rkr, jnp.zeros_like(rkr)], axis=1).reshape(RET_HEADS * LANES, tm)
    rk[0] = rkp.T.astype(BF16)
    rv[0] = rows(_O_RV, RET_W).reshape(RET_HEADS, RET_DIM, tm).astype(BF16)
    rg[0] = rows(_O_RG, RET_W)


def _in_proj(l, xc, modr, wT_all, tabs, gq, wuqT, gkv, wvT, wkT, S):
    B, Sp, D = xc.shape
    nt = Sp // TM
    n_lat_tiles = S // TM
    cosA, sinA, cosR, sinR = tabs

    def mod_spec(j):
        return pl.BlockSpec((1, 1, 1, 1, D), lambda b, t: (l, jnp.where(t >= n_lat_tiles, B, b), j, 0, 0))

    def wspec(a):
        return pl.BlockSpec((1,) + a.shape[1:], lambda b, t: (l,) + (0,) * (a.ndim - 1))

    def fm4(h, d):
        return pl.BlockSpec((1, h, d, TM), lambda b, t: (b, 0, 0, t))

    def tmaj(w):
        return pl.BlockSpec((1, TM, w), lambda b, t: (b, t, 0))

    kern = functools.partial(
        _in_kernel,
        cq_da=float(DA_DIM ** -0.5 * LOG2E),
        cq_mla=float((MLA_NOPE + MLA_ROPE) ** -0.5 * LOG2E),
        kscale=float(RET_DIM ** -0.5),
    )
    sds = jax.ShapeDtypeStruct
    return pl.pallas_call(
        kern,
        grid=(B, nt),
        in_specs=[
            pl.BlockSpec((1, TM, D), lambda b, t: (b, t, 0)),
            mod_spec(0), mod_spec(1),
            wspec(wT_all),
            pl.BlockSpec((DA_DIM, TM), lambda b, t: (0, t)),
            pl.BlockSpec((DA_DIM, TM), lambda b, t: (0, t)),
            pl.BlockSpec((RET_DIM, TM), lambda b, t: (0, t)),
            pl.BlockSpec((RET_DIM, TM), lambda b, t: (0, t)),
            wspec(gq), wspec(wuqT), wspec(gkv), wspec(wvT), wspec(wkT),
        ],
        out_specs=[
            fm4(12, DA_DIM), fm4(12, DA_DIM), tmaj(DA_HEADS * LANES), fm4(DA_HEADS, V_ROWS),
            fm4(MLA_HEADS, 96), fm4(MLA_HEADS, 96), tmaj(MLA_HEADS * LANES), fm4(MLA_HEADS, V_ROWS),
            fm4(RET_HEADS, RET_DIM), tmaj(RET_HEADS * LANES), fm4(RET_HEADS, RET_DIM),
            pl.BlockSpec((1, RET_W, TM), lambda b, t: (b, 0, t)),
        ],
        out_shape=[
            sds((B, 12, DA_DIM, Sp), BF16), sds((B, 12, DA_DIM, Sp), BF16),
            sds((B, Sp, DA_HEADS * LANES), BF16), sds((B, DA_HEADS, V_ROWS, Sp), BF16),
            sds((B, MLA_HEADS, 96, Sp), BF16), sds((B, MLA_HEADS, 96, Sp), BF16),
            sds((B, Sp, MLA_HEADS * LANES), BF16), sds((B, MLA_HEADS, V_ROWS, Sp), BF16),
            sds((B, RET_HEADS, RET_DIM, Sp), BF16), sds((B, Sp, RET_HEADS * LANES), BF16),
            sds((B, RET_HEADS, RET_DIM, Sp), BF16), sds((B, RET_W, Sp), F32),
        ],
        scratch_shapes=[pltpu.VMEM((N_ZROWS, TM), F32)],
        compiler_params=pltpu.CompilerParams(
            dimension_semantics=("arbitrary", "arbitrary"), vmem_limit_bytes=56 * 1024 * 1024),
        name="in_proj",
    )(xc, modr, modr, wT_all, cosA, sinA, cosR, sinR, gq, wuqT, gkv, wvT, wkT)


def _attn_kernel(*refs, mode, S, C, lam_init, ctx_only, nsub):
    refs = list(refs)
    qr_ref, qf_ref, k_ref, v_ref = refs[:4]
    del refs[:4]
    if mode == "da":
        lam_ref, g_ref = refs[:2]
        del refs[:2]
    if ctx_only:
        del refs[:1]
    o_ref, m_sc, acc_sc, q_sc, jump_sc = refs[:5]
    nb = ATTN_NBUF
    pbuf, albuf = (refs[5 + i * nb: 5 + (i + 1) * nb] for i in range(2))
    tq = o_ref.shape[2] // nsub
    n_lat = S // TK

    def qpad(ref, t):
        cols = slice(t * tq, (t + 1) * tq)
        if mode == "da":
            z = jnp.zeros((DA_DIM, tq), BF16)
            c1 = jnp.concatenate([ref[0, 0, :, cols], z, z, z], axis=0)
            c2 = jnp.concatenate([z, ref[0, 1, :, cols], z, z], axis=0)
            return jnp.concatenate([c1, c2], axis=1)
        return jnp.concatenate([ref[0, 0, :, cols], jnp.zeros((LANES - 96, tq), BF16)], axis=0)

    def chunk(j):
        return pl.ds(j * TK if isinstance(j, int) else pl.multiple_of(j * TK, TK), TK)

    def context_keys(t):
        c0 = 0 if ctx_only else S
        s = jnp.dot(k_ref[0, c0:c0 + C, :], qpad(qf_ref, t), preferred_element_type=F32)
        m_c = jnp.max(s, axis=0, keepdims=True)
        m_sc[t] = m_c
        acc_sc[t] = jnp.dot(v_ref[0, 0, :, c0:c0 + C], jnp.exp2(s - m_c).astype(BF16), preferred_element_type=F32)

    for t in range(nsub):
        context_keys(t)

    if not ctx_only:
        for t in range(nsub):
            q_sc[t] = qpad(qr_ref, t)
        jump_sc[...] = jnp.full(jump_sc.shape, -jnp.inf, F32)

        def probs(t, j, r):
            s = jnp.dot(k_ref[0, chunk(j), :], q_sc[t], preferred_element_type=F32)
            m_prev = m_sc[t]
            pbuf[r][t] = jnp.exp2(s - m_prev).astype(BF16)
            m_chunk = jnp.max(s, axis=0, keepdims=True)
            m_new = jnp.maximum(m_prev, m_chunk)
            jump_sc[t] = jnp.maximum(jump_sc[t], m_chunk - m_prev)
            albuf[r][t] = jnp.exp2(m_prev - m_new)
            m_sc[t] = m_new

        def accumulate(t, j, r):
            pv = jnp.dot(v_ref[0, 0, :, chunk(j)], pbuf[r][t], preferred_element_type=F32)
            acc_sc[t] = (acc_sc[t] + pv) * albuf[r][t]

        def step(j, r):
            for t in range(nsub):
                accumulate(t, j, r)
                probs(t, j + 1, (r + 1) % nb)

        for t in range(nsub):
            probs(t, 0, 0)
        n_steps = n_lat - 1
        trips = n_steps // ATTN_UNROLL

        def body(i, carry):
            for u in range(ATTN_UNROLL):
                step(i * ATTN_UNROLL + u, u % nb)
            return carry

        lax.fori_loop(0, trips, body, 0)
        for j in range(trips * ATTN_UNROLL, n_steps):
            step(j, j % nb)
        for t in range(nsub):
            accumulate(t, n_lat - 1, (n_lat - 1) % nb)

        @pl.when(jnp.max(jump_sc[...]) > ATTN_MAX_JUMP)
        def _():
            for t in range(nsub):
                context_keys(t)

                def exact_step(j, carry, t=t):
                    s = jnp.dot(k_ref[0, chunk(j), :], q_sc[t], preferred_element_type=F32)
                    m_old = m_sc[t]
                    m_new = jnp.maximum(m_old, jnp.max(s, axis=0, keepdims=True))
                    pv = jnp.dot(v_ref[0, 0, :, chunk(j)], jnp.exp2(s - m_new).astype(BF16),
                                 preferred_element_type=F32)
                    acc_sc[t] = acc_sc[t] * jnp.exp2(m_old - m_new) + pv
                    m_sc[t] = m_new
                    return carry

                lax.fori_loop(0, n_lat, exact_step, 0)

    for t in range(nsub):
        acc = acc_sc[t]
        o = acc[0:64, :] / acc[64:65, :]
        cols = slice(t * tq, (t + 1) * tq)
        if mode == "da":
            lf = lam_ref[0]
            lam = (jnp.exp(jnp.sum(lf[0:1] * lf[1:2], axis=1, keepdims=True))
                   - jnp.exp(jnp.sum(lf[2:3] * lf[3:4], axis=1, keepdims=True)) + lam_init)
            od = o[:, 0:tq] - lam * o[:, tq:2 * tq]
            y = _rms_rows(od) * g_ref[0]
            o_ref[0, :, cols] = (y * (1.0 - lam_init)).astype(BF16)
        else:
            o_ref[0, :, cols] = o.astype(BF16)


def _attention_call(mode, l, qr, qf, k, v, S, C, lam, g, y_latent):
    B, nmaps, d, Sp = qr.shape
    H = v.shape[1]
    nm = nmaps // H
    ctx_only = y_latent is not None
    nsub = 1 if ctx_only else ATTN_SUBTILES
    tq = C if ctx_only else ATTN_N // nm
    n = nm * tq
    tqb = nsub * tq
    q0 = S // tqb if ctx_only else 0
    lam_init = 0.8 - 0.6 * math.exp(-0.3 * l)
    kern = functools.partial(_attn_kernel, mode=mode, S=S, C=C, lam_init=lam_init, ctx_only=ctx_only, nsub=nsub)
    in_specs = [
        pl.BlockSpec((1, nm, d, tqb), lambda b, h, i: (b, h, 0, q0 + i)),
        pl.BlockSpec((1, nm, d, tqb), lambda b, h, i: (b, h, 0, q0 + i)),
        (pl.BlockSpec((1, C, LANES), lambda b, h, i: (b, S // C, h)) if ctx_only
         else pl.BlockSpec((1, Sp, LANES), lambda b, h, i: (b, 0, h))),
        (pl.BlockSpec((1, 1, V_ROWS, C), lambda b, h, i: (b, h, 0, S // C)) if ctx_only
         else pl.BlockSpec((1, 1, V_ROWS, Sp), lambda b, h, i: (b, h, 0, 0))),
    ]
    args = [qr, qf, k, v]
    if mode == "da":
        in_specs += [pl.BlockSpec((1,) + lam.shape[1:], lambda b, h, i: (l, 0, 0)),
                     pl.BlockSpec((1,) + g.shape[1:], lambda b, h, i: (l, 0, 0))]
        args += [lam, g]
    aliases = {}
    if ctx_only:
        in_specs.append(pl.BlockSpec(memory_space=pl.ANY))
        args.append(y_latent)
        aliases = {len(args) - 1: 0}
    tk = C if ctx_only else TK
    return pl.pallas_call(
        kern,
        grid=(B, H, 1 if ctx_only else S // tqb),
        in_specs=in_specs,
        out_specs=pl.BlockSpec((1, 64, tqb), lambda b, h, i: (b, h, q0 + i)),
        out_shape=jax.ShapeDtypeStruct((B, H * 64, Sp), BF16),
        input_output_aliases=aliases,
        scratch_shapes=(
            [pltpu.VMEM((nsub, 1, n), F32), pltpu.VMEM((nsub, V_ROWS, n), F32), pltpu.VMEM((nsub, LANES, n), BF16),
             pltpu.VMEM((nsub, 1, n), F32)]
            + [pltpu.VMEM((nsub, tk, n), BF16)] * ATTN_NBUF + [pltpu.VMEM((nsub, 1, n), F32)] * ATTN_NBUF),
        compiler_params=pltpu.CompilerParams(
            dimension_semantics=("arbitrary", "arbitrary", "arbitrary"), vmem_limit_bytes=48 * 1024 * 1024),
        name="attn_" + mode + ("_ctx" if ctx_only else ""),
    )(*args)


def _attention(mode, l, qr, qf, k, v, S, C, with_ctx, lam=None, g=None):
    y = _attention_call(mode, l, qr, qf, k, v, S, C, lam, g, None)
    return _attention_call(mode, l, qr, qf, k, v, S, C, lam, g, y) if with_ctx else y


def _ret_kernel(q_ref, k_ref, v_ref, g_ref, dec_ref, o_ref, sb_sc, *, S, C):
    CH = RET_CHUNK
    nl, nc = S // CH, C // CH

    def log_sigmoid(x):
        return jnp.minimum(x, 0.0) - jnp.log1p(jnp.exp(-jnp.abs(x)))

    lgf = log_sigmoid(dec_ref[0, 0][0:1, :])
    lgb = log_sigmoid(dec_ref[1, 0][0:1, :])
    ii = lax.broadcasted_iota(jnp.int32, (CH, CH), 1).astype(F32)
    jj = lax.broadcasted_iota(jnp.int32, (CH, CH), 0).astype(F32)
    dd = ii - jj
    fwd = dd >= 0
    mt = (jnp.where(fwd, jnp.exp(lgf * jnp.where(fwd, dd, 0.0)), 0.0)
          + jnp.where(fwd, 0.0, jnp.exp(lgb * jnp.where(fwd, 0.0, -dd))))
    xi_f = jnp.exp(lgf * (ii + 1.0))
    xi_b = jnp.exp(lgb * (CH - ii))
    zeta_f = jnp.exp(lgf * (CH - 1.0 - jj))
    zeta_b = jnp.exp(lgb * jj)
    cdf = jnp.exp(lgf * CH)
    cdb = jnp.exp(lgb * CH)
    zq = jnp.zeros((RET_DIM, CH), BF16)

    def sl(c):
        return pl.ds(pl.multiple_of(c * CH, CH), CH)

    def state_update(c, zeta):
        kz = (k_ref[0, sl(c), :].astype(F32) * zeta).astype(BF16)
        return jnp.dot(v_ref[0, 0, :, sl(c)], kz, preferred_element_type=F32)

    def bwd_step(c, sb):
        sb_sc[c] = sb
        return sb * cdb + state_update(c, zeta_b)

    sb = jnp.zeros((RET_DIM, CH), F32)
    for c in range(nl + nc - 1, nl - 1, -1):
        sb = bwd_step(c, sb)
    lax.fori_loop(0, nl, lambda n, s: bwd_step(nl - 1 - n, s), sb, unroll=RET_UNROLL)

    def fwd_step(c, sf):
        qc = q_ref[0, 0, :, sl(c)]
        qp = jnp.concatenate([qc, zq], axis=0)
        kc = k_ref[0, sl(c), :]
        vc = v_ref[0, 0, :, sl(c)]
        at = jnp.dot(kc, qp, preferred_element_type=F32)
        qpf = qp.astype(F32)
        rhs = jnp.concatenate([(at * mt).astype(BF16), (qpf * xi_f).astype(BF16), (qpf * xi_b).astype(BF16)], axis=0)
        lhs = jnp.concatenate([vc, sf.astype(BF16), sb_sc[c].astype(BF16)], axis=1)
        o = jnp.dot(lhs, rhs, preferred_element_type=F32)
        gch = g_ref[0, :, sl(c)]
        o_ref[0, :, sl(c)] = (gch * _sigmoid(gch) * _rms_rows(o)).astype(BF16)
        return sf * cdf + state_update(c, zeta_f)

    sf = jnp.zeros((RET_DIM, CH), F32)
    for c in range(nl, nl + nc):
        sf = fwd_step(c, sf)
    lax.fori_loop(0, nl, fwd_step, sf, unroll=RET_UNROLL)


def _retention(rq, rk, rv, rg, dec, S, C):
    B, H, d, Sp = rq.shape
    kern = functools.partial(_ret_kernel, S=S, C=C)
    return pl.pallas_call(
        kern,
        grid=(B, H),
        in_specs=[
            pl.BlockSpec((1, 1, d, Sp), lambda b, h: (b, h, 0, 0)),
            pl.BlockSpec((1, Sp, LANES), lambda b, h: (b, 0, h)),
            pl.BlockSpec((1, 1, d, Sp), lambda b, h: (b, h, 0, 0)),
            pl.BlockSpec((1, d, Sp), lambda b, h: (b, h, 0)),
            pl.BlockSpec((2, 1, 8, LANES), lambda b, h: (0, h, 0, 0)),
        ],
        out_specs=pl.BlockSpec((1, d, Sp), lambda b, h: (b, h, 0)),
        out_shape=jax.ShapeDtypeStruct((B, H * d, Sp), BF16),
        scratch_shapes=[pltpu.VMEM((Sp // RET_CHUNK, RET_DIM, RET_CHUNK), F32)],
        compiler_params=pltpu.CompilerParams(
            dimension_semantics=("arbitrary", "arbitrary"), vmem_limit_bytes=56 * 1024 * 1024),
        name="retention",
    )(rq, rk, rv, rg, dec)


def _layer_norm(t, g, b):
    mu = jnp.mean(t, axis=-1, keepdims=True)
    tc = t - mu
    var = jnp.mean(tc * tc, axis=-1, keepdims=True)
    return tc * lax.rsqrt(var + LN_EPS) * g + b


def _out_kernel(yda_ref, ymla_ref, yret_ref, w_ref, x_ref, gate_ref, g1_ref, b1_ref, sh2_ref, sc2_ref, rw_ref,
                x1_ref, h2_ref, lg_ref, *, alpha):
    proj = (lax.dot_general(yda_ref[0], w_ref[0, 0:DA_VW, :], TN_DIMS, preferred_element_type=F32)
            + lax.dot_general(ymla_ref[0], w_ref[0, DA_VW:DA_VW + MLA_VW, :], TN_DIMS, preferred_element_type=F32)
            + lax.dot_general(yret_ref[0], w_ref[0, DA_VW + MLA_VW:, :], TN_DIMS, preferred_element_type=F32))
    x1 = _layer_norm(alpha * x_ref[0] + gate_ref[0, 0, 0] * proj, g1_ref[0], b1_ref[0])
    x1_ref[0] = x1
    h2 = x1 * (1.0 + sc2_ref[0, 0, 0]) + sh2_ref[0, 0, 0]
    h2_ref[0] = h2.astype(BF16)
    h2_hi = h2.astype(BF16)
    h2_lo = (h2 - h2_hi.astype(F32)).astype(BF16)
    hi = jnp.dot(h2_hi, rw_ref[...], preferred_element_type=F32)
    lo = jnp.dot(h2_lo, rw_ref[:, 0:LANES], preferred_element_type=F32)
    lg_ref[0] = hi[:, 0:LANES] + (hi[:, LANES:] + lo)


def _out_proj(l, yda, ymla, yret, w_out, xc, modr, ln_g, ln_b, rw_pad, S, alpha, latents_only):
    B, Sp, D = xc.shape
    n_lat_tiles = S // TM
    n_rows = S if latents_only else Sp

    def mod_spec(j):
        return pl.BlockSpec((1, 1, 1, 1, D), lambda b, t: (l, jnp.where(t >= n_lat_tiles, B, b), j, 0, 0))

    def fm(a):
        return pl.BlockSpec((1, a.shape[1], TM), lambda b, t: (b, 0, t))

    def wspec(a):
        return pl.BlockSpec((1,) + a.shape[1:], lambda b, t: (l,) + (0,) * (a.ndim - 1))

    row = pl.BlockSpec((1, TM, D), lambda b, t: (b, t, 0))
    sds = jax.ShapeDtypeStruct
    return pl.pallas_call(
        functools.partial(_out_kernel, alpha=alpha),
        grid=(B, n_rows // TM),
        in_specs=[fm(yda), fm(ymla), fm(yret), wspec(w_out), row, mod_spec(2), wspec(ln_g), wspec(ln_b),
                  mod_spec(3), mod_spec(4), pl.BlockSpec(rw_pad.shape, lambda b, t: (0, 0))],
        out_specs=[row, row, pl.BlockSpec((1, TM, LANES), lambda b, t: (b, t, 0))],
        out_shape=[sds((B, n_rows, D), F32), sds((B, n_rows, D), BF16), sds((B, n_rows, LANES), F32)],
        compiler_params=pltpu.CompilerParams(dimension_semantics=("arbitrary", "arbitrary")),
        name="out_proj",
    )(yda, ymla, yret, w_out, xc, modr, ln_g, ln_b, modr, modr, rw_pad)


def _gates_T(logits_T, rb):
    s = _sigmoid(logits_T)
    ch = s + rb
    srow = [s[e:e + 1] for e in range(N_EXPERTS)]
    crow = [ch[e:e + 1] for e in range(N_EXPERTS)]
    per = N_EXPERTS // N_GROUPS
    gs = []
    for g in range(N_GROUPS):
        a, b, c, d = crow[per * g: per * g + per]
        m1, n1, m2, n2 = jnp.maximum(a, b), jnp.minimum(a, b), jnp.maximum(c, d), jnp.minimum(c, d)
        gs.append(jnp.maximum(m1, m2) + jnp.maximum(jnp.minimum(m1, m2), jnp.maximum(n1, n2)))
    gmax = functools.reduce(jnp.maximum, gs)
    taken = jnp.zeros(gmax.shape, jnp.bool_)
    gsel = []
    for g in range(N_GROUPS):
        sg = jnp.logical_and(gs[g] == gmax, jnp.logical_not(taken))
        gsel.append(sg)
        taken = jnp.logical_or(taken, sg)
    neg = jnp.full(gmax.shape, -jnp.inf, F32)
    mc = [jnp.where(gsel[e // per], crow[e], neg) for e in range(N_EXPERTS)]
    sel = [jnp.zeros(gmax.shape, jnp.bool_) for _ in range(N_EXPERTS)]
    for _ in range(2):
        top = functools.reduce(jnp.maximum, mc)
        taken = jnp.zeros(gmax.shape, jnp.bool_)
        for e in range(N_EXPERTS):
            hit = jnp.logical_and(mc[e] == top, jnp.logical_not(taken))
            taken = jnp.logical_or(taken, hit)
            sel[e] = jnp.logical_or(sel[e], hit)
            mc[e] = jnp.where(hit, neg, mc[e])
    w = [jnp.where(sel[e], srow[e], 0.0) for e in range(N_EXPERTS)]
    wsum = functools.reduce(lambda a, b: a + b, w)
    gates = jnp.concatenate([we / wsum * ROUTED_SCALE for we in w], axis=0)
    return gates, [jnp.where(sg, 1.0, 0.0) for sg in gsel]


def _moe_capacity(tm):
    return -(-int(tm / N_GROUPS + 5.5 * math.sqrt(tm * (N_GROUPS - 1)) / N_GROUPS) // 32) * 32


def _moe_kernel(h_ref, lg_ref, rb_ref, w13_ref, w2_ref, x1_ref, gb_ref, gc_ref, g2_ref, b2_ref, o_ref,
                gates_sc, acc_sc, dense_sc, perm_sc, hs_sc, gs_sc, accs_sc, earlier_sc, *, S, alpha, cap):
    e = pl.program_id(2)
    tm = h_ref.shape[1]
    per = N_EXPERTS // N_GROUPS
    lane = lax.broadcasted_iota(jnp.int32, (1, LANES), 1)

    def expert(x):
        ab = jnp.dot(x, w13_ref[0, 0], preferred_element_type=F32)
        a = ab[:, 0:D_EXPERT]
        hid = (a * _sigmoid(a) * ab[:, D_EXPERT:]).astype(BF16)
        return jnp.dot(hid, w2_ref[0, 0], preferred_element_type=F32)

    def gate_col(g):
        return jnp.sum(jnp.where(lane == e, g, 0.0), axis=1, keepdims=True)

    @pl.when(jnp.logical_and(jnp.logical_and(pl.program_id(0) == 0, pl.program_id(1) == 0), e == 0))
    def _():
        before = lax.broadcasted_iota(jnp.int32, (tm, tm), 0) < lax.broadcasted_iota(jnp.int32, (tm, tm), 1)
        earlier_sc[...] = jnp.where(before, 1.0, 0.0).astype(BF16)

    @pl.when(e == 0)
    def _():
        gT, gsel = _gates_T(lg_ref[0].T[0:N_EXPERTS, :], rb_ref[...])
        gates = jnp.concatenate([gT, jnp.zeros((LANES - N_EXPERTS, tm), F32)], axis=0).T
        gates_sc[...] = gates
        acc_sc[...] = jnp.zeros(acc_sc.shape, F32)
        member = jnp.concatenate(gsel + [jnp.zeros((16 - N_GROUPS, tm), F32)], axis=0)
        rank = jnp.dot(member.astype(BF16), earlier_sc[...], preferred_element_type=F32)
        slot = functools.reduce(lambda a, b: a + b,
                                [gsel[g] * (rank[g:g + 1] + float(g * cap)) for g in range(N_GROUPS)])
        count = jnp.sum(member, axis=1, keepdims=True)
        dense_sc[0] = jnp.where(jnp.max(count) > float(cap), 1, 0)
        perm = jnp.where(lax.broadcasted_iota(jnp.int32, (N_GROUPS * cap, tm), 0) == slot.astype(jnp.int32), 1.0, 0.0)
        perm = perm.astype(BF16)
        perm_sc[...] = perm
        hs_sc[...] = jnp.dot(perm, h_ref[0], preferred_element_type=F32).astype(BF16)
        g_hi = gates.astype(BF16)
        g_lo = (gates - g_hi.astype(F32)).astype(BF16)
        gs_sc[...] = (jnp.dot(perm, g_hi, preferred_element_type=F32)
                      + jnp.dot(perm, g_lo, preferred_element_type=F32))
        accs_sc[...] = jnp.zeros(accs_sc.shape, F32)

    @pl.when(dense_sc[0] == 0)
    def _():
        rows = pl.ds(pl.multiple_of((e // per) * cap, 16), cap)
        accs_sc[rows, :] += gate_col(gs_sc[rows, :]) * expert(hs_sc[rows, :])

    @pl.when(dense_sc[0] != 0)
    def _():
        acc_sc[...] += gate_col(gates_sc[...]) * expert(h_ref[0])

    @pl.when(e == N_EXPERTS - 1)
    def _():
        @pl.when(dense_sc[0] == 0)
        def _():
            acc_sc[...] = lax.dot_general(perm_sc[...], accs_sc[...].astype(BF16), TN_DIMS,
                                          preferred_element_type=F32)

        n_lat = S - pl.program_id(1) * tm
        rowi = lax.broadcasted_iota(jnp.int32, (tm, 1), 0)
        gate = jnp.where(rowi < n_lat, gb_ref[0, 0, 0], gc_ref[0, 0, 0])
        o_ref[0] = _layer_norm(alpha * x1_ref[0] + gate * acc_sc[...], g2_ref[0], b2_ref[0])


def _moe(l, h2, logits, rb, w13, w2, x1, modr, ln_g, ln_b, S, alpha, latents_only):
    B, Sp, D = x1.shape
    tm = TM_MOE_LAST if latents_only else TM_MOE
    n_rows = S if latents_only else Sp

    def wspec(a):
        return pl.BlockSpec((1,) + a.shape[1:], lambda b, t, e: (l,) + (0,) * (a.ndim - 1))

    row = lambda b, t, e: (b, t, 0)
    cap = _moe_capacity(tm)
    return pl.pallas_call(
        functools.partial(_moe_kernel, S=S, alpha=alpha, cap=cap),
        grid=(B, n_rows // tm, N_EXPERTS),
        in_specs=[
            pl.BlockSpec((1, tm, D), row),
            pl.BlockSpec((1, tm, LANES), row),
            pl.BlockSpec(rb.shape, lambda b, t, e: (0, 0)),
            pl.BlockSpec((1, 1, D, 2 * D_EXPERT), lambda b, t, e: (l, e, 0, 0)),
            pl.BlockSpec((1, 1, D_EXPERT, D), lambda b, t, e: (l, e, 0, 0)),
            pl.BlockSpec((1, tm, D), row),
            pl.BlockSpec((1, 1, 1, 1, D), lambda b, t, e: (l, b, 5, 0, 0)),
            pl.BlockSpec((1, 1, 1, 1, D), lambda b, t, e: (l, B, 5, 0, 0)),
            wspec(ln_g), wspec(ln_b),
        ],
        out_specs=pl.BlockSpec((1, tm, D), row),
        out_shape=jax.ShapeDtypeStruct((B, n_rows, D), F32),
        scratch_shapes=[
            pltpu.VMEM((tm, LANES), F32), pltpu.VMEM((tm, D), F32), pltpu.SMEM((1,), jnp.int32),
            pltpu.VMEM((N_GROUPS * cap, tm), BF16), pltpu.VMEM((N_GROUPS * cap, D), BF16),
            pltpu.VMEM((N_GROUPS * cap, LANES), F32), pltpu.VMEM((N_GROUPS * cap, D), F32),
            pltpu.VMEM((tm, tm), BF16)],
        compiler_params=pltpu.CompilerParams(
            dimension_semantics=("arbitrary", "arbitrary", "arbitrary"), vmem_limit_bytes=56 * 1024 * 1024),
        name="moe",
    )(h2, logits, rb, w13, w2, x1, modr, modr, ln_g, ln_b)


def _rope_tables(S, C, rot_dim):
    rows = S // GRID_W
    r = jnp.repeat(jnp.arange(rows, dtype=F32), GRID_W)
    col = jnp.tile(jnp.arange(GRID_W, dtype=F32), rows)
    n_freq = rot_dim // 4
    freqs = ROPE_BASE ** (-jnp.arange(n_freq, dtype=F32) / n_freq)
    ang = jnp.concatenate([r[:, None] * freqs, col[:, None] * freqs], -1)
    cos, sin = jnp.cos(ang), jnp.sin(ang)
    cos_rep = jnp.repeat(cos, 2, axis=-1)
    sin_alt = jnp.stack([-sin, sin], -1).reshape(S, rot_dim)
    cos_rep = jnp.concatenate([cos_rep, jnp.ones((C, rot_dim), F32)], 0)
    sin_alt = jnp.concatenate([sin_alt, jnp.zeros((C, rot_dim), F32)], 0)
    return cos_rep.T, sin_alt.T


def kernel(x, c, ctx, c_ctx, w_ada, b_ada, w_in, da_lambda, da_subln, mla_q_norm, mla_w_uq, mla_kv_norm, mla_w_ukv, ret_decay_f, ret_decay_b, w_out, ln1_g, ln1_b, router_w, router_b, exp_w1, exp_w3, exp_w2, ln2_g, ln2_b):
    B, S, D = x.shape
    C = ctx.shape[1]
    depth = w_in.shape[0]
    alpha = float((2 * depth) ** 0.25)

    cvec = jnp.concatenate([c, c_ctx[None, :], jnp.zeros((8 - B - 1, D), F32)], 0)
    mod = _modulation(cvec, w_ada, b_ada)
    modr = mod.reshape(depth, 8, N_MOD, 1, D)

    wT_all = jnp.swapaxes(w_in, 1, 2).astype(BF16)
    wuqT = jnp.swapaxes(mla_w_uq, 1, 2).astype(BF16)
    ukv = mla_w_ukv.reshape(depth, MLA_KV_RANK, MLA_HEADS, MLA_NOPE + MLA_VDIM)
    wvT = jnp.swapaxes(ukv[..., MLA_NOPE:].reshape(depth, MLA_KV_RANK, MLA_VW), 1, 2).astype(BF16)
    nopeT = jnp.transpose(ukv[..., :MLA_NOPE], (0, 2, 3, 1))
    top = jnp.concatenate([nopeT, jnp.zeros((depth, MLA_HEADS, MLA_NOPE, MLA_ROPE), F32)], -1)
    mid = jnp.concatenate([jnp.zeros((MLA_ROPE, MLA_KV_RANK), F32), jnp.eye(MLA_ROPE, dtype=F32)], -1)
    mid = jnp.broadcast_to(mid, (depth, MLA_HEADS, MLA_ROPE, MLA_KV_RANK + MLA_ROPE))
    bot = jnp.zeros((depth, MLA_HEADS, LANES - MLA_NOPE - MLA_ROPE, MLA_KV_RANK + MLA_ROPE), F32)
    wkT = jnp.concatenate([top, mid, bot], 2).reshape(depth, MLA_HEADS * LANES, MLA_KV_RANK + MLA_ROPE).astype(BF16)
    gq = mla_q_norm[:, :, None]
    gkv = mla_kv_norm[:, :, None]
    subln = da_subln[:, :, None]
    w_out_b = w_out.astype(BF16)
    w13 = jnp.concatenate([exp_w1, exp_w3], -1).astype(BF16)
    w2 = exp_w2.astype(BF16)
    rw_f = jnp.concatenate([router_w, jnp.zeros((D, LANES - N_EXPERTS), F32)], -1)
    rw_hi = rw_f.astype(BF16)
    rw_pad = jnp.concatenate([rw_hi, (rw_f - rw_hi.astype(F32)).astype(BF16)], -1)
    rb = router_b[:, None]
    dec = jnp.broadcast_to(jnp.stack([ret_decay_f, ret_decay_b], 1)[:, :, :, None, None],
                           (depth, 2, RET_HEADS, 8, LANES))
    tabs = _rope_tables(S, C, DA_DIM) + _rope_tables(S, C, RET_DIM)
    ln1g, ln1b, ln2g, ln2b = (a[:, None, :] for a in (ln1_g, ln1_b, ln2_g, ln2_b))

    xc = jnp.concatenate([x, ctx], 1)
    for l in range(depth):
        last = l == depth - 1
        (daqr, daqf, dak, dav, mqr, mqf, mk, mv, rq, rk, rv, rg) = _in_proj(
            l, xc, modr, wT_all, tabs, gq, wuqT, gkv, wvT, wkT, S)
        yda = _attention("da", l, daqr, daqf, dak, dav, S, C, not last, lam=da_lambda, g=subln)
        ymla = _attention("mla", l, mqr, mqf, mk, mv, S, C, not last)
        yret = _retention(rq, rk, rv, rg, dec[l], S, C)
        x1, h2, logits = _out_proj(l, yda, ymla, yret, w_out_b, xc, modr, ln1g, ln1b, rw_pad, S, alpha, last)
        xc = _moe(l, h2, logits, rb, w13, w2, x1, modr, ln2g, ln2b, S, alpha, last)
    return xc
```

```python
import functools
import math

import jax
import jax.numpy as jnp
import numpy as np
from jax import lax
from jax.experimental import pallas as pl
from jax.experimental.pallas import tpu as pltpu

F32 = jnp.float32
BF16 = jnp.bfloat16

GRID_W = 64
DA_HEADS, DA_DIM, DA_VDIM = 6, 32, 64
MLA_HEADS, MLA_NOPE, MLA_ROPE, MLA_VDIM = 6, 64, 32, 64
MLA_Q_RANK, MLA_KV_RANK = 256, 128
RET_HEADS, RET_DIM, RET_CHUNK = 4, 64, 128
ROPE_BASE = 10000.0
N_EXPERTS, N_GROUPS, D_EXPERT = 16, 4, 512
ROUTED_SCALE = 1.0
N_MOD = 6
LN_EPS = 1e-5
RMS_EPS = 1e-6
LOG2E = math.log2(math.e)

DA_W = DA_HEADS * 2 * DA_DIM
DA_VW = DA_HEADS * DA_VDIM
RET_W = RET_HEADS * RET_DIM
MLA_QW = MLA_HEADS * (MLA_NOPE + MLA_ROPE)
MLA_VW = MLA_HEADS * MLA_VDIM
IN_SPLIT = (DA_W, DA_W, DA_VW, MLA_Q_RANK, MLA_KV_RANK, MLA_ROPE, RET_W, RET_W, RET_W, RET_W)

LANES = 128
V_ROWS = 80
TM = 256
ATTN_N = 512
TK = 2048
ATTN_SUBTILES = 2
ATTN_NBUF = 2
ATTN_MAX_JUMP = 32.0
ATTN_UNROLL = 2
TM_MOE = 640
TM_MOE_LAST = 512
MOE_SUBTILES = 2
RET_UNROLL = 8

NT_DIMS = (((1,), (1,)), ((), ()))
TN_DIMS = (((0,), (0,)), ((), ()))


def _sigmoid(x):
    return 1.0 / (1.0 + jnp.exp(-x))


def _mod_kernel(c_ref, w_ref, b_ref, o_ref):
    c = c_ref[...]
    sc = (c * _sigmoid(c)).astype(BF16)
    o_ref[0] = jnp.dot(sc, w_ref[0].astype(BF16), preferred_element_type=F32) + b_ref[0]


def _modulation(cvec, w_ada, b_ada):
    depth, d, n = w_ada.shape
    tn = 1536
    return pl.pallas_call(
        _mod_kernel,
        grid=(depth, n // tn),
        in_specs=[
            pl.BlockSpec((8, d), lambda l, j: (0, 0)),
            pl.BlockSpec((1, d, tn), lambda l, j: (l, 0, j)),
            pl.BlockSpec((1, 1, tn), lambda l, j: (l, 0, j)),
        ],
        out_specs=pl.BlockSpec((1, 8, tn), lambda l, j: (l, 0, j)),
        out_shape=jax.ShapeDtypeStruct((depth, 8, n), F32),
        name="adaln_mod",
    )(cvec, w_ada, b_ada.reshape(depth, 1, n))


_O_DAQ, _O_DAK, _O_DAV, _O_CQ, _O_CKV, _O_KPE, _O_RQ, _O_RK, _O_RV, _O_RG = (
    int(v) for v in np.cumsum((0,) + IN_SPLIT)[:-1])
N_ZROWS = sum(IN_SPLIT)


def _rms_rows(x):
    return x * lax.rsqrt(jnp.mean(x * x, axis=0, keepdims=True) + RMS_EPS)


def _pairswap_rows(x):
    n = x.shape[0]
    even = (lax.broadcasted_iota(jnp.int32, x.shape, 0) & 1) == 0
    return jnp.where(even, pltpu.roll(x, n - 1, 0), pltpu.roll(x, 1, 0))


def _in_kernel(x_ref, sh_ref, sc_ref, wT_ref, cosA_ref, sinA_ref, cosR_ref, sinR_ref,
               gq_ref, wuqT_ref, gkv_ref, wvT_ref, wkT_ref,
               daqr, daqf, dak, dav, mqr, mqf, mk, mv, rq, rk, rv, rg, z_ref, *, cq_da, cq_mla, kscale):
    tm = x_ref.shape[1]
    x = x_ref[0]
    h = (x * (1.0 + sc_ref[0, 0, 0]) + sh_ref[0, 0, 0]).astype(BF16)
    z_ref[...] = lax.dot_general(wT_ref[0], h, NT_DIMS, preferred_element_type=F32)
    cosA = cosA_ref[...]
    sinA = sinA_ref[...]
    cosR = cosR_ref[...]
    sinR = sinR_ref[...]
    ones_rows = (lax.broadcasted_iota(jnp.int32, (DA_HEADS, V_ROWS - 64, tm), 1) == 0).astype(BF16)

    def rows(off, n):
        return z_ref[off:off + n, :]

    q2 = rows(_O_DAQ, DA_W)
    q = q2.reshape(12, DA_DIM, tm)
    qs = _pairswap_rows(q2).reshape(12, DA_DIM, tm)
    daqr[0] = ((q * cosA + qs * sinA) * cq_da).astype(BF16)
    daqf[0] = (q * cq_da).astype(BF16)
    k2 = rows(_O_DAK, DA_W)
    k = k2.reshape(12, DA_DIM, tm)
    ks = _pairswap_rows(k2).reshape(12, DA_DIM, tm)
    kr = (k * cosA + ks * sinA).reshape(DA_HEADS, 2 * DA_DIM, tm)
    kp = jnp.concatenate([kr, jnp.zeros_like(kr)], axis=1).reshape(DA_HEADS * LANES, tm)
    dak[0] = kp.T.astype(BF16)
    dav[0, :, 0:64, :] = rows(_O_DAV, DA_VW).reshape(DA_HEADS, DA_VDIM, tm).astype(BF16)
    dav[0, :, 64:V_ROWS, :] = ones_rows

    cqn = (_rms_rows(rows(_O_CQ, MLA_Q_RANK)) * gq_ref[0]).astype(BF16)
    qm = jnp.dot(wuqT_ref[0], cqn, preferred_element_type=F32)
    qh = qm.reshape(MLA_HEADS, MLA_NOPE + MLA_ROPE, tm)
    qsw = _pairswap_rows(qm).reshape(MLA_HEADS, MLA_NOPE + MLA_ROPE, tm)[:, MLA_NOPE:, :]
    rot = qh[:, MLA_NOPE:, :] * cosA + qsw * sinA
    mqr[0, :, 0:MLA_NOPE, :] = (qh[:, 0:MLA_NOPE, :] * cq_mla).astype(BF16)
    mqr[0, :, MLA_NOPE:, :] = (rot * cq_mla).astype(BF16)
    mqf[0] = (qh * cq_mla).astype(BF16)
    ckvn = (_rms_rows(rows(_O_CKV, MLA_KV_RANK)) * gkv_ref[0]).astype(BF16)
    vm = jnp.dot(wvT_ref[0], ckvn, preferred_element_type=F32)
    mv[0, :, 0:64, :] = vm.reshape(MLA_HEADS, MLA_VDIM, tm).astype(BF16)
    mv[0, :, 64:V_ROWS, :] = ones_rows
    kpe = rows(_O_KPE, MLA_ROPE)
    kper = kpe * cosA + _pairswap_rows(kpe) * sinA
    kin = jnp.concatenate([ckvn, kper.astype(BF16)], axis=0)
    kmT = jnp.dot(wkT_ref[0], kin, preferred_element_type=F32)
    mk[0] = kmT.T.astype(BF16)

    r_q2 = rows(_O_RQ, RET_W)
    r_q = r_q2.reshape(RET_HEADS, RET_DIM, tm)
    r_qs = _pairswap_rows(r_q2).reshape(RET_HEADS, RET_DIM, tm)
    rq[0] = (r_q * cosR + r_qs * sinR).astype(BF16)
    r_k2 = rows(_O_RK, RET_W)
    r_k = r_k2.reshape(RET_HEADS, RET_DIM, tm)
    r_ks = _pairswap_rows(r_k2).reshape(RET_HEADS, RET_DIM, tm)
    rkr = (r_k * cosR + r_ks * sinR) * kscale
    rkp = jnp.concatenate([rkr, jnp.zeros_like(rkr)], axis=1).reshape(RET_HEADS * LANES, tm)
    rk[0] = rkp.T.astype(BF16)
    rv[0] = rows(_O_RV, RET_W).reshape(RET_HEADS, RET_DIM, tm).astype(BF16)
    rg[0] = rows(_O_RG, RET_W)


def _in_proj(l, xc, modr, wT_all, tabs, gq, wuqT, gkv, wvT, wkT, S):
    B, Sp, D = xc.shape
    nt = Sp // TM
    n_lat_tiles = S // TM
    cosA, sinA, cosR, sinR = tabs

    def mod_spec(j):
        return pl.BlockSpec((1, 1, 1, 1, D), lambda b, t: (l, jnp.where(t >= n_lat_tiles, B, b), j, 0, 0))

    def wspec(a):
        return pl.BlockSpec((1,) + a.shape[1:], lambda b, t: (l,) + (0,) * (a.ndim - 1))

    def fm4(h, d):
        return pl.BlockSpec((1, h, d, TM), lambda b, t: (b, 0, 0, t))

    def tmaj(w):
        return pl.BlockSpec((1, TM, w), lambda b, t: (b, t, 0))

    kern = functools.partial(
        _in_kernel,
        cq_da=float(DA_DIM ** -0.5 * LOG2E),
        cq_mla=float((MLA_NOPE + MLA_ROPE) ** -0.5 * LOG2E),
        kscale=float(RET_DIM ** -0.5),
    )
    sds = jax.ShapeDtypeStruct
    return pl.pallas_call(
        kern,
        grid=(B, nt),
        in_specs=[
            pl.BlockSpec((1, TM, D), lambda b, t: (b, t, 0)),
            mod_spec(0), mod_spec(1),
            wspec(wT_all),
            pl.BlockSpec((DA_DIM, TM), lambda b, t: (0, t)),
            pl.BlockSpec((DA_DIM, TM), lambda b, t: (0, t)),
            pl.BlockSpec((RET_DIM, TM), lambda b, t: (0, t)),
            pl.BlockSpec((RET_DIM, TM), lambda b, t: (0, t)),
            wspec(gq), wspec(wuqT), wspec(gkv), wspec(wvT), wspec(wkT),
        ],
        out_specs=[
            fm4(12, DA_DIM), fm4(12, DA_DIM), tmaj(DA_HEADS * LANES), fm4(DA_HEADS, V_ROWS),
            fm4(MLA_HEADS, 96), fm4(MLA_HEADS, 96), tmaj(MLA_HEADS * LANES), fm4(MLA_HEADS, V_ROWS),
            fm4(RET_HEADS, RET_DIM), tmaj(RET_HEADS * LANES), fm4(RET_HEADS, RET_DIM),
            pl.BlockSpec((1, RET_W, TM), lambda b, t: (b, 0, t)),
        ],
        out_shape=[
            sds((B, 12, DA_DIM, Sp), BF16), sds((B, 12, DA_DIM, Sp), BF16),
            sds((B, Sp, DA_HEADS * LANES), BF16), sds((B, DA_HEADS, V_ROWS, Sp), BF16),
            sds((B, MLA_HEADS, 96, Sp), BF16), sds((B, MLA_HEADS, 96, Sp), BF16),
            sds((B, Sp, MLA_HEADS * LANES), BF16), sds((B, MLA_HEADS, V_ROWS, Sp), BF16),
            sds((B, RET_HEADS, RET_DIM, Sp), BF16), sds((B, Sp, RET_HEADS * LANES), BF16),
            sds((B, RET_HEADS, RET_DIM, Sp), BF16), sds((B, RET_W, Sp), F32),
        ],
        scratch_shapes=[pltpu.VMEM((N_ZROWS, TM), F32)],
        compiler_params=pltpu.CompilerParams(
            dimension_semantics=("arbitrary", "arbitrary"), vmem_limit_bytes=56 * 1024 * 1024),
        name="in_proj",
    )(xc, modr, modr, wT_all, cosA, sinA, cosR, sinR, gq, wuqT, gkv, wvT, wkT)


def _attn_kernel(*refs, mode, S, C, lam_init, ctx_only, nsub):
    refs = list(refs)
    qr_ref, qf_ref, k_ref, v_ref = refs[:4]
    del refs[:4]
    if mode == "da":
        lam_ref, g_ref = refs[:2]
        del refs[:2]
    if ctx_only:
        del refs[:1]
    o_ref, m_sc, acc_sc, q_sc, jump_sc = refs[:5]
    nb = ATTN_NBUF
    pbuf, albuf = (refs[5 + i * nb: 5 + (i + 1) * nb] for i in range(2))
    tq = o_ref.shape[2] // nsub
    n_lat = S // TK

    def qpad(ref, t):
        cols = slice(t * tq, (t + 1) * tq)
        if mode == "da":
            z = jnp.zeros((DA_DIM, tq), BF16)
            c1 = jnp.concatenate([ref[0, 0, :, cols], z, z, z], axis=0)
            c2 = jnp.concatenate([z, ref[0, 1, :, cols], z, z], axis=0)
            return jnp.concatenate([c1, c2], axis=1)
        return jnp.concatenate([ref[0, 0, :, cols], jnp.zeros((LANES - 96, tq), BF16)], axis=0)

    def chunk(j):
        return pl.ds(j * TK if isinstance(j, int) else pl.multiple_of(j * TK, TK), TK)

    def context_keys(t):
        c0 = 0 if ctx_only else S
        s = jnp.dot(k_ref[0, c0:c0 + C, :], qpad(qf_ref, t), preferred_element_type=F32)
        m_c = jnp.max(s, axis=0, keepdims=True)
        m_sc[t] = m_c
        acc_sc[t] = jnp.dot(v_ref[0, 0, :, c0:c0 + C], jnp.exp2(s - m_c).astype(BF16), preferred_element_type=F32)

    for t in range(nsub):
        context_keys(t)

    if not ctx_only:
        for t in range(nsub):
            q_sc[t] = qpad(qr_ref, t)
        jump_sc[...] = jnp.full(jump_sc.shape, -jnp.inf, F32)

        def probs(t, j, r):
            s = jnp.dot(k_ref[0, chunk(j), :], q_sc[t], preferred_element_type=F32)
            m_prev = m_sc[t]
            pbuf[r][t] = jnp.exp2(s - m_prev).astype(BF16)
            m_chunk = jnp.max(s, axis=0, keepdims=True)
            m_new = jnp.maximum(m_prev, m_chunk)
            jump_sc[t] = jnp.maximum(jump_sc[t], m_chunk - m_prev)
            albuf[r][t] = jnp.exp2(m_prev - m_new)
            m_sc[t] = m_new

        def accumulate(t, j, r):
            pv = jnp.dot(v_ref[0, 0, :, chunk(j)], pbuf[r][t], preferred_element_type=F32)
            acc_sc[t] = (acc_sc[t] + pv) * albuf[r][t]

        def step(j, r):
            for t in range(nsub):
                accumulate(t, j, r)
                probs(t, j + 1, (r + 1) % nb)

        for t in range(nsub):
            probs(t, 0, 0)
        n_steps = n_lat - 1
        trips = n_steps // ATTN_UNROLL

        def body(i, carry):
            for u in range(ATTN_UNROLL):
                step(i * ATTN_UNROLL + u, u % nb)
            return carry

        lax.fori_loop(0, trips, body, 0)
        for j in range(trips * ATTN_UNROLL, n_steps):
            step(j, j % nb)
        for t in range(nsub):
            accumulate(t, n_lat - 1, (n_lat - 1) % nb)

        @pl.when(jnp.max(jump_sc[...]) > ATTN_MAX_JUMP)
        def _():
            for t in range(nsub):
                context_keys(t)

                def exact_step(j, carry, t=t):
                    s = jnp.dot(k_ref[0, chunk(j), :], q_sc[t], preferred_element_type=F32)
                    m_old = m_sc[t]
                    m_new = jnp.maximum(m_old, jnp.max(s, axis=0, keepdims=True))
                    pv = jnp.dot(v_ref[0, 0, :, chunk(j)], jnp.exp2(s - m_new).astype(BF16),
                                 preferred_element_type=F32)
                    acc_sc[t] = acc_sc[t] * jnp.exp2(m_old - m_new) + pv
                    m_sc[t] = m_new
                    return carry

                lax.fori_loop(0, n_lat, exact_step, 0)

    for t in range(nsub):
        acc = acc_sc[t]
        o = acc[0:64, :] / acc[64:65, :]
        cols = slice(t * tq, (t + 1) * tq)
        if mode == "da":
            lf = lam_ref[0]
            lam = (jnp.exp(jnp.sum(lf[0:1] * lf[1:2], axis=1, keepdims=True))
                   - jnp.exp(jnp.sum(lf[2:3] * lf[3:4], axis=1, keepdims=True)) + lam_init)
            od = o[:, 0:tq] - lam * o[:, tq:2 * tq]
            y = _rms_rows(od) * g_ref[0]
            o_ref[0, :, cols] = (y * (1.0 - lam_init)).astype(BF16)
        else:
            o_ref[0, :, cols] = o.astype(BF16)


def _attention_call(mode, l, qr, qf, k, v, S, C, lam, g, y_latent):
    B, nmaps, d, Sp = qr.shape
    H = v.shape[1]
    nm = nmaps // H
    ctx_only = y_latent is not None
    nsub = 1 if ctx_only else ATTN_SUBTILES
    tq = C if ctx_only else ATTN_N // nm
    n = nm * tq
    tqb = nsub * tq
    q0 = S // tqb if ctx_only else 0
    lam_init = 0.8 - 0.6 * math.exp(-0.3 * l)
    kern = functools.partial(_attn_kernel, mode=mode, S=S, C=C, lam_init=lam_init, ctx_only=ctx_only, nsub=nsub)
    in_specs = [
        pl.BlockSpec((1, nm, d, tqb), lambda b, h, i: (b, h, 0, q0 + i)),
        pl.BlockSpec((1, nm, d, tqb), lambda b, h, i: (b, h, 0, q0 + i)),
        (pl.BlockSpec((1, C, LANES), lambda b, h, i: (b, S // C, h)) if ctx_only
         else pl.BlockSpec((1, Sp, LANES), lambda b, h, i: (b, 0, h))),
        (pl.BlockSpec((1, 1, V_ROWS, C), lambda b, h, i: (b, h, 0, S // C)) if ctx_only
         else pl.BlockSpec((1, 1, V_ROWS, Sp), lambda b, h, i: (b, h, 0, 0))),
    ]
    args = [qr, qf, k, v]
    if mode == "da":
        in_specs += [pl.BlockSpec((1,) + lam.shape[1:], lambda b, h, i: (l, 0, 0)),
                     pl.BlockSpec((1,) + g.shape[1:], lambda b, h, i: (l, 0, 0))]
        args += [lam, g]
    aliases = {}
    if ctx_only:
        in_specs.append(pl.BlockSpec(memory_space=pl.ANY))
        args.append(y_latent)
        aliases = {len(args) - 1: 0}
    tk = C if ctx_only else TK
    return pl.pallas_call(
        kern,
        grid=(B, H, 1 if ctx_only else S // tqb),
        in_specs=in_specs,
        out_specs=pl.BlockSpec((1, 64, tqb), lambda b, h, i: (b, h, q0 + i)),
        out_shape=jax.ShapeDtypeStruct((B, H * 64, Sp), BF16),
        input_output_aliases=aliases,
        scratch_shapes=(
            [pltpu.VMEM((nsub, 1, n), F32), pltpu.VMEM((nsub, V_ROWS, n), F32), pltpu.VMEM((nsub, LANES, n), BF16),
             pltpu.VMEM((nsub, 1, n), F32)]
            + [pltpu.VMEM((nsub, tk, n), BF16)] * ATTN_NBUF + [pltpu.VMEM((nsub, 1, n), F32)] * ATTN_NBUF),
        compiler_params=pltpu.CompilerParams(
            dimension_semantics=("arbitrary", "arbitrary", "arbitrary"), vmem_limit_bytes=48 * 1024 * 1024),
        name="attn_" + mode + ("_ctx" if ctx_only else ""),
    )(*args)


def _attention(mode, l, qr, qf, k, v, S, C, with_ctx, lam=None, g=None):
    y = _attention_call(mode, l, qr, qf, k, v, S, C, lam, g, None)
    return _attention_call(mode, l, qr, qf, k, v, S, C, lam, g, y) if with_ctx else y


def _ret_kernel(q_ref, k_ref, v_ref, g_ref, dec_ref, o_ref, sb_sc, *, S, C):
    CH = RET_CHUNK
    nl, nc = S // CH, C // CH

    def log_sigmoid(x):
        return jnp.minimum(x, 0.0) - jnp.log1p(jnp.exp(-jnp.abs(x)))

    lgf = log_sigmoid(dec_ref[0, 0][0:1, :])
    lgb = log_sigmoid(dec_ref[1, 0][0:1, :])
    ii = lax.broadcasted_iota(jnp.int32, (CH, CH), 1).astype(F32)
    jj = lax.broadcasted_iota(jnp.int32, (CH, CH), 0).astype(F32)
    dd = ii - jj
    fwd = dd >= 0
    mt = (jnp.where(fwd, jnp.exp(lgf * jnp.where(fwd, dd, 0.0)), 0.0)
          + jnp.where(fwd, 0.0, jnp.exp(lgb * jnp.where(fwd, 0.0, -dd))))
    xi_f = jnp.exp(lgf * (ii + 1.0))
    xi_b = jnp.exp(lgb * (CH - ii))
    zeta_f = jnp.exp(lgf * (CH - 1.0 - jj))
    zeta_b = jnp.exp(lgb * jj)
    cdf = jnp.exp(lgf * CH)
    cdb = jnp.exp(lgb * CH)
    zq = jnp.zeros((RET_DIM, CH), BF16)

    def sl(c):
        return pl.ds(pl.multiple_of(c * CH, CH), CH)

    def state_update(c, zeta):
        kz = (k_ref[0, sl(c), :].astype(F32) * zeta).astype(BF16)
        return jnp.dot(v_ref[0, 0, :, sl(c)], kz, preferred_element_type=F32)

    def bwd_step(c, sb):
        sb_sc[c] = sb
        return sb * cdb + state_update(c, zeta_b)

    sb = jnp.zeros((RET_DIM, CH), F32)
    for c in range(nl + nc - 1, nl - 1, -1):
        sb = bwd_step(c, sb)
    lax.fori_loop(0, nl, lambda n, s: bwd_step(nl - 1 - n, s), sb, unroll=RET_UNROLL)

    def fwd_step(c, sf):
        qc = q_ref[0, 0, :, sl(c)]
        qp = jnp.concatenate([qc, zq], axis=0)
        kc = k_ref[0, sl(c), :]
        vc = v_ref[0, 0, :, sl(c)]
        at = jnp.dot(kc, qp, preferred_element_type=F32)
        qpf = qp.astype(F32)
        rhs = jnp.concatenate([(at * mt).astype(BF16), (qpf * xi_f).astype(BF16), (qpf * xi_b).astype(BF16)], axis=0)
        lhs = jnp.concatenate([vc, sf.astype(BF16), sb_sc[c].astype(BF16)], axis=1)
        o = jnp.dot(lhs, rhs, preferred_element_type=F32)
        gch = g_ref[0, :, sl(c)]
        o_ref[0, :, sl(c)] = (gch * _sigmoid(gch) * _rms_rows(o)).astype(BF16)
        return sf * cdf + state_update(c, zeta_f)

    sf = jnp.zeros((RET_DIM, CH), F32)
    for c in range(nl, nl + nc):
        sf = fwd_step(c, sf)
    lax.fori_loop(0, nl, fwd_step, sf, unroll=RET_UNROLL)


def _retention(rq, rk, rv, rg, dec, S, C):
    B, H, d, Sp = rq.shape
    kern = functools.partial(_ret_kernel, S=S, C=C)
    return pl.pallas_call(
        kern,
        grid=(B, H),
        in_specs=[
            pl.BlockSpec((1, 1, d, Sp), lambda b, h: (b, h, 0, 0)),
            pl.BlockSpec((1, Sp, LANES), lambda b, h: (b, 0, h)),
            pl.BlockSpec((1, 1, d, Sp), lambda b, h: (b, h, 0, 0)),
            pl.BlockSpec((1, d, Sp), lambda b, h: (b, h, 0)),
            pl.BlockSpec((2, 1, 8, LANES), lambda b, h: (0, h, 0, 0)),
        ],
        out_specs=pl.BlockSpec((1, d, Sp), lambda b, h: (b, h, 0)),
        out_shape=jax.ShapeDtypeStruct((B, H * d, Sp), BF16),
        scratch_shapes=[pltpu.VMEM((Sp // RET_CHUNK, RET_DIM, RET_CHUNK), F32)],
        compiler_params=pltpu.CompilerParams(
            dimension_semantics=("arbitrary", "arbitrary"), vmem_limit_bytes=56 * 1024 * 1024),
        name="retention",
    )(rq, rk, rv, rg, dec)


def _layer_norm(t, g, b):
    mu = jnp.mean(t, axis=-1, keepdims=True)
    tc = t - mu
    var = jnp.mean(tc * tc, axis=-1, keepdims=True)
    return tc * lax.rsqrt(var + LN_EPS) * g + b


def _out_kernel(yda_ref, ymla_ref, yret_ref, w_ref, x_ref, gate_ref, g1_ref, b1_ref, sh2_ref, sc2_ref, rw_ref,
                x1_ref, h2_ref, lg_ref, *, alpha):
    proj = (lax.dot_general(yda_ref[0], w_ref[0, 0:DA_VW, :], TN_DIMS, preferred_element_type=F32)
            + lax.dot_general(ymla_ref[0], w_ref[0, DA_VW:DA_VW + MLA_VW, :], TN_DIMS, preferred_element_type=F32)
            + lax.dot_general(yret_ref[0], w_ref[0, DA_VW + MLA_VW:, :], TN_DIMS, preferred_element_type=F32))
    x1 = _layer_norm(alpha * x_ref[0] + gate_ref[0, 0, 0] * proj, g1_ref[0], b1_ref[0])
    x1_ref[0] = x1
    h2 = x1 * (1.0 + sc2_ref[0, 0, 0]) + sh2_ref[0, 0, 0]
    h2_ref[0] = h2.astype(BF16)
    h2_hi = h2.astype(BF16)
    h2_lo = (h2 - h2_hi.astype(F32)).astype(BF16)
    hi = jnp.dot(h2_hi, rw_ref[...], preferred_element_type=F32)
    lo = jnp.dot(h2_lo, rw_ref[:, 0:LANES], preferred_element_type=F32)
    lg_ref[0] = hi[:, 0:LANES] + (hi[:, LANES:] + lo)


def _out_proj(l, yda, ymla, yret, w_out, xc, modr, ln_g, ln_b, rw_pad, S, alpha, latents_only):
    B, Sp, D = xc.shape
    n_lat_tiles = S // TM
    n_rows = S if latents_only else Sp

    def mod_spec(j):
        return pl.BlockSpec((1, 1, 1, 1, D), lambda b, t: (l, jnp.where(t >= n_lat_tiles, B, b), j, 0, 0))

    def fm(a):
        return pl.BlockSpec((1, a.shape[1], TM), lambda b, t: (b, 0, t))

    def wspec(a):
        return pl.BlockSpec((1,) + a.shape[1:], lambda b, t: (l,) + (0,) * (a.ndim - 1))

    row = pl.BlockSpec((1, TM, D), lambda b, t: (b, t, 0))
    sds = jax.ShapeDtypeStruct
    return pl.pallas_call(
        functools.partial(_out_kernel, alpha=alpha),
        grid=(B, n_rows // TM),
        in_specs=[fm(yda), fm(ymla), fm(yret), wspec(w_out), row, mod_spec(2), wspec(ln_g), wspec(ln_b),
                  mod_spec(3), mod_spec(4), pl.BlockSpec(rw_pad.shape, lambda b, t: (0, 0))],
        out_specs=[row, row, pl.BlockSpec((1, TM, LANES), lambda b, t: (b, t, 0))],
        out_shape=[sds((B, n_rows, D), F32), sds((B, n_rows, D), BF16), sds((B, n_rows, LANES), F32)],
        compiler_params=pltpu.CompilerParams(dimension_semantics=("arbitrary", "arbitrary")),
        name="out_proj",
    )(yda, ymla, yret, w_out, xc, modr, ln_g, ln_b, modr, modr, rw_pad)


def _gates_T(logits_T, rb):
    s = _sigmoid(logits_T)
    ch = s + rb
    srow = [s[e:e + 1] for e in range(N_EXPERTS)]
    crow = [ch[e:e + 1] for e in range(N_EXPERTS)]
    per = N_EXPERTS // N_GROUPS
    gs = []
    for g in range(N_GROUPS):
        a, b, c, d = crow[per * g: per * g + per]
        m1, n1, m2, n2 = jnp.maximum(a, b), jnp.minimum(a, b), jnp.maximum(c, d), jnp.minimum(c, d)
        gs.append(jnp.maximum(m1, m2) + jnp.maximum(jnp.minimum(m1, m2), jnp.maximum(n1, n2)))
    gmax = functools.reduce(jnp.maximum, gs)
    taken = jnp.zeros(gmax.shape, jnp.bool_)
    gsel = []
    for g in range(N_GROUPS):
        sg = jnp.logical_and(gs[g] == gmax, jnp.logical_not(taken))
        gsel.append(sg)
        taken = jnp.logical_or(taken, sg)
    neg = jnp.full(gmax.shape, -jnp.inf, F32)
    mc = [jnp.where(gsel[e // per], crow[e], neg) for e in range(N_EXPERTS)]
    sel = [jnp.zeros(gmax.shape, jnp.bool_) for _ in range(N_EXPERTS)]
    for _ in range(2):
        top = functools.reduce(jnp.maximum, mc)
        taken = jnp.zeros(gmax.shape, jnp.bool_)
        for e in range(N_EXPERTS):
            hit = jnp.logical_and(mc[e] == top, jnp.logical_not(taken))
            taken = jnp.logical_or(taken, hit)
            sel[e] = jnp.logical_or(sel[e], hit)
            mc[e] = jnp.where(hit, neg, mc[e])
    w = [jnp.where(sel[e], srow[e], 0.0) for e in range(N_EXPERTS)]
    wsum = functools.reduce(lambda a, b: a + b, w)
    gates = jnp.concatenate([we / wsum * ROUTED_SCALE for we in w], axis=0)
    return gates, [jnp.where(sg, 1.0, 0.0) for sg in gsel]


def _moe_capacity(tm):
    return -(-int(tm / N_GROUPS + 5.5 * math.sqrt(tm * (N_GROUPS - 1)) / N_GROUPS) // 32) * 32


def _moe_kernel(h_ref, lg_ref, rb_ref, w13_ref, w2_ref, o_ref,
                gates_sc, acc_sc, dense_sc, perm_sc, hs_sc, gs_sc, accs_sc, earlier_sc, *, cap, nsub):
    e = pl.program_id(2)
    tm = h_ref.shape[1] // nsub
    per = N_EXPERTS // N_GROUPS
    lane = lax.broadcasted_iota(jnp.int32, (1, LANES), 1)

    def expert(x):
        ab = jnp.dot(x, w13_ref[0, 0], preferred_element_type=F32)
        a = ab[:, 0:D_EXPERT]
        hid = (a * _sigmoid(a) * ab[:, D_EXPERT:]).astype(BF16)
        return jnp.dot(hid, w2_ref[0, 0], preferred_element_type=F32)

    def gate_col(g):
        return jnp.sum(jnp.where(lane == e, g, 0.0), axis=1, keepdims=True)

    def sub(j):
        return slice(j * tm, (j + 1) * tm)

    @pl.when(jnp.logical_and(jnp.logical_and(pl.program_id(0) == 0, pl.program_id(1) == 0), e == 0))
    def _():
        before = lax.broadcasted_iota(jnp.int32, (tm, tm), 0) < lax.broadcasted_iota(jnp.int32, (tm, tm), 1)
        earlier_sc[...] = jnp.where(before, 1.0, 0.0).astype(BF16)

    @pl.when(e == 0)
    def _():
        for j in range(nsub):
            gT, gsel = _gates_T(lg_ref[0, sub(j), :].T[0:N_EXPERTS, :], rb_ref[...])
            gates = jnp.concatenate([gT, jnp.zeros((LANES - N_EXPERTS, tm), F32)], axis=0).T
            gates_sc[j] = gates
            acc_sc[j] = jnp.zeros(acc_sc.shape[1:], F32)
            member = jnp.concatenate(gsel + [jnp.zeros((16 - N_GROUPS, tm), F32)], axis=0)
            rank = jnp.dot(member.astype(BF16), earlier_sc[...], preferred_element_type=F32)
            slot = functools.reduce(lambda a, b: a + b,
                                    [gsel[g] * (rank[g:g + 1] + float(g * cap)) for g in range(N_GROUPS)])
            count = jnp.sum(member, axis=1, keepdims=True)
            dense_sc[j] = jnp.where(jnp.max(count) > float(cap), 1, 0)
            rows_i = lax.broadcasted_iota(jnp.int32, (N_GROUPS * cap, tm), 0)
            perm = jnp.where(rows_i == slot.astype(jnp.int32), 1.0, 0.0).astype(BF16)
            perm_sc[j] = perm
            hs_sc[j] = jnp.dot(perm, h_ref[0, sub(j), :], preferred_element_type=F32).astype(BF16)
            g_hi = gates.astype(BF16)
            g_lo = (gates - g_hi.astype(F32)).astype(BF16)
            gs_sc[j] = (jnp.dot(perm, g_hi, preferred_element_type=F32)
                        + jnp.dot(perm, g_lo, preferred_element_type=F32))
            accs_sc[j] = jnp.zeros(accs_sc.shape[1:], F32)

    rows = pl.ds(pl.multiple_of((e // per) * cap, 16), cap)
    any_dense = functools.reduce(lambda a, b: a + b, [dense_sc[j] for j in range(nsub)])

    @pl.when(any_dense == 0)
    def _():
        y = expert(jnp.concatenate([hs_sc[j, rows, :] for j in range(nsub)], axis=0))
        for j in range(nsub):
            accs_sc[j, rows, :] += gate_col(gs_sc[j, rows, :]) * y[j * cap:(j + 1) * cap]

    @pl.when(any_dense != 0)
    def _():
        for j in range(nsub):
            @pl.when(dense_sc[j] == 0)
            def _(j=j):
                accs_sc[j, rows, :] += gate_col(gs_sc[j, rows, :]) * expert(hs_sc[j, rows, :])

            @pl.when(dense_sc[j] != 0)
            def _(j=j):
                acc_sc[j] += gate_col(gates_sc[j]) * expert(h_ref[0, sub(j), :])

    @pl.when(e == N_EXPERTS - 1)
    def _():
        for j in range(nsub):
            @pl.when(dense_sc[j] == 0)
            def _(j=j):
                acc_sc[j] = lax.dot_general(perm_sc[j], accs_sc[j].astype(BF16), TN_DIMS,
                                            preferred_element_type=F32)

            o_ref[0, sub(j), :] = acc_sc[j].astype(BF16)


def _moe_ln_kernel(x1_ref, f_ref, gb_ref, gc_ref, g2_ref, b2_ref, o_ref, *, S, alpha):
    tm = x1_ref.shape[1]
    n_lat = S - pl.program_id(1) * tm
    rowi = lax.broadcasted_iota(jnp.int32, (tm, 1), 0)
    gate = jnp.where(rowi < n_lat, gb_ref[0, 0, 0], gc_ref[0, 0, 0])
    o_ref[0] = _layer_norm(alpha * x1_ref[0] + gate * f_ref[0].astype(F32), g2_ref[0], b2_ref[0])


def _moe(l, h2, logits, rb, w13, w2, x1, modr, ln_g, ln_b, S, alpha, latents_only):
    B, n_rows, D = x1.shape
    sub = TM_MOE_LAST if latents_only else TM_MOE
    tm = MOE_SUBTILES * sub
    cap = _moe_capacity(sub)

    def wspec(a):
        return pl.BlockSpec((1,) + a.shape[1:], lambda b, t: (l,) + (0,) * (a.ndim - 1))

    row = lambda b, t, e: (b, t, 0)
    f = pl.pallas_call(
        functools.partial(_moe_kernel, cap=cap, nsub=MOE_SUBTILES),
        grid=(B, n_rows // tm, N_EXPERTS),
        in_specs=[
            pl.BlockSpec((1, tm, D), row),
            pl.BlockSpec((1, tm, LANES), row),
            pl.BlockSpec(rb.shape, lambda b, t, e: (0, 0)),
            pl.BlockSpec((1, 1, D, 2 * D_EXPERT), lambda b, t, e: (l, e, 0, 0)),
            pl.BlockSpec((1, 1, D_EXPERT, D), lambda b, t, e: (l, e, 0, 0)),
        ],
        out_specs=pl.BlockSpec((1, tm, D), row),
        out_shape=jax.ShapeDtypeStruct((B, n_rows, D), BF16),
        scratch_shapes=[
            pltpu.VMEM((MOE_SUBTILES, sub, LANES), F32), pltpu.VMEM((MOE_SUBTILES, sub, D), F32),
            pltpu.SMEM((MOE_SUBTILES,), jnp.int32),
            pltpu.VMEM((MOE_SUBTILES, N_GROUPS * cap, sub), BF16),
            pltpu.VMEM((MOE_SUBTILES, N_GROUPS * cap, D), BF16),
            pltpu.VMEM((MOE_SUBTILES, N_GROUPS * cap, LANES), F32),
            pltpu.VMEM((MOE_SUBTILES, N_GROUPS * cap, D), F32),
            pltpu.VMEM((sub, sub), BF16)],
        compiler_params=pltpu.CompilerParams(
            dimension_semantics=("arbitrary", "arbitrary", "arbitrary"), vmem_limit_bytes=56 * 1024 * 1024),
        name="moe",
    )(h2, logits, rb, w13, w2)
    rows = pl.BlockSpec((1, tm, D), lambda b, t: (b, t, 0))
    return pl.pallas_call(
        functools.partial(_moe_ln_kernel, S=S, alpha=alpha),
        grid=(B, n_rows // tm),
        in_specs=[rows, rows,
                  pl.BlockSpec((1, 1, 1, 1, D), lambda b, t: (l, b, 5, 0, 0)),
                  pl.BlockSpec((1, 1, 1, 1, D), lambda b, t: (l, B, 5, 0, 0)),
                  wspec(ln_g), wspec(ln_b)],
        out_specs=rows,
        out_shape=jax.ShapeDtypeStruct((B, n_rows, D), F32),
        compiler_params=pltpu.CompilerParams(
            dimension_semantics=("arbitrary", "arbitrary"), vmem_limit_bytes=56 * 1024 * 1024),
        name="moe_ln",
    )(x1, f, modr, modr, ln_g, ln_b)


def _rope_tables(S, C, rot_dim):
    rows = S // GRID_W
    r = jnp.repeat(jnp.arange(rows, dtype=F32), GRID_W)
    col = jnp.tile(jnp.arange(GRID_W, dtype=F32), rows)
    n_freq = rot_dim // 4
    freqs = ROPE_BASE ** (-jnp.arange(n_freq, dtype=F32) / n_freq)
    ang = jnp.concatenate([r[:, None] * freqs, col[:, None] * freqs], -1)
    cos, sin = jnp.cos(ang), jnp.sin(ang)
    cos_rep = jnp.repeat(cos, 2, axis=-1)
    sin_alt = jnp.stack([-sin, sin], -1).reshape(S, rot_dim)
    cos_rep = jnp.concatenate([cos_rep, jnp.ones((C, rot_dim), F32)], 0)
    sin_alt = jnp.concatenate([sin_alt, jnp.zeros((C, rot_dim), F32)], 0)
    return cos_rep.T, sin_alt.T


def kernel(x, c, ctx, c_ctx, w_ada, b_ada, w_in, da_lambda, da_subln, mla_q_norm, mla_w_uq, mla_kv_norm, mla_w_ukv, ret_decay_f, ret_decay_b, w_out, ln1_g, ln1_b, router_w, router_b, exp_w1, exp_w3, exp_w2, ln2_g, ln2_b):
    B, S, D = x.shape
    C = ctx.shape[1]
    depth = w_in.shape[0]
    alpha = float((2 * depth) ** 0.25)

    cvec = jnp.concatenate([c, c_ctx[None, :], jnp.zeros((8 - B - 1, D), F32)], 0)
    mod = _modulation(cvec, w_ada, b_ada)
    modr = mod.reshape(depth, 8, N_MOD, 1, D)

    wT_all = jnp.swapaxes(w_in, 1, 2).astype(BF16)
    wuqT = jnp.swapaxes(mla_w_uq, 1, 2).astype(BF16)
    ukv = mla_w_ukv.reshape(depth, MLA_KV_RANK, MLA_HEADS, MLA_NOPE + MLA_VDIM)
    wvT = jnp.swapaxes(ukv[..., MLA_NOPE:].reshape(depth, MLA_KV_RANK, MLA_VW), 1, 2).astype(BF16)
    nopeT = jnp.transpose(ukv[..., :MLA_NOPE], (0, 2, 3, 1))
    top = jnp.concatenate([nopeT, jnp.zeros((depth, MLA_HEADS, MLA_NOPE, MLA_ROPE), F32)], -1)
    mid = jnp.concatenate([jnp.zeros((MLA_ROPE, MLA_KV_RANK), F32), jnp.eye(MLA_ROPE, dtype=F32)], -1)
    mid = jnp.broadcast_to(mid, (depth, MLA_HEADS, MLA_ROPE, MLA_KV_RANK + MLA_ROPE))
    bot = jnp.zeros((depth, MLA_HEADS, LANES - MLA_NOPE - MLA_ROPE, MLA_KV_RANK + MLA_ROPE), F32)
    wkT = jnp.concatenate([top, mid, bot], 2).reshape(depth, MLA_HEADS * LANES, MLA_KV_RANK + MLA_ROPE).astype(BF16)
    gq = mla_q_norm[:, :, None]
    gkv = mla_kv_norm[:, :, None]
    subln = da_subln[:, :, None]
    w_out_b = w_out.astype(BF16)
    w13 = jnp.concatenate([exp_w1, exp_w3], -1).astype(BF16)
    w2 = exp_w2.astype(BF16)
    rw_f = jnp.concatenate([router_w, jnp.zeros((D, LANES - N_EXPERTS), F32)], -1)
    rw_hi = rw_f.astype(BF16)
    rw_pad = jnp.concatenate([rw_hi, (rw_f - rw_hi.astype(F32)).astype(BF16)], -1)
    rb = router_b[:, None]
    dec = jnp.broadcast_to(jnp.stack([ret_decay_f, ret_decay_b], 1)[:, :, :, None, None],
                           (depth, 2, RET_HEADS, 8, LANES))
    tabs = _rope_tables(S, C, DA_DIM) + _rope_tables(S, C, RET_DIM)
    ln1g, ln1b, ln2g, ln2b = (a[:, None, :] for a in (ln1_g, ln1_b, ln2_g, ln2_b))

    xc = jnp.concatenate([x, ctx], 1)
    for l in range(depth):
        last = l == depth - 1
        (daqr, daqf, dak, dav, mqr, mqf, mk, mv, rq, rk, rv, rg) = _in_proj(
            l, xc, modr, wT_all, tabs, gq, wuqT, gkv, wvT, wkT, S)
        yda = _attention("da", l, daqr, daqf, dak, dav, S, C, not last, lam=da_lambda, g=subln)
        ymla = _attention("mla", l, mqr, mqf, mk, mv, S, C, not last)
        yret = _retention(rq, rk, rv, rg, dec[l], S, C)
        x1, h2, logits = _out_proj(l, yda, ymla, yret, w_out_b, xc, modr, ln1g, ln1b, rw_pad, S, alpha, last)
        xc = _moe(l, h2, logits, rb, w13, w2, x1, modr, ln2g, ln2b, S, alpha, last)
    return xc
```

```python
import functools
import math

import jax
import jax.numpy as jnp
import numpy as np
from jax import lax
from jax.experimental import pallas as pl
from jax.experimental.pallas import tpu as pltpu

F32 = jnp.float32
BF16 = jnp.bfloat16

GRID_W = 64
DA_HEADS, DA_DIM, DA_VDIM = 6, 32, 64
MLA_HEADS, MLA_NOPE, MLA_ROPE, MLA_VDIM = 6, 64, 32, 64
MLA_Q_RANK, MLA_KV_RANK = 256, 128
RET_HEADS, RET_DIM, RET_CHUNK = 4, 64, 128
ROPE_BASE = 10000.0
N_EXPERTS, N_GROUPS, D_EXPERT = 16, 4, 512
ROUTED_SCALE = 1.0
N_MOD = 6
LN_EPS = 1e-5
RMS_EPS = 1e-6
LOG2E = math.log2(math.e)

DA_W = DA_HEADS * 2 * DA_DIM
DA_VW = DA_HEADS * DA_VDIM
RET_W = RET_HEADS * RET_DIM
MLA_QW = MLA_HEADS * (MLA_NOPE + MLA_ROPE)
MLA_VW = MLA_HEADS * MLA_VDIM
IN_SPLIT = (DA_W, DA_W, DA_VW, MLA_Q_RANK, MLA_KV_RANK, MLA_ROPE, RET_W, RET_W, RET_W, RET_W)

LANES = 128
V_ROWS = 80
TM = 256
ATTN_N = 512
TK = 2048
ATTN_SUBTILES = 2
ATTN_NBUF = 2
ATTN_MAX_JUMP = 32.0
ATTN_UNROLL = 2
TM_MOE = 640
TM_MOE_LAST = 512
MOE_SUBTILES = 2
MOE_BLOCK = 256
MOE_ALIGN = 32
RET_UNROLL = 8

NT_DIMS = (((1,), (1,)), ((), ()))
TN_DIMS = (((0,), (0,)), ((), ()))


def _sigmoid(x):
    return 1.0 / (1.0 + jnp.exp(-x))


def _mod_kernel(c_ref, w_ref, b_ref, o_ref):
    c = c_ref[...]
    sc = (c * _sigmoid(c)).astype(BF16)
    o_ref[0] = jnp.dot(sc, w_ref[0].astype(BF16), preferred_element_type=F32) + b_ref[0]


def _modulation(cvec, w_ada, b_ada):
    depth, d, n = w_ada.shape
    tn = 1536
    return pl.pallas_call(
        _mod_kernel,
        grid=(depth, n // tn),
        in_specs=[
            pl.BlockSpec((8, d), lambda l, j: (0, 0)),
            pl.BlockSpec((1, d, tn), lambda l, j: (l, 0, j)),
            pl.BlockSpec((1, 1, tn), lambda l, j: (l, 0, j)),
        ],
        out_specs=pl.BlockSpec((1, 8, tn), lambda l, j: (l, 0, j)),
        out_shape=jax.ShapeDtypeStruct((depth, 8, n), F32),
        name="adaln_mod",
    )(cvec, w_ada, b_ada.reshape(depth, 1, n))


_O_DAQ, _O_DAK, _O_DAV, _O_CQ, _O_CKV, _O_KPE, _O_RQ, _O_RK, _O_RV, _O_RG = (
    int(v) for v in np.cumsum((0,) + IN_SPLIT)[:-1])
N_ZROWS = sum(IN_SPLIT)


def _rms_rows(x):
    return x * lax.rsqrt(jnp.mean(x * x, axis=0, keepdims=True) + RMS_EPS)


def _pairswap_rows(x):
    n = x.shape[0]
    even = (lax.broadcasted_iota(jnp.int32, x.shape, 0) & 1) == 0
    return jnp.where(even, pltpu.roll(x, n - 1, 0), pltpu.roll(x, 1, 0))


def _in_kernel(x_ref, sh_ref, sc_ref, wT_ref, cosA_ref, sinA_ref, cosR_ref, sinR_ref,
               gq_ref, wuqT_ref, gkv_ref, wvT_ref, wkT_ref,
               daqr, daqf, dak, dav, mqr, mqf, mk, mv, rq, rk, rv, rg, z_ref, *, cq_da, cq_mla, kscale):
    tm = x_ref.shape[1]
    x = x_ref[0]
    h = (x * (1.0 + sc_ref[0, 0, 0]) + sh_ref[0, 0, 0]).astype(BF16)
    z_ref[...] = lax.dot_general(wT_ref[0], h, NT_DIMS, preferred_element_type=F32)
    cosA = cosA_ref[...]
    sinA = sinA_ref[...]
    cosR = cosR_ref[...]
    sinR = sinR_ref[...]
    ones_rows = (lax.broadcasted_iota(jnp.int32, (DA_HEADS, V_ROWS - 64, tm), 1) == 0).astype(BF16)

    def rows(off, n):
        return z_ref[off:off + n, :]

    q2 = rows(_O_DAQ, DA_W)
    q = q2.reshape(12, DA_DIM, tm)
    qs = _pairswap_rows(q2).reshape(12, DA_DIM, tm)
    daqr[0] = ((q * cosA + qs * sinA) * cq_da).astype(BF16)
    daqf[0] = (q * cq_da).astype(BF16)
    k2 = rows(_O_DAK, DA_W)
    k = k2.reshape(12, DA_DIM, tm)
    ks = _pairswap_rows(k2).reshape(12, DA_DIM, tm)
    kr = (k * cosA + ks * sinA).reshape(DA_HEADS, 2 * DA_DIM, tm)
    kp = jnp.concatenate([kr, jnp.zeros_like(kr)], axis=1).reshape(DA_HEADS * LANES, tm)
    dak[0] = kp.T.astype(BF16)
    dav[0, :, 0:64, :] = rows(_O_DAV, DA_VW).reshape(DA_HEADS, DA_VDIM, tm).astype(BF16)
    dav[0, :, 64:V_ROWS, :] = ones_rows

    cqn = (_rms_rows(rows(_O_CQ, MLA_Q_RANK)) * gq_ref[0]).astype(BF16)
    qm = jnp.dot(wuqT_ref[0], cqn, preferred_element_type=F32)
    qh = qm.reshape(MLA_HEADS, MLA_NOPE + MLA_ROPE, tm)
    qsw = _pairswap_rows(qm).reshape(MLA_HEADS, MLA_NOPE + MLA_ROPE, tm)[:, MLA_NOPE:, :]
    rot = qh[:, MLA_NOPE:, :] * cosA + qsw * sinA
    mqr[0, :, 0:MLA_NOPE, :] = (qh[:, 0:MLA_NOPE, :] * cq_mla).astype(BF16)
    mqr[0, :, MLA_NOPE:, :] = (rot * cq_mla).astype(BF16)
    mqf[0] = (qh * cq_mla).astype(BF16)
    ckvn = (_rms_rows(rows(_O_CKV, MLA_KV_RANK)) * gkv_ref[0]).astype(BF16)
    vm = jnp.dot(wvT_ref[0], ckvn, preferred_element_type=F32)
    mv[0, :, 0:64, :] = vm.reshape(MLA_HEADS, MLA_VDIM, tm).astype(BF16)
    mv[0, :, 64:V_ROWS, :] = ones_rows
    kpe = rows(_O_KPE, MLA_ROPE)
    kper = kpe * cosA + _pairswap_rows(kpe) * sinA
    kin = jnp.concatenate([ckvn, kper.astype(BF16)], axis=0)
    kmT = jnp.dot(wkT_ref[0], kin, preferred_element_type=F32)
    mk[0] = kmT.T.astype(BF16)

    r_q2 = rows(_O_RQ, RET_W)
    r_q = r_q2.reshape(RET_HEADS, RET_DIM, tm)
    r_qs = _pairswap_rows(r_q2).reshape(RET_HEADS, RET_DIM, tm)
    rq[0] = (r_q * cosR + r_qs * sinR).astype(BF16)
    r_k2 = rows(_O_RK, RET_W)
    r_k = r_k2.reshape(RET_HEADS, RET_DIM, tm)
    r_ks = _pairswap_rows(r_k2).reshape(RET_HEADS, RET_DIM, tm)
    rkr = (r_k * cosR + r_ks * sinR) * kscale
    rkp = jnp.concatenate([rkr, jnp.zeros_like(rkr)], axis=1).reshape(RET_HEADS * LANES, tm)
    rk[0] = rkp.T.astype(BF16)
    rv[0] = rows(_O_RV, RET_W).reshape(RET_HEADS, RET_DIM, tm).astype(BF16)
    rg[0] = rows(_O_RG, RET_W)


def _in_proj(l, xc, modr, wT_all, tabs, gq, wuqT, gkv, wvT, wkT, S):
    B, Sp, D = xc.shape
    nt = Sp // TM
    n_lat_tiles = S // TM
    cosA, sinA, cosR, sinR = tabs

    def mod_spec(j):
        return pl.BlockSpec((1, 1, 1, 1, D), lambda b, t: (l, jnp.where(t >= n_lat_tiles, B, b), j, 0, 0))

    def wspec(a):
        return pl.BlockSpec((1,) + a.shape[1:], lambda b, t: (l,) + (0,) * (a.ndim - 1))

    def fm4(h, d):
        return pl.BlockSpec((1, h, d, TM), lambda b, t: (b, 0, 0, t))

    def tmaj(w):
        return pl.BlockSpec((1, TM, w), lambda b, t: (b, t, 0))

    kern = functools.partial(
        _in_kernel,
        cq_da=float(DA_DIM ** -0.5 * LOG2E),
        cq_mla=float((MLA_NOPE + MLA_ROPE) ** -0.5 * LOG2E),
        kscale=float(RET_DIM ** -0.5),
    )
    sds = jax.ShapeDtypeStruct
    return pl.pallas_call(
        kern,
        grid=(B, nt),
        in_specs=[
            pl.BlockSpec((1, TM, D), lambda b, t: (b, t, 0)),
            mod_spec(0), mod_spec(1),
            wspec(wT_all),
            pl.BlockSpec((DA_DIM, TM), lambda b, t: (0, t)),
            pl.BlockSpec((DA_DIM, TM), lambda b, t: (0, t)),
            pl.BlockSpec((RET_DIM, TM), lambda b, t: (0, t)),
            pl.BlockSpec((RET_DIM, TM), lambda b, t: (0, t)),
            wspec(gq), wspec(wuqT), wspec(gkv), wspec(wvT), wspec(wkT),
        ],
        out_specs=[
            fm4(12, DA_DIM), fm4(12, DA_DIM), tmaj(DA_HEADS * LANES), fm4(DA_HEADS, V_ROWS),
            fm4(MLA_HEADS, 96), fm4(MLA_HEADS, 96), tmaj(MLA_HEADS * LANES), fm4(MLA_HEADS, V_ROWS),
            fm4(RET_HEADS, RET_DIM), tmaj(RET_HEADS * LANES), fm4(RET_HEADS, RET_DIM),
            pl.BlockSpec((1, RET_W, TM), lambda b, t: (b, 0, t)),
        ],
        out_shape=[
            sds((B, 12, DA_DIM, Sp), BF16), sds((B, 12, DA_DIM, Sp), BF16),
            sds((B, Sp, DA_HEADS * LANES), BF16), sds((B, DA_HEADS, V_ROWS, Sp), BF16),
            sds((B, MLA_HEADS, 96, Sp), BF16), sds((B, MLA_HEADS, 96, Sp), BF16),
            sds((B, Sp, MLA_HEADS * LANES), BF16), sds((B, MLA_HEADS, V_ROWS, Sp), BF16),
            sds((B, RET_HEADS, RET_DIM, Sp), BF16), sds((B, Sp, RET_HEADS * LANES), BF16),
            sds((B, RET_HEADS, RET_DIM, Sp), BF16), sds((B, RET_W, Sp), F32),
        ],
        scratch_shapes=[pltpu.VMEM((N_ZROWS, TM), F32)],
        compiler_params=pltpu.CompilerParams(
            dimension_semantics=("arbitrary", "arbitrary"), vmem_limit_bytes=56 * 1024 * 1024),
        name="in_proj",
    )(xc, modr, modr, wT_all, cosA, sinA, cosR, sinR, gq, wuqT, gkv, wvT, wkT)


def _attn_kernel(*refs, mode, S, C, lam_init, ctx_only, nsub):
    refs = list(refs)
    qr_ref, qf_ref, k_ref, v_ref = refs[:4]
    del refs[:4]
    if mode == "da":
        lam_ref, g_ref = refs[:2]
        del refs[:2]
    if ctx_only:
        del refs[:1]
    o_ref, m_sc, acc_sc, q_sc, jump_sc = refs[:5]
    nb = ATTN_NBUF
    pbuf, albuf = (refs[5 + i * nb: 5 + (i + 1) * nb] for i in range(2))
    tq = o_ref.shape[2] // nsub
    n_lat = S // TK

    def qpad(ref, t):
        cols = slice(t * tq, (t + 1) * tq)
        if mode == "da":
            z = jnp.zeros((DA_DIM, tq), BF16)
            c1 = jnp.concatenate([ref[0, 0, :, cols], z, z, z], axis=0)
            c2 = jnp.concatenate([z, ref[0, 1, :, cols], z, z], axis=0)
            return jnp.concatenate([c1, c2], axis=1)
        return jnp.concatenate([ref[0, 0, :, cols], jnp.zeros((LANES - 96, tq), BF16)], axis=0)

    def chunk(j):
        return pl.ds(j * TK if isinstance(j, int) else pl.multiple_of(j * TK, TK), TK)

    def context_keys(t):
        c0 = 0 if ctx_only else S
        s = jnp.dot(k_ref[0, c0:c0 + C, :], qpad(qf_ref, t), preferred_element_type=F32)
        m_c = jnp.max(s, axis=0, keepdims=True)
        m_sc[t] = m_c
        acc_sc[t] = jnp.dot(v_ref[0, 0, :, c0:c0 + C], jnp.exp2(s - m_c).astype(BF16), preferred_element_type=F32)

    for t in range(nsub):
        context_keys(t)

    if not ctx_only:
        for t in range(nsub):
            q_sc[t] = qpad(qr_ref, t)
        jump_sc[...] = jnp.full(jump_sc.shape, -jnp.inf, F32)

        def probs(t, j, r):
            s = jnp.dot(k_ref[0, chunk(j), :], q_sc[t], preferred_element_type=F32)
            m_prev = m_sc[t]
            pbuf[r][t] = jnp.exp2(s - m_prev).astype(BF16)
            m_chunk = jnp.max(s, axis=0, keepdims=True)
            m_new = jnp.maximum(m_prev, m_chunk)
            jump_sc[t] = jnp.maximum(jump_sc[t], m_chunk - m_prev)
            albuf[r][t] = jnp.exp2(m_prev - m_new)
            m_sc[t] = m_new

        def accumulate(t, j, r):
            pv = jnp.dot(v_ref[0, 0, :, chunk(j)], pbuf[r][t], preferred_element_type=F32)
            acc_sc[t] = (acc_sc[t] + pv) * albuf[r][t]

        def step(j, r):
            for t in range(nsub):
                accumulate(t, j, r)
                probs(t, j + 1, (r + 1) % nb)

        for t in range(nsub):
            probs(t, 0, 0)
        n_steps = n_lat - 1
        trips = n_steps // ATTN_UNROLL

        def body(i, carry):
            for u in range(ATTN_UNROLL):
                step(i * ATTN_UNROLL + u, u % nb)
            return carry

        lax.fori_loop(0, trips, body, 0)
        for j in range(trips * ATTN_UNROLL, n_steps):
            step(j, j % nb)
        for t in range(nsub):
            accumulate(t, n_lat - 1, (n_lat - 1) % nb)

        @pl.when(jnp.max(jump_sc[...]) > ATTN_MAX_JUMP)
        def _():
            for t in range(nsub):
                context_keys(t)

                def exact_step(j, carry, t=t):
                    s = jnp.dot(k_ref[0, chunk(j), :], q_sc[t], preferred_element_type=F32)
                    m_old = m_sc[t]
                    m_new = jnp.maximum(m_old, jnp.max(s, axis=0, keepdims=True))
                    pv = jnp.dot(v_ref[0, 0, :, chunk(j)], jnp.exp2(s - m_new).astype(BF16),
                                 preferred_element_type=F32)
                    acc_sc[t] = acc_sc[t] * jnp.exp2(m_old - m_new) + pv
                    m_sc[t] = m_new
                    return carry

                lax.fori_loop(0, n_lat, exact_step, 0)

    for t in range(nsub):
        acc = acc_sc[t]
        o = acc[0:64, :] / acc[64:65, :]
        cols = slice(t * tq, (t + 1) * tq)
        if mode == "da":
            lf = lam_ref[0]
            lam = (jnp.exp(jnp.sum(lf[0:1] * lf[1:2], axis=1, keepdims=True))
                   - jnp.exp(jnp.sum(lf[2:3] * lf[3:4], axis=1, keepdims=True)) + lam_init)
            od = o[:, 0:tq] - lam * o[:, tq:2 * tq]
            y = _rms_rows(od) * g_ref[0]
            o_ref[0, :, cols] = (y * (1.0 - lam_init)).astype(BF16)
        else:
            o_ref[0, :, cols] = o.astype(BF16)


def _attention_call(mode, l, qr, qf, k, v, S, C, lam, g, y_latent):
    B, nmaps, d, Sp = qr.shape
    H = v.shape[1]
    nm = nmaps // H
    ctx_only = y_latent is not None
    nsub = 1 if ctx_only else ATTN_SUBTILES
    tq = C if ctx_only else ATTN_N // nm
    n = nm * tq
    tqb = nsub * tq
    q0 = S // tqb if ctx_only else 0
    lam_init = 0.8 - 0.6 * math.exp(-0.3 * l)
    kern = functools.partial(_attn_kernel, mode=mode, S=S, C=C, lam_init=lam_init, ctx_only=ctx_only, nsub=nsub)
    in_specs = [
        pl.BlockSpec((1, nm, d, tqb), lambda b, h, i: (b, h, 0, q0 + i)),
        pl.BlockSpec((1, nm, d, tqb), lambda b, h, i: (b, h, 0, q0 + i)),
        (pl.BlockSpec((1, C, LANES), lambda b, h, i: (b, S // C, h)) if ctx_only
         else pl.BlockSpec((1, Sp, LANES), lambda b, h, i: (b, 0, h))),
        (pl.BlockSpec((1, 1, V_ROWS, C), lambda b, h, i: (b, h, 0, S // C)) if ctx_only
         else pl.BlockSpec((1, 1, V_ROWS, Sp), lambda b, h, i: (b, h, 0, 0))),
    ]
    args = [qr, qf, k, v]
    if mode == "da":
        in_specs += [pl.BlockSpec((1,) + lam.shape[1:], lambda b, h, i: (l, 0, 0)),
                     pl.BlockSpec((1,) + g.shape[1:], lambda b, h, i: (l, 0, 0))]
        args += [lam, g]
    aliases = {}
    if ctx_only:
        in_specs.append(pl.BlockSpec(memory_space=pl.ANY))
        args.append(y_latent)
        aliases = {len(args) - 1: 0}
    tk = C if ctx_only else TK
    return pl.pallas_call(
        kern,
        grid=(B, H, 1 if ctx_only else S // tqb),
        in_specs=in_specs,
        out_specs=pl.BlockSpec((1, 64, tqb), lambda b, h, i: (b, h, q0 + i)),
        out_shape=jax.ShapeDtypeStruct((B, H * 64, Sp), BF16),
        input_output_aliases=aliases,
        scratch_shapes=(
            [pltpu.VMEM((nsub, 1, n), F32), pltpu.VMEM((nsub, V_ROWS, n), F32), pltpu.VMEM((nsub, LANES, n), BF16),
             pltpu.VMEM((nsub, 1, n), F32)]
            + [pltpu.VMEM((nsub, tk, n), BF16)] * ATTN_NBUF + [pltpu.VMEM((nsub, 1, n), F32)] * ATTN_NBUF),
        compiler_params=pltpu.CompilerParams(
            dimension_semantics=("arbitrary", "arbitrary", "arbitrary"), vmem_limit_bytes=48 * 1024 * 1024),
        name="attn_" + mode + ("_ctx" if ctx_only else ""),
    )(*args)


def _attention(mode, l, qr, qf, k, v, S, C, with_ctx, lam=None, g=None):
    y = _attention_call(mode, l, qr, qf, k, v, S, C, lam, g, None)
    return _attention_call(mode, l, qr, qf, k, v, S, C, lam, g, y) if with_ctx else y


def _ret_kernel(q_ref, k_ref, v_ref, g_ref, dec_ref, o_ref, sb_sc, *, S, C):
    CH = RET_CHUNK
    nl, nc = S // CH, C // CH

    def log_sigmoid(x):
        return jnp.minimum(x, 0.0) - jnp.log1p(jnp.exp(-jnp.abs(x)))

    lgf = log_sigmoid(dec_ref[0, 0][0:1, :])
    lgb = log_sigmoid(dec_ref[1, 0][0:1, :])
    ii = lax.broadcasted_iota(jnp.int32, (CH, CH), 1).astype(F32)
    jj = lax.broadcasted_iota(jnp.int32, (CH, CH), 0).astype(F32)
    dd = ii - jj
    fwd = dd >= 0
    mt = (jnp.where(fwd, jnp.exp(lgf * jnp.where(fwd, dd, 0.0)), 0.0)
          + jnp.where(fwd, 0.0, jnp.exp(lgb * jnp.where(fwd, 0.0, -dd))))
    xi_f = jnp.exp(lgf * (ii + 1.0))
    xi_b = jnp.exp(lgb * (CH - ii))
    zeta_f = jnp.exp(lgf * (CH - 1.0 - jj))
    zeta_b = jnp.exp(lgb * jj)
    cdf = jnp.exp(lgf * CH)
    cdb = jnp.exp(lgb * CH)
    zq = jnp.zeros((RET_DIM, CH), BF16)

    def sl(c):
        return pl.ds(pl.multiple_of(c * CH, CH), CH)

    def state_update(c, zeta):
        kz = (k_ref[0, sl(c), :].astype(F32) * zeta).astype(BF16)
        return jnp.dot(v_ref[0, 0, :, sl(c)], kz, preferred_element_type=F32)

    def bwd_step(c, sb):
        sb_sc[c] = sb
        return sb * cdb + state_update(c, zeta_b)

    sb = jnp.zeros((RET_DIM, CH), F32)
    for c in range(nl + nc - 1, nl - 1, -1):
        sb = bwd_step(c, sb)
    lax.fori_loop(0, nl, lambda n, s: bwd_step(nl - 1 - n, s), sb, unroll=RET_UNROLL)

    def fwd_step(c, sf):
        qc = q_ref[0, 0, :, sl(c)]
        qp = jnp.concatenate([qc, zq], axis=0)
        kc = k_ref[0, sl(c), :]
        vc = v_ref[0, 0, :, sl(c)]
        at = jnp.dot(kc, qp, preferred_element_type=F32)
        qpf = qp.astype(F32)
        rhs = jnp.concatenate([(at * mt).astype(BF16), (qpf * xi_f).astype(BF16), (qpf * xi_b).astype(BF16)], axis=0)
        lhs = jnp.concatenate([vc, sf.astype(BF16), sb_sc[c].astype(BF16)], axis=1)
        o = jnp.dot(lhs, rhs, preferred_element_type=F32)
        gch = g_ref[0, :, sl(c)]
        o_ref[0, :, sl(c)] = (gch * _sigmoid(gch) * _rms_rows(o)).astype(BF16)
        return sf * cdf + state_update(c, zeta_f)

    sf = jnp.zeros((RET_DIM, CH), F32)
    for c in range(nl, nl + nc):
        sf = fwd_step(c, sf)
    lax.fori_loop(0, nl, fwd_step, sf, unroll=RET_UNROLL)


def _retention(rq, rk, rv, rg, dec, S, C):
    B, H, d, Sp = rq.shape
    kern = functools.partial(_ret_kernel, S=S, C=C)
    return pl.pallas_call(
        kern,
        grid=(B, H),
        in_specs=[
            pl.BlockSpec((1, 1, d, Sp), lambda b, h: (b, h, 0, 0)),
            pl.BlockSpec((1, Sp, LANES), lambda b, h: (b, 0, h)),
            pl.BlockSpec((1, 1, d, Sp), lambda b, h: (b, h, 0, 0)),
            pl.BlockSpec((1, d, Sp), lambda b, h: (b, h, 0)),
            pl.BlockSpec((2, 1, 8, LANES), lambda b, h: (0, h, 0, 0)),
        ],
        out_specs=pl.BlockSpec((1, d, Sp), lambda b, h: (b, h, 0)),
        out_shape=jax.ShapeDtypeStruct((B, H * d, Sp), BF16),
        scratch_shapes=[pltpu.VMEM((Sp // RET_CHUNK, RET_DIM, RET_CHUNK), F32)],
        compiler_params=pltpu.CompilerParams(
            dimension_semantics=("arbitrary", "arbitrary"), vmem_limit_bytes=56 * 1024 * 1024),
        name="retention",
    )(rq, rk, rv, rg, dec)


def _layer_norm(t, g, b):
    mu = jnp.mean(t, axis=-1, keepdims=True)
    tc = t - mu
    var = jnp.mean(tc * tc, axis=-1, keepdims=True)
    return tc * lax.rsqrt(var + LN_EPS) * g + b


def _out_kernel(yda_ref, ymla_ref, yret_ref, w_ref, x_ref, gate_ref, g1_ref, b1_ref, sh2_ref, sc2_ref, rw_ref,
                x1_ref, h2_ref, lg_ref, *, alpha):
    proj = (lax.dot_general(yda_ref[0], w_ref[0, 0:DA_VW, :], TN_DIMS, preferred_element_type=F32)
            + lax.dot_general(ymla_ref[0], w_ref[0, DA_VW:DA_VW + MLA_VW, :], TN_DIMS, preferred_element_type=F32)
            + lax.dot_general(yret_ref[0], w_ref[0, DA_VW + MLA_VW:, :], TN_DIMS, preferred_element_type=F32))
    x1 = _layer_norm(alpha * x_ref[0] + gate_ref[0, 0, 0] * proj, g1_ref[0], b1_ref[0])
    x1_ref[0] = x1
    h2 = x1 * (1.0 + sc2_ref[0, 0, 0]) + sh2_ref[0, 0, 0]
    h2_ref[0] = h2.astype(BF16)
    h2_hi = h2.astype(BF16)
    h2_lo = (h2 - h2_hi.astype(F32)).astype(BF16)
    hi = jnp.dot(h2_hi, rw_ref[...], preferred_element_type=F32)
    lo = jnp.dot(h2_lo, rw_ref[:, 0:LANES], preferred_element_type=F32)
    lg_ref[0] = hi[:, 0:LANES] + (hi[:, LANES:] + lo)


def _out_proj(l, yda, ymla, yret, w_out, xc, modr, ln_g, ln_b, rw_pad, S, alpha, latents_only):
    B, Sp, D = xc.shape
    n_lat_tiles = S // TM
    n_rows = S if latents_only else Sp

    def mod_spec(j):
        return pl.BlockSpec((1, 1, 1, 1, D), lambda b, t: (l, jnp.where(t >= n_lat_tiles, B, b), j, 0, 0))

    def fm(a):
        return pl.BlockSpec((1, a.shape[1], TM), lambda b, t: (b, 0, t))

    def wspec(a):
        return pl.BlockSpec((1,) + a.shape[1:], lambda b, t: (l,) + (0,) * (a.ndim - 1))

    row = pl.BlockSpec((1, TM, D), lambda b, t: (b, t, 0))
    sds = jax.ShapeDtypeStruct
    return pl.pallas_call(
        functools.partial(_out_kernel, alpha=alpha),
        grid=(B, n_rows // TM),
        in_specs=[fm(yda), fm(ymla), fm(yret), wspec(w_out), row, mod_spec(2), wspec(ln_g), wspec(ln_b),
                  mod_spec(3), mod_spec(4), pl.BlockSpec(rw_pad.shape, lambda b, t: (0, 0))],
        out_specs=[row, row, pl.BlockSpec((1, TM, LANES), lambda b, t: (b, t, 0))],
        out_shape=[sds((B, n_rows, D), F32), sds((B, n_rows, D), BF16), sds((B, n_rows, LANES), F32)],
        compiler_params=pltpu.CompilerParams(dimension_semantics=("arbitrary", "arbitrary")),
        name="out_proj",
    )(yda, ymla, yret, w_out, xc, modr, ln_g, ln_b, modr, modr, rw_pad)


def _gates_T(logits_T, rb):
    s = _sigmoid(logits_T)
    ch = s + rb
    srow = [s[e:e + 1] for e in range(N_EXPERTS)]
    crow = [ch[e:e + 1] for e in range(N_EXPERTS)]
    per = N_EXPERTS // N_GROUPS
    gs = []
    for g in range(N_GROUPS):
        a, b, c, d = crow[per * g: per * g + per]
        m1, n1, m2, n2 = jnp.maximum(a, b), jnp.minimum(a, b), jnp.maximum(c, d), jnp.minimum(c, d)
        gs.append(jnp.maximum(m1, m2) + jnp.maximum(jnp.minimum(m1, m2), jnp.maximum(n1, n2)))
    gmax = functools.reduce(jnp.maximum, gs)
    taken = jnp.zeros(gmax.shape, jnp.bool_)
    gsel = []
    for g in range(N_GROUPS):
        sg = jnp.logical_and(gs[g] == gmax, jnp.logical_not(taken))
        gsel.append(sg)
        taken = jnp.logical_or(taken, sg)
    neg = jnp.full(gmax.shape, -jnp.inf, F32)
    mc = [jnp.where(gsel[e // per], crow[e], neg) for e in range(N_EXPERTS)]
    sel = [jnp.zeros(gmax.shape, jnp.bool_) for _ in range(N_EXPERTS)]
    for _ in range(2):
        top = functools.reduce(jnp.maximum, mc)
        taken = jnp.zeros(gmax.shape, jnp.bool_)
        for e in range(N_EXPERTS):
            hit = jnp.logical_and(mc[e] == top, jnp.logical_not(taken))
            taken = jnp.logical_or(taken, hit)
            sel[e] = jnp.logical_or(sel[e], hit)
            mc[e] = jnp.where(hit, neg, mc[e])
    w = [jnp.where(sel[e], srow[e], 0.0) for e in range(N_EXPERTS)]
    wsum = functools.reduce(lambda a, b: a + b, w)
    gates = jnp.concatenate([we / wsum * ROUTED_SCALE for we in w], axis=0)
    return gates, [jnp.where(sg, 1.0, 0.0) for sg in gsel]


def _moe_sorted_rows(tm):
    return tm + N_GROUPS * MOE_ALIGN + MOE_BLOCK


def _moe_kernel(h_ref, lg_ref, rb_ref, w13_ref, w2_ref, o_ref,
                off_sc, nblk_sc, perm_sc, hs_sc, gs_sc, accs_sc, earlier_sc, *, nsub):
    e = pl.program_id(2)
    tm = h_ref.shape[1] // nsub
    n_sorted = perm_sc.shape[1]
    per = N_EXPERTS // N_GROUPS
    lane = lax.broadcasted_iota(jnp.int32, (1, LANES), 1)

    def expert(x):
        ab = jnp.dot(x, w13_ref[0, 0], preferred_element_type=F32)
        a = ab[:, 0:D_EXPERT]
        hid = (a * _sigmoid(a) * ab[:, D_EXPERT:]).astype(BF16)
        return jnp.dot(hid, w2_ref[0, 0], preferred_element_type=F32)

    def gate_col(g):
        return jnp.sum(jnp.where(lane == e, g, 0.0), axis=1, keepdims=True)

    def sub(j):
        return slice(j * tm, (j + 1) * tm)

    @pl.when(jnp.logical_and(jnp.logical_and(pl.program_id(0) == 0, pl.program_id(1) == 0), e == 0))
    def _():
        before = lax.broadcasted_iota(jnp.int32, (tm, tm), 0) < lax.broadcasted_iota(jnp.int32, (tm, tm), 1)
        earlier_sc[...] = jnp.where(before, 1.0, 0.0).astype(BF16)

    @pl.when(e == 0)
    def _():
        for j in range(nsub):
            gT, gsel = _gates_T(lg_ref[0, sub(j), :].T[0:N_EXPERTS, :], rb_ref[...])
            gates = jnp.concatenate([gT, jnp.zeros((LANES - N_EXPERTS, tm), F32)], axis=0).T
            member = jnp.concatenate(gsel + [jnp.zeros((16 - N_GROUPS, tm), F32)], axis=0)
            rank = jnp.dot(member.astype(BF16), earlier_sc[...], preferred_element_type=F32)
            count = jnp.sum(member, axis=1, keepdims=True)
            off = jnp.int32(0)
            slot = jnp.zeros((1, tm), F32)
            for g in range(N_GROUPS):
                n_g = jnp.max(count[g:g + 1]).astype(jnp.int32)
                off_sc[j * N_GROUPS + g] = off
                nblk_sc[j * N_GROUPS + g] = (n_g + (MOE_BLOCK - 1)) // MOE_BLOCK
                slot = slot + gsel[g] * (rank[g:g + 1] + off.astype(F32))
                off = off + ((n_g + (MOE_ALIGN - 1)) // MOE_ALIGN) * MOE_ALIGN
            rows_i = lax.broadcasted_iota(jnp.int32, (n_sorted, tm), 0)
            perm = jnp.where(rows_i == slot.astype(jnp.int32), 1.0, 0.0).astype(BF16)
            perm_sc[j] = perm
            hs_sc[j] = jnp.dot(perm, h_ref[0, sub(j), :], preferred_element_type=F32).astype(BF16)
            g_hi = gates.astype(BF16)
            g_lo = (gates - g_hi.astype(F32)).astype(BF16)
            gs_sc[j] = (jnp.dot(perm, g_hi, preferred_element_type=F32)
                        + jnp.dot(perm, g_lo, preferred_element_type=F32))
            accs_sc[j] = jnp.zeros(accs_sc.shape[1:], F32)

    for j in range(nsub):
        start = off_sc[j * N_GROUPS + e // per]

        def block(k, carry, j=j, start=start):
            rows = pl.ds(pl.multiple_of(start + k * MOE_BLOCK, MOE_ALIGN), MOE_BLOCK)
            accs_sc[j, rows, :] += gate_col(gs_sc[j, rows, :]) * expert(hs_sc[j, rows, :])
            return carry

        lax.fori_loop(0, nblk_sc[j * N_GROUPS + e // per], block, 0)

    @pl.when(e == N_EXPERTS - 1)
    def _():
        for j in range(nsub):
            o_ref[0, sub(j), :] = lax.dot_general(perm_sc[j], accs_sc[j].astype(BF16), TN_DIMS,
                                                  preferred_element_type=F32).astype(BF16)


def _moe_ln_kernel(x1_ref, f_ref, gb_ref, gc_ref, g2_ref, b2_ref, o_ref, *, S, alpha):
    tm = x1_ref.shape[1]
    n_lat = S - pl.program_id(1) * tm
    rowi = lax.broadcasted_iota(jnp.int32, (tm, 1), 0)
    gate = jnp.where(rowi < n_lat, gb_ref[0, 0, 0], gc_ref[0, 0, 0])
    o_ref[0] = _layer_norm(alpha * x1_ref[0] + gate * f_ref[0].astype(F32), g2_ref[0], b2_ref[0])


def _moe(l, h2, logits, rb, w13, w2, x1, modr, ln_g, ln_b, S, alpha, latents_only):
    B, n_rows, D = x1.shape
    sub = TM_MOE_LAST if latents_only else TM_MOE
    tm = MOE_SUBTILES * sub
    n_sorted = _moe_sorted_rows(sub)

    def wspec(a):
        return pl.BlockSpec((1,) + a.shape[1:], lambda b, t: (l,) + (0,) * (a.ndim - 1))

    row = lambda b, t, e: (b, t, 0)
    f = pl.pallas_call(
        functools.partial(_moe_kernel, nsub=MOE_SUBTILES),
        grid=(B, n_rows // tm, N_EXPERTS),
        in_specs=[
            pl.BlockSpec((1, tm, D), row),
            pl.BlockSpec((1, tm, LANES), row),
            pl.BlockSpec(rb.shape, lambda b, t, e: (0, 0)),
            pl.BlockSpec((1, 1, D, 2 * D_EXPERT), lambda b, t, e: (l, e, 0, 0)),
            pl.BlockSpec((1, 1, D_EXPERT, D), lambda b, t, e: (l, e, 0, 0)),
        ],
        out_specs=pl.BlockSpec((1, tm, D), row),
        out_shape=jax.ShapeDtypeStruct((B, n_rows, D), BF16),
        scratch_shapes=[
            pltpu.SMEM((MOE_SUBTILES * N_GROUPS,), jnp.int32), pltpu.SMEM((MOE_SUBTILES * N_GROUPS,), jnp.int32),
            pltpu.VMEM((MOE_SUBTILES, n_sorted, sub), BF16),
            pltpu.VMEM((MOE_SUBTILES, n_sorted, D), BF16),
            pltpu.VMEM((MOE_SUBTILES, n_sorted, LANES), F32),
            pltpu.VMEM((MOE_SUBTILES, n_sorted, D), F32),
            pltpu.VMEM((sub, sub), BF16)],
        compiler_params=pltpu.CompilerParams(
            dimension_semantics=("arbitrary", "arbitrary", "arbitrary"), vmem_limit_bytes=56 * 1024 * 1024),
        name="moe",
    )(h2, logits, rb, w13, w2)
    rows = pl.BlockSpec((1, tm, D), lambda b, t: (b, t, 0))
    return pl.pallas_call(
        functools.partial(_moe_ln_kernel, S=S, alpha=alpha),
        grid=(B, n_rows // tm),
        in_specs=[rows, rows,
                  pl.BlockSpec((1, 1, 1, 1, D), lambda b, t: (l, b, 5, 0, 0)),
                  pl.BlockSpec((1, 1, 1, 1, D), lambda b, t: (l, B, 5, 0, 0)),
                  wspec(ln_g), wspec(ln_b)],
        out_specs=rows,
        out_shape=jax.ShapeDtypeStruct((B, n_rows, D), F32),
        compiler_params=pltpu.CompilerParams(
            dimension_semantics=("arbitrary", "arbitrary"), vmem_limit_bytes=56 * 1024 * 1024),
        name="moe_ln",
    )(x1, f, modr, modr, ln_g, ln_b)


def _rope_tables(S, C, rot_dim):
    rows = S // GRID_W
    r = jnp.repeat(jnp.arange(rows, dtype=F32), GRID_W)
    col = jnp.tile(jnp.arange(GRID_W, dtype=F32), rows)
    n_freq = rot_dim // 4
    freqs = ROPE_BASE ** (-jnp.arange(n_freq, dtype=F32) / n_freq)
    ang = jnp.concatenate([r[:, None] * freqs, col[:, None] * freqs], -1)
    cos, sin = jnp.cos(ang), jnp.sin(ang)
    cos_rep = jnp.repeat(cos, 2, axis=-1)
    sin_alt = jnp.stack([-sin, sin], -1).reshape(S, rot_dim)
    cos_rep = jnp.concatenate([cos_rep, jnp.ones((C, rot_dim), F32)], 0)
    sin_alt = jnp.concatenate([sin_alt, jnp.zeros((C, rot_dim), F32)], 0)
    return cos_rep.T, sin_alt.T


def kernel(x, c, ctx, c_ctx, w_ada, b_ada, w_in, da_lambda, da_subln, mla_q_norm, mla_w_uq, mla_kv_norm, mla_w_ukv, ret_decay_f, ret_decay_b, w_out, ln1_g, ln1_b, router_w, router_b, exp_w1, exp_w3, exp_w2, ln2_g, ln2_b):
    B, S, D = x.shape
    C = ctx.shape[1]
    depth = w_in.shape[0]
    alpha = float((2 * depth) ** 0.25)

    cvec = jnp.concatenate([c, c_ctx[None, :], jnp.zeros((8 - B - 1, D), F32)], 0)
    mod = _modulation(cvec, w_ada, b_ada)
    modr = mod.reshape(depth, 8, N_MOD, 1, D)

    wT_all = jnp.swapaxes(w_in, 1, 2).astype(BF16)
    wuqT = jnp.swapaxes(mla_w_uq, 1, 2).astype(BF16)
    ukv = mla_w_ukv.reshape(depth, MLA_KV_RANK, MLA_HEADS, MLA_NOPE + MLA_VDIM)
    wvT = jnp.swapaxes(ukv[..., MLA_NOPE:].reshape(depth, MLA_KV_RANK, MLA_VW), 1, 2).astype(BF16)
    nopeT = jnp.transpose(ukv[..., :MLA_NOPE], (0, 2, 3, 1))
    top = jnp.concatenate([nopeT, jnp.zeros((depth, MLA_HEADS, MLA_NOPE, MLA_ROPE), F32)], -1)
    mid = jnp.concatenate([jnp.zeros((MLA_ROPE, MLA_KV_RANK), F32), jnp.eye(MLA_ROPE, dtype=F32)], -1)
    mid = jnp.broadcast_to(mid, (depth, MLA_HEADS, MLA_ROPE, MLA_KV_RANK + MLA_ROPE))
    bot = jnp.zeros((depth, MLA_HEADS, LANES - MLA_NOPE - MLA_ROPE, MLA_KV_RANK + MLA_ROPE), F32)
    wkT = jnp.concatenate([top, mid, bot], 2).reshape(depth, MLA_HEADS * LANES, MLA_KV_RANK + MLA_ROPE).astype(BF16)
    gq = mla_q_norm[:, :, None]
    gkv = mla_kv_norm[:, :, None]
    subln = da_subln[:, :, None]
    w_out_b = w_out.astype(BF16)
    w13 = jnp.concatenate([exp_w1, exp_w3], -1).astype(BF16)
    w2 = exp_w2.astype(BF16)
    rw_f = jnp.concatenate([router_w, jnp.zeros((D, LANES - N_EXPERTS), F32)], -1)
    rw_hi = rw_f.astype(BF16)
    rw_pad = jnp.concatenate([rw_hi, (rw_f - rw_hi.astype(F32)).astype(BF16)], -1)
    rb = router_b[:, None]
    dec = jnp.broadcast_to(jnp.stack([ret_decay_f, ret_decay_b], 1)[:, :, :, None, None],
                           (depth, 2, RET_HEADS, 8, LANES))
    tabs = _rope_tables(S, C, DA_DIM) + _rope_tables(S, C, RET_DIM)
    ln1g, ln1b, ln2g, ln2b = (a[:, None, :] for a in (ln1_g, ln1_b, ln2_g, ln2_b))

    xc = jnp.concatenate([x, ctx], 1)
    for l in range(depth):
        last = l == depth - 1
        (daqr, daqf, dak, dav, mqr, mqf, mk, mv, rq, rk, rv, rg) = _in_proj(
            l, xc, modr, wT_all, tabs, gq, wuqT, gkv, wvT, wkT, S)
        yda = _attention("da", l, daqr, daqf, dak, dav, S, C, not last, lam=da_lambda, g=subln)
        ymla = _attention("mla", l, mqr, mqf, mk, mv, S, C, not last)
        yret = _retention(rq, rk, rv, rg, dec[l], S, C)
        x1, h2, logits = _out_proj(l, yda, ymla, yret, w_out_b, xc, modr, ln1g, ln1b, rw_pad, S, alpha, last)
        xc = _moe(l, h2, logits, rb, w13, w2, x1, modr, ln2g, ln2b, S, alpha, last)
    return xc
```

```python
import functools
import math

import jax
import jax.numpy as jnp
import numpy as np
from jax import lax
from jax.experimental import pallas as pl
from jax.experimental.pallas import tpu as pltpu

F32 = jnp.float32
BF16 = jnp.bfloat16

GRID_W = 64
DA_HEADS, DA_DIM, DA_VDIM = 6, 32, 64
MLA_HEADS, MLA_NOPE, MLA_ROPE, MLA_VDIM = 6, 64, 32, 64
MLA_Q_RANK, MLA_KV_RANK = 256, 128
RET_HEADS, RET_DIM, RET_CHUNK = 4, 64, 128
ROPE_BASE = 10000.0
N_EXPERTS, N_GROUPS, D_EXPERT = 16, 4, 512
ROUTED_SCALE = 1.0
N_MOD = 6
LN_EPS = 1e-5
RMS_EPS = 1e-6
LOG2E = math.log2(math.e)

DA_W = DA_HEADS * 2 * DA_DIM
DA_VW = DA_HEADS * DA_VDIM
RET_W = RET_HEADS * RET_DIM
MLA_QW = MLA_HEADS * (MLA_NOPE + MLA_ROPE)
MLA_VW = MLA_HEADS * MLA_VDIM
IN_SPLIT = (DA_W, DA_W, DA_VW, MLA_Q_RANK, MLA_KV_RANK, MLA_ROPE, RET_W, RET_W, RET_W, RET_W)

LANES = 128
V_ROWS = 80
TM = 256
ATTN_N = 512
TK = 2048
ATTN_SUBTILES = 4
ATTN_NBUF = 2
ATTN_MAX_JUMP = 32.0
ATTN_UNROLL = 2
TM_MOE = 640
TM_MOE_LAST = 512
MOE_SUBTILES = 2
MOE_BLOCK = 256
MOE_ALIGN = 32
RET_UNROLL = 8

NT_DIMS = (((1,), (1,)), ((), ()))
TN_DIMS = (((0,), (0,)), ((), ()))


def _sigmoid(x):
    return 1.0 / (1.0 + jnp.exp(-x))


def _mod_kernel(c_ref, w_ref, b_ref, o_ref):
    c = c_ref[...]
    sc = (c * _sigmoid(c)).astype(BF16)
    o_ref[0] = jnp.dot(sc, w_ref[0].astype(BF16), preferred_element_type=F32) + b_ref[0]


def _modulation(cvec, w_ada, b_ada):
    depth, d, n = w_ada.shape
    tn = 1536
    return pl.pallas_call(
        _mod_kernel,
        grid=(depth, n // tn),
        in_specs=[
            pl.BlockSpec((8, d), lambda l, j: (0, 0)),
            pl.BlockSpec((1, d, tn), lambda l, j: (l, 0, j)),
            pl.BlockSpec((1, 1, tn), lambda l, j: (l, 0, j)),
        ],
        out_specs=pl.BlockSpec((1, 8, tn), lambda l, j: (l, 0, j)),
        out_shape=jax.ShapeDtypeStruct((depth, 8, n), F32),
        name="adaln_mod",
    )(cvec, w_ada, b_ada.reshape(depth, 1, n))


_O_DAQ, _O_DAK, _O_DAV, _O_CQ, _O_CKV, _O_KPE, _O_RQ, _O_RK, _O_RV, _O_RG = (
    int(v) for v in np.cumsum((0,) + IN_SPLIT)[:-1])
N_ZROWS = sum(IN_SPLIT)


def _rms_rows(x):
    return x * lax.rsqrt(jnp.mean(x * x, axis=0, keepdims=True) + RMS_EPS)


def _pairswap_rows(x):
    n = x.shape[0]
    even = (lax.broadcasted_iota(jnp.int32, x.shape, 0) & 1) == 0
    return jnp.where(even, pltpu.roll(x, n - 1, 0), pltpu.roll(x, 1, 0))


def _in_kernel(x_ref, sh_ref, sc_ref, wT_ref, cosA_ref, sinA_ref, cosR_ref, sinR_ref,
               gq_ref, wuqT_ref, gkv_ref, wvT_ref, wkT_ref,
               daqr, daqf, dak, dav, mqr, mqf, mk, mv, rq, rk, rv, rg, z_ref, *, cq_da, cq_mla, kscale):
    tm = x_ref.shape[1]
    x = x_ref[0]
    h = (x * (1.0 + sc_ref[0, 0, 0]) + sh_ref[0, 0, 0]).astype(BF16)
    z_ref[...] = lax.dot_general(wT_ref[0], h, NT_DIMS, preferred_element_type=F32)
    cosA = cosA_ref[...]
    sinA = sinA_ref[...]
    cosR = cosR_ref[...]
    sinR = sinR_ref[...]
    ones_rows = (lax.broadcasted_iota(jnp.int32, (DA_HEADS, V_ROWS - 64, tm), 1) == 0).astype(BF16)

    def rows(off, n):
        return z_ref[off:off + n, :]

    q2 = rows(_O_DAQ, DA_W)
    q = q2.reshape(12, DA_DIM, tm)
    qs = _pairswap_rows(q2).reshape(12, DA_DIM, tm)
    daqr[0] = ((q * cosA + qs * sinA) * cq_da).astype(BF16)
    daqf[0] = (q * cq_da).astype(BF16)
    k2 = rows(_O_DAK, DA_W)
    k = k2.reshape(12, DA_DIM, tm)
    ks = _pairswap_rows(k2).reshape(12, DA_DIM, tm)
    kr = (k * cosA + ks * sinA).reshape(DA_HEADS, 2 * DA_DIM, tm)
    kp = jnp.concatenate([kr, jnp.zeros_like(kr)], axis=1).reshape(DA_HEADS * LANES, tm)
    dak[0] = kp.T.astype(BF16)
    dav[0, :, 0:64, :] = rows(_O_DAV, DA_VW).reshape(DA_HEADS, DA_VDIM, tm).astype(BF16)
    dav[0, :, 64:V_ROWS, :] = ones_rows

    cqn = (_rms_rows(rows(_O_CQ, MLA_Q_RANK)) * gq_ref[0]).astype(BF16)
    qm = jnp.dot(wuqT_ref[0], cqn, preferred_element_type=F32)
    qh = qm.reshape(MLA_HEADS, MLA_NOPE + MLA_ROPE, tm)
    qsw = _pairswap_rows(qm).reshape(MLA_HEADS, MLA_NOPE + MLA_ROPE, tm)[:, MLA_NOPE:, :]
    rot = qh[:, MLA_NOPE:, :] * cosA + qsw * sinA
    mqr[0, :, 0:MLA_NOPE, :] = (qh[:, 0:MLA_NOPE, :] * cq_mla).astype(BF16)
    mqr[0, :, MLA_NOPE:, :] = (rot * cq_mla).astype(BF16)
    mqf[0] = (qh * cq_mla).astype(BF16)
    ckvn = (_rms_rows(rows(_O_CKV, MLA_KV_RANK)) * gkv_ref[0]).astype(BF16)
    vm = jnp.dot(wvT_ref[0], ckvn, preferred_element_type=F32)
    mv[0, :, 0:64, :] = vm.reshape(MLA_HEADS, MLA_VDIM, tm).astype(BF16)
    mv[0, :, 64:V_ROWS, :] = ones_rows
    kpe = rows(_O_KPE, MLA_ROPE)
    kper = kpe * cosA + _pairswap_rows(kpe) * sinA
    kin = jnp.concatenate([ckvn, kper.astype(BF16)], axis=0)
    kmT = jnp.dot(wkT_ref[0], kin, preferred_element_type=F32)
    mk[0] = kmT.T.astype(BF16)

    r_q2 = rows(_O_RQ, RET_W)
    r_q = r_q2.reshape(RET_HEADS, RET_DIM, tm)
    r_qs = _pairswap_rows(r_q2).reshape(RET_HEADS, RET_DIM, tm)
    rq[0] = (r_q * cosR + r_qs * sinR).astype(BF16)
    r_k2 = rows(_O_RK, RET_W)
    r_k = r_k2.reshape(RET_HEADS, RET_DIM, tm)
    r_ks = _pairswap_rows(r_k2).reshape(RET_HEADS, RET_DIM, tm)
    rkr = (r_k * cosR + r_ks * sinR) * kscale
    rkp = jnp.concatenate([rkr, jnp.zeros_like(rkr)], axis=1).reshape(RET_HEADS * LANES, tm)
    rk[0] = rkp.T.astype(BF16)
    rv[0] = rows(_O_RV, RET_W).reshape(RET_HEADS, RET_DIM, tm).astype(BF16)
    rg[0] = rows(_O_RG, RET_W)


def _in_proj(l, xc, modr, wT_all, tabs, gq, wuqT, gkv, wvT, wkT, S):
    B, Sp, D = xc.shape
    nt = Sp // TM
    n_lat_tiles = S // TM
    cosA, sinA, cosR, sinR = tabs

    def mod_spec(j):
        return pl.BlockSpec((1, 1, 1, 1, D), lambda b, t: (l, jnp.where(t >= n_lat_tiles, B, b), j, 0, 0))

    def wspec(a):
        return pl.BlockSpec((1,) + a.shape[1:], lambda b, t: (l,) + (0,) * (a.ndim - 1))

    def fm4(h, d):
        return pl.BlockSpec((1, h, d, TM), lambda b, t: (b, 0, 0, t))

    def tmaj(w):
        return pl.BlockSpec((1, TM, w), lambda b, t: (b, t, 0))

    kern = functools.partial(
        _in_kernel,
        cq_da=float(DA_DIM ** -0.5 * LOG2E),
        cq_mla=float((MLA_NOPE + MLA_ROPE) ** -0.5 * LOG2E),
        kscale=float(RET_DIM ** -0.5),
    )
    sds = jax.ShapeDtypeStruct
    return pl.pallas_call(
        kern,
        grid=(B, nt),
        in_specs=[
            pl.BlockSpec((1, TM, D), lambda b, t: (b, t, 0)),
            mod_spec(0), mod_spec(1),
            wspec(wT_all),
            pl.BlockSpec((DA_DIM, TM), lambda b, t: (0, t)),
            pl.BlockSpec((DA_DIM, TM), lambda b, t: (0, t)),
            pl.BlockSpec((RET_DIM, TM), lambda b, t: (0, t)),
            pl.BlockSpec((RET_DIM, TM), lambda b, t: (0, t)),
            wspec(gq), wspec(wuqT), wspec(gkv), wspec(wvT), wspec(wkT),
        ],
        out_specs=[
            fm4(12, DA_DIM), fm4(12, DA_DIM), tmaj(DA_HEADS * LANES), fm4(DA_HEADS, V_ROWS),
            fm4(MLA_HEADS, 96), fm4(MLA_HEADS, 96), tmaj(MLA_HEADS * LANES), fm4(MLA_HEADS, V_ROWS),
            fm4(RET_HEADS, RET_DIM), tmaj(RET_HEADS * LANES), fm4(RET_HEADS, RET_DIM),
            pl.BlockSpec((1, RET_W, TM), lambda b, t: (b, 0, t)),
        ],
        out_shape=[
            sds((B, 12, DA_DIM, Sp), BF16), sds((B, 12, DA_DIM, Sp), BF16),
            sds((B, Sp, DA_HEADS * LANES), BF16), sds((B, DA_HEADS, V_ROWS, Sp), BF16),
            sds((B, MLA_HEADS, 96, Sp), BF16), sds((B, MLA_HEADS, 96, Sp), BF16),
            sds((B, Sp, MLA_HEADS * LANES), BF16), sds((B, MLA_HEADS, V_ROWS, Sp), BF16),
            sds((B, RET_HEADS, RET_DIM, Sp), BF16), sds((B, Sp, RET_HEADS * LANES), BF16),
            sds((B, RET_HEADS, RET_DIM, Sp), BF16), sds((B, RET_W, Sp), F32),
        ],
        scratch_shapes=[pltpu.VMEM((N_ZROWS, TM), F32)],
        compiler_params=pltpu.CompilerParams(
            dimension_semantics=("arbitrary", "arbitrary"), vmem_limit_bytes=56 * 1024 * 1024),
        name="in_proj",
    )(xc, modr, modr, wT_all, cosA, sinA, cosR, sinR, gq, wuqT, gkv, wvT, wkT)


def _attn_kernel(*refs, mode, S, C, lam_init, ctx_only, nsub):
    refs = list(refs)
    qr_ref, qf_ref, k_ref, v_ref = refs[:4]
    del refs[:4]
    if mode == "da":
        lam_ref, g_ref = refs[:2]
        del refs[:2]
    if ctx_only:
        del refs[:1]
    o_ref, m_sc, acc_sc, q_sc, jump_sc = refs[:5]
    nb = ATTN_NBUF
    pbuf, albuf = (refs[5 + i * nb: 5 + (i + 1) * nb] for i in range(2))
    tq = o_ref.shape[2] // nsub
    n_lat = S // TK

    def qpad(ref, t):
        cols = slice(t * tq, (t + 1) * tq)
        if mode == "da":
            z = jnp.zeros((DA_DIM, tq), BF16)
            c1 = jnp.concatenate([ref[0, 0, :, cols], z, z, z], axis=0)
            c2 = jnp.concatenate([z, ref[0, 1, :, cols], z, z], axis=0)
            return jnp.concatenate([c1, c2], axis=1)
        return jnp.concatenate([ref[0, 0, :, cols], jnp.zeros((LANES - 96, tq), BF16)], axis=0)

    def chunk(j):
        return pl.ds(j * TK if isinstance(j, int) else pl.multiple_of(j * TK, TK), TK)

    def context_keys(t):
        c0 = 0 if ctx_only else S
        s = jnp.dot(k_ref[0, c0:c0 + C, :], qpad(qf_ref, t), preferred_element_type=F32)
        m_c = jnp.max(s, axis=0, keepdims=True)
        m_sc[t] = m_c
        acc_sc[t] = jnp.dot(v_ref[0, 0, :, c0:c0 + C], jnp.exp2(s - m_c).astype(BF16), preferred_element_type=F32)

    for t in range(nsub):
        context_keys(t)

    if not ctx_only:
        for t in range(nsub):
            q_sc[t] = qpad(qr_ref, t)
        jump_sc[...] = jnp.full(jump_sc.shape, -jnp.inf, F32)

        def probs(t, j, r):
            s = jnp.dot(k_ref[0, chunk(j), :], q_sc[t], preferred_element_type=F32)
            m_prev = m_sc[t]
            pbuf[r][t] = jnp.exp2(s - m_prev).astype(BF16)
            m_chunk = jnp.max(s, axis=0, keepdims=True)
            m_new = jnp.maximum(m_prev, m_chunk)
            jump_sc[t] = jnp.maximum(jump_sc[t], m_chunk - m_prev)
            albuf[r][t] = jnp.exp2(m_prev - m_new)
            m_sc[t] = m_new

        def accumulate(t, j, r):
            pv = jnp.dot(v_ref[0, 0, :, chunk(j)], pbuf[r][t], preferred_element_type=F32)
            acc_sc[t] = (acc_sc[t] + pv) * albuf[r][t]

        def step(j, r):
            for t in range(nsub):
                accumulate(t, j, r)
                probs(t, j + 1, (r + 1) % nb)

        for t in range(nsub):
            probs(t, 0, 0)
        n_steps = n_lat - 1
        trips = n_steps // ATTN_UNROLL

        def body(i, carry):
            for u in range(ATTN_UNROLL):
                step(i * ATTN_UNROLL + u, u % nb)
            return carry

        lax.fori_loop(0, trips, body, 0)
        for j in range(trips * ATTN_UNROLL, n_steps):
            step(j, j % nb)
        for t in range(nsub):
            accumulate(t, n_lat - 1, (n_lat - 1) % nb)

        @pl.when(jnp.max(jump_sc[...]) > ATTN_MAX_JUMP)
        def _():
            for t in range(nsub):
                context_keys(t)

                def exact_step(j, carry, t=t):
                    s = jnp.dot(k_ref[0, chunk(j), :], q_sc[t], preferred_element_type=F32)
                    m_old = m_sc[t]
                    m_new = jnp.maximum(m_old, jnp.max(s, axis=0, keepdims=True))
                    pv = jnp.dot(v_ref[0, 0, :, chunk(j)], jnp.exp2(s - m_new).astype(BF16),
                                 preferred_element_type=F32)
                    acc_sc[t] = acc_sc[t] * jnp.exp2(m_old - m_new) + pv
                    m_sc[t] = m_new
                    return carry

                lax.fori_loop(0, n_lat, exact_step, 0)

    for t in range(nsub):
        acc = acc_sc[t]
        o = acc[0:64, :] / acc[64:65, :]
        cols = slice(t * tq, (t + 1) * tq)
        if mode == "da":
            lf = lam_ref[0]
            lam = (jnp.exp(jnp.sum(lf[0:1] * lf[1:2], axis=1, keepdims=True))
                   - jnp.exp(jnp.sum(lf[2:3] * lf[3:4], axis=1, keepdims=True)) + lam_init)
            od = o[:, 0:tq] - lam * o[:, tq:2 * tq]
            y = _rms_rows(od) * g_ref[0]
            o_ref[0, :, cols] = (y * (1.0 - lam_init)).astype(BF16)
        else:
            o_ref[0, :, cols] = o.astype(BF16)


def _attention_call(mode, l, qr, qf, k, v, S, C, lam, g, y_latent):
    B, nmaps, d, Sp = qr.shape
    H = v.shape[1]
    nm = nmaps // H
    ctx_only = y_latent is not None
    nsub = 1 if ctx_only else ATTN_SUBTILES
    tq = C if ctx_only else ATTN_N // nm
    n = nm * tq
    tqb = nsub * tq
    q0 = S // tqb if ctx_only else 0
    lam_init = 0.8 - 0.6 * math.exp(-0.3 * l)
    kern = functools.partial(_attn_kernel, mode=mode, S=S, C=C, lam_init=lam_init, ctx_only=ctx_only, nsub=nsub)
    in_specs = [
        pl.BlockSpec((1, nm, d, tqb), lambda b, h, i: (b, h, 0, q0 + i)),
        pl.BlockSpec((1, nm, d, tqb), lambda b, h, i: (b, h, 0, q0 + i)),
        (pl.BlockSpec((1, C, LANES), lambda b, h, i: (b, S // C, h)) if ctx_only
         else pl.BlockSpec((1, Sp, LANES), lambda b, h, i: (b, 0, h))),
        (pl.BlockSpec((1, 1, V_ROWS, C), lambda b, h, i: (b, h, 0, S // C)) if ctx_only
         else pl.BlockSpec((1, 1, V_ROWS, Sp), lambda b, h, i: (b, h, 0, 0))),
    ]
    args = [qr, qf, k, v]
    if mode == "da":
        in_specs += [pl.BlockSpec((1,) + lam.shape[1:], lambda b, h, i: (l, 0, 0)),
                     pl.BlockSpec((1,) + g.shape[1:], lambda b, h, i: (l, 0, 0))]
        args += [lam, g]
    aliases = {}
    if ctx_only:
        in_specs.append(pl.BlockSpec(memory_space=pl.ANY))
        args.append(y_latent)
        aliases = {len(args) - 1: 0}
    tk = C if ctx_only else TK
    return pl.pallas_call(
        kern,
        grid=(B, H, 1 if ctx_only else S // tqb),
        in_specs=in_specs,
        out_specs=pl.BlockSpec((1, 64, tqb), lambda b, h, i: (b, h, q0 + i)),
        out_shape=jax.ShapeDtypeStruct((B, H * 64, Sp), BF16),
        input_output_aliases=aliases,
        scratch_shapes=(
            [pltpu.VMEM((nsub, 1, n), F32), pltpu.VMEM((nsub, V_ROWS, n), F32), pltpu.VMEM((nsub, LANES, n), BF16),
             pltpu.VMEM((nsub, 1, n), F32)]
            + [pltpu.VMEM((nsub, tk, n), BF16)] * ATTN_NBUF + [pltpu.VMEM((nsub, 1, n), F32)] * ATTN_NBUF),
        compiler_params=pltpu.CompilerParams(
            dimension_semantics=("arbitrary", "arbitrary", "arbitrary"), vmem_limit_bytes=48 * 1024 * 1024),
        name="attn_" + mode + ("_ctx" if ctx_only else ""),
    )(*args)


def _attention(mode, l, qr, qf, k, v, S, C, with_ctx, lam=None, g=None):
    y = _attention_call(mode, l, qr, qf, k, v, S, C, lam, g, None)
    return _attention_call(mode, l, qr, qf, k, v, S, C, lam, g, y) if with_ctx else y


def _ret_kernel(q_ref, k_ref, v_ref, g_ref, dec_ref, o_ref, sb_sc, *, S, C):
    CH = RET_CHUNK
    nl, nc = S // CH, C // CH

    def log_sigmoid(x):
        return jnp.minimum(x, 0.0) - jnp.log1p(jnp.exp(-jnp.abs(x)))

    lgf = log_sigmoid(dec_ref[0, 0][0:1, :])
    lgb = log_sigmoid(dec_ref[1, 0][0:1, :])
    ii = lax.broadcasted_iota(jnp.int32, (CH, CH), 1).astype(F32)
    jj = lax.broadcasted_iota(jnp.int32, (CH, CH), 0).astype(F32)
    dd = ii - jj
    fwd = dd >= 0
    mt = (jnp.where(fwd, jnp.exp(lgf * jnp.where(fwd, dd, 0.0)), 0.0)
          + jnp.where(fwd, 0.0, jnp.exp(lgb * jnp.where(fwd, 0.0, -dd))))
    xi_f = jnp.exp(lgf * (ii + 1.0))
    xi_b = jnp.exp(lgb * (CH - ii))
    zeta_f = jnp.exp(lgf * (CH - 1.0 - jj))
    zeta_b = jnp.exp(lgb * jj)
    cdf = jnp.exp(lgf * CH)
    cdb = jnp.exp(lgb * CH)
    zq = jnp.zeros((RET_DIM, CH), BF16)

    def sl(c):
        return pl.ds(pl.multiple_of(c * CH, CH), CH)

    def state_update(c, zeta):
        kz = (k_ref[0, sl(c), :].astype(F32) * zeta).astype(BF16)
        return jnp.dot(v_ref[0, 0, :, sl(c)], kz, preferred_element_type=F32)

    def bwd_step(c, sb):
        sb_sc[c] = sb
        return sb * cdb + state_update(c, zeta_b)

    sb = jnp.zeros((RET_DIM, CH), F32)
    for c in range(nl + nc - 1, nl - 1, -1):
        sb = bwd_step(c, sb)
    lax.fori_loop(0, nl, lambda n, s: bwd_step(nl - 1 - n, s), sb, unroll=RET_UNROLL)

    def fwd_step(c, sf):
        qc = q_ref[0, 0, :, sl(c)]
        qp = jnp.concatenate([qc, zq], axis=0)
        kc = k_ref[0, sl(c), :]
        vc = v_ref[0, 0, :, sl(c)]
        at = jnp.dot(kc, qp, preferred_element_type=F32)
        qpf = qp.astype(F32)
        rhs = jnp.concatenate([(at * mt).astype(BF16), (qpf * xi_f).astype(BF16), (qpf * xi_b).astype(BF16)], axis=0)
        lhs = jnp.concatenate([vc, sf.astype(BF16), sb_sc[c].astype(BF16)], axis=1)
        o = jnp.dot(lhs, rhs, preferred_element_type=F32)
        gch = g_ref[0, :, sl(c)]
        o_ref[0, :, sl(c)] = (gch * _sigmoid(gch) * _rms_rows(o)).astype(BF16)
        return sf * cdf + state_update(c, zeta_f)

    sf = jnp.zeros((RET_DIM, CH), F32)
    for c in range(nl, nl + nc):
        sf = fwd_step(c, sf)
    lax.fori_loop(0, nl, fwd_step, sf, unroll=RET_UNROLL)


def _retention(rq, rk, rv, rg, dec, S, C):
    B, H, d, Sp = rq.shape
    kern = functools.partial(_ret_kernel, S=S, C=C)
    return pl.pallas_call(
        kern,
        grid=(B, H),
        in_specs=[
            pl.BlockSpec((1, 1, d, Sp), lambda b, h: (b, h, 0, 0)),
            pl.BlockSpec((1, Sp, LANES), lambda b, h: (b, 0, h)),
            pl.BlockSpec((1, 1, d, Sp), lambda b, h: (b, h, 0, 0)),
            pl.BlockSpec((1, d, Sp), lambda b, h: (b, h, 0)),
            pl.BlockSpec((2, 1, 8, LANES), lambda b, h: (0, h, 0, 0)),
        ],
        out_specs=pl.BlockSpec((1, d, Sp), lambda b, h: (b, h, 0)),
        out_shape=jax.ShapeDtypeStruct((B, H * d, Sp), BF16),
        scratch_shapes=[pltpu.VMEM((Sp // RET_CHUNK, RET_DIM, RET_CHUNK), F32)],
        compiler_params=pltpu.CompilerParams(
            dimension_semantics=("arbitrary", "arbitrary"), vmem_limit_bytes=56 * 1024 * 1024),
        name="retention",
    )(rq, rk, rv, rg, dec)


def _layer_norm(t, g, b):
    mu = jnp.mean(t, axis=-1, keepdims=True)
    tc = t - mu
    var = jnp.mean(tc * tc, axis=-1, keepdims=True)
    return tc * lax.rsqrt(var + LN_EPS) * g + b


def _out_kernel(yda_ref, ymla_ref, yret_ref, w_ref, x_ref, gate_ref, g1_ref, b1_ref, sh2_ref, sc2_ref, rw_ref,
                x1_ref, h2_ref, lg_ref, *, alpha):
    proj = (lax.dot_general(yda_ref[0], w_ref[0, 0:DA_VW, :], TN_DIMS, preferred_element_type=F32)
            + lax.dot_general(ymla_ref[0], w_ref[0, DA_VW:DA_VW + MLA_VW, :], TN_DIMS, preferred_element_type=F32)
            + lax.dot_general(yret_ref[0], w_ref[0, DA_VW + MLA_VW:, :], TN_DIMS, preferred_element_type=F32))
    x1 = _layer_norm(alpha * x_ref[0] + gate_ref[0, 0, 0] * proj, g1_ref[0], b1_ref[0])
    x1_ref[0] = x1
    h2 = x1 * (1.0 + sc2_ref[0, 0, 0]) + sh2_ref[0, 0, 0]
    h2_ref[0] = h2.astype(BF16)
    h2_hi = h2.astype(BF16)
    h2_lo = (h2 - h2_hi.astype(F32)).astype(BF16)
    hi = jnp.dot(h2_hi, rw_ref[...], preferred_element_type=F32)
    lo = jnp.dot(h2_lo, rw_ref[:, 0:LANES], preferred_element_type=F32)
    lg_ref[0] = hi[:, 0:LANES] + (hi[:, LANES:] + lo)


def _out_proj(l, yda, ymla, yret, w_out, xc, modr, ln_g, ln_b, rw_pad, S, alpha, latents_only):
    B, Sp, D = xc.shape
    n_lat_tiles = S // TM
    n_rows = S if latents_only else Sp

    def mod_spec(j):
        return pl.BlockSpec((1, 1, 1, 1, D), lambda b, t: (l, jnp.where(t >= n_lat_tiles, B, b), j, 0, 0))

    def fm(a):
        return pl.BlockSpec((1, a.shape[1], TM), lambda b, t: (b, 0, t))

    def wspec(a):
        return pl.BlockSpec((1,) + a.shape[1:], lambda b, t: (l,) + (0,) * (a.ndim - 1))

    row = pl.BlockSpec((1, TM, D), lambda b, t: (b, t, 0))
    sds = jax.ShapeDtypeStruct
    return pl.pallas_call(
        functools.partial(_out_kernel, alpha=alpha),
        grid=(B, n_rows // TM),
        in_specs=[fm(yda), fm(ymla), fm(yret), wspec(w_out), row, mod_spec(2), wspec(ln_g), wspec(ln_b),
                  mod_spec(3), mod_spec(4), pl.BlockSpec(rw_pad.shape, lambda b, t: (0, 0))],
        out_specs=[row, row, pl.BlockSpec((1, TM, LANES), lambda b, t: (b, t, 0))],
        out_shape=[sds((B, n_rows, D), F32), sds((B, n_rows, D), BF16), sds((B, n_rows, LANES), F32)],
        compiler_params=pltpu.CompilerParams(dimension_semantics=("arbitrary", "arbitrary")),
        name="out_proj",
    )(yda, ymla, yret, w_out, xc, modr, ln_g, ln_b, modr, modr, rw_pad)


def _gates_T(logits_T, rb):
    s = _sigmoid(logits_T)
    ch = s + rb
    srow = [s[e:e + 1] for e in range(N_EXPERTS)]
    crow = [ch[e:e + 1] for e in range(N_EXPERTS)]
    per = N_EXPERTS // N_GROUPS
    gs = []
    for g in range(N_GROUPS):
        a, b, c, d = crow[per * g: per * g + per]
        m1, n1, m2, n2 = jnp.maximum(a, b), jnp.minimum(a, b), jnp.maximum(c, d), jnp.minimum(c, d)
        gs.append(jnp.maximum(m1, m2) + jnp.maximum(jnp.minimum(m1, m2), jnp.maximum(n1, n2)))
    gmax = functools.reduce(jnp.maximum, gs)
    taken = jnp.zeros(gmax.shape, jnp.bool_)
    gsel = []
    for g in range(N_GROUPS):
        sg = jnp.logical_and(gs[g] == gmax, jnp.logical_not(taken))
        gsel.append(sg)
        taken = jnp.logical_or(taken, sg)
    neg = jnp.full(gmax.shape, -jnp.inf, F32)
    mc = [jnp.where(gsel[e // per], crow[e], neg) for e in range(N_EXPERTS)]
    sel = [jnp.zeros(gmax.shape, jnp.bool_) for _ in range(N_EXPERTS)]
    for _ in range(2):
        top = functools.reduce(jnp.maximum, mc)
        taken = jnp.zeros(gmax.shape, jnp.bool_)
        for e in range(N_EXPERTS):
            hit = jnp.logical_and(mc[e] == top, jnp.logical_not(taken))
            taken = jnp.logical_or(taken, hit)
            sel[e] = jnp.logical_or(sel[e], hit)
            mc[e] = jnp.where(hit, neg, mc[e])
    w = [jnp.where(sel[e], srow[e], 0.0) for e in range(N_EXPERTS)]
    wsum = functools.reduce(lambda a, b: a + b, w)
    gates = jnp.concatenate([we / wsum * ROUTED_SCALE for we in w], axis=0)
    return gates, [jnp.where(sg, 1.0, 0.0) for sg in gsel]


def _moe_sorted_rows(tm):
    return tm + N_GROUPS * MOE_ALIGN + MOE_BLOCK


def _moe_kernel(h_ref, lg_ref, rb_ref, w13_ref, w2_ref, o_ref,
                off_sc, nblk_sc, perm_sc, hs_sc, gs_sc, accs_sc, earlier_sc, *, nsub):
    e = pl.program_id(2)
    tm = h_ref.shape[1] // nsub
    n_sorted = perm_sc.shape[1]
    per = N_EXPERTS // N_GROUPS
    lane = lax.broadcasted_iota(jnp.int32, (1, LANES), 1)

    def expert(x):
        ab = jnp.dot(x, w13_ref[0, 0], preferred_element_type=F32)
        a = ab[:, 0:D_EXPERT]
        hid = (a * _sigmoid(a) * ab[:, D_EXPERT:]).astype(BF16)
        return jnp.dot(hid, w2_ref[0, 0], preferred_element_type=F32)

    def gate_col(g):
        return jnp.sum(jnp.where(lane == e, g, 0.0), axis=1, keepdims=True)

    def sub(j):
        return slice(j * tm, (j + 1) * tm)

    @pl.when(jnp.logical_and(jnp.logical_and(pl.program_id(0) == 0, pl.program_id(1) == 0), e == 0))
    def _():
        before = lax.broadcasted_iota(jnp.int32, (tm, tm), 0) < lax.broadcasted_iota(jnp.int32, (tm, tm), 1)
        earlier_sc[...] = jnp.where(before, 1.0, 0.0).astype(BF16)

    @pl.when(e == 0)
    def _():
        for j in range(nsub):
            gT, gsel = _gates_T(lg_ref[0, sub(j), :].T[0:N_EXPERTS, :], rb_ref[...])
            gates = jnp.concatenate([gT, jnp.zeros((LANES - N_EXPERTS, tm), F32)], axis=0).T
            member = jnp.concatenate(gsel + [jnp.zeros((16 - N_GROUPS, tm), F32)], axis=0)
            rank = jnp.dot(member.astype(BF16), earlier_sc[...], preferred_element_type=F32)
            count = jnp.sum(member, axis=1, keepdims=True)
            off = jnp.int32(0)
            slot = jnp.zeros((1, tm), F32)
            for g in range(N_GROUPS):
                n_g = jnp.max(count[g:g + 1]).astype(jnp.int32)
                off_sc[j * N_GROUPS + g] = off
                nblk_sc[j * N_GROUPS + g] = (n_g + (MOE_BLOCK - 1)) // MOE_BLOCK
                slot = slot + gsel[g] * (rank[g:g + 1] + off.astype(F32))
                off = off + ((n_g + (MOE_ALIGN - 1)) // MOE_ALIGN) * MOE_ALIGN
            rows_i = lax.broadcasted_iota(jnp.int32, (n_sorted, tm), 0)
            perm = jnp.where(rows_i == slot.astype(jnp.int32), 1.0, 0.0).astype(BF16)
            perm_sc[j] = perm
            hs_sc[j] = jnp.dot(perm, h_ref[0, sub(j), :], preferred_element_type=F32).astype(BF16)
            g_hi = gates.astype(BF16)
            g_lo = (gates - g_hi.astype(F32)).astype(BF16)
            gs_sc[j] = (jnp.dot(perm, g_hi, preferred_element_type=F32)
                        + jnp.dot(perm, g_lo, preferred_element_type=F32))
            accs_sc[j] = jnp.zeros(accs_sc.shape[1:], F32)

    for j in range(nsub):
        start = off_sc[j * N_GROUPS + e // per]

        def block(k, carry, j=j, start=start):
            rows = pl.ds(pl.multiple_of(start + k * MOE_BLOCK, MOE_ALIGN), MOE_BLOCK)
            accs_sc[j, rows, :] += gate_col(gs_sc[j, rows, :]) * expert(hs_sc[j, rows, :])
            return carry

        lax.fori_loop(0, nblk_sc[j * N_GROUPS + e // per], block, 0)

    @pl.when(e == N_EXPERTS - 1)
    def _():
        for j in range(nsub):
            o_ref[0, sub(j), :] = lax.dot_general(perm_sc[j], accs_sc[j].astype(BF16), TN_DIMS,
                                                  preferred_element_type=F32).astype(BF16)


def _moe_ln_kernel(x1_ref, f_ref, gb_ref, gc_ref, g2_ref, b2_ref, o_ref, *, S, alpha):
    tm = x1_ref.shape[1]
    n_lat = S - pl.program_id(1) * tm
    rowi = lax.broadcasted_iota(jnp.int32, (tm, 1), 0)
    gate = jnp.where(rowi < n_lat, gb_ref[0, 0, 0], gc_ref[0, 0, 0])
    o_ref[0] = _layer_norm(alpha * x1_ref[0] + gate * f_ref[0].astype(F32), g2_ref[0], b2_ref[0])


def _moe(l, h2, logits, rb, w13, w2, x1, modr, ln_g, ln_b, S, alpha, latents_only):
    B, n_rows, D = x1.shape
    sub = TM_MOE_LAST if latents_only else TM_MOE
    tm = MOE_SUBTILES * sub
    n_sorted = _moe_sorted_rows(sub)

    def wspec(a):
        return pl.BlockSpec((1,) + a.shape[1:], lambda b, t: (l,) + (0,) * (a.ndim - 1))

    row = lambda b, t, e: (b, t, 0)
    f = pl.pallas_call(
        functools.partial(_moe_kernel, nsub=MOE_SUBTILES),
        grid=(B, n_rows // tm, N_EXPERTS),
        in_specs=[
            pl.BlockSpec((1, tm, D), row),
            pl.BlockSpec((1, tm, LANES), row),
            pl.BlockSpec(rb.shape, lambda b, t, e: (0, 0)),
            pl.BlockSpec((1, 1, D, 2 * D_EXPERT), lambda b, t, e: (l, e, 0, 0)),
            pl.BlockSpec((1, 1, D_EXPERT, D), lambda b, t, e: (l, e, 0, 0)),
        ],
        out_specs=pl.BlockSpec((1, tm, D), row),
        out_shape=jax.ShapeDtypeStruct((B, n_rows, D), BF16),
        scratch_shapes=[
            pltpu.SMEM((MOE_SUBTILES * N_GROUPS,), jnp.int32), pltpu.SMEM((MOE_SUBTILES * N_GROUPS,), jnp.int32),
            pltpu.VMEM((MOE_SUBTILES, n_sorted, sub), BF16),
            pltpu.VMEM((MOE_SUBTILES, n_sorted, D), BF16),
            pltpu.VMEM((MOE_SUBTILES, n_sorted, LANES), F32),
            pltpu.VMEM((MOE_SUBTILES, n_sorted, D), F32),
            pltpu.VMEM((sub, sub), BF16)],
        compiler_params=pltpu.CompilerParams(
            dimension_semantics=("arbitrary", "arbitrary", "arbitrary"), vmem_limit_bytes=56 * 1024 * 1024),
        name="moe",
    )(h2, logits, rb, w13, w2)
    rows = pl.BlockSpec((1, tm, D), lambda b, t: (b, t, 0))
    return pl.pallas_call(
        functools.partial(_moe_ln_kernel, S=S, alpha=alpha),
        grid=(B, n_rows // tm),
        in_specs=[rows, rows,
                  pl.BlockSpec((1, 1, 1, 1, D), lambda b, t: (l, b, 5, 0, 0)),
                  pl.BlockSpec((1, 1, 1, 1, D), lambda b, t: (l, B, 5, 0, 0)),
                  wspec(ln_g), wspec(ln_b)],
        out_specs=rows,
        out_shape=jax.ShapeDtypeStruct((B, n_rows, D), F32),
        compiler_params=pltpu.CompilerParams(
            dimension_semantics=("arbitrary", "arbitrary"), vmem_limit_bytes=56 * 1024 * 1024),
        name="moe_ln",
    )(x1, f, modr, modr, ln_g, ln_b)


def _rope_tables(S, C, rot_dim):
    rows = S // GRID_W
    r = jnp.repeat(jnp.arange(rows, dtype=F32), GRID_W)
    col = jnp.tile(jnp.arange(GRID_W, dtype=F32), rows)
    n_freq = rot_dim // 4
    freqs = ROPE_BASE ** (-jnp.arange(n_freq, dtype=F32) / n_freq)
    ang = jnp.concatenate([r[:, None] * freqs, col[:, None] * freqs], -1)
    cos, sin = jnp.cos(ang), jnp.sin(ang)
    cos_rep = jnp.repeat(cos, 2, axis=-1)
    sin_alt = jnp.stack([-sin, sin], -1).reshape(S, rot_dim)
    cos_rep = jnp.concatenate([cos_rep, jnp.ones((C, rot_dim), F32)], 0)
    sin_alt = jnp.concatenate([sin_alt, jnp.zeros((C, rot_dim), F32)], 0)
    return cos_rep.T, sin_alt.T


def kernel(x, c, ctx, c_ctx, w_ada, b_ada, w_in, da_lambda, da_subln, mla_q_norm, mla_w_uq, mla_kv_norm, mla_w_ukv, ret_decay_f, ret_decay_b, w_out, ln1_g, ln1_b, router_w, router_b, exp_w1, exp_w3, exp_w2, ln2_g, ln2_b):
    B, S, D = x.shape
    C = ctx.shape[1]
    depth = w_in.shape[0]
    alpha = float((2 * depth) ** 0.25)

    cvec = jnp.concatenate([c, c_ctx[None, :], jnp.zeros((8 - B - 1, D), F32)], 0)
    mod = _modulation(cvec, w_ada, b_ada)
    modr = mod.reshape(depth, 8, N_MOD, 1, D)

    wT_all = jnp.swapaxes(w_in, 1, 2).astype(BF16)
    wuqT = jnp.swapaxes(mla_w_uq, 1, 2).astype(BF16)
    ukv = mla_w_ukv.reshape(depth, MLA_KV_RANK, MLA_HEADS, MLA_NOPE + MLA_VDIM)
    wvT = jnp.swapaxes(ukv[..., MLA_NOPE:].reshape(depth, MLA_KV_RANK, MLA_VW), 1, 2).astype(BF16)
    nopeT = jnp.transpose(ukv[..., :MLA_NOPE], (0, 2, 3, 1))
    top = jnp.concatenate([nopeT, jnp.zeros((depth, MLA_HEADS, MLA_NOPE, MLA_ROPE), F32)], -1)
    mid = jnp.concatenate([jnp.zeros((MLA_ROPE, MLA_KV_RANK), F32), jnp.eye(MLA_ROPE, dtype=F32)], -1)
    mid = jnp.broadcast_to(mid, (depth, MLA_HEADS, MLA_ROPE, MLA_KV_RANK + MLA_ROPE))
    bot = jnp.zeros((depth, MLA_HEADS, LANES - MLA_NOPE - MLA_ROPE, MLA_KV_RANK + MLA_ROPE), F32)
    wkT = jnp.concatenate([top, mid, bot], 2).reshape(depth, MLA_HEADS * LANES, MLA_KV_RANK + MLA_ROPE).astype(BF16)
    gq = mla_q_norm[:, :, None]
    gkv = mla_kv_norm[:, :, None]
    subln = da_subln[:, :, None]
    w_out_b = w_out.astype(BF16)
    w13 = jnp.concatenate([exp_w1, exp_w3], -1).astype(BF16)
    w2 = exp_w2.astype(BF16)
    rw_f = jnp.concatenate([router_w, jnp.zeros((D, LANES - N_EXPERTS), F32)], -1)
    rw_hi = rw_f.astype(BF16)
    rw_pad = jnp.concatenate([rw_hi, (rw_f - rw_hi.astype(F32)).astype(BF16)], -1)
    rb = router_b[:, None]
    dec = jnp.broadcast_to(jnp.stack([ret_decay_f, ret_decay_b], 1)[:, :, :, None, None],
                           (depth, 2, RET_HEADS, 8, LANES))
    tabs = _rope_tables(S, C, DA_DIM) + _rope_tables(S, C, RET_DIM)
    ln1g, ln1b, ln2g, ln2b = (a[:, None, :] for a in (ln1_g, ln1_b, ln2_g, ln2_b))

    xc = jnp.concatenate([x, ctx], 1)
    for l in range(depth):
        last = l == depth - 1
        (daqr, daqf, dak, dav, mqr, mqf, mk, mv, rq, rk, rv, rg) = _in_proj(
            l, xc, modr, wT_all, tabs, gq, wuqT, gkv, wvT, wkT, S)
        yda = _attention("da", l, daqr, daqf, dak, dav, S, C, not last, lam=da_lambda, g=subln)
        ymla = _attention("mla", l, mqr, mqf, mk, mv, S, C, not last)
        yret = _retention(rq, rk, rv, rg, dec[l], S, C)
        x1, h2, logits = _out_proj(l, yda, ymla, yret, w_out_b, xc, modr, ln1g, ln1b, rw_pad, S, alpha, last)
        xc = _moe(l, h2, logits, rb, w13, w2, x1, modr, ln2g, ln2b, S, alpha, last)
    return xc
```

```python
import functools
import math

import jax
import jax.numpy as jnp
import numpy as np
from jax import lax
from jax.experimental import pallas as pl
from jax.experimental.pallas import tpu as pltpu

F32 = jnp.float32
BF16 = jnp.bfloat16

GRID_W = 64
DA_HEADS, DA_DIM, DA_VDIM = 6, 32, 64
MLA_HEADS, MLA_NOPE, MLA_ROPE, MLA_VDIM = 6, 64, 32, 64
MLA_Q_RANK, MLA_KV_RANK = 256, 128
RET_HEADS, RET_DIM, RET_CHUNK = 4, 64, 128
ROPE_BASE = 10000.0
N_EXPERTS, N_GROUPS, D_EXPERT = 16, 4, 512
ROUTED_SCALE = 1.0
N_MOD = 6
LN_EPS = 1e-5
RMS_EPS = 1e-6
LOG2E = math.log2(math.e)

DA_W = DA_HEADS * 2 * DA_DIM
DA_VW = DA_HEADS * DA_VDIM
RET_W = RET_HEADS * RET_DIM
MLA_QW = MLA_HEADS * (MLA_NOPE + MLA_ROPE)
MLA_VW = MLA_HEADS * MLA_VDIM
MLA_QD = MLA_NOPE + MLA_ROPE
DA_MAPS = 2 * DA_HEADS
V_DIM = DA_VDIM
assert MLA_VDIM == V_DIM
IN_SPLIT = (DA_W, DA_W, DA_VW, MLA_Q_RANK, MLA_KV_RANK, MLA_ROPE, RET_W, RET_W, RET_W, RET_W)

LANES = 128
BF16_ROWS = 16
VMEM_LIMIT = 56 * 1024 * 1024
V_ROWS = V_DIM + BF16_ROWS
TM = 256
ATTN_N = 512
TK = 2048
ATTN_SUBTILES = 4
ATTN_NBUF = 2
ATTN_MAX_JUMP = 32.0
ATTN_UNROLL = 2
TM_MOE = 640
TM_MOE_LAST = 512
MOE_SUBTILES = 2
MOE_BLOCK = 256
MOE_ALIGN = 32
RET_UNROLL = 8

NT_DIMS = (((1,), (1,)), ((), ()))
TN_DIMS = (((0,), (0,)), ((), ()))


def _sigmoid(x):
    return 1.0 / (1.0 + jnp.exp(-x))


def _mod_kernel(c_ref, w_ref, b_ref, o_ref):
    c = c_ref[...]
    sc = (c * _sigmoid(c)).astype(BF16)
    o_ref[0] = jnp.dot(sc, w_ref[0].astype(BF16), preferred_element_type=F32) + b_ref[0]


def _modulation(cvec, w_ada, b_ada):
    depth, d, n = w_ada.shape
    tn = n // 4
    return pl.pallas_call(
        _mod_kernel,
        grid=(depth, n // tn),
        in_specs=[
            pl.BlockSpec((8, d), lambda l, j: (0, 0)),
            pl.BlockSpec((1, d, tn), lambda l, j: (l, 0, j)),
            pl.BlockSpec((1, 1, tn), lambda l, j: (l, 0, j)),
        ],
        out_specs=pl.BlockSpec((1, 8, tn), lambda l, j: (l, 0, j)),
        out_shape=jax.ShapeDtypeStruct((depth, 8, n), F32),
        name="adaln_mod",
    )(cvec, w_ada, b_ada.reshape(depth, 1, n))


_O_DAQ, _O_DAK, _O_DAV, _O_CQ, _O_CKV, _O_KPE, _O_RQ, _O_RK, _O_RV, _O_RG = (
    int(v) for v in np.cumsum((0,) + IN_SPLIT)[:-1])
N_ZROWS = sum(IN_SPLIT)


def _rms_rows(x):
    return x * lax.rsqrt(jnp.mean(x * x, axis=0, keepdims=True) + RMS_EPS)


def _pairswap_rows(x):
    n = x.shape[0]
    even = (lax.broadcasted_iota(jnp.int32, x.shape, 0) & 1) == 0
    return jnp.where(even, pltpu.roll(x, n - 1, 0), pltpu.roll(x, 1, 0))


def _in_kernel(x_ref, sh_ref, sc_ref, wT_ref, cosA_ref, sinA_ref, cosR_ref, sinR_ref,
               gq_ref, wuqT_ref, gkv_ref, wvT_ref, wkT_ref,
               daqr, daqf, dak, dav, mqr, mqf, mk, mv, rq, rk, rv, rg, z_ref, *, cq_da, cq_mla, kscale):
    tm = x_ref.shape[1]
    x = x_ref[0]
    h = (x * (1.0 + sc_ref[0, 0, 0]) + sh_ref[0, 0, 0]).astype(BF16)
    z_ref[...] = lax.dot_general(wT_ref[0], h, NT_DIMS, preferred_element_type=F32)
    cosA = cosA_ref[...]
    sinA = sinA_ref[...]
    cosR = cosR_ref[...]
    sinR = sinR_ref[...]
    ones_rows = (lax.broadcasted_iota(jnp.int32, (DA_HEADS, V_ROWS - V_DIM, tm), 1) == 0).astype(BF16)

    def rows(off, n):
        return z_ref[off:off + n, :]

    q2 = rows(_O_DAQ, DA_W)
    q = q2.reshape(DA_MAPS, DA_DIM, tm)
    qs = _pairswap_rows(q2).reshape(DA_MAPS, DA_DIM, tm)
    daqr[0] = ((q * cosA + qs * sinA) * cq_da).astype(BF16)
    daqf[0] = (q * cq_da).astype(BF16)
    k2 = rows(_O_DAK, DA_W)
    k = k2.reshape(DA_MAPS, DA_DIM, tm)
    ks = _pairswap_rows(k2).reshape(DA_MAPS, DA_DIM, tm)
    kr = (k * cosA + ks * sinA).reshape(DA_HEADS, 2 * DA_DIM, tm)
    kp = jnp.concatenate([kr, jnp.zeros_like(kr)], axis=1).reshape(DA_HEADS * LANES, tm)
    dak[0] = kp.T.astype(BF16)
    dav[0, :, 0:V_DIM, :] = rows(_O_DAV, DA_VW).reshape(DA_HEADS, DA_VDIM, tm).astype(BF16)
    dav[0, :, V_DIM:V_ROWS, :] = ones_rows

    cqn = (_rms_rows(rows(_O_CQ, MLA_Q_RANK)) * gq_ref[0]).astype(BF16)
    qm = jnp.dot(wuqT_ref[0], cqn, preferred_element_type=F32)
    qh = qm.reshape(MLA_HEADS, MLA_NOPE + MLA_ROPE, tm)
    qsw = _pairswap_rows(qm).reshape(MLA_HEADS, MLA_NOPE + MLA_ROPE, tm)[:, MLA_NOPE:, :]
    rot = qh[:, MLA_NOPE:, :] * cosA + qsw * sinA
    mqr[0, :, 0:MLA_NOPE, :] = (qh[:, 0:MLA_NOPE, :] * cq_mla).astype(BF16)
    mqr[0, :, MLA_NOPE:, :] = (rot * cq_mla).astype(BF16)
    mqf[0] = (qh * cq_mla).astype(BF16)
    ckvn = (_rms_rows(rows(_O_CKV, MLA_KV_RANK)) * gkv_ref[0]).astype(BF16)
    vm = jnp.dot(wvT_ref[0], ckvn, preferred_element_type=F32)
    mv[0, :, 0:V_DIM, :] = vm.reshape(MLA_HEADS, MLA_VDIM, tm).astype(BF16)
    mv[0, :, V_DIM:V_ROWS, :] = ones_rows
    kpe = rows(_O_KPE, MLA_ROPE)
    kper = kpe * cosA + _pairswap_rows(kpe) * sinA
    kin = jnp.concatenate([ckvn, kper.astype(BF16)], axis=0)
    kmT = jnp.dot(wkT_ref[0], kin, preferred_element_type=F32)
    mk[0] = kmT.T.astype(BF16)

    r_q2 = rows(_O_RQ, RET_W)
    r_q = r_q2.reshape(RET_HEADS, RET_DIM, tm)
    r_qs = _pairswap_rows(r_q2).reshape(RET_HEADS, RET_DIM, tm)
    rq[0] = (r_q * cosR + r_qs * sinR).astype(BF16)
    r_k2 = rows(_O_RK, RET_W)
    r_k = r_k2.reshape(RET_HEADS, RET_DIM, tm)
    r_ks = _pairswap_rows(r_k2).reshape(RET_HEADS, RET_DIM, tm)
    rkr = (r_k * cosR + r_ks * sinR) * kscale
    rkp = jnp.concatenate([rkr, jnp.zeros_like(rkr)], axis=1).reshape(RET_HEADS * LANES, tm)
    rk[0] = rkp.T.astype(BF16)
    rv[0] = rows(_O_RV, RET_W).reshape(RET_HEADS, RET_DIM, tm).astype(BF16)
    rg[0] = rows(_O_RG, RET_W)


def _in_proj(l, xc, modr, wT_all, tabs, gq, wuqT, gkv, wvT, wkT, S):
    B, Sp, D = xc.shape
    nt = Sp // TM
    n_lat_tiles = S // TM
    cosA, sinA, cosR, sinR = tabs

    def mod_spec(j):
        return pl.BlockSpec((1, 1, 1, 1, D), lambda b, t: (l, jnp.where(t >= n_lat_tiles, B, b), j, 0, 0))

    def wspec(a):
        return pl.BlockSpec((1,) + a.shape[1:], lambda b, t: (l,) + (0,) * (a.ndim - 1))

    def fm4(h, d):
        return pl.BlockSpec((1, h, d, TM), lambda b, t: (b, 0, 0, t))

    def tmaj(w):
        return pl.BlockSpec((1, TM, w), lambda b, t: (b, t, 0))

    kern = functools.partial(
        _in_kernel,
        cq_da=float(DA_DIM ** -0.5 * LOG2E),
        cq_mla=float((MLA_NOPE + MLA_ROPE) ** -0.5 * LOG2E),
        kscale=float(RET_DIM ** -0.5),
    )
    sds = jax.ShapeDtypeStruct
    return pl.pallas_call(
        kern,
        grid=(B, nt),
        in_specs=[
            pl.BlockSpec((1, TM, D), lambda b, t: (b, t, 0)),
            mod_spec(0), mod_spec(1),
            wspec(wT_all),
            pl.BlockSpec((DA_DIM, TM), lambda b, t: (0, t)),
            pl.BlockSpec((DA_DIM, TM), lambda b, t: (0, t)),
            pl.BlockSpec((RET_DIM, TM), lambda b, t: (0, t)),
            pl.BlockSpec((RET_DIM, TM), lambda b, t: (0, t)),
            wspec(gq), wspec(wuqT), wspec(gkv), wspec(wvT), wspec(wkT),
        ],
        out_specs=[
            fm4(DA_MAPS, DA_DIM), fm4(DA_MAPS, DA_DIM), tmaj(DA_HEADS * LANES), fm4(DA_HEADS, V_ROWS),
            fm4(MLA_HEADS, MLA_QD), fm4(MLA_HEADS, MLA_QD), tmaj(MLA_HEADS * LANES), fm4(MLA_HEADS, V_ROWS),
            fm4(RET_HEADS, RET_DIM), tmaj(RET_HEADS * LANES), fm4(RET_HEADS, RET_DIM),
            pl.BlockSpec((1, RET_W, TM), lambda b, t: (b, 0, t)),
        ],
        out_shape=[
            sds((B, DA_MAPS, DA_DIM, Sp), BF16), sds((B, DA_MAPS, DA_DIM, Sp), BF16),
            sds((B, Sp, DA_HEADS * LANES), BF16), sds((B, DA_HEADS, V_ROWS, Sp), BF16),
            sds((B, MLA_HEADS, MLA_QD, Sp), BF16), sds((B, MLA_HEADS, MLA_QD, Sp), BF16),
            sds((B, Sp, MLA_HEADS * LANES), BF16), sds((B, MLA_HEADS, V_ROWS, Sp), BF16),
            sds((B, RET_HEADS, RET_DIM, Sp), BF16), sds((B, Sp, RET_HEADS * LANES), BF16),
            sds((B, RET_HEADS, RET_DIM, Sp), BF16), sds((B, RET_W, Sp), F32),
        ],
        scratch_shapes=[pltpu.VMEM((N_ZROWS, TM), F32)],
        compiler_params=pltpu.CompilerParams(
            dimension_semantics=("arbitrary", "arbitrary"), vmem_limit_bytes=VMEM_LIMIT),
        name="in_proj",
    )(xc, modr, modr, wT_all, cosA, sinA, cosR, sinR, gq, wuqT, gkv, wvT, wkT)


def _attn_kernel(*refs, mode, S, C, lam_init, ctx_only, nsub):
    refs = list(refs)
    qr_ref, qf_ref, k_ref, v_ref = refs[:4]
    del refs[:4]
    if mode == "da":
        lam_ref, g_ref = refs[:2]
        del refs[:2]
    if ctx_only:
        del refs[:1]
    o_ref, m_sc, acc_sc, q_sc, jump_sc = refs[:5]
    nb = ATTN_NBUF
    pbuf, albuf = (refs[5 + i * nb: 5 + (i + 1) * nb] for i in range(2))
    tq = o_ref.shape[2] // nsub
    n_lat = S // TK

    def qpad(ref, t):
        cols = slice(t * tq, (t + 1) * tq)
        if mode == "da":
            z = jnp.zeros((DA_DIM, tq), BF16)
            c1 = jnp.concatenate([ref[0, 0, :, cols], z, z, z], axis=0)
            c2 = jnp.concatenate([z, ref[0, 1, :, cols], z, z], axis=0)
            return jnp.concatenate([c1, c2], axis=1)
        return jnp.concatenate([ref[0, 0, :, cols], jnp.zeros((LANES - MLA_QD, tq), BF16)], axis=0)

    def chunk(j):
        return pl.ds(j * TK if isinstance(j, int) else pl.multiple_of(j * TK, TK), TK)

    def context_keys(t):
        c0 = 0 if ctx_only else S
        s = jnp.dot(k_ref[0, c0:c0 + C, :], qpad(qf_ref, t), preferred_element_type=F32)
        m_c = jnp.max(s, axis=0, keepdims=True)
        m_sc[t] = m_c
        acc_sc[t] = jnp.dot(v_ref[0, 0, :, c0:c0 + C], jnp.exp2(s - m_c).astype(BF16), preferred_element_type=F32)

    for t in range(nsub):
        context_keys(t)

    if not ctx_only:
        for t in range(nsub):
            q_sc[t] = qpad(qr_ref, t)
        jump_sc[...] = jnp.full(jump_sc.shape, -jnp.inf, F32)

        def probs(t, j, r):
            s = jnp.dot(k_ref[0, chunk(j), :], q_sc[t], preferred_element_type=F32)
            m_prev = m_sc[t]
            pbuf[r][t] = jnp.exp2(s - m_prev).astype(BF16)
            m_chunk = jnp.max(s, axis=0, keepdims=True)
            m_new = jnp.maximum(m_prev, m_chunk)
            jump_sc[t] = jnp.maximum(jump_sc[t], m_chunk - m_prev)
            albuf[r][t] = jnp.exp2(m_prev - m_new)
            m_sc[t] = m_new

        def accumulate(t, j, r):
            pv = jnp.dot(v_ref[0, 0, :, chunk(j)], pbuf[r][t], preferred_element_type=F32)
            acc_sc[t] = (acc_sc[t] + pv) * albuf[r][t]

        def step(j, r):
            for t in range(nsub):
                accumulate(t, j, r)
                probs(t, j + 1, (r + 1) % nb)

        for t in range(nsub):
            probs(t, 0, 0)
        n_steps = n_lat - 1
        trips = n_steps // ATTN_UNROLL

        def body(i, carry):
            for u in range(ATTN_UNROLL):
                step(i * ATTN_UNROLL + u, u % nb)
            return carry

        lax.fori_loop(0, trips, body, 0)
        for j in range(trips * ATTN_UNROLL, n_steps):
            step(j, j % nb)
        for t in range(nsub):
            accumulate(t, n_lat - 1, (n_lat - 1) % nb)

        @pl.when(jnp.max(jump_sc[...]) > ATTN_MAX_JUMP)
        def _():
            for t in range(nsub):
                context_keys(t)

                def exact_step(j, carry, t=t):
                    s = jnp.dot(k_ref[0, chunk(j), :], q_sc[t], preferred_element_type=F32)
                    m_old = m_sc[t]
                    m_new = jnp.maximum(m_old, jnp.max(s, axis=0, keepdims=True))
                    pv = jnp.dot(v_ref[0, 0, :, chunk(j)], jnp.exp2(s - m_new).astype(BF16),
                                 preferred_element_type=F32)
                    acc_sc[t] = acc_sc[t] * jnp.exp2(m_old - m_new) + pv
                    m_sc[t] = m_new
                    return carry

                lax.fori_loop(0, n_lat, exact_step, 0)

    for t in range(nsub):
        acc = acc_sc[t]
        o = acc[0:V_DIM, :] / acc[V_DIM:V_DIM + 1, :]
        cols = slice(t * tq, (t + 1) * tq)
        if mode == "da":
            lf = lam_ref[0]
            lam = (jnp.exp(jnp.sum(lf[0:1] * lf[1:2], axis=1, keepdims=True))
                   - jnp.exp(jnp.sum(lf[2:3] * lf[3:4], axis=1, keepdims=True)) + lam_init)
            od = o[:, 0:tq] - lam * o[:, tq:2 * tq]
            y = _rms_rows(od) * g_ref[0]
            o_ref[0, :, cols] = (y * (1.0 - lam_init)).astype(BF16)
        else:
            o_ref[0, :, cols] = o.astype(BF16)


def _attention_call(mode, l, qr, qf, k, v, S, C, lam, g, y_latent):
    B, nmaps, d, Sp = qr.shape
    H = v.shape[1]
    nm = nmaps // H
    ctx_only = y_latent is not None
    nsub = 1 if ctx_only else ATTN_SUBTILES
    tq = C if ctx_only else ATTN_N // nm
    n = nm * tq
    tqb = nsub * tq
    q0 = S // tqb if ctx_only else 0
    lam_init = 0.8 - 0.6 * math.exp(-0.3 * l)
    kern = functools.partial(_attn_kernel, mode=mode, S=S, C=C, lam_init=lam_init, ctx_only=ctx_only, nsub=nsub)
    in_specs = [
        pl.BlockSpec((1, nm, d, tqb), lambda b, h, i: (b, h, 0, q0 + i)),
        pl.BlockSpec((1, nm, d, tqb), lambda b, h, i: (b, h, 0, q0 + i)),
        (pl.BlockSpec((1, C, LANES), lambda b, h, i: (b, S // C, h)) if ctx_only
         else pl.BlockSpec((1, Sp, LANES), lambda b, h, i: (b, 0, h))),
        (pl.BlockSpec((1, 1, V_ROWS, C), lambda b, h, i: (b, h, 0, S // C)) if ctx_only
         else pl.BlockSpec((1, 1, V_ROWS, Sp), lambda b, h, i: (b, h, 0, 0))),
    ]
    args = [qr, qf, k, v]
    if mode == "da":
        in_specs += [pl.BlockSpec((1,) + lam.shape[1:], lambda b, h, i: (l, 0, 0)),
                     pl.BlockSpec((1,) + g.shape[1:], lambda b, h, i: (l, 0, 0))]
        args += [lam, g]
    aliases = {}
    if ctx_only:
        in_specs.append(pl.BlockSpec(memory_space=pl.ANY))
        args.append(y_latent)
        aliases = {len(args) - 1: 0}
    tk = C if ctx_only else TK
    return pl.pallas_call(
        kern,
        grid=(B, H, 1 if ctx_only else S // tqb),
        in_specs=in_specs,
        out_specs=pl.BlockSpec((1, V_DIM, tqb), lambda b, h, i: (b, h, q0 + i)),
        out_shape=jax.ShapeDtypeStruct((B, H * V_DIM, Sp), BF16),
        input_output_aliases=aliases,
        scratch_shapes=(
            [pltpu.VMEM((nsub, 1, n), F32), pltpu.VMEM((nsub, V_ROWS, n), F32), pltpu.VMEM((nsub, LANES, n), BF16),
             pltpu.VMEM((nsub, 1, n), F32)]
            + [pltpu.VMEM((nsub, tk, n), BF16)] * ATTN_NBUF + [pltpu.VMEM((nsub, 1, n), F32)] * ATTN_NBUF),
        compiler_params=pltpu.CompilerParams(
            dimension_semantics=("arbitrary", "arbitrary", "arbitrary"), vmem_limit_bytes=VMEM_LIMIT),
        name="attn_" + mode + ("_ctx" if ctx_only else ""),
    )(*args)


def _attention(mode, l, qr, qf, k, v, S, C, with_ctx, lam=None, g=None):
    y = _attention_call(mode, l, qr, qf, k, v, S, C, lam, g, None)
    return _attention_call(mode, l, qr, qf, k, v, S, C, lam, g, y) if with_ctx else y


def _ret_kernel(q_ref, k_ref, v_ref, g_ref, dec_ref, o_ref, sb_sc, *, S, C):
    CH = RET_CHUNK
    nl, nc = S // CH, C // CH

    def log_sigmoid(x):
        return jnp.minimum(x, 0.0) - jnp.log1p(jnp.exp(-jnp.abs(x)))

    lgf = log_sigmoid(dec_ref[0, 0][0:1, :])
    lgb = log_sigmoid(dec_ref[1, 0][0:1, :])
    ii = lax.broadcasted_iota(jnp.int32, (CH, CH), 1).astype(F32)
    jj = lax.broadcasted_iota(jnp.int32, (CH, CH), 0).astype(F32)
    dd = ii - jj
    fwd = dd >= 0
    mt = (jnp.where(fwd, jnp.exp(lgf * jnp.where(fwd, dd, 0.0)), 0.0)
          + jnp.where(fwd, 0.0, jnp.exp(lgb * jnp.where(fwd, 0.0, -dd))))
    xi_f = jnp.exp(lgf * (ii + 1.0))
    xi_b = jnp.exp(lgb * (CH - ii))
    zeta_f = jnp.exp(lgf * (CH - 1.0 - jj))
    zeta_b = jnp.exp(lgb * jj)
    cdf = jnp.exp(lgf * CH)
    cdb = jnp.exp(lgb * CH)
    zq = jnp.zeros((RET_DIM, CH), BF16)

    def sl(c):
        return pl.ds(pl.multiple_of(c * CH, CH), CH)

    def state_update(c, zeta):
        kz = (k_ref[0, sl(c), :].astype(F32) * zeta).astype(BF16)
        return jnp.dot(v_ref[0, 0, :, sl(c)], kz, preferred_element_type=F32)

    def bwd_step(c, sb):
        sb_sc[c] = sb
        return sb * cdb + state_update(c, zeta_b)

    sb = jnp.zeros((RET_DIM, CH), F32)
    for c in range(nl + nc - 1, nl - 1, -1):
        sb = bwd_step(c, sb)
    lax.fori_loop(0, nl, lambda n, s: bwd_step(nl - 1 - n, s), sb, unroll=RET_UNROLL)

    def fwd_step(c, sf):
        qc = q_ref[0, 0, :, sl(c)]
        qp = jnp.concatenate([qc, zq], axis=0)
        kc = k_ref[0, sl(c), :]
        vc = v_ref[0, 0, :, sl(c)]
        at = jnp.dot(kc, qp, preferred_element_type=F32)
        qpf = qp.astype(F32)
        rhs = jnp.concatenate([(at * mt).astype(BF16), (qpf * xi_f).astype(BF16), (qpf * xi_b).astype(BF16)], axis=0)
        lhs = jnp.concatenate([vc, sf.astype(BF16), sb_sc[c].astype(BF16)], axis=1)
        o = jnp.dot(lhs, rhs, preferred_element_type=F32)
        gch = g_ref[0, :, sl(c)]
        o_ref[0, :, sl(c)] = (gch * _sigmoid(gch) * _rms_rows(o)).astype(BF16)
        return sf * cdf + state_update(c, zeta_f)

    sf = jnp.zeros((RET_DIM, CH), F32)
    for c in range(nl, nl + nc):
        sf = fwd_step(c, sf)
    lax.fori_loop(0, nl, fwd_step, sf, unroll=RET_UNROLL)


def _retention(rq, rk, rv, rg, dec, S, C):
    B, H, d, Sp = rq.shape
    kern = functools.partial(_ret_kernel, S=S, C=C)
    return pl.pallas_call(
        kern,
        grid=(B, H),
        in_specs=[
            pl.BlockSpec((1, 1, d, Sp), lambda b, h: (b, h, 0, 0)),
            pl.BlockSpec((1, Sp, LANES), lambda b, h: (b, 0, h)),
            pl.BlockSpec((1, 1, d, Sp), lambda b, h: (b, h, 0, 0)),
            pl.BlockSpec((1, d, Sp), lambda b, h: (b, h, 0)),
            pl.BlockSpec((2, 1, 8, LANES), lambda b, h: (0, h, 0, 0)),
        ],
        out_specs=pl.BlockSpec((1, d, Sp), lambda b, h: (b, h, 0)),
        out_shape=jax.ShapeDtypeStruct((B, H * d, Sp), BF16),
        scratch_shapes=[pltpu.VMEM((Sp // RET_CHUNK, RET_DIM, RET_CHUNK), F32)],
        compiler_params=pltpu.CompilerParams(
            dimension_semantics=("arbitrary", "arbitrary"), vmem_limit_bytes=VMEM_LIMIT),
        name="retention",
    )(rq, rk, rv, rg, dec)


def _layer_norm(t, g, b):
    mu = jnp.mean(t, axis=-1, keepdims=True)
    tc = t - mu
    var = jnp.mean(tc * tc, axis=-1, keepdims=True)
    return tc * lax.rsqrt(var + LN_EPS) * g + b


def _out_kernel(yda_ref, ymla_ref, yret_ref, w_ref, x_ref, gate_ref, g1_ref, b1_ref, sh2_ref, sc2_ref, rw_ref,
                x1_ref, h2_ref, lg_ref, *, alpha):
    proj = (lax.dot_general(yda_ref[0], w_ref[0, 0:DA_VW, :], TN_DIMS, preferred_element_type=F32)
            + lax.dot_general(ymla_ref[0], w_ref[0, DA_VW:DA_VW + MLA_VW, :], TN_DIMS, preferred_element_type=F32)
            + lax.dot_general(yret_ref[0], w_ref[0, DA_VW + MLA_VW:, :], TN_DIMS, preferred_element_type=F32))
    x1 = _layer_norm(alpha * x_ref[0] + gate_ref[0, 0, 0] * proj, g1_ref[0], b1_ref[0])
    x1_ref[0] = x1
    h2 = x1 * (1.0 + sc2_ref[0, 0, 0]) + sh2_ref[0, 0, 0]
    h2_ref[0] = h2.astype(BF16)
    h2_hi = h2.astype(BF16)
    h2_lo = (h2 - h2_hi.astype(F32)).astype(BF16)
    hi = jnp.dot(h2_hi, rw_ref[...], preferred_element_type=F32)
    lo = jnp.dot(h2_lo, rw_ref[:, 0:LANES], preferred_element_type=F32)
    lg_ref[0] = hi[:, 0:LANES] + (hi[:, LANES:] + lo)


def _out_proj(l, yda, ymla, yret, w_out, xc, modr, ln_g, ln_b, rw_pad, S, alpha, latents_only):
    B, Sp, D = xc.shape
    n_lat_tiles = S // TM
    n_rows = S if latents_only else Sp

    def mod_spec(j):
        return pl.BlockSpec((1, 1, 1, 1, D), lambda b, t: (l, jnp.where(t >= n_lat_tiles, B, b), j, 0, 0))

    def fm(a):
        return pl.BlockSpec((1, a.shape[1], TM), lambda b, t: (b, 0, t))

    def wspec(a):
        return pl.BlockSpec((1,) + a.shape[1:], lambda b, t: (l,) + (0,) * (a.ndim - 1))

    row = pl.BlockSpec((1, TM, D), lambda b, t: (b, t, 0))
    sds = jax.ShapeDtypeStruct
    return pl.pallas_call(
        functools.partial(_out_kernel, alpha=alpha),
        grid=(B, n_rows // TM),
        in_specs=[fm(yda), fm(ymla), fm(yret), wspec(w_out), row, mod_spec(2), wspec(ln_g), wspec(ln_b),
                  mod_spec(3), mod_spec(4), pl.BlockSpec(rw_pad.shape, lambda b, t: (0, 0))],
        out_specs=[row, row, pl.BlockSpec((1, TM, LANES), lambda b, t: (b, t, 0))],
        out_shape=[sds((B, n_rows, D), F32), sds((B, n_rows, D), BF16), sds((B, n_rows, LANES), F32)],
        compiler_params=pltpu.CompilerParams(dimension_semantics=("arbitrary", "arbitrary")),
        name="out_proj",
    )(yda, ymla, yret, w_out, xc, modr, ln_g, ln_b, modr, modr, rw_pad)


def _gates_T(logits_T, rb):
    s = _sigmoid(logits_T)
    ch = s + rb
    srow = [s[e:e + 1] for e in range(N_EXPERTS)]
    crow = [ch[e:e + 1] for e in range(N_EXPERTS)]
    per = N_EXPERTS // N_GROUPS
    gs = []
    for g in range(N_GROUPS):
        a, b, c, d = crow[per * g: per * g + per]
        m1, n1, m2, n2 = jnp.maximum(a, b), jnp.minimum(a, b), jnp.maximum(c, d), jnp.minimum(c, d)
        gs.append(jnp.maximum(m1, m2) + jnp.maximum(jnp.minimum(m1, m2), jnp.maximum(n1, n2)))
    gmax = functools.reduce(jnp.maximum, gs)
    taken = jnp.zeros(gmax.shape, jnp.bool_)
    gsel = []
    for g in range(N_GROUPS):
        sg = jnp.logical_and(gs[g] == gmax, jnp.logical_not(taken))
        gsel.append(sg)
        taken = jnp.logical_or(taken, sg)
    neg = jnp.full(gmax.shape, -jnp.inf, F32)
    mc = [jnp.where(gsel[e // per], crow[e], neg) for e in range(N_EXPERTS)]
    sel = [jnp.zeros(gmax.shape, jnp.bool_) for _ in range(N_EXPERTS)]
    for _ in range(2):
        top = functools.reduce(jnp.maximum, mc)
        taken = jnp.zeros(gmax.shape, jnp.bool_)
        for e in range(N_EXPERTS):
            hit = jnp.logical_and(mc[e] == top, jnp.logical_not(taken))
            taken = jnp.logical_or(taken, hit)
            sel[e] = jnp.logical_or(sel[e], hit)
            mc[e] = jnp.where(hit, neg, mc[e])
    w = [jnp.where(sel[e], srow[e], 0.0) for e in range(N_EXPERTS)]
    wsum = functools.reduce(lambda a, b: a + b, w)
    gates = jnp.concatenate([we / wsum * ROUTED_SCALE for we in w], axis=0)
    return gates, [jnp.where(sg, 1.0, 0.0) for sg in gsel]


def _moe_sorted_rows(tm):
    return tm + N_GROUPS * MOE_ALIGN + MOE_BLOCK


def _moe_kernel(h_ref, lg_ref, rb_ref, w13_ref, w2_ref, o_ref,
                off_sc, nblk_sc, perm_sc, hs_sc, gs_sc, accs_sc, earlier_sc, *, nsub):
    e = pl.program_id(2)
    tm = h_ref.shape[1] // nsub
    n_sorted = perm_sc.shape[1]
    per = N_EXPERTS // N_GROUPS
    lane = lax.broadcasted_iota(jnp.int32, (1, LANES), 1)

    def expert(x):
        ab = jnp.dot(x, w13_ref[0, 0], preferred_element_type=F32)
        a = ab[:, 0:D_EXPERT]
        hid = (a * _sigmoid(a) * ab[:, D_EXPERT:]).astype(BF16)
        return jnp.dot(hid, w2_ref[0, 0], preferred_element_type=F32)

    def gate_col(g):
        return jnp.sum(jnp.where(lane == e, g, 0.0), axis=1, keepdims=True)

    def sub(j):
        return slice(j * tm, (j + 1) * tm)

    @pl.when(jnp.logical_and(jnp.logical_and(pl.program_id(0) == 0, pl.program_id(1) == 0), e == 0))
    def _():
        before = lax.broadcasted_iota(jnp.int32, (tm, tm), 0) < lax.broadcasted_iota(jnp.int32, (tm, tm), 1)
        earlier_sc[...] = jnp.where(before, 1.0, 0.0).astype(BF16)

    @pl.when(e == 0)
    def _():
        for j in range(nsub):
            gT, gsel = _gates_T(lg_ref[0, sub(j), :].T[0:N_EXPERTS, :], rb_ref[...])
            gates = jnp.concatenate([gT, jnp.zeros((LANES - N_EXPERTS, tm), F32)], axis=0).T
            member = jnp.concatenate(gsel + [jnp.zeros((BF16_ROWS - N_GROUPS, tm), F32)], axis=0)
            rank = jnp.dot(member.astype(BF16), earlier_sc[...], preferred_element_type=F32)
            count = jnp.sum(member, axis=1, keepdims=True)
            off = jnp.int32(0)
            slot = jnp.zeros((1, tm), F32)
            for g in range(N_GROUPS):
                n_g = jnp.max(count[g:g + 1]).astype(jnp.int32)
                off_sc[j * N_GROUPS + g] = off
                nblk_sc[j * N_GROUPS + g] = (n_g + (MOE_BLOCK - 1)) // MOE_BLOCK
                slot = slot + gsel[g] * (rank[g:g + 1] + off.astype(F32))
                off = off + ((n_g + (MOE_ALIGN - 1)) // MOE_ALIGN) * MOE_ALIGN
            rows_i = lax.broadcasted_iota(jnp.int32, (n_sorted, tm), 0)
            perm = jnp.where(rows_i == slot.astype(jnp.int32), 1.0, 0.0).astype(BF16)
            perm_sc[j] = perm
            hs_sc[j] = jnp.dot(perm, h_ref[0, sub(j), :], preferred_element_type=F32).astype(BF16)
            g_hi = gates.astype(BF16)
            g_lo = (gates - g_hi.astype(F32)).astype(BF16)
            gs_sc[j] = (jnp.dot(perm, g_hi, preferred_element_type=F32)
                        + jnp.dot(perm, g_lo, preferred_element_type=F32))
            accs_sc[j] = jnp.zeros(accs_sc.shape[1:], F32)

    for j in range(nsub):
        start = off_sc[j * N_GROUPS + e // per]

        def block(k, carry, j=j, start=start):
            rows = pl.ds(pl.multiple_of(start + k * MOE_BLOCK, MOE_ALIGN), MOE_BLOCK)
            accs_sc[j, rows, :] += gate_col(gs_sc[j, rows, :]) * expert(hs_sc[j, rows, :])
            return carry

        lax.fori_loop(0, nblk_sc[j * N_GROUPS + e // per], block, 0)

    @pl.when(e == N_EXPERTS - 1)
    def _():
        for j in range(nsub):
            o_ref[0, sub(j), :] = lax.dot_general(perm_sc[j], accs_sc[j].astype(BF16), TN_DIMS,
                                                  preferred_element_type=F32).astype(BF16)


def _moe_ln_kernel(x1_ref, f_ref, gb_ref, gc_ref, g2_ref, b2_ref, o_ref, *, S, alpha):
    tm = x1_ref.shape[1]
    n_lat = S - pl.program_id(1) * tm
    rowi = lax.broadcasted_iota(jnp.int32, (tm, 1), 0)
    gate = jnp.where(rowi < n_lat, gb_ref[0, 0, 0], gc_ref[0, 0, 0])
    o_ref[0] = _layer_norm(alpha * x1_ref[0] + gate * f_ref[0].astype(F32), g2_ref[0], b2_ref[0])


def _moe(l, h2, logits, rb, w13, w2, x1, modr, ln_g, ln_b, S, alpha, latents_only):
    B, n_rows, D = x1.shape
    sub = TM_MOE_LAST if latents_only else TM_MOE
    tm = MOE_SUBTILES * sub
    n_sorted = _moe_sorted_rows(sub)

    def wspec(a):
        return pl.BlockSpec((1,) + a.shape[1:], lambda b, t: (l,) + (0,) * (a.ndim - 1))

    row = lambda b, t, e: (b, t, 0)
    f = pl.pallas_call(
        functools.partial(_moe_kernel, nsub=MOE_SUBTILES),
        grid=(B, n_rows // tm, N_EXPERTS),
        in_specs=[
            pl.BlockSpec((1, tm, D), row),
            pl.BlockSpec((1, tm, LANES), row),
            pl.BlockSpec(rb.shape, lambda b, t, e: (0, 0)),
            pl.BlockSpec((1, 1, D, 2 * D_EXPERT), lambda b, t, e: (l, e, 0, 0)),
            pl.BlockSpec((1, 1, D_EXPERT, D), lambda b, t, e: (l, e, 0, 0)),
        ],
        out_specs=pl.BlockSpec((1, tm, D), row),
        out_shape=jax.ShapeDtypeStruct((B, n_rows, D), BF16),
        scratch_shapes=[
            pltpu.SMEM((MOE_SUBTILES * N_GROUPS,), jnp.int32), pltpu.SMEM((MOE_SUBTILES * N_GROUPS,), jnp.int32),
            pltpu.VMEM((MOE_SUBTILES, n_sorted, sub), BF16),
            pltpu.VMEM((MOE_SUBTILES, n_sorted, D), BF16),
            pltpu.VMEM((MOE_SUBTILES, n_sorted, LANES), F32),
            pltpu.VMEM((MOE_SUBTILES, n_sorted, D), F32),
            pltpu.VMEM((sub, sub), BF16)],
        compiler_params=pltpu.CompilerParams(
            dimension_semantics=("arbitrary", "arbitrary", "arbitrary"), vmem_limit_bytes=VMEM_LIMIT),
        name="moe",
    )(h2, logits, rb, w13, w2)
    rows = pl.BlockSpec((1, tm, D), lambda b, t: (b, t, 0))
    return pl.pallas_call(
        functools.partial(_moe_ln_kernel, S=S, alpha=alpha),
        grid=(B, n_rows // tm),
        in_specs=[rows, rows,
                  pl.BlockSpec((1, 1, 1, 1, D), lambda b, t: (l, b, 5, 0, 0)),
                  pl.BlockSpec((1, 1, 1, 1, D), lambda b, t: (l, B, 5, 0, 0)),
                  wspec(ln_g), wspec(ln_b)],
        out_specs=rows,
        out_shape=jax.ShapeDtypeStruct((B, n_rows, D), F32),
        compiler_params=pltpu.CompilerParams(
            dimension_semantics=("arbitrary", "arbitrary"), vmem_limit_bytes=VMEM_LIMIT),
        name="moe_ln",
    )(x1, f, modr, modr, ln_g, ln_b)


def _rope_tables(S, C, rot_dim):
    rows = S // GRID_W
    r = jnp.repeat(jnp.arange(rows, dtype=F32), GRID_W)
    col = jnp.tile(jnp.arange(GRID_W, dtype=F32), rows)
    n_freq = rot_dim // 4
    freqs = ROPE_BASE ** (-jnp.arange(n_freq, dtype=F32) / n_freq)
    ang = jnp.concatenate([r[:, None] * freqs, col[:, None] * freqs], -1)
    cos, sin = jnp.cos(ang), jnp.sin(ang)
    cos_rep = jnp.repeat(cos, 2, axis=-1)
    sin_alt = jnp.stack([-sin, sin], -1).reshape(S, rot_dim)
    cos_rep = jnp.concatenate([cos_rep, jnp.ones((C, rot_dim), F32)], 0)
    sin_alt = jnp.concatenate([sin_alt, jnp.zeros((C, rot_dim), F32)], 0)
    return cos_rep.T, sin_alt.T


def kernel(x, c, ctx, c_ctx, w_ada, b_ada, w_in, da_lambda, da_subln, mla_q_norm, mla_w_uq, mla_kv_norm, mla_w_ukv, ret_decay_f, ret_decay_b, w_out, ln1_g, ln1_b, router_w, router_b, exp_w1, exp_w3, exp_w2, ln2_g, ln2_b):
    B, S, D = x.shape
    C = ctx.shape[1]
    depth = w_in.shape[0]
    alpha = float((2 * depth) ** 0.25)

    cvec = jnp.concatenate([c, c_ctx[None, :], jnp.zeros((8 - B - 1, D), F32)], 0)
    mod = _modulation(cvec, w_ada, b_ada)
    modr = mod.reshape(depth, 8, N_MOD, 1, D)

    wT_all = jnp.swapaxes(w_in, 1, 2).astype(BF16)
    wuqT = jnp.swapaxes(mla_w_uq, 1, 2).astype(BF16)
    ukv = mla_w_ukv.reshape(depth, MLA_KV_RANK, MLA_HEADS, MLA_NOPE + MLA_VDIM)
    wvT = jnp.swapaxes(ukv[..., MLA_NOPE:].reshape(depth, MLA_KV_RANK, MLA_VW), 1, 2).astype(BF16)
    nopeT = jnp.transpose(ukv[..., :MLA_NOPE], (0, 2, 3, 1))
    top = jnp.concatenate([nopeT, jnp.zeros((depth, MLA_HEADS, MLA_NOPE, MLA_ROPE), F32)], -1)
    mid = jnp.concatenate([jnp.zeros((MLA_ROPE, MLA_KV_RANK), F32), jnp.eye(MLA_ROPE, dtype=F32)], -1)
    mid = jnp.broadcast_to(mid, (depth, MLA_HEADS, MLA_ROPE, MLA_KV_RANK + MLA_ROPE))
    bot = jnp.zeros((depth, MLA_HEADS, LANES - MLA_NOPE - MLA_ROPE, MLA_KV_RANK + MLA_ROPE), F32)
    wkT = jnp.concatenate([top, mid, bot], 2).reshape(depth, MLA_HEADS * LANES, MLA_KV_RANK + MLA_ROPE).astype(BF16)
    gq = mla_q_norm[:, :, None]
    gkv = mla_kv_norm[:, :, None]
    subln = da_subln[:, :, None]
    w_out_b = w_out.astype(BF16)
    w13 = jnp.concatenate([exp_w1, exp_w3], -1).astype(BF16)
    w2 = exp_w2.astype(BF16)
    rw_f = jnp.concatenate([router_w, jnp.zeros((D, LANES - N_EXPERTS), F32)], -1)
    rw_hi = rw_f.astype(BF16)
    rw_pad = jnp.concatenate([rw_hi, (rw_f - rw_hi.astype(F32)).astype(BF16)], -1)
    rb = router_b[:, None]
    dec = jnp.broadcast_to(jnp.stack([ret_decay_f, ret_decay_b], 1)[:, :, :, None, None],
                           (depth, 2, RET_HEADS, 8, LANES))
    tabs = _rope_tables(S, C, DA_DIM) + _rope_tables(S, C, RET_DIM)
    ln1g, ln1b, ln2g, ln2b = (a[:, None, :] for a in (ln1_g, ln1_b, ln2_g, ln2_b))

    xc = jnp.concatenate([x, ctx], 1)
    for l in range(depth):
        last = l == depth - 1
        (daqr, daqf, dak, dav, mqr, mqf, mk, mv, rq, rk, rv, rg) = _in_proj(
            l, xc, modr, wT_all, tabs, gq, wuqT, gkv, wvT, wkT, S)
        yda = _attention("da", l, daqr, daqf, dak, dav, S, C, not last, lam=da_lambda, g=subln)
        ymla = _attention("mla", l, mqr, mqf, mk, mv, S, C, not last)
        yret = _retention(rq, rk, rv, rg, dec[l], S, C)
        x1, h2, logits = _out_proj(l, yda, ymla, yret, w_out_b, xc, modr, ln1g, ln1b, rw_pad, S, alpha, last)
        xc = _moe(l, h2, logits, rb, w13, w2, x1, modr, ln2g, ln2b, S, alpha, last)
    return xc
```

```python
import functools
import math

import jax
import jax.numpy as jnp
import numpy as np
from jax import lax
from jax.experimental import pallas as pl
from jax.experimental.pallas import tpu as pltpu

F32 = jnp.float32
BF16 = jnp.bfloat16

GRID_W = 64
DA_HEADS, DA_DIM, DA_VDIM = 6, 32, 64
MLA_HEADS, MLA_NOPE, MLA_ROPE, MLA_VDIM = 6, 64, 32, 64
MLA_Q_RANK, MLA_KV_RANK = 256, 128
RET_HEADS, RET_DIM, RET_CHUNK = 4, 64, 128
ROPE_BASE = 10000.0
N_EXPERTS, N_GROUPS, D_EXPERT = 16, 4, 512
ROUTED_SCALE = 1.0
N_MOD = 6
LN_EPS = 1e-5
RMS_EPS = 1e-6
LOG2E = math.log2(math.e)

DA_W = DA_HEADS * 2 * DA_DIM
DA_VW = DA_HEADS * DA_VDIM
RET_W = RET_HEADS * RET_DIM
MLA_QW = MLA_HEADS * (MLA_NOPE + MLA_ROPE)
MLA_VW = MLA_HEADS * MLA_VDIM
MLA_QD = MLA_NOPE + MLA_ROPE
DA_MAPS = 2 * DA_HEADS
V_DIM = DA_VDIM
assert MLA_VDIM == V_DIM
IN_SPLIT = (DA_W, DA_W, DA_VW, MLA_Q_RANK, MLA_KV_RANK, MLA_ROPE, RET_W, RET_W, RET_W, RET_W)

LANES = 128
BF16_ROWS = 16
VMEM_LIMIT = 56 * 1024 * 1024
V_ROWS = V_DIM + BF16_ROWS
TM = 256
ATTN_N = 512
TK = 2048
ATTN_SUBTILES = {"da": 8, "mla": 4}
ATTN_NBUF = 2
ATTN_MAX_JUMP = 32.0
ATTN_UNROLL = 2
TM_MOE = 640
TM_MOE_LAST = 512
MOE_SUBTILES = 2
MOE_BLOCK = 256
MOE_ALIGN = 32
RET_UNROLL = 8

NT_DIMS = (((1,), (1,)), ((), ()))
TN_DIMS = (((0,), (0,)), ((), ()))


def _sigmoid(x):
    return 1.0 / (1.0 + jnp.exp(-x))


def _mod_kernel(c_ref, w_ref, b_ref, o_ref):
    c = c_ref[...]
    sc = (c * _sigmoid(c)).astype(BF16)
    o_ref[0] = jnp.dot(sc, w_ref[0].astype(BF16), preferred_element_type=F32) + b_ref[0]


def _modulation(cvec, w_ada, b_ada):
    depth, d, n = w_ada.shape
    tn = n // 4
    return pl.pallas_call(
        _mod_kernel,
        grid=(depth, n // tn),
        in_specs=[
            pl.BlockSpec((8, d), lambda l, j: (0, 0)),
            pl.BlockSpec((1, d, tn), lambda l, j: (l, 0, j)),
            pl.BlockSpec((1, 1, tn), lambda l, j: (l, 0, j)),
        ],
        out_specs=pl.BlockSpec((1, 8, tn), lambda l, j: (l, 0, j)),
        out_shape=jax.ShapeDtypeStruct((depth, 8, n), F32),
        name="adaln_mod",
    )(cvec, w_ada, b_ada.reshape(depth, 1, n))


_O_DAQ, _O_DAK, _O_DAV, _O_CQ, _O_CKV, _O_KPE, _O_RQ, _O_RK, _O_RV, _O_RG = (
    int(v) for v in np.cumsum((0,) + IN_SPLIT)[:-1])
N_ZROWS = sum(IN_SPLIT)


def _rms_rows(x):
    return x * lax.rsqrt(jnp.mean(x * x, axis=0, keepdims=True) + RMS_EPS)


def _pairswap_rows(x):
    n = x.shape[0]
    even = (lax.broadcasted_iota(jnp.int32, x.shape, 0) & 1) == 0
    return jnp.where(even, pltpu.roll(x, n - 1, 0), pltpu.roll(x, 1, 0))


def _in_kernel(x_ref, sh_ref, sc_ref, wT_ref, cosA_ref, sinA_ref, cosR_ref, sinR_ref,
               gq_ref, wuqT_ref, gkv_ref, wvT_ref, wkT_ref,
               daqr, daqf, dak, dav, mqr, mqf, mk, mv, rq, rk, rv, rg, z_ref, *, cq_da, cq_mla, kscale):
    tm = x_ref.shape[1]
    x = x_ref[0]
    h = (x * (1.0 + sc_ref[0, 0, 0]) + sh_ref[0, 0, 0]).astype(BF16)
    z_ref[...] = lax.dot_general(wT_ref[0], h, NT_DIMS, preferred_element_type=F32)
    cosA = cosA_ref[...]
    sinA = sinA_ref[...]
    cosR = cosR_ref[...]
    sinR = sinR_ref[...]
    ones_rows = (lax.broadcasted_iota(jnp.int32, (DA_HEADS, V_ROWS - V_DIM, tm), 1) == 0).astype(BF16)

    def rows(off, n):
        return z_ref[off:off + n, :]

    q2 = rows(_O_DAQ, DA_W)
    q = q2.reshape(DA_MAPS, DA_DIM, tm)
    qs = _pairswap_rows(q2).reshape(DA_MAPS, DA_DIM, tm)
    daqr[0] = ((q * cosA + qs * sinA) * cq_da).astype(BF16)
    daqf[0] = (q * cq_da).astype(BF16)
    k2 = rows(_O_DAK, DA_W)
    k = k2.reshape(DA_MAPS, DA_DIM, tm)
    ks = _pairswap_rows(k2).reshape(DA_MAPS, DA_DIM, tm)
    kr = (k * cosA + ks * sinA).reshape(DA_HEADS, 2 * DA_DIM, tm)
    kp = jnp.concatenate([kr, jnp.zeros_like(kr)], axis=1).reshape(DA_HEADS * LANES, tm)
    dak[0] = kp.T.astype(BF16)
    dav[0, :, 0:V_DIM, :] = rows(_O_DAV, DA_VW).reshape(DA_HEADS, DA_VDIM, tm).astype(BF16)
    dav[0, :, V_DIM:V_ROWS, :] = ones_rows

    cqn = (_rms_rows(rows(_O_CQ, MLA_Q_RANK)) * gq_ref[0]).astype(BF16)
    qm = jnp.dot(wuqT_ref[0], cqn, preferred_element_type=F32)
    qh = qm.reshape(MLA_HEADS, MLA_NOPE + MLA_ROPE, tm)
    qsw = _pairswap_rows(qm).reshape(MLA_HEADS, MLA_NOPE + MLA_ROPE, tm)[:, MLA_NOPE:, :]
    rot = qh[:, MLA_NOPE:, :] * cosA + qsw * sinA
    mqr[0, :, 0:MLA_NOPE, :] = (qh[:, 0:MLA_NOPE, :] * cq_mla).astype(BF16)
    mqr[0, :, MLA_NOPE:, :] = (rot * cq_mla).astype(BF16)
    mqf[0] = (qh * cq_mla).astype(BF16)
    ckvn = (_rms_rows(rows(_O_CKV, MLA_KV_RANK)) * gkv_ref[0]).astype(BF16)
    vm = jnp.dot(wvT_ref[0], ckvn, preferred_element_type=F32)
    mv[0, :, 0:V_DIM, :] = vm.reshape(MLA_HEADS, MLA_VDIM, tm).astype(BF16)
    mv[0, :, V_DIM:V_ROWS, :] = ones_rows
    kpe = rows(_O_KPE, MLA_ROPE)
    kper = kpe * cosA + _pairswap_rows(kpe) * sinA
    kin = jnp.concatenate([ckvn, kper.astype(BF16)], axis=0)
    kmT = jnp.dot(wkT_ref[0], kin, preferred_element_type=F32)
    mk[0] = kmT.T.astype(BF16)

    r_q2 = rows(_O_RQ, RET_W)
    r_q = r_q2.reshape(RET_HEADS, RET_DIM, tm)
    r_qs = _pairswap_rows(r_q2).reshape(RET_HEADS, RET_DIM, tm)
    rq[0] = (r_q * cosR + r_qs * sinR).astype(BF16)
    r_k2 = rows(_O_RK, RET_W)
    r_k = r_k2.reshape(RET_HEADS, RET_DIM, tm)
    r_ks = _pairswap_rows(r_k2).reshape(RET_HEADS, RET_DIM, tm)
    rkr = (r_k * cosR + r_ks * sinR) * kscale
    rkp = jnp.concatenate([rkr, jnp.zeros_like(rkr)], axis=1).reshape(RET_HEADS * LANES, tm)
    rk[0] = rkp.T.astype(BF16)
    rv[0] = rows(_O_RV, RET_W).reshape(RET_HEADS, RET_DIM, tm).astype(BF16)
    rg[0] = rows(_O_RG, RET_W)


def _in_proj(l, xc, modr, wT_all, tabs, gq, wuqT, gkv, wvT, wkT, S):
    B, Sp, D = xc.shape
    nt = Sp // TM
    n_lat_tiles = S // TM
    cosA, sinA, cosR, sinR = tabs

    def mod_spec(j):
        return pl.BlockSpec((1, 1, 1, 1, D), lambda b, t: (l, jnp.where(t >= n_lat_tiles, B, b), j, 0, 0))

    def wspec(a):
        return pl.BlockSpec((1,) + a.shape[1:], lambda b, t: (l,) + (0,) * (a.ndim - 1))

    def fm4(h, d):
        return pl.BlockSpec((1, h, d, TM), lambda b, t: (b, 0, 0, t))

    def tmaj(w):
        return pl.BlockSpec((1, TM, w), lambda b, t: (b, t, 0))

    kern = functools.partial(
        _in_kernel,
        cq_da=float(DA_DIM ** -0.5 * LOG2E),
        cq_mla=float((MLA_NOPE + MLA_ROPE) ** -0.5 * LOG2E),
        kscale=float(RET_DIM ** -0.5),
    )
    sds = jax.ShapeDtypeStruct
    return pl.pallas_call(
        kern,
        grid=(B, nt),
        in_specs=[
            pl.BlockSpec((1, TM, D), lambda b, t: (b, t, 0)),
            mod_spec(0), mod_spec(1),
            wspec(wT_all),
            pl.BlockSpec((DA_DIM, TM), lambda b, t: (0, t)),
            pl.BlockSpec((DA_DIM, TM), lambda b, t: (0, t)),
            pl.BlockSpec((RET_DIM, TM), lambda b, t: (0, t)),
            pl.BlockSpec((RET_DIM, TM), lambda b, t: (0, t)),
            wspec(gq), wspec(wuqT), wspec(gkv), wspec(wvT), wspec(wkT),
        ],
        out_specs=[
            fm4(DA_MAPS, DA_DIM), fm4(DA_MAPS, DA_DIM), tmaj(DA_HEADS * LANES), fm4(DA_HEADS, V_ROWS),
            fm4(MLA_HEADS, MLA_QD), fm4(MLA_HEADS, MLA_QD), tmaj(MLA_HEADS * LANES), fm4(MLA_HEADS, V_ROWS),
            fm4(RET_HEADS, RET_DIM), tmaj(RET_HEADS * LANES), fm4(RET_HEADS, RET_DIM),
            pl.BlockSpec((1, RET_W, TM), lambda b, t: (b, 0, t)),
        ],
        out_shape=[
            sds((B, DA_MAPS, DA_DIM, Sp), BF16), sds((B, DA_MAPS, DA_DIM, Sp), BF16),
            sds((B, Sp, DA_HEADS * LANES), BF16), sds((B, DA_HEADS, V_ROWS, Sp), BF16),
            sds((B, MLA_HEADS, MLA_QD, Sp), BF16), sds((B, MLA_HEADS, MLA_QD, Sp), BF16),
            sds((B, Sp, MLA_HEADS * LANES), BF16), sds((B, MLA_HEADS, V_ROWS, Sp), BF16),
            sds((B, RET_HEADS, RET_DIM, Sp), BF16), sds((B, Sp, RET_HEADS * LANES), BF16),
            sds((B, RET_HEADS, RET_DIM, Sp), BF16), sds((B, RET_W, Sp), F32),
        ],
        scratch_shapes=[pltpu.VMEM((N_ZROWS, TM), F32)],
        compiler_params=pltpu.CompilerParams(
            dimension_semantics=("arbitrary", "arbitrary"), vmem_limit_bytes=VMEM_LIMIT),
        name="in_proj",
    )(xc, modr, modr, wT_all, cosA, sinA, cosR, sinR, gq, wuqT, gkv, wvT, wkT)


def _attn_kernel(*refs, mode, S, C, lam_init, ctx_only, nsub):
    refs = list(refs)
    qr_ref, qf_ref, k_ref, v_ref = refs[:4]
    del refs[:4]
    if mode == "da":
        lam_ref, g_ref = refs[:2]
        del refs[:2]
    if ctx_only:
        del refs[:1]
    o_ref, m_sc, acc_sc, q_sc, jump_sc = refs[:5]
    nb = ATTN_NBUF
    pbuf, albuf = (refs[5 + i * nb: 5 + (i + 1) * nb] for i in range(2))
    tq = o_ref.shape[2] // nsub
    n_lat = S // TK

    def qpad(ref, t):
        cols = slice(t * tq, (t + 1) * tq)
        if mode == "da":
            z = jnp.zeros((DA_DIM, tq), BF16)
            c1 = jnp.concatenate([ref[0, 0, :, cols], z, z, z], axis=0)
            c2 = jnp.concatenate([z, ref[0, 1, :, cols], z, z], axis=0)
            return jnp.concatenate([c1, c2], axis=1)
        return jnp.concatenate([ref[0, 0, :, cols], jnp.zeros((LANES - MLA_QD, tq), BF16)], axis=0)

    def chunk(j):
        return pl.ds(j * TK if isinstance(j, int) else pl.multiple_of(j * TK, TK), TK)

    def context_keys(t):
        c0 = 0 if ctx_only else S
        s = jnp.dot(k_ref[0, c0:c0 + C, :], qpad(qf_ref, t), preferred_element_type=F32)
        m_c = jnp.max(s, axis=0, keepdims=True)
        m_sc[t] = m_c
        acc_sc[t] = jnp.dot(v_ref[0, 0, :, c0:c0 + C], jnp.exp2(s - m_c).astype(BF16), preferred_element_type=F32)

    for t in range(nsub):
        context_keys(t)

    if not ctx_only:
        for t in range(nsub):
            q_sc[t] = qpad(qr_ref, t)
        jump_sc[...] = jnp.full(jump_sc.shape, -jnp.inf, F32)

        def probs(t, j, r):
            s = jnp.dot(k_ref[0, chunk(j), :], q_sc[t], preferred_element_type=F32)
            m_prev = m_sc[t]
            pbuf[r][t] = jnp.exp2(s - m_prev).astype(BF16)
            m_chunk = jnp.max(s, axis=0, keepdims=True)
            m_new = jnp.maximum(m_prev, m_chunk)
            jump_sc[t] = jnp.maximum(jump_sc[t], m_chunk - m_prev)
            albuf[r][t] = jnp.exp2(m_prev - m_new)
            m_sc[t] = m_new

        def accumulate(t, j, r):
            pv = jnp.dot(v_ref[0, 0, :, chunk(j)], pbuf[r][t], preferred_element_type=F32)
            acc_sc[t] = (acc_sc[t] + pv) * albuf[r][t]

        def step(j, r):
            for t in range(nsub):
                accumulate(t, j, r)
                probs(t, j + 1, (r + 1) % nb)

        for t in range(nsub):
            probs(t, 0, 0)
        n_steps = n_lat - 1
        trips = n_steps // ATTN_UNROLL

        def body(i, carry):
            for u in range(ATTN_UNROLL):
                step(i * ATTN_UNROLL + u, u % nb)
            return carry

        lax.fori_loop(0, trips, body, 0)
        for j in range(trips * ATTN_UNROLL, n_steps):
            step(j, j % nb)
        for t in range(nsub):
            accumulate(t, n_lat - 1, (n_lat - 1) % nb)

        @pl.when(jnp.max(jump_sc[...]) > ATTN_MAX_JUMP)
        def _():
            for t in range(nsub):
                context_keys(t)

                def exact_step(j, carry, t=t):
                    s = jnp.dot(k_ref[0, chunk(j), :], q_sc[t], preferred_element_type=F32)
                    m_old = m_sc[t]
                    m_new = jnp.maximum(m_old, jnp.max(s, axis=0, keepdims=True))
                    pv = jnp.dot(v_ref[0, 0, :, chunk(j)], jnp.exp2(s - m_new).astype(BF16),
                                 preferred_element_type=F32)
                    acc_sc[t] = acc_sc[t] * jnp.exp2(m_old - m_new) + pv
                    m_sc[t] = m_new
                    return carry

                lax.fori_loop(0, n_lat, exact_step, 0)

    for t in range(nsub):
        acc = acc_sc[t]
        o = acc[0:V_DIM, :] / acc[V_DIM:V_DIM + 1, :]
        cols = slice(t * tq, (t + 1) * tq)
        if mode == "da":
            lf = lam_ref[0]
            lam = (jnp.exp(jnp.sum(lf[0:1] * lf[1:2], axis=1, keepdims=True))
                   - jnp.exp(jnp.sum(lf[2:3] * lf[3:4], axis=1, keepdims=True)) + lam_init)
            od = o[:, 0:tq] - lam * o[:, tq:2 * tq]
            y = _rms_rows(od) * g_ref[0]
            o_ref[0, :, cols] = (y * (1.0 - lam_init)).astype(BF16)
        else:
            o_ref[0, :, cols] = o.astype(BF16)


def _attention_call(mode, l, qr, qf, k, v, S, C, lam, g, y_latent):
    B, nmaps, d, Sp = qr.shape
    H = v.shape[1]
    nm = nmaps // H
    ctx_only = y_latent is not None
    nsub = 1 if ctx_only else ATTN_SUBTILES[mode]
    tq = C if ctx_only else ATTN_N // nm
    n = nm * tq
    tqb = nsub * tq
    q0 = S // tqb if ctx_only else 0
    lam_init = 0.8 - 0.6 * math.exp(-0.3 * l)
    kern = functools.partial(_attn_kernel, mode=mode, S=S, C=C, lam_init=lam_init, ctx_only=ctx_only, nsub=nsub)
    in_specs = [
        pl.BlockSpec((1, nm, d, tqb), lambda b, h, i: (b, h, 0, q0 + i)),
        pl.BlockSpec((1, nm, d, tqb), lambda b, h, i: (b, h, 0, q0 + i)),
        (pl.BlockSpec((1, C, LANES), lambda b, h, i: (b, S // C, h)) if ctx_only
         else pl.BlockSpec((1, Sp, LANES), lambda b, h, i: (b, 0, h))),
        (pl.BlockSpec((1, 1, V_ROWS, C), lambda b, h, i: (b, h, 0, S // C)) if ctx_only
         else pl.BlockSpec((1, 1, V_ROWS, Sp), lambda b, h, i: (b, h, 0, 0))),
    ]
    args = [qr, qf, k, v]
    if mode == "da":
        in_specs += [pl.BlockSpec((1,) + lam.shape[1:], lambda b, h, i: (l, 0, 0)),
                     pl.BlockSpec((1,) + g.shape[1:], lambda b, h, i: (l, 0, 0))]
        args += [lam, g]
    aliases = {}
    if ctx_only:
        in_specs.append(pl.BlockSpec(memory_space=pl.ANY))
        args.append(y_latent)
        aliases = {len(args) - 1: 0}
    tk = C if ctx_only else TK
    return pl.pallas_call(
        kern,
        grid=(B, H, 1 if ctx_only else S // tqb),
        in_specs=in_specs,
        out_specs=pl.BlockSpec((1, V_DIM, tqb), lambda b, h, i: (b, h, q0 + i)),
        out_shape=jax.ShapeDtypeStruct((B, H * V_DIM, Sp), BF16),
        input_output_aliases=aliases,
        scratch_shapes=(
            [pltpu.VMEM((nsub, 1, n), F32), pltpu.VMEM((nsub, V_ROWS, n), F32), pltpu.VMEM((nsub, LANES, n), BF16),
             pltpu.VMEM((nsub, 1, n), F32)]
            + [pltpu.VMEM((nsub, tk, n), BF16)] * ATTN_NBUF + [pltpu.VMEM((nsub, 1, n), F32)] * ATTN_NBUF),
        compiler_params=pltpu.CompilerParams(
            dimension_semantics=("arbitrary", "arbitrary", "arbitrary"), vmem_limit_bytes=VMEM_LIMIT),
        name="attn_" + mode + ("_ctx" if ctx_only else ""),
    )(*args)


def _attention(mode, l, qr, qf, k, v, S, C, with_ctx, lam=None, g=None):
    y = _attention_call(mode, l, qr, qf, k, v, S, C, lam, g, None)
    return _attention_call(mode, l, qr, qf, k, v, S, C, lam, g, y) if with_ctx else y


def _ret_kernel(q_ref, k_ref, v_ref, g_ref, dec_ref, o_ref, sb_sc, *, S, C):
    CH = RET_CHUNK
    nl, nc = S // CH, C // CH

    def log_sigmoid(x):
        return jnp.minimum(x, 0.0) - jnp.log1p(jnp.exp(-jnp.abs(x)))

    lgf = log_sigmoid(dec_ref[0, 0][0:1, :])
    lgb = log_sigmoid(dec_ref[1, 0][0:1, :])
    ii = lax.broadcasted_iota(jnp.int32, (CH, CH), 1).astype(F32)
    jj = lax.broadcasted_iota(jnp.int32, (CH, CH), 0).astype(F32)
    dd = ii - jj
    fwd = dd >= 0
    mt = (jnp.where(fwd, jnp.exp(lgf * jnp.where(fwd, dd, 0.0)), 0.0)
          + jnp.where(fwd, 0.0, jnp.exp(lgb * jnp.where(fwd, 0.0, -dd))))
    xi_f = jnp.exp(lgf * (ii + 1.0))
    xi_b = jnp.exp(lgb * (CH - ii))
    zeta_f = jnp.exp(lgf * (CH - 1.0 - jj))
    zeta_b = jnp.exp(lgb * jj)
    cdf = jnp.exp(lgf * CH)
    cdb = jnp.exp(lgb * CH)
    zq = jnp.zeros((RET_DIM, CH), BF16)

    def sl(c):
        return pl.ds(pl.multiple_of(c * CH, CH), CH)

    def state_update(c, zeta):
        kz = (k_ref[0, sl(c), :].astype(F32) * zeta).astype(BF16)
        return jnp.dot(v_ref[0, 0, :, sl(c)], kz, preferred_element_type=F32)

    def bwd_step(c, sb):
        sb_sc[c] = sb
        return sb * cdb + state_update(c, zeta_b)

    sb = jnp.zeros((RET_DIM, CH), F32)
    for c in range(nl + nc - 1, nl - 1, -1):
        sb = bwd_step(c, sb)
    lax.fori_loop(0, nl, lambda n, s: bwd_step(nl - 1 - n, s), sb, unroll=RET_UNROLL)

    def fwd_step(c, sf):
        qc = q_ref[0, 0, :, sl(c)]
        qp = jnp.concatenate([qc, zq], axis=0)
        kc = k_ref[0, sl(c), :]
        vc = v_ref[0, 0, :, sl(c)]
        at = jnp.dot(kc, qp, preferred_element_type=F32)
        qpf = qp.astype(F32)
        rhs = jnp.concatenate([(at * mt).astype(BF16), (qpf * xi_f).astype(BF16), (qpf * xi_b).astype(BF16)], axis=0)
        lhs = jnp.concatenate([vc, sf.astype(BF16), sb_sc[c].astype(BF16)], axis=1)
        o = jnp.dot(lhs, rhs, preferred_element_type=F32)
        gch = g_ref[0, :, sl(c)]
        o_ref[0, :, sl(c)] = (gch * _sigmoid(gch) * _rms_rows(o)).astype(BF16)
        return sf * cdf + state_update(c, zeta_f)

    sf = jnp.zeros((RET_DIM, CH), F32)
    for c in range(nl, nl + nc):
        sf = fwd_step(c, sf)
    lax.fori_loop(0, nl, fwd_step, sf, unroll=RET_UNROLL)


def _retention(rq, rk, rv, rg, dec, S, C):
    B, H, d, Sp = rq.shape
    kern = functools.partial(_ret_kernel, S=S, C=C)
    return pl.pallas_call(
        kern,
        grid=(B, H),
        in_specs=[
            pl.BlockSpec((1, 1, d, Sp), lambda b, h: (b, h, 0, 0)),
            pl.BlockSpec((1, Sp, LANES), lambda b, h: (b, 0, h)),
            pl.BlockSpec((1, 1, d, Sp), lambda b, h: (b, h, 0, 0)),
            pl.BlockSpec((1, d, Sp), lambda b, h: (b, h, 0)),
            pl.BlockSpec((2, 1, 8, LANES), lambda b, h: (0, h, 0, 0)),
        ],
        out_specs=pl.BlockSpec((1, d, Sp), lambda b, h: (b, h, 0)),
        out_shape=jax.ShapeDtypeStruct((B, H * d, Sp), BF16),
        scratch_shapes=[pltpu.VMEM((Sp // RET_CHUNK, RET_DIM, RET_CHUNK), F32)],
        compiler_params=pltpu.CompilerParams(
            dimension_semantics=("arbitrary", "arbitrary"), vmem_limit_bytes=VMEM_LIMIT),
        name="retention",
    )(rq, rk, rv, rg, dec)


def _layer_norm(t, g, b):
    mu = jnp.mean(t, axis=-1, keepdims=True)
    tc = t - mu
    var = jnp.mean(tc * tc, axis=-1, keepdims=True)
    return tc * lax.rsqrt(var + LN_EPS) * g + b


def _out_kernel(yda_ref, ymla_ref, yret_ref, w_ref, x_ref, gate_ref, g1_ref, b1_ref, sh2_ref, sc2_ref, rw_ref,
                x1_ref, h2_ref, lg_ref, *, alpha):
    proj = (lax.dot_general(yda_ref[0], w_ref[0, 0:DA_VW, :], TN_DIMS, preferred_element_type=F32)
            + lax.dot_general(ymla_ref[0], w_ref[0, DA_VW:DA_VW + MLA_VW, :], TN_DIMS, preferred_element_type=F32)
            + lax.dot_general(yret_ref[0], w_ref[0, DA_VW + MLA_VW:, :], TN_DIMS, preferred_element_type=F32))
    x1 = _layer_norm(alpha * x_ref[0] + gate_ref[0, 0, 0] * proj, g1_ref[0], b1_ref[0])
    x1_ref[0] = x1
    h2 = x1 * (1.0 + sc2_ref[0, 0, 0]) + sh2_ref[0, 0, 0]
    h2_ref[0] = h2.astype(BF16)
    h2_hi = h2.astype(BF16)
    h2_lo = (h2 - h2_hi.astype(F32)).astype(BF16)
    hi = jnp.dot(h2_hi, rw_ref[...], preferred_element_type=F32)
    lo = jnp.dot(h2_lo, rw_ref[:, 0:LANES], preferred_element_type=F32)
    lg_ref[0] = hi[:, 0:LANES] + (hi[:, LANES:] + lo)


def _out_proj(l, yda, ymla, yret, w_out, xc, modr, ln_g, ln_b, rw_pad, S, alpha, latents_only):
    B, Sp, D = xc.shape
    n_lat_tiles = S // TM
    n_rows = S if latents_only else Sp

    def mod_spec(j):
        return pl.BlockSpec((1, 1, 1, 1, D), lambda b, t: (l, jnp.where(t >= n_lat_tiles, B, b), j, 0, 0))

    def fm(a):
        return pl.BlockSpec((1, a.shape[1], TM), lambda b, t: (b, 0, t))

    def wspec(a):
        return pl.BlockSpec((1,) + a.shape[1:], lambda b, t: (l,) + (0,) * (a.ndim - 1))

    row = pl.BlockSpec((1, TM, D), lambda b, t: (b, t, 0))
    sds = jax.ShapeDtypeStruct
    return pl.pallas_call(
        functools.partial(_out_kernel, alpha=alpha),
        grid=(B, n_rows // TM),
        in_specs=[fm(yda), fm(ymla), fm(yret), wspec(w_out), row, mod_spec(2), wspec(ln_g), wspec(ln_b),
                  mod_spec(3), mod_spec(4), pl.BlockSpec(rw_pad.shape, lambda b, t: (0, 0))],
        out_specs=[row, row, pl.BlockSpec((1, TM, LANES), lambda b, t: (b, t, 0))],
        out_shape=[sds((B, n_rows, D), F32), sds((B, n_rows, D), BF16), sds((B, n_rows, LANES), F32)],
        compiler_params=pltpu.CompilerParams(dimension_semantics=("arbitrary", "arbitrary")),
        name="out_proj",
    )(yda, ymla, yret, w_out, xc, modr, ln_g, ln_b, modr, modr, rw_pad)


def _gates_T(logits_T, rb):
    s = _sigmoid(logits_T)
    ch = s + rb
    srow = [s[e:e + 1] for e in range(N_EXPERTS)]
    crow = [ch[e:e + 1] for e in range(N_EXPERTS)]
    per = N_EXPERTS // N_GROUPS
    gs = []
    for g in range(N_GROUPS):
        a, b, c, d = crow[per * g: per * g + per]
        m1, n1, m2, n2 = jnp.maximum(a, b), jnp.minimum(a, b), jnp.maximum(c, d), jnp.minimum(c, d)
        gs.append(jnp.maximum(m1, m2) + jnp.maximum(jnp.minimum(m1, m2), jnp.maximum(n1, n2)))
    gmax = functools.reduce(jnp.maximum, gs)
    taken = jnp.zeros(gmax.shape, jnp.bool_)
    gsel = []
    for g in range(N_GROUPS):
        sg = jnp.logical_and(gs[g] == gmax, jnp.logical_not(taken))
        gsel.append(sg)
        taken = jnp.logical_or(taken, sg)
    neg = jnp.full(gmax.shape, -jnp.inf, F32)
    mc = [jnp.where(gsel[e // per], crow[e], neg) for e in range(N_EXPERTS)]
    sel = [jnp.zeros(gmax.shape, jnp.bool_) for _ in range(N_EXPERTS)]
    for _ in range(2):
        top = functools.reduce(jnp.maximum, mc)
        taken = jnp.zeros(gmax.shape, jnp.bool_)
        for e in range(N_EXPERTS):
            hit = jnp.logical_and(mc[e] == top, jnp.logical_not(taken))
            taken = jnp.logical_or(taken, hit)
            sel[e] = jnp.logical_or(sel[e], hit)
            mc[e] = jnp.where(hit, neg, mc[e])
    w = [jnp.where(sel[e], srow[e], 0.0) for e in range(N_EXPERTS)]
    wsum = functools.reduce(lambda a, b: a + b, w)
    gates = jnp.concatenate([we / wsum * ROUTED_SCALE for we in w], axis=0)
    return gates, [jnp.where(sg, 1.0, 0.0) for sg in gsel]


def _moe_sorted_rows(tm):
    return tm + N_GROUPS * MOE_ALIGN + MOE_BLOCK


def _moe_kernel(h_ref, lg_ref, rb_ref, w13_ref, w2_ref, o_ref,
                off_sc, nblk_sc, perm_sc, hs_sc, gs_sc, accs_sc, earlier_sc, *, nsub):
    e = pl.program_id(2)
    tm = h_ref.shape[1] // nsub
    n_sorted = perm_sc.shape[1]
    per = N_EXPERTS // N_GROUPS
    lane = lax.broadcasted_iota(jnp.int32, (1, LANES), 1)

    def expert(x):
        ab = jnp.dot(x, w13_ref[0, 0], preferred_element_type=F32)
        a = ab[:, 0:D_EXPERT]
        hid = (a * _sigmoid(a) * ab[:, D_EXPERT:]).astype(BF16)
        return jnp.dot(hid, w2_ref[0, 0], preferred_element_type=F32)

    def gate_col(g):
        return jnp.sum(jnp.where(lane == e, g, 0.0), axis=1, keepdims=True)

    def sub(j):
        return slice(j * tm, (j + 1) * tm)

    @pl.when(jnp.logical_and(jnp.logical_and(pl.program_id(0) == 0, pl.program_id(1) == 0), e == 0))
    def _():
        before = lax.broadcasted_iota(jnp.int32, (tm, tm), 0) < lax.broadcasted_iota(jnp.int32, (tm, tm), 1)
        earlier_sc[...] = jnp.where(before, 1.0, 0.0).astype(BF16)

    @pl.when(e == 0)
    def _():
        for j in range(nsub):
            gT, gsel = _gates_T(lg_ref[0, sub(j), :].T[0:N_EXPERTS, :], rb_ref[...])
            gates = jnp.concatenate([gT, jnp.zeros((LANES - N_EXPERTS, tm), F32)], axis=0).T
            member = jnp.concatenate(gsel + [jnp.zeros((BF16_ROWS - N_GROUPS, tm), F32)], axis=0)
            rank = jnp.dot(member.astype(BF16), earlier_sc[...], preferred_element_type=F32)
            count = jnp.sum(member, axis=1, keepdims=True)
            off = jnp.int32(0)
            slot = jnp.zeros((1, tm), F32)
            for g in range(N_GROUPS):
                n_g = jnp.max(count[g:g + 1]).astype(jnp.int32)
                off_sc[j * N_GROUPS + g] = off
                nblk_sc[j * N_GROUPS + g] = (n_g + (MOE_BLOCK - 1)) // MOE_BLOCK
                slot = slot + gsel[g] * (rank[g:g + 1] + off.astype(F32))
                off = off + ((n_g + (MOE_ALIGN - 1)) // MOE_ALIGN) * MOE_ALIGN
            rows_i = lax.broadcasted_iota(jnp.int32, (n_sorted, tm), 0)
            perm = jnp.where(rows_i == slot.astype(jnp.int32), 1.0, 0.0).astype(BF16)
            perm_sc[j] = perm
            hs_sc[j] = jnp.dot(perm, h_ref[0, sub(j), :], preferred_element_type=F32).astype(BF16)
            g_hi = gates.astype(BF16)
            g_lo = (gates - g_hi.astype(F32)).astype(BF16)
            gs_sc[j] = (jnp.dot(perm, g_hi, preferred_element_type=F32)
                        + jnp.dot(perm, g_lo, preferred_element_type=F32))
            accs_sc[j] = jnp.zeros(accs_sc.shape[1:], F32)

    start = [off_sc[j * N_GROUPS + e // per] for j in range(nsub)]
    n_blocks = [nblk_sc[j * N_GROUPS + e // per] for j in range(nsub)]
    n_common = functools.reduce(jnp.minimum, n_blocks)

    def rows_of(j, k):
        return pl.ds(pl.multiple_of(start[j] + k * MOE_BLOCK, MOE_ALIGN), MOE_BLOCK)

    def together(k, carry):
        y = expert(jnp.concatenate([hs_sc[j, rows_of(j, k), :] for j in range(nsub)], axis=0))
        for j in range(nsub):
            rows = rows_of(j, k)
            accs_sc[j, rows, :] += gate_col(gs_sc[j, rows, :]) * y[j * MOE_BLOCK:(j + 1) * MOE_BLOCK]
        return carry

    lax.fori_loop(0, n_common, together, 0)
    for j in range(nsub):
        def alone(k, carry, j=j):
            rows = rows_of(j, k)
            accs_sc[j, rows, :] += gate_col(gs_sc[j, rows, :]) * expert(hs_sc[j, rows, :])
            return carry

        lax.fori_loop(n_common, n_blocks[j], alone, 0)

    @pl.when(e == N_EXPERTS - 1)
    def _():
        for j in range(nsub):
            o_ref[0, sub(j), :] = lax.dot_general(perm_sc[j], accs_sc[j].astype(BF16), TN_DIMS,
                                                  preferred_element_type=F32).astype(BF16)


def _moe_ln_kernel(x1_ref, f_ref, gb_ref, gc_ref, g2_ref, b2_ref, o_ref, *, S, alpha):
    tm = x1_ref.shape[1]
    n_lat = S - pl.program_id(1) * tm
    rowi = lax.broadcasted_iota(jnp.int32, (tm, 1), 0)
    gate = jnp.where(rowi < n_lat, gb_ref[0, 0, 0], gc_ref[0, 0, 0])
    o_ref[0] = _layer_norm(alpha * x1_ref[0] + gate * f_ref[0].astype(F32), g2_ref[0], b2_ref[0])


def _moe(l, h2, logits, rb, w13, w2, x1, modr, ln_g, ln_b, S, alpha, latents_only):
    B, n_rows, D = x1.shape
    sub = TM_MOE_LAST if latents_only else TM_MOE
    tm = MOE_SUBTILES * sub
    n_sorted = _moe_sorted_rows(sub)

    def wspec(a):
        return pl.BlockSpec((1,) + a.shape[1:], lambda b, t: (l,) + (0,) * (a.ndim - 1))

    row = lambda b, t, e: (b, t, 0)
    f = pl.pallas_call(
        functools.partial(_moe_kernel, nsub=MOE_SUBTILES),
        grid=(B, n_rows // tm, N_EXPERTS),
        in_specs=[
            pl.BlockSpec((1, tm, D), row),
            pl.BlockSpec((1, tm, LANES), row),
            pl.BlockSpec(rb.shape, lambda b, t, e: (0, 0)),
            pl.BlockSpec((1, 1, D, 2 * D_EXPERT), lambda b, t, e: (l, e, 0, 0)),
            pl.BlockSpec((1, 1, D_EXPERT, D), lambda b, t, e: (l, e, 0, 0)),
        ],
        out_specs=pl.BlockSpec((1, tm, D), row),
        out_shape=jax.ShapeDtypeStruct((B, n_rows, D), BF16),
        scratch_shapes=[
            pltpu.SMEM((MOE_SUBTILES * N_GROUPS,), jnp.int32), pltpu.SMEM((MOE_SUBTILES * N_GROUPS,), jnp.int32),
            pltpu.VMEM((MOE_SUBTILES, n_sorted, sub), BF16),
            pltpu.VMEM((MOE_SUBTILES, n_sorted, D), BF16),
            pltpu.VMEM((MOE_SUBTILES, n_sorted, LANES), F32),
            pltpu.VMEM((MOE_SUBTILES, n_sorted, D), F32),
            pltpu.VMEM((sub, sub), BF16)],
        compiler_params=pltpu.CompilerParams(
            dimension_semantics=("arbitrary", "arbitrary", "arbitrary"), vmem_limit_bytes=VMEM_LIMIT),
        name="moe",
    )(h2, logits, rb, w13, w2)
    rows = pl.BlockSpec((1, tm, D), lambda b, t: (b, t, 0))
    return pl.pallas_call(
        functools.partial(_moe_ln_kernel, S=S, alpha=alpha),
        grid=(B, n_rows // tm),
        in_specs=[rows, rows,
                  pl.BlockSpec((1, 1, 1, 1, D), lambda b, t: (l, b, 5, 0, 0)),
                  pl.BlockSpec((1, 1, 1, 1, D), lambda b, t: (l, B, 5, 0, 0)),
                  wspec(ln_g), wspec(ln_b)],
        out_specs=rows,
        out_shape=jax.ShapeDtypeStruct((B, n_rows, D), F32),
        compiler_params=pltpu.CompilerParams(
            dimension_semantics=("arbitrary", "arbitrary"), vmem_limit_bytes=VMEM_LIMIT),
        name="moe_ln",
    )(x1, f, modr, modr, ln_g, ln_b)


def _rope_tables(S, C, rot_dim):
    rows = S // GRID_W
    r = jnp.repeat(jnp.arange(rows, dtype=F32), GRID_W)
    col = jnp.tile(jnp.arange(GRID_W, dtype=F32), rows)
    n_freq = rot_dim // 4
    freqs = ROPE_BASE ** (-jnp.arange(n_freq, dtype=F32) / n_freq)
    ang = jnp.concatenate([r[:, None] * freqs, col[:, None] * freqs], -1)
    cos, sin = jnp.cos(ang), jnp.sin(ang)
    cos_rep = jnp.repeat(cos, 2, axis=-1)
    sin_alt = jnp.stack([-sin, sin], -1).reshape(S, rot_dim)
    cos_rep = jnp.concatenate([cos_rep, jnp.ones((C, rot_dim), F32)], 0)
    sin_alt = jnp.concatenate([sin_alt, jnp.zeros((C, rot_dim), F32)], 0)
    return cos_rep.T, sin_alt.T


def kernel(x, c, ctx, c_ctx, w_ada, b_ada, w_in, da_lambda, da_subln, mla_q_norm, mla_w_uq, mla_kv_norm, mla_w_ukv, ret_decay_f, ret_decay_b, w_out, ln1_g, ln1_b, router_w, router_b, exp_w1, exp_w3, exp_w2, ln2_g, ln2_b):
    B, S, D = x.shape
    C = ctx.shape[1]
    depth = w_in.shape[0]
    alpha = float((2 * depth) ** 0.25)

    cvec = jnp.concatenate([c, c_ctx[None, :], jnp.zeros((8 - B - 1, D), F32)], 0)
    mod = _modulation(cvec, w_ada, b_ada)
    modr = mod.reshape(depth, 8, N_MOD, 1, D)

    wT_all = jnp.swapaxes(w_in, 1, 2).astype(BF16)
    wuqT = jnp.swapaxes(mla_w_uq, 1, 2).astype(BF16)
    ukv = mla_w_ukv.reshape(depth, MLA_KV_RANK, MLA_HEADS, MLA_NOPE + MLA_VDIM)
    wvT = jnp.swapaxes(ukv[..., MLA_NOPE:].reshape(depth, MLA_KV_RANK, MLA_VW), 1, 2).astype(BF16)
    nopeT = jnp.transpose(ukv[..., :MLA_NOPE], (0, 2, 3, 1))
    top = jnp.concatenate([nopeT, jnp.zeros((depth, MLA_HEADS, MLA_NOPE, MLA_ROPE), F32)], -1)
    mid = jnp.concatenate([jnp.zeros((MLA_ROPE, MLA_KV_RANK), F32), jnp.eye(MLA_ROPE, dtype=F32)], -1)
    mid = jnp.broadcast_to(mid, (depth, MLA_HEADS, MLA_ROPE, MLA_KV_RANK + MLA_ROPE))
    bot = jnp.zeros((depth, MLA_HEADS, LANES - MLA_NOPE - MLA_ROPE, MLA_KV_RANK + MLA_ROPE), F32)
    wkT = jnp.concatenate([top, mid, bot], 2).reshape(depth, MLA_HEADS * LANES, MLA_KV_RANK + MLA_ROPE).astype(BF16)
    gq = mla_q_norm[:, :, None]
    gkv = mla_kv_norm[:, :, None]
    subln = da_subln[:, :, None]
    w_out_b = w_out.astype(BF16)
    w13 = jnp.concatenate([exp_w1, exp_w3], -1).astype(BF16)
    w2 = exp_w2.astype(BF16)
    rw_f = jnp.concatenate([router_w, jnp.zeros((D, LANES - N_EXPERTS), F32)], -1)
    rw_hi = rw_f.astype(BF16)
    rw_pad = jnp.concatenate([rw_hi, (rw_f - rw_hi.astype(F32)).astype(BF16)], -1)
    rb = router_b[:, None]
    dec = jnp.broadcast_to(jnp.stack([ret_decay_f, ret_decay_b], 1)[:, :, :, None, None],
                           (depth, 2, RET_HEADS, 8, LANES))
    tabs = _rope_tables(S, C, DA_DIM) + _rope_tables(S, C, RET_DIM)
    ln1g, ln1b, ln2g, ln2b = (a[:, None, :] for a in (ln1_g, ln1_b, ln2_g, ln2_b))

    xc = jnp.concatenate([x, ctx], 1)
    for l in range(depth):
        last = l == depth - 1
        (daqr, daqf, dak, dav, mqr, mqf, mk, mv, rq, rk, rv, rg) = _in_proj(
            l, xc, modr, wT_all, tabs, gq, wuqT, gkv, wvT, wkT, S)
        yda = _attention("da", l, daqr, daqf, dak, dav, S, C, not last, lam=da_lambda, g=subln)
        ymla = _attention("mla", l, mqr, mqf, mk, mv, S, C, not last)
        yret = _retention(rq, rk, rv, rg, dec[l], S, C)
        x1, h2, logits = _out_proj(l, yda, ymla, yret, w_out_b, xc, modr, ln1g, ln1b, rw_pad, S, alpha, last)
        xc = _moe(l, h2, logits, rb, w13, w2, x1, modr, ln2g, ln2b, S, alpha, last)
    return xc
```

```python
import functools
import math

import jax
import jax.numpy as jnp
import numpy as np
from jax import lax
from jax.experimental import pallas as pl
from jax.experimental.pallas import tpu as pltpu

F32 = jnp.float32
BF16 = jnp.bfloat16

GRID_W = 64
DA_HEADS, DA_DIM, DA_VDIM = 6, 32, 64
MLA_HEADS, MLA_NOPE, MLA_ROPE, MLA_VDIM = 6, 64, 32, 64
MLA_Q_RANK, MLA_KV_RANK = 256, 128
RET_HEADS, RET_DIM, RET_CHUNK = 4, 64, 128
ROPE_BASE = 10000.0
N_EXPERTS, N_GROUPS, D_EXPERT = 16, 4, 512
ROUTED_SCALE = 1.0
N_MOD = 6
LN_EPS = 1e-5
RMS_EPS = 1e-6
LOG2E = math.log2(math.e)

DA_W = DA_HEADS * 2 * DA_DIM
DA_VW = DA_HEADS * DA_VDIM
RET_W = RET_HEADS * RET_DIM
MLA_QW = MLA_HEADS * (MLA_NOPE + MLA_ROPE)
MLA_VW = MLA_HEADS * MLA_VDIM
MLA_QD = MLA_NOPE + MLA_ROPE
DA_MAPS = 2 * DA_HEADS
V_DIM = DA_VDIM
assert MLA_VDIM == V_DIM
IN_SPLIT = (DA_W, DA_W, DA_VW, MLA_Q_RANK, MLA_KV_RANK, MLA_ROPE, RET_W, RET_W, RET_W, RET_W)

LANES = 128
BF16_ROWS = 16
VMEM_LIMIT = 56 * 1024 * 1024
V_ROWS = V_DIM + BF16_ROWS
TM = 256
ATTN_N = 512
TK = 2048
ATTN_SUBTILES = 4
ATTN_NBUF = 2
ATTN_MAX_JUMP = 32.0
ATTN_UNROLL = 2
TM_MOE = 640
TM_MOE_LAST = 512
MOE_SUBTILES = 2
MOE_BLOCK = 256
MOE_ALIGN = 32
RET_UNROLL = 8

NT_DIMS = (((1,), (1,)), ((), ()))
TN_DIMS = (((0,), (0,)), ((), ()))


def _sigmoid(x):
    return 1.0 / (1.0 + jnp.exp(-x))


def _mod_kernel(c_ref, w_ref, b_ref, o_ref):
    c = c_ref[...]
    sc = (c * _sigmoid(c)).astype(BF16)
    o_ref[0] = jnp.dot(sc, w_ref[0].astype(BF16), preferred_element_type=F32) + b_ref[0]


def _modulation(cvec, w_ada, b_ada):
    depth, d, n = w_ada.shape
    tn = n // 4
    return pl.pallas_call(
        _mod_kernel,
        grid=(depth, n // tn),
        in_specs=[
            pl.BlockSpec((8, d), lambda l, j: (0, 0)),
            pl.BlockSpec((1, d, tn), lambda l, j: (l, 0, j)),
            pl.BlockSpec((1, 1, tn), lambda l, j: (l, 0, j)),
        ],
        out_specs=pl.BlockSpec((1, 8, tn), lambda l, j: (l, 0, j)),
        out_shape=jax.ShapeDtypeStruct((depth, 8, n), F32),
        name="adaln_mod",
    )(cvec, w_ada, b_ada.reshape(depth, 1, n))


_O_DAQ, _O_DAK, _O_DAV, _O_CQ, _O_CKV, _O_KPE, _O_RQ, _O_RK, _O_RV, _O_RG = (
    int(v) for v in np.cumsum((0,) + IN_SPLIT)[:-1])
N_ZROWS = sum(IN_SPLIT)


def _rms_rows(x):
    return x * lax.rsqrt(jnp.mean(x * x, axis=0, keepdims=True) + RMS_EPS)


def _pairswap_rows(x):
    n = x.shape[0]
    even = (lax.broadcasted_iota(jnp.int32, x.shape, 0) & 1) == 0
    return jnp.where(even, pltpu.roll(x, n - 1, 0), pltpu.roll(x, 1, 0))


def _in_kernel(x_ref, sh_ref, sc_ref, wT_ref, cosA_ref, sinA_ref, cosR_ref, sinR_ref,
               gq_ref, wuqT_ref, gkv_ref, wvT_ref, wkT_ref,
               daqr, daqf, dak, dav, mqr, mqf, mk, mv, rq, rk, rv, rg, z_ref, *, cq_da, cq_mla, kscale):
    tm = x_ref.shape[1]
    x = x_ref[0]
    h = (x * (1.0 + sc_ref[0, 0, 0]) + sh_ref[0, 0, 0]).astype(BF16)
    z_ref[...] = lax.dot_general(wT_ref[0], h, NT_DIMS, preferred_element_type=F32)
    cosA = cosA_ref[...]
    sinA = sinA_ref[...]
    cosR = cosR_ref[...]
    sinR = sinR_ref[...]
    ones_rows = (lax.broadcasted_iota(jnp.int32, (DA_HEADS, V_ROWS - V_DIM, tm), 1) == 0).astype(BF16)

    def rows(off, n):
        return z_ref[off:off + n, :]

    q2 = rows(_O_DAQ, DA_W)
    q = q2.reshape(DA_MAPS, DA_DIM, tm)
    qs = _pairswap_rows(q2).reshape(DA_MAPS, DA_DIM, tm)
    daqr[0] = ((q * cosA + qs * sinA) * cq_da).astype(BF16)
    daqf[0] = (q * cq_da).astype(BF16)
    k2 = rows(_O_DAK, DA_W)
    k = k2.reshape(DA_MAPS, DA_DIM, tm)
    ks = _pairswap_rows(k2).reshape(DA_MAPS, DA_DIM, tm)
    kr = (k * cosA + ks * sinA).reshape(DA_HEADS, 2 * DA_DIM, tm)
    kp = jnp.concatenate([kr, jnp.zeros_like(kr)], axis=1).reshape(DA_HEADS * LANES, tm)
    dak[0] = kp.T.astype(BF16)
    dav[0, :, 0:V_DIM, :] = rows(_O_DAV, DA_VW).reshape(DA_HEADS, DA_VDIM, tm).astype(BF16)
    dav[0, :, V_DIM:V_ROWS, :] = ones_rows

    cqn = (_rms_rows(rows(_O_CQ, MLA_Q_RANK)) * gq_ref[0]).astype(BF16)
    qm = jnp.dot(wuqT_ref[0], cqn, preferred_element_type=F32)
    qh = qm.reshape(MLA_HEADS, MLA_NOPE + MLA_ROPE, tm)
    qsw = _pairswap_rows(qm).reshape(MLA_HEADS, MLA_NOPE + MLA_ROPE, tm)[:, MLA_NOPE:, :]
    rot = qh[:, MLA_NOPE:, :] * cosA + qsw * sinA
    mqr[0, :, 0:MLA_NOPE, :] = (qh[:, 0:MLA_NOPE, :] * cq_mla).astype(BF16)
    mqr[0, :, MLA_NOPE:, :] = (rot * cq_mla).astype(BF16)
    mqf[0] = (qh * cq_mla).astype(BF16)
    ckvn = (_rms_rows(rows(_O_CKV, MLA_KV_RANK)) * gkv_ref[0]).astype(BF16)
    vm = jnp.dot(wvT_ref[0], ckvn, preferred_element_type=F32)
    mv[0, :, 0:V_DIM, :] = vm.reshape(MLA_HEADS, MLA_VDIM, tm).astype(BF16)
    mv[0, :, V_DIM:V_ROWS, :] = ones_rows
    kpe = rows(_O_KPE, MLA_ROPE)
    kper = kpe * cosA + _pairswap_rows(kpe) * sinA
    kin = jnp.concatenate([ckvn, kper.astype(BF16)], axis=0)
    kmT = jnp.dot(wkT_ref[0], kin, preferred_element_type=F32)
    mk[0] = kmT.T.astype(BF16)

    r_q2 = rows(_O_RQ, RET_W)
    r_q = r_q2.reshape(RET_HEADS, RET_DIM, tm)
    r_qs = _pairswap_rows(r_q2).reshape(RET_HEADS, RET_DIM, tm)
    rq[0] = (r_q * cosR + r_qs * sinR).astype(BF16)
    r_k2 = rows(_O_RK, RET_W)
    r_k = r_k2.reshape(RET_HEADS, RET_DIM, tm)
    r_ks = _pairswap_rows(r_k2).reshape(RET_HEADS, RET_DIM, tm)
    rkr = (r_k * cosR + r_ks * sinR) * kscale
    rkp = jnp.concatenate([rkr, jnp.zeros_like(rkr)], axis=1).reshape(RET_HEADS * LANES, tm)
    rk[0] = rkp.T.astype(BF16)
    rv[0] = rows(_O_RV, RET_W).reshape(RET_HEADS, RET_DIM, tm).astype(BF16)
    rg[0] = rows(_O_RG, RET_W)


def _in_proj(l, xc, modr, wT_all, tabs, gq, wuqT, gkv, wvT, wkT, S):
    B, Sp, D = xc.shape
    nt = Sp // TM
    n_lat_tiles = S // TM
    cosA, sinA, cosR, sinR = tabs

    def mod_spec(j):
        return pl.BlockSpec((1, 1, 1, 1, D), lambda b, t: (l, jnp.where(t >= n_lat_tiles, B, b), j, 0, 0))

    def wspec(a):
        return pl.BlockSpec((1,) + a.shape[1:], lambda b, t: (l,) + (0,) * (a.ndim - 1))

    def fm4(h, d):
        return pl.BlockSpec((1, h, d, TM), lambda b, t: (b, 0, 0, t))

    def tmaj(w):
        return pl.BlockSpec((1, TM, w), lambda b, t: (b, t, 0))

    kern = functools.partial(
        _in_kernel,
        cq_da=float(DA_DIM ** -0.5 * LOG2E),
        cq_mla=float((MLA_NOPE + MLA_ROPE) ** -0.5 * LOG2E),
        kscale=float(RET_DIM ** -0.5),
    )
    sds = jax.ShapeDtypeStruct
    return pl.pallas_call(
        kern,
        grid=(B, nt),
        in_specs=[
            pl.BlockSpec((1, TM, D), lambda b, t: (b, t, 0)),
            mod_spec(0), mod_spec(1),
            wspec(wT_all),
            pl.BlockSpec((DA_DIM, TM), lambda b, t: (0, t)),
            pl.BlockSpec((DA_DIM, TM), lambda b, t: (0, t)),
            pl.BlockSpec((RET_DIM, TM), lambda b, t: (0, t)),
            pl.BlockSpec((RET_DIM, TM), lambda b, t: (0, t)),
            wspec(gq), wspec(wuqT), wspec(gkv), wspec(wvT), wspec(wkT),
        ],
        out_specs=[
            fm4(DA_MAPS, DA_DIM), fm4(DA_MAPS, DA_DIM), tmaj(DA_HEADS * LANES), fm4(DA_HEADS, V_ROWS),
            fm4(MLA_HEADS, MLA_QD), fm4(MLA_HEADS, MLA_QD), tmaj(MLA_HEADS * LANES), fm4(MLA_HEADS, V_ROWS),
            fm4(RET_HEADS, RET_DIM), tmaj(RET_HEADS * LANES), fm4(RET_HEADS, RET_DIM),
            pl.BlockSpec((1, RET_W, TM), lambda b, t: (b, 0, t)),
        ],
        out_shape=[
            sds((B, DA_MAPS, DA_DIM, Sp), BF16), sds((B, DA_MAPS, DA_DIM, Sp), BF16),
            sds((B, Sp, DA_HEADS * LANES), BF16), sds((B, DA_HEADS, V_ROWS, Sp), BF16),
            sds((B, MLA_HEADS, MLA_QD, Sp), BF16), sds((B, MLA_HEADS, MLA_QD, Sp), BF16),
            sds((B, Sp, MLA_HEADS * LANES), BF16), sds((B, MLA_HEADS, V_ROWS, Sp), BF16),
            sds((B, RET_HEADS, RET_DIM, Sp), BF16), sds((B, Sp, RET_HEADS * LANES), BF16),
            sds((B, RET_HEADS, RET_DIM, Sp), BF16), sds((B, RET_W, Sp), F32),
        ],
        scratch_shapes=[pltpu.VMEM((N_ZROWS, TM), F32)],
        compiler_params=pltpu.CompilerParams(
            dimension_semantics=("arbitrary", "arbitrary"), vmem_limit_bytes=VMEM_LIMIT),
        name="in_proj",
    )(xc, modr, modr, wT_all, cosA, sinA, cosR, sinR, gq, wuqT, gkv, wvT, wkT)


def _attn_kernel(*refs, mode, S, C, lam_init, ctx_only, nsub):
    refs = list(refs)
    qr_ref, qf_ref, k_ref, v_ref = refs[:4]
    del refs[:4]
    if mode == "da":
        lam_ref, g_ref = refs[:2]
        del refs[:2]
    if ctx_only:
        del refs[:1]
    o_ref, m_sc, acc_sc, q_sc, jump_sc = refs[:5]
    nb = ATTN_NBUF
    pbuf, albuf = (refs[5 + i * nb: 5 + (i + 1) * nb] for i in range(2))
    tq = o_ref.shape[2] // nsub
    n_lat = S // TK

    def qpad(ref, t):
        cols = slice(t * tq, (t + 1) * tq)
        if mode == "da":
            z = jnp.zeros((DA_DIM, tq), BF16)
            c1 = jnp.concatenate([ref[0, 0, :, cols], z, z, z], axis=0)
            c2 = jnp.concatenate([z, ref[0, 1, :, cols], z, z], axis=0)
            return jnp.concatenate([c1, c2], axis=1)
        return jnp.concatenate([ref[0, 0, :, cols], jnp.zeros((LANES - MLA_QD, tq), BF16)], axis=0)

    def chunk(j):
        return pl.ds(j * TK if isinstance(j, int) else pl.multiple_of(j * TK, TK), TK)

    def context_keys(t):
        c0 = 0 if ctx_only else S
        s = jnp.dot(k_ref[0, c0:c0 + C, :], qpad(qf_ref, t), preferred_element_type=F32)
        m_c = jnp.max(s, axis=0, keepdims=True)
        m_sc[t] = m_c
        acc_sc[t] = jnp.dot(v_ref[0, 0, :, c0:c0 + C], jnp.exp2(s - m_c).astype(BF16), preferred_element_type=F32)

    for t in range(nsub):
        context_keys(t)

    if not ctx_only:
        for t in range(nsub):
            q_sc[t] = qpad(qr_ref, t)
        jump_sc[...] = jnp.full(jump_sc.shape, -jnp.inf, F32)

        def probs(t, j, r):
            s = jnp.dot(k_ref[0, chunk(j), :], q_sc[t], preferred_element_type=F32)
            m_prev = m_sc[t]
            pbuf[r][t] = jnp.exp2(s - m_prev).astype(BF16)
            m_chunk = jnp.max(s, axis=0, keepdims=True)
            m_new = jnp.maximum(m_prev, m_chunk)
            jump_sc[t] = jnp.maximum(jump_sc[t], m_chunk - m_prev)
            albuf[r][t] = jnp.exp2(m_prev - m_new)
            m_sc[t] = m_new

        def accumulate(t, j, r):
            pv = jnp.dot(v_ref[0, 0, :, chunk(j)], pbuf[r][t], preferred_element_type=F32)
            acc_sc[t] = (acc_sc[t] + pv) * albuf[r][t]

        def step(j, r):
            for t in range(nsub):
                accumulate(t, j, r)
                probs(t, j + 1, (r + 1) % nb)

        for t in range(nsub):
            probs(t, 0, 0)
        n_steps = n_lat - 1
        trips = n_steps // ATTN_UNROLL

        def body(i, carry):
            for u in range(ATTN_UNROLL):
                step(i * ATTN_UNROLL + u, u % nb)
            return carry

        lax.fori_loop(0, trips, body, 0)
        for j in range(trips * ATTN_UNROLL, n_steps):
            step(j, j % nb)
        for t in range(nsub):
            accumulate(t, n_lat - 1, (n_lat - 1) % nb)

        @pl.when(jnp.max(jump_sc[...]) > ATTN_MAX_JUMP)
        def _():
            for t in range(nsub):
                context_keys(t)

                def exact_step(j, carry, t=t):
                    s = jnp.dot(k_ref[0, chunk(j), :], q_sc[t], preferred_element_type=F32)
                    m_old = m_sc[t]
                    m_new = jnp.maximum(m_old, jnp.max(s, axis=0, keepdims=True))
                    pv = jnp.dot(v_ref[0, 0, :, chunk(j)], jnp.exp2(s - m_new).astype(BF16),
                                 preferred_element_type=F32)
                    acc_sc[t] = acc_sc[t] * jnp.exp2(m_old - m_new) + pv
                    m_sc[t] = m_new
                    return carry

                lax.fori_loop(0, n_lat, exact_step, 0)

    for t in range(nsub):
        acc = acc_sc[t]
        o = acc[0:V_DIM, :] / acc[V_DIM:V_DIM + 1, :]
        cols = slice(t * tq, (t + 1) * tq)
        if mode == "da":
            lf = lam_ref[0]
            lam = (jnp.exp(jnp.sum(lf[0:1] * lf[1:2], axis=1, keepdims=True))
                   - jnp.exp(jnp.sum(lf[2:3] * lf[3:4], axis=1, keepdims=True)) + lam_init)
            od = o[:, 0:tq] - lam * o[:, tq:2 * tq]
            y = _rms_rows(od) * g_ref[0]
            o_ref[0, :, cols] = (y * (1.0 - lam_init)).astype(BF16)
        else:
            o_ref[0, :, cols] = o.astype(BF16)


def _attention_call(mode, l, qr, qf, k, v, S, C, lam, g, y_latent):
    B, nmaps, d, Sp = qr.shape
    H = v.shape[1]
    nm = nmaps // H
    ctx_only = y_latent is not None
    nsub = 1 if ctx_only else ATTN_SUBTILES
    tq = C if ctx_only else ATTN_N // nm
    n = nm * tq
    tqb = nsub * tq
    q0 = S // tqb if ctx_only else 0
    lam_init = 0.8 - 0.6 * math.exp(-0.3 * l)
    kern = functools.partial(_attn_kernel, mode=mode, S=S, C=C, lam_init=lam_init, ctx_only=ctx_only, nsub=nsub)
    in_specs = [
        pl.BlockSpec((1, nm, d, tqb), lambda b, h, i: (b, h, 0, q0 + i)),
        pl.BlockSpec((1, nm, d, tqb), lambda b, h, i: (b, h, 0, q0 + i)),
        (pl.BlockSpec((1, C, LANES), lambda b, h, i: (b, S // C, h)) if ctx_only
         else pl.BlockSpec((1, Sp, LANES), lambda b, h, i: (b, 0, h))),
        (pl.BlockSpec((1, 1, V_ROWS, C), lambda b, h, i: (b, h, 0, S // C)) if ctx_only
         else pl.BlockSpec((1, 1, V_ROWS, Sp), lambda b, h, i: (b, h, 0, 0))),
    ]
    args = [qr, qf, k, v]
    if mode == "da":
        in_specs += [pl.BlockSpec((1,) + lam.shape[1:], lambda b, h, i: (l, 0, 0)),
                     pl.BlockSpec((1,) + g.shape[1:], lambda b, h, i: (l, 0, 0))]
        args += [lam, g]
    aliases = {}
    if ctx_only:
        in_specs.append(pl.BlockSpec(memory_space=pl.ANY))
        args.append(y_latent)
        aliases = {len(args) - 1: 0}
    tk = C if ctx_only else TK
    return pl.pallas_call(
        kern,
        grid=(B, H, 1 if ctx_only else S // tqb),
        in_specs=in_specs,
        out_specs=pl.BlockSpec((1, V_DIM, tqb), lambda b, h, i: (b, h, q0 + i)),
        out_shape=jax.ShapeDtypeStruct((B, H * V_DIM, Sp), BF16),
        input_output_aliases=aliases,
        scratch_shapes=(
            [pltpu.VMEM((nsub, 1, n), F32), pltpu.VMEM((nsub, V_ROWS, n), F32), pltpu.VMEM((nsub, LANES, n), BF16),
             pltpu.VMEM((nsub, 1, n), F32)]
            + [pltpu.VMEM((nsub, tk, n), BF16)] * ATTN_NBUF + [pltpu.VMEM((nsub, 1, n), F32)] * ATTN_NBUF),
        compiler_params=pltpu.CompilerParams(
            dimension_semantics=("arbitrary", "arbitrary", "arbitrary"), vmem_limit_bytes=VMEM_LIMIT),
        name="attn_" + mode + ("_ctx" if ctx_only else ""),
    )(*args)


def _attention(mode, l, qr, qf, k, v, S, C, with_ctx, lam=None, g=None):
    y = _attention_call(mode, l, qr, qf, k, v, S, C, lam, g, None)
    return _attention_call(mode, l, qr, qf, k, v, S, C, lam, g, y) if with_ctx else y


def _ret_kernel(q_ref, k_ref, v_ref, g_ref, dec_ref, o_ref, sb_sc, *, S, C):
    CH = RET_CHUNK
    nl, nc = S // CH, C // CH

    def log_sigmoid(x):
        return jnp.minimum(x, 0.0) - jnp.log1p(jnp.exp(-jnp.abs(x)))

    lgf = log_sigmoid(dec_ref[0, 0][0:1, :])
    lgb = log_sigmoid(dec_ref[1, 0][0:1, :])
    ii = lax.broadcasted_iota(jnp.int32, (CH, CH), 1).astype(F32)
    jj = lax.broadcasted_iota(jnp.int32, (CH, CH), 0).astype(F32)
    dd = ii - jj
    fwd = dd >= 0
    mt = (jnp.where(fwd, jnp.exp(lgf * jnp.where(fwd, dd, 0.0)), 0.0)
          + jnp.where(fwd, 0.0, jnp.exp(lgb * jnp.where(fwd, 0.0, -dd))))
    xi_f = jnp.exp(lgf * (ii + 1.0))
    xi_b = jnp.exp(lgb * (CH - ii))
    zeta_f = jnp.exp(lgf * (CH - 1.0 - jj))
    zeta_b = jnp.exp(lgb * jj)
    cdf = jnp.exp(lgf * CH)
    cdb = jnp.exp(lgb * CH)
    zq = jnp.zeros((RET_DIM, CH), BF16)

    def sl(c):
        return pl.ds(pl.multiple_of(c * CH, CH), CH)

    def state_update(c, zeta):
        kz = (k_ref[0, sl(c), :].astype(F32) * zeta).astype(BF16)
        return jnp.dot(v_ref[0, 0, :, sl(c)], kz, preferred_element_type=F32)

    def bwd_step(c, sb):
        sb_sc[c] = sb
        return sb * cdb + state_update(c, zeta_b)

    sb = jnp.zeros((RET_DIM, CH), F32)
    for c in range(nl + nc - 1, nl - 1, -1):
        sb = bwd_step(c, sb)
    lax.fori_loop(0, nl, lambda n, s: bwd_step(nl - 1 - n, s), sb, unroll=RET_UNROLL)

    def fwd_step(c, sf):
        qc = q_ref[0, 0, :, sl(c)]
        qp = jnp.concatenate([qc, zq], axis=0)
        kc = k_ref[0, sl(c), :]
        vc = v_ref[0, 0, :, sl(c)]
        at = jnp.dot(kc, qp, preferred_element_type=F32)
        qpf = qp.astype(F32)
        rhs = jnp.concatenate([(at * mt).astype(BF16), (qpf * xi_f).astype(BF16), (qpf * xi_b).astype(BF16)], axis=0)
        lhs = jnp.concatenate([vc, sf.astype(BF16), sb_sc[c].astype(BF16)], axis=1)
        o = jnp.dot(lhs, rhs, preferred_element_type=F32)
        gch = g_ref[0, :, sl(c)]
        o_ref[0, :, sl(c)] = (gch * _sigmoid(gch) * _rms_rows(o)).astype(BF16)
        return sf * cdf + state_update(c, zeta_f)

    sf = jnp.zeros((RET_DIM, CH), F32)
    for c in range(nl, nl + nc):
        sf = fwd_step(c, sf)
    lax.fori_loop(0, nl, fwd_step, sf, unroll=RET_UNROLL)


def _retention(rq, rk, rv, rg, dec, S, C):
    B, H, d, Sp = rq.shape
    kern = functools.partial(_ret_kernel, S=S, C=C)
    return pl.pallas_call(
        kern,
        grid=(B, H),
        in_specs=[
            pl.BlockSpec((1, 1, d, Sp), lambda b, h: (b, h, 0, 0)),
            pl.BlockSpec((1, Sp, LANES), lambda b, h: (b, 0, h)),
            pl.BlockSpec((1, 1, d, Sp), lambda b, h: (b, h, 0, 0)),
            pl.BlockSpec((1, d, Sp), lambda b, h: (b, h, 0)),
            pl.BlockSpec((2, 1, 8, LANES), lambda b, h: (0, h, 0, 0)),
        ],
        out_specs=pl.BlockSpec((1, d, Sp), lambda b, h: (b, h, 0)),
        out_shape=jax.ShapeDtypeStruct((B, H * d, Sp), BF16),
        scratch_shapes=[pltpu.VMEM((Sp // RET_CHUNK, RET_DIM, RET_CHUNK), F32)],
        compiler_params=pltpu.CompilerParams(
            dimension_semantics=("arbitrary", "arbitrary"), vmem_limit_bytes=VMEM_LIMIT),
        name="retention",
    )(rq, rk, rv, rg, dec)


def _layer_norm(t, g, b):
    mu = jnp.mean(t, axis=-1, keepdims=True)
    tc = t - mu
    var = jnp.mean(tc * tc, axis=-1, keepdims=True)
    return tc * lax.rsqrt(var + LN_EPS) * g + b


def _out_kernel(yda_ref, ymla_ref, yret_ref, w_ref, x_ref, gate_ref, g1_ref, b1_ref, sh2_ref, sc2_ref, rw_ref,
                x1_ref, h2_ref, lg_ref, *, alpha):
    proj = (lax.dot_general(yda_ref[0], w_ref[0, 0:DA_VW, :], TN_DIMS, preferred_element_type=F32)
            + lax.dot_general(ymla_ref[0], w_ref[0, DA_VW:DA_VW + MLA_VW, :], TN_DIMS, preferred_element_type=F32)
            + lax.dot_general(yret_ref[0], w_ref[0, DA_VW + MLA_VW:, :], TN_DIMS, preferred_element_type=F32))
    x1 = _layer_norm(alpha * x_ref[0] + gate_ref[0, 0, 0] * proj, g1_ref[0], b1_ref[0])
    x1_ref[0] = x1
    h2 = x1 * (1.0 + sc2_ref[0, 0, 0]) + sh2_ref[0, 0, 0]
    h2_ref[0] = h2.astype(BF16)
    h2_hi = h2.astype(BF16)
    h2_lo = (h2 - h2_hi.astype(F32)).astype(BF16)
    hi = jnp.dot(h2_hi, rw_ref[...], preferred_element_type=F32)
    lo = jnp.dot(h2_lo, rw_ref[:, 0:LANES], preferred_element_type=F32)
    lg_ref[0] = hi[:, 0:LANES] + (hi[:, LANES:] + lo)


def _out_proj(l, yda, ymla, yret, w_out, xc, modr, ln_g, ln_b, rw_pad, S, alpha, latents_only):
    B, Sp, D = xc.shape
    n_lat_tiles = S // TM
    n_rows = S if latents_only else Sp

    def mod_spec(j):
        return pl.BlockSpec((1, 1, 1, 1, D), lambda b, t: (l, jnp.where(t >= n_lat_tiles, B, b), j, 0, 0))

    def fm(a):
        return pl.BlockSpec((1, a.shape[1], TM), lambda b, t: (b, 0, t))

    def wspec(a):
        return pl.BlockSpec((1,) + a.shape[1:], lambda b, t: (l,) + (0,) * (a.ndim - 1))

    row = pl.BlockSpec((1, TM, D), lambda b, t: (b, t, 0))
    sds = jax.ShapeDtypeStruct
    return pl.pallas_call(
        functools.partial(_out_kernel, alpha=alpha),
        grid=(B, n_rows // TM),
        in_specs=[fm(yda), fm(ymla), fm(yret), wspec(w_out), row, mod_spec(2), wspec(ln_g), wspec(ln_b),
                  mod_spec(3), mod_spec(4), pl.BlockSpec(rw_pad.shape, lambda b, t: (0, 0))],
        out_specs=[row, row, pl.BlockSpec((1, TM, LANES), lambda b, t: (b, t, 0))],
        out_shape=[sds((B, n_rows, D), F32), sds((B, n_rows, D), BF16), sds((B, n_rows, LANES), F32)],
        compiler_params=pltpu.CompilerParams(dimension_semantics=("arbitrary", "arbitrary")),
        name="out_proj",
    )(yda, ymla, yret, w_out, xc, modr, ln_g, ln_b, modr, modr, rw_pad)


def _gates_T(logits_T, rb):
    s = _sigmoid(logits_T)
    ch = s + rb
    srow = [s[e:e + 1] for e in range(N_EXPERTS)]
    crow = [ch[e:e + 1] for e in range(N_EXPERTS)]
    per = N_EXPERTS // N_GROUPS
    gs = []
    for g in range(N_GROUPS):
        a, b, c, d = crow[per * g: per * g + per]
        m1, n1, m2, n2 = jnp.maximum(a, b), jnp.minimum(a, b), jnp.maximum(c, d), jnp.minimum(c, d)
        gs.append(jnp.maximum(m1, m2) + jnp.maximum(jnp.minimum(m1, m2), jnp.maximum(n1, n2)))
    gmax = functools.reduce(jnp.maximum, gs)
    taken = jnp.zeros(gmax.shape, jnp.bool_)
    gsel = []
    for g in range(N_GROUPS):
        sg = jnp.logical_and(gs[g] == gmax, jnp.logical_not(taken))
        gsel.append(sg)
        taken = jnp.logical_or(taken, sg)
    neg = jnp.full(gmax.shape, -jnp.inf, F32)
    mc = [jnp.where(gsel[e // per], crow[e], neg) for e in range(N_EXPERTS)]
    sel = [jnp.zeros(gmax.shape, jnp.bool_) for _ in range(N_EXPERTS)]
    for _ in range(2):
        top = functools.reduce(jnp.maximum, mc)
        taken = jnp.zeros(gmax.shape, jnp.bool_)
        for e in range(N_EXPERTS):
            hit = jnp.logical_and(mc[e] == top, jnp.logical_not(taken))
            taken = jnp.logical_or(taken, hit)
            sel[e] = jnp.logical_or(sel[e], hit)
            mc[e] = jnp.where(hit, neg, mc[e])
    w = [jnp.where(sel[e], srow[e], 0.0) for e in range(N_EXPERTS)]
    wsum = functools.reduce(lambda a, b: a + b, w)
    gates = jnp.concatenate([we / wsum * ROUTED_SCALE for we in w], axis=0)
    return gates, [jnp.where(sg, 1.0, 0.0) for sg in gsel]


def _moe_sorted_rows(tm):
    return tm + N_GROUPS * MOE_ALIGN + MOE_BLOCK


def _moe_kernel(h_ref, lg_ref, rb_ref, w13_ref, w2_ref, o_ref,
                off_sc, nblk_sc, perm_sc, hs_sc, gs_sc, accs_sc, earlier_sc, *, nsub):
    e = pl.program_id(2)
    tm = h_ref.shape[1] // nsub
    n_sorted = perm_sc.shape[1]
    per = N_EXPERTS // N_GROUPS
    lane = lax.broadcasted_iota(jnp.int32, (1, LANES), 1)

    def expert(x):
        ab = jnp.dot(x, w13_ref[0, 0], preferred_element_type=F32)
        a = ab[:, 0:D_EXPERT]
        hid = (a * _sigmoid(a) * ab[:, D_EXPERT:]).astype(BF16)
        return jnp.dot(hid, w2_ref[0, 0], preferred_element_type=F32)

    def gate_col(g):
        return jnp.sum(jnp.where(lane == e, g, 0.0), axis=1, keepdims=True)

    def sub(j):
        return slice(j * tm, (j + 1) * tm)

    @pl.when(jnp.logical_and(jnp.logical_and(pl.program_id(0) == 0, pl.program_id(1) == 0), e == 0))
    def _():
        before = lax.broadcasted_iota(jnp.int32, (tm, tm), 0) < lax.broadcasted_iota(jnp.int32, (tm, tm), 1)
        earlier_sc[...] = jnp.where(before, 1.0, 0.0).astype(BF16)

    @pl.when(e == 0)
    def _():
        for j in range(nsub):
            gT, gsel = _gates_T(lg_ref[0, sub(j), :].T[0:N_EXPERTS, :], rb_ref[...])
            gates = jnp.concatenate([gT, jnp.zeros((LANES - N_EXPERTS, tm), F32)], axis=0).T
            member = jnp.concatenate(gsel + [jnp.zeros((BF16_ROWS - N_GROUPS, tm), F32)], axis=0)
            rank = jnp.dot(member.astype(BF16), earlier_sc[...], preferred_element_type=F32)
            count = jnp.sum(member, axis=1, keepdims=True)
            off = jnp.int32(0)
            slot = jnp.zeros((1, tm), F32)
            for g in range(N_GROUPS):
                n_g = jnp.max(count[g:g + 1]).astype(jnp.int32)
                off_sc[j * N_GROUPS + g] = off
                nblk_sc[j * N_GROUPS + g] = (n_g + (MOE_BLOCK - 1)) // MOE_BLOCK
                slot = slot + gsel[g] * (rank[g:g + 1] + off.astype(F32))
                off = off + ((n_g + (MOE_ALIGN - 1)) // MOE_ALIGN) * MOE_ALIGN
            rows_i = lax.broadcasted_iota(jnp.int32, (n_sorted, tm), 0)
            perm = jnp.where(rows_i == slot.astype(jnp.int32), 1.0, 0.0).astype(BF16)
            perm_sc[j] = perm
            hs_sc[j] = jnp.dot(perm, h_ref[0, sub(j), :], preferred_element_type=F32).astype(BF16)
            g_hi = gates.astype(BF16)
            g_lo = (gates - g_hi.astype(F32)).astype(BF16)
            gs_sc[j] = (jnp.dot(perm, g_hi, preferred_element_type=F32)
                        + jnp.dot(perm, g_lo, preferred_element_type=F32))
            accs_sc[j] = jnp.zeros(accs_sc.shape[1:], F32)

    start = [off_sc[j * N_GROUPS + e // per] for j in range(nsub)]
    n_blocks = [nblk_sc[j * N_GROUPS + e // per] for j in range(nsub)]
    n_common = functools.reduce(jnp.minimum, n_blocks)

    def rows_of(j, k):
        return pl.ds(pl.multiple_of(start[j] + k * MOE_BLOCK, MOE_ALIGN), MOE_BLOCK)

    def together(k, carry):
        y = expert(jnp.concatenate([hs_sc[j, rows_of(j, k), :] for j in range(nsub)], axis=0))
        for j in range(nsub):
            rows = rows_of(j, k)
            accs_sc[j, rows, :] += gate_col(gs_sc[j, rows, :]) * y[j * MOE_BLOCK:(j + 1) * MOE_BLOCK]
        return carry

    lax.fori_loop(0, n_common, together, 0)
    for j in range(nsub):
        def alone(k, carry, j=j):
            rows = rows_of(j, k)
            accs_sc[j, rows, :] += gate_col(gs_sc[j, rows, :]) * expert(hs_sc[j, rows, :])
            return carry

        lax.fori_loop(n_common, n_blocks[j], alone, 0)

    @pl.when(e == N_EXPERTS - 1)
    def _():
        for j in range(nsub):
            o_ref[0, sub(j), :] = lax.dot_general(perm_sc[j], accs_sc[j].astype(BF16), TN_DIMS,
                                                  preferred_element_type=F32).astype(BF16)


def _moe_ln_kernel(x1_ref, f_ref, gb_ref, gc_ref, g2_ref, b2_ref, o_ref, *, S, alpha):
    tm = x1_ref.shape[1]
    n_lat = S - pl.program_id(1) * tm
    rowi = lax.broadcasted_iota(jnp.int32, (tm, 1), 0)
    gate = jnp.where(rowi < n_lat, gb_ref[0, 0, 0], gc_ref[0, 0, 0])
    o_ref[0] = _layer_norm(alpha * x1_ref[0] + gate * f_ref[0].astype(F32), g2_ref[0], b2_ref[0])


def _moe(l, h2, logits, rb, w13, w2, x1, modr, ln_g, ln_b, S, alpha, latents_only):
    B, n_rows, D = x1.shape
    sub = TM_MOE_LAST if latents_only else TM_MOE
    tm = MOE_SUBTILES * sub
    n_sorted = _moe_sorted_rows(sub)

    def wspec(a):
        return pl.BlockSpec((1,) + a.shape[1:], lambda b, t: (l,) + (0,) * (a.ndim - 1))

    row = lambda b, t, e: (b, t, 0)
    f = pl.pallas_call(
        functools.partial(_moe_kernel, nsub=MOE_SUBTILES),
        grid=(B, n_rows // tm, N_EXPERTS),
        in_specs=[
            pl.BlockSpec((1, tm, D), row),
            pl.BlockSpec((1, tm, LANES), row),
            pl.BlockSpec(rb.shape, lambda b, t, e: (0, 0)),
            pl.BlockSpec((1, 1, D, 2 * D_EXPERT), lambda b, t, e: (l, e, 0, 0)),
            pl.BlockSpec((1, 1, D_EXPERT, D), lambda b, t, e: (l, e, 0, 0)),
        ],
        out_specs=pl.BlockSpec((1, tm, D), row),
        out_shape=jax.ShapeDtypeStruct((B, n_rows, D), BF16),
        scratch_shapes=[
            pltpu.SMEM((MOE_SUBTILES * N_GROUPS,), jnp.int32), pltpu.SMEM((MOE_SUBTILES * N_GROUPS,), jnp.int32),
            pltpu.VMEM((MOE_SUBTILES, n_sorted, sub), BF16),
            pltpu.VMEM((MOE_SUBTILES, n_sorted, D), BF16),
            pltpu.VMEM((MOE_SUBTILES, n_sorted, LANES), F32),
            pltpu.VMEM((MOE_SUBTILES, n_sorted, D), F32),
            pltpu.VMEM((sub, sub), BF16)],
        compiler_params=pltpu.CompilerParams(
            dimension_semantics=("arbitrary", "arbitrary", "arbitrary"), vmem_limit_bytes=VMEM_LIMIT),
        name="moe",
    )(h2, logits, rb, w13, w2)
    rows = pl.BlockSpec((1, tm, D), lambda b, t: (b, t, 0))
    return pl.pallas_call(
        functools.partial(_moe_ln_kernel, S=S, alpha=alpha),
        grid=(B, n_rows // tm),
        in_specs=[rows, rows,
                  pl.BlockSpec((1, 1, 1, 1, D), lambda b, t: (l, b, 5, 0, 0)),
                  pl.BlockSpec((1, 1, 1, 1, D), lambda b, t: (l, B, 5, 0, 0)),
                  wspec(ln_g), wspec(ln_b)],
        out_specs=rows,
        out_shape=jax.ShapeDtypeStruct((B, n_rows, D), F32),
        compiler_params=pltpu.CompilerParams(
            dimension_semantics=("arbitrary", "arbitrary"), vmem_limit_bytes=VMEM_LIMIT),
        name="moe_ln",
    )(x1, f, modr, modr, ln_g, ln_b)


def _rope_tables(S, C, rot_dim):
    rows = S // GRID_W
    r = jnp.repeat(jnp.arange(rows, dtype=F32), GRID_W)
    col = jnp.tile(jnp.arange(GRID_W, dtype=F32), rows)
    n_freq = rot_dim // 4
    freqs = ROPE_BASE ** (-jnp.arange(n_freq, dtype=F32) / n_freq)
    ang = jnp.concatenate([r[:, None] * freqs, col[:, None] * freqs], -1)
    cos, sin = jnp.cos(ang), jnp.sin(ang)
    cos_rep = jnp.repeat(cos, 2, axis=-1)
    sin_alt = jnp.stack([-sin, sin], -1).reshape(S, rot_dim)
    cos_rep = jnp.concatenate([cos_rep, jnp.ones((C, rot_dim), F32)], 0)
    sin_alt = jnp.concatenate([sin_alt, jnp.zeros((C, rot_dim), F32)], 0)
    return cos_rep.T, sin_alt.T


def kernel(x, c, ctx, c_ctx, w_ada, b_ada, w_in, da_lambda, da_subln, mla_q_norm, mla_w_uq, mla_kv_norm, mla_w_ukv, ret_decay_f, ret_decay_b, w_out, ln1_g, ln1_b, router_w, router_b, exp_w1, exp_w3, exp_w2, ln2_g, ln2_b):
    B, S, D = x.shape
    C = ctx.shape[1]
    depth = w_in.shape[0]
    alpha = float((2 * depth) ** 0.25)

    cvec = jnp.concatenate([c, c_ctx[None, :], jnp.zeros((8 - B - 1, D), F32)], 0)
    mod = _modulation(cvec, w_ada, b_ada)
    modr = mod.reshape(depth, 8, N_MOD, 1, D)

    wT_all = jnp.swapaxes(w_in, 1, 2).astype(BF16)
    wuqT = jnp.swapaxes(mla_w_uq, 1, 2).astype(BF16)
    ukv = mla_w_ukv.reshape(depth, MLA_KV_RANK, MLA_HEADS, MLA_NOPE + MLA_VDIM)
    wvT = jnp.swapaxes(ukv[..., MLA_NOPE:].reshape(depth, MLA_KV_RANK, MLA_VW), 1, 2).astype(BF16)
    nopeT = jnp.transpose(ukv[..., :MLA_NOPE], (0, 2, 3, 1))
    top = jnp.concatenate([nopeT, jnp.zeros((depth, MLA_HEADS, MLA_NOPE, MLA_ROPE), F32)], -1)
    mid = jnp.concatenate([jnp.zeros((MLA_ROPE, MLA_KV_RANK), F32), jnp.eye(MLA_ROPE, dtype=F32)], -1)
    mid = jnp.broadcast_to(mid, (depth, MLA_HEADS, MLA_ROPE, MLA_KV_RANK + MLA_ROPE))
    bot = jnp.zeros((depth, MLA_HEADS, LANES - MLA_NOPE - MLA_ROPE, MLA_KV_RANK + MLA_ROPE), F32)
    wkT = jnp.concatenate([top, mid, bot], 2).reshape(depth, MLA_HEADS * LANES, MLA_KV_RANK + MLA_ROPE).astype(BF16)
    gq = mla_q_norm[:, :, None]
    gkv = mla_kv_norm[:, :, None]
    subln = da_subln[:, :, None]
    w_out_b = w_out.astype(BF16)
    w13 = jnp.concatenate([exp_w1, exp_w3], -1).astype(BF16)
    w2 = exp_w2.astype(BF16)
    rw_f = jnp.concatenate([router_w, jnp.zeros((D, LANES - N_EXPERTS), F32)], -1)
    rw_hi = rw_f.astype(BF16)
    rw_pad = jnp.concatenate([rw_hi, (rw_f - rw_hi.astype(F32)).astype(BF16)], -1)
    rb = router_b[:, None]
    dec = jnp.broadcast_to(jnp.stack([ret_decay_f, ret_decay_b], 1)[:, :, :, None, None],
                           (depth, 2, RET_HEADS, 8, LANES))
    tabs = _rope_tables(S, C, DA_DIM) + _rope_tables(S, C, RET_DIM)
    ln1g, ln1b, ln2g, ln2b = (a[:, None, :] for a in (ln1_g, ln1_b, ln2_g, ln2_b))

    xc = jnp.concatenate([x, ctx], 1)
    for l in range(depth):
        last = l == depth - 1
        (daqr, daqf, dak, dav, mqr, mqf, mk, mv, rq, rk, rv, rg) = _in_proj(
            l, xc, modr, wT_all, tabs, gq, wuqT, gkv, wvT, wkT, S)
        yda = _attention("da", l, daqr, daqf, dak, dav, S, C, not last, lam=da_lambda, g=subln)
        ymla = _attention("mla", l, mqr, mqf, mk, mv, S, C, not last)
        yret = _retention(rq, rk, rv, rg, dec[l], S, C)
        x1, h2, logits = _out_proj(l, yda, ymla, yret, w_out_b, xc, modr, ln1g, ln1b, rw_pad, S, alpha, last)
        xc = _moe(l, h2, logits, rb, w13, w2, x1, modr, ln2g, ln2b, S, alpha, last)
    return xc
```

```python
import functools
import math

import jax
import jax.numpy as jnp
import numpy as np
from jax import lax
from jax.experimental import pallas as pl
from jax.experimental.pallas import tpu as pltpu

F32 = jnp.float32
BF16 = jnp.bfloat16

GRID_W = 64
DA_HEADS, DA_DIM, DA_VDIM = 6, 32, 64
MLA_HEADS, MLA_NOPE, MLA_ROPE, MLA_VDIM = 6, 64, 32, 64
MLA_Q_RANK, MLA_KV_RANK = 256, 128
RET_HEADS, RET_DIM, RET_CHUNK = 4, 64, 128
ROPE_BASE = 10000.0
N_EXPERTS, N_GROUPS, D_EXPERT = 16, 4, 512
ROUTED_SCALE = 1.0
N_MOD = 6
LN_EPS = 1e-5
RMS_EPS = 1e-6
LOG2E = math.log2(math.e)

DA_W = DA_HEADS * 2 * DA_DIM
DA_VW = DA_HEADS * DA_VDIM
RET_W = RET_HEADS * RET_DIM
MLA_QW = MLA_HEADS * (MLA_NOPE + MLA_ROPE)
MLA_VW = MLA_HEADS * MLA_VDIM
MLA_QD = MLA_NOPE + MLA_ROPE
DA_MAPS = 2 * DA_HEADS
V_DIM = DA_VDIM
assert MLA_VDIM == V_DIM
IN_SPLIT = (DA_W, DA_W, DA_VW, MLA_Q_RANK, MLA_KV_RANK, MLA_ROPE, RET_W, RET_W, RET_W, RET_W)

LANES = 128
BF16_ROWS = 16
VMEM_LIMIT = 56 * 1024 * 1024
V_ROWS = V_DIM + BF16_ROWS
TM = 256
ATTN_N = 512
TK = 2048
ATTN_SUBTILES = 4
ATTN_NBUF = 2
ATTN_MAX_JUMP = 32.0
ATTN_UNROLL = 2
TM_MOE = 640
TM_MOE_LAST = 512
MOE_SUBTILES = 2
MOE_BLOCK = 256
MOE_ALIGN = 32
RET_UNROLL = 8

NT_DIMS = (((1,), (1,)), ((), ()))
TN_DIMS = (((0,), (0,)), ((), ()))


def _sigmoid(x):
    return 1.0 / (1.0 + jnp.exp(-x))


def _mod_kernel(c_ref, w_ref, b_ref, o_ref):
    c = c_ref[...]
    sc = (c * _sigmoid(c)).astype(BF16)
    o_ref[0] = jnp.dot(sc, w_ref[0].astype(BF16), preferred_element_type=F32) + b_ref[0]


def _modulation(cvec, w_ada, b_ada):
    depth, d, n = w_ada.shape
    tn = n // 4
    return pl.pallas_call(
        _mod_kernel,
        grid=(depth, n // tn),
        in_specs=[
            pl.BlockSpec((8, d), lambda l, j: (0, 0)),
            pl.BlockSpec((1, d, tn), lambda l, j: (l, 0, j)),
            pl.BlockSpec((1, 1, tn), lambda l, j: (l, 0, j)),
        ],
        out_specs=pl.BlockSpec((1, 8, tn), lambda l, j: (l, 0, j)),
        out_shape=jax.ShapeDtypeStruct((depth, 8, n), F32),
        name="adaln_mod",
    )(cvec, w_ada, b_ada.reshape(depth, 1, n))


_O_DAQ, _O_DAK, _O_DAV, _O_CQ, _O_CKV, _O_KPE, _O_RQ, _O_RK, _O_RV, _O_RG = (
    int(v) for v in np.cumsum((0,) + IN_SPLIT)[:-1])
N_ZROWS = sum(IN_SPLIT)


def _rms_rows(x):
    return x * lax.rsqrt(jnp.mean(x * x, axis=0, keepdims=True) + RMS_EPS)


def _pairswap_rows(x):
    n = x.shape[0]
    even = (lax.broadcasted_iota(jnp.int32, x.shape, 0) & 1) == 0
    return jnp.where(even, pltpu.roll(x, n - 1, 0), pltpu.roll(x, 1, 0))


def _in_kernel(*refs, cq_da, cq_mla, kscale, alpha, fused_ln):
    refs = list(refs)
    if fused_ln:
        x1_ref, f_ref, gate_ref, g2_ref, b2_ref = refs[:5]
        del refs[:5]
        xo_ref = refs.pop(-2)
        x = _layer_norm(alpha * x1_ref[0] + gate_ref[0, 0, 0] * f_ref[0].astype(F32), g2_ref[0], b2_ref[0])
        xo_ref[0] = x
    else:
        x = refs.pop(0)[0]
    (sh_ref, sc_ref, wT_ref, cosA_ref, sinA_ref, cosR_ref, sinR_ref, gq_ref, wuqT_ref, gkv_ref, wvT_ref, wkT_ref,
     daqr, daqf, dak, dav, mqr, mqf, mk, mv, rq, rk, rv, rg, z_ref) = refs
    tm = x.shape[0]
    h = (x * (1.0 + sc_ref[0, 0, 0]) + sh_ref[0, 0, 0]).astype(BF16)
    z_ref[...] = lax.dot_general(wT_ref[0], h, NT_DIMS, preferred_element_type=F32)
    cosA = cosA_ref[...]
    sinA = sinA_ref[...]
    cosR = cosR_ref[...]
    sinR = sinR_ref[...]
    ones_rows = (lax.broadcasted_iota(jnp.int32, (DA_HEADS, V_ROWS - V_DIM, tm), 1) == 0).astype(BF16)

    def rows(off, n):
        return z_ref[off:off + n, :]

    q2 = rows(_O_DAQ, DA_W)
    q = q2.reshape(DA_MAPS, DA_DIM, tm)
    qs = _pairswap_rows(q2).reshape(DA_MAPS, DA_DIM, tm)
    daqr[0] = ((q * cosA + qs * sinA) * cq_da).astype(BF16)
    daqf[0] = (q * cq_da).astype(BF16)
    k2 = rows(_O_DAK, DA_W)
    k = k2.reshape(DA_MAPS, DA_DIM, tm)
    ks = _pairswap_rows(k2).reshape(DA_MAPS, DA_DIM, tm)
    kr = (k * cosA + ks * sinA).reshape(DA_HEADS, 2 * DA_DIM, tm)
    kp = jnp.concatenate([kr, jnp.zeros_like(kr)], axis=1).reshape(DA_HEADS * LANES, tm)
    dak[0] = kp.T.astype(BF16)
    dav[0, :, 0:V_DIM, :] = rows(_O_DAV, DA_VW).reshape(DA_HEADS, DA_VDIM, tm).astype(BF16)
    dav[0, :, V_DIM:V_ROWS, :] = ones_rows

    cqn = (_rms_rows(rows(_O_CQ, MLA_Q_RANK)) * gq_ref[0]).astype(BF16)
    qm = jnp.dot(wuqT_ref[0], cqn, preferred_element_type=F32)
    qh = qm.reshape(MLA_HEADS, MLA_NOPE + MLA_ROPE, tm)
    qsw = _pairswap_rows(qm).reshape(MLA_HEADS, MLA_NOPE + MLA_ROPE, tm)[:, MLA_NOPE:, :]
    rot = qh[:, MLA_NOPE:, :] * cosA + qsw * sinA
    mqr[0, :, 0:MLA_NOPE, :] = (qh[:, 0:MLA_NOPE, :] * cq_mla).astype(BF16)
    mqr[0, :, MLA_NOPE:, :] = (rot * cq_mla).astype(BF16)
    mqf[0] = (qh * cq_mla).astype(BF16)
    ckvn = (_rms_rows(rows(_O_CKV, MLA_KV_RANK)) * gkv_ref[0]).astype(BF16)
    vm = jnp.dot(wvT_ref[0], ckvn, preferred_element_type=F32)
    mv[0, :, 0:V_DIM, :] = vm.reshape(MLA_HEADS, MLA_VDIM, tm).astype(BF16)
    mv[0, :, V_DIM:V_ROWS, :] = ones_rows
    kpe = rows(_O_KPE, MLA_ROPE)
    kper = kpe * cosA + _pairswap_rows(kpe) * sinA
    kin = jnp.concatenate([ckvn, kper.astype(BF16)], axis=0)
    kmT = jnp.dot(wkT_ref[0], kin, preferred_element_type=F32)
    mk[0] = kmT.T.astype(BF16)

    r_q2 = rows(_O_RQ, RET_W)
    r_q = r_q2.reshape(RET_HEADS, RET_DIM, tm)
    r_qs = _pairswap_rows(r_q2).reshape(RET_HEADS, RET_DIM, tm)
    rq[0] = (r_q * cosR + r_qs * sinR).astype(BF16)
    r_k2 = rows(_O_RK, RET_W)
    r_k = r_k2.reshape(RET_HEADS, RET_DIM, tm)
    r_ks = _pairswap_rows(r_k2).reshape(RET_HEADS, RET_DIM, tm)
    rkr = (r_k * cosR + r_ks * sinR) * kscale
    rkp = jnp.concatenate([rkr, jnp.zeros_like(rkr)], axis=1).reshape(RET_HEADS * LANES, tm)
    rk[0] = rkp.T.astype(BF16)
    rv[0] = rows(_O_RV, RET_W).reshape(RET_HEADS, RET_DIM, tm).astype(BF16)
    rg[0] = rows(_O_RG, RET_W)


def _in_proj(l, xc, modr, wT_all, tabs, gq, wuqT, gkv, wvT, wkT, S, alpha, prev=None):
    B, Sp, D = (prev[0] if prev is not None else xc).shape
    nt = Sp // TM
    n_lat_tiles = S // TM
    cosA, sinA, cosR, sinR = tabs

    def mod_spec(j):
        return pl.BlockSpec((1, 1, 1, 1, D), lambda b, t: (l, jnp.where(t >= n_lat_tiles, B, b), j, 0, 0))

    def wspec(a):
        return pl.BlockSpec((1,) + a.shape[1:], lambda b, t: (l,) + (0,) * (a.ndim - 1))

    def fm4(h, d):
        return pl.BlockSpec((1, h, d, TM), lambda b, t: (b, 0, 0, t))

    def tmaj(w):
        return pl.BlockSpec((1, TM, w), lambda b, t: (b, t, 0))

    kern = functools.partial(
        _in_kernel,
        cq_da=float(DA_DIM ** -0.5 * LOG2E),
        cq_mla=float((MLA_NOPE + MLA_ROPE) ** -0.5 * LOG2E),
        kscale=float(RET_DIM ** -0.5),
        alpha=alpha,
        fused_ln=prev is not None,
    )
    sds = jax.ShapeDtypeStruct
    row = pl.BlockSpec((1, TM, D), lambda b, t: (b, t, 0))
    if prev is None:
        x_specs, x_args, x_out_specs, x_out_shapes = [row], [xc], [], []
    else:
        x1, f, ln_g, ln_b = prev
        prev_gate = pl.BlockSpec((1, 1, 1, 1, D),
                                 lambda b, t: (l - 1, jnp.where(t >= n_lat_tiles, B, b), N_MOD - 1, 0, 0))
        prev_ln = pl.BlockSpec((1,) + ln_g.shape[1:], lambda b, t: (l - 1, 0, 0))
        x_specs, x_args = [row, row, prev_gate, prev_ln, prev_ln], [x1, f, modr, ln_g, ln_b]
        x_out_specs, x_out_shapes = [row], [sds((B, Sp, D), F32)]
    return pl.pallas_call(
        kern,
        grid=(B, nt),
        in_specs=x_specs + [
            mod_spec(0), mod_spec(1),
            wspec(wT_all),
            pl.BlockSpec((DA_DIM, TM), lambda b, t: (0, t)),
            pl.BlockSpec((DA_DIM, TM), lambda b, t: (0, t)),
            pl.BlockSpec((RET_DIM, TM), lambda b, t: (0, t)),
            pl.BlockSpec((RET_DIM, TM), lambda b, t: (0, t)),
            wspec(gq), wspec(wuqT), wspec(gkv), wspec(wvT), wspec(wkT),
        ],
        out_specs=[
            fm4(DA_MAPS, DA_DIM), fm4(DA_MAPS, DA_DIM), tmaj(DA_HEADS * LANES), fm4(DA_HEADS, V_ROWS),
            fm4(MLA_HEADS, MLA_QD), fm4(MLA_HEADS, MLA_QD), tmaj(MLA_HEADS * LANES), fm4(MLA_HEADS, V_ROWS),
            fm4(RET_HEADS, RET_DIM), tmaj(RET_HEADS * LANES), fm4(RET_HEADS, RET_DIM),
            pl.BlockSpec((1, RET_W, TM), lambda b, t: (b, 0, t)),
        ] + x_out_specs,
        out_shape=[
            sds((B, DA_MAPS, DA_DIM, Sp), BF16), sds((B, DA_MAPS, DA_DIM, Sp), BF16),
            sds((B, Sp, DA_HEADS * LANES), BF16), sds((B, DA_HEADS, V_ROWS, Sp), BF16),
            sds((B, MLA_HEADS, MLA_QD, Sp), BF16), sds((B, MLA_HEADS, MLA_QD, Sp), BF16),
            sds((B, Sp, MLA_HEADS * LANES), BF16), sds((B, MLA_HEADS, V_ROWS, Sp), BF16),
            sds((B, RET_HEADS, RET_DIM, Sp), BF16), sds((B, Sp, RET_HEADS * LANES), BF16),
            sds((B, RET_HEADS, RET_DIM, Sp), BF16), sds((B, RET_W, Sp), F32),
        ] + x_out_shapes,
        scratch_shapes=[pltpu.VMEM((N_ZROWS, TM), F32)],
        compiler_params=pltpu.CompilerParams(
            dimension_semantics=("arbitrary", "arbitrary"), vmem_limit_bytes=VMEM_LIMIT),
        name="in_proj",
    )(*x_args, modr, modr, wT_all, cosA, sinA, cosR, sinR, gq, wuqT, gkv, wvT, wkT)


def _attn_kernel(*refs, mode, S, C, lam_init, ctx_only, nsub):
    refs = list(refs)
    qr_ref, qf_ref, k_ref, v_ref = refs[:4]
    del refs[:4]
    if mode == "da":
        lam_ref, g_ref = refs[:2]
        del refs[:2]
    if ctx_only:
        del refs[:1]
    o_ref, m_sc, acc_sc, q_sc, jump_sc = refs[:5]
    nb = ATTN_NBUF
    pbuf, albuf = (refs[5 + i * nb: 5 + (i + 1) * nb] for i in range(2))
    tq = o_ref.shape[2] // nsub
    n_lat = S // TK

    def qpad(ref, t):
        cols = slice(t * tq, (t + 1) * tq)
        if mode == "da":
            z = jnp.zeros((DA_DIM, tq), BF16)
            c1 = jnp.concatenate([ref[0, 0, :, cols], z, z, z], axis=0)
            c2 = jnp.concatenate([z, ref[0, 1, :, cols], z, z], axis=0)
            return jnp.concatenate([c1, c2], axis=1)
        return jnp.concatenate([ref[0, 0, :, cols], jnp.zeros((LANES - MLA_QD, tq), BF16)], axis=0)

    def chunk(j):
        return pl.ds(j * TK if isinstance(j, int) else pl.multiple_of(j * TK, TK), TK)

    def context_keys(t):
        c0 = 0 if ctx_only else S
        s = jnp.dot(k_ref[0, c0:c0 + C, :], qpad(qf_ref, t), preferred_element_type=F32)
        m_c = jnp.max(s, axis=0, keepdims=True)
        m_sc[t] = m_c
        acc_sc[t] = jnp.dot(v_ref[0, 0, :, c0:c0 + C], jnp.exp2(s - m_c).astype(BF16), preferred_element_type=F32)

    for t in range(nsub):
        context_keys(t)

    if not ctx_only:
        for t in range(nsub):
            q_sc[t] = qpad(qr_ref, t)
        jump_sc[...] = jnp.full(jump_sc.shape, -jnp.inf, F32)

        def probs(t, j, r):
            s = jnp.dot(k_ref[0, chunk(j), :], q_sc[t], preferred_element_type=F32)
            m_prev = m_sc[t]
            pbuf[r][t] = jnp.exp2(s - m_prev).astype(BF16)
            m_chunk = jnp.max(s, axis=0, keepdims=True)
            m_new = jnp.maximum(m_prev, m_chunk)
            jump_sc[t] = jnp.maximum(jump_sc[t], m_chunk - m_prev)
            albuf[r][t] = jnp.exp2(m_prev - m_new)
            m_sc[t] = m_new

        def accumulate(t, j, r):
            pv = jnp.dot(v_ref[0, 0, :, chunk(j)], pbuf[r][t], preferred_element_type=F32)
            acc_sc[t] = (acc_sc[t] + pv) * albuf[r][t]

        def step(j, r):
            for t in range(nsub):
                accumulate(t, j, r)
                probs(t, j + 1, (r + 1) % nb)

        for t in range(nsub):
            probs(t, 0, 0)
        n_steps = n_lat - 1
        trips = n_steps // ATTN_UNROLL

        def body(i, carry):
            for u in range(ATTN_UNROLL):
                step(i * ATTN_UNROLL + u, u % nb)
            return carry

        lax.fori_loop(0, trips, body, 0)
        for j in range(trips * ATTN_UNROLL, n_steps):
            step(j, j % nb)
        for t in range(nsub):
            accumulate(t, n_lat - 1, (n_lat - 1) % nb)

        @pl.when(jnp.max(jump_sc[...]) > ATTN_MAX_JUMP)
        def _():
            for t in range(nsub):
                context_keys(t)

                def exact_step(j, carry, t=t):
                    s = jnp.dot(k_ref[0, chunk(j), :], q_sc[t], preferred_element_type=F32)
                    m_old = m_sc[t]
                    m_new = jnp.maximum(m_old, jnp.max(s, axis=0, keepdims=True))
                    pv = jnp.dot(v_ref[0, 0, :, chunk(j)], jnp.exp2(s - m_new).astype(BF16),
                                 preferred_element_type=F32)
                    acc_sc[t] = acc_sc[t] * jnp.exp2(m_old - m_new) + pv
                    m_sc[t] = m_new
                    return carry

                lax.fori_loop(0, n_lat, exact_step, 0)

    for t in range(nsub):
        acc = acc_sc[t]
        o = acc[0:V_DIM, :] / acc[V_DIM:V_DIM + 1, :]
        cols = slice(t * tq, (t + 1) * tq)
        if mode == "da":
            lf = lam_ref[0]
            lam = (jnp.exp(jnp.sum(lf[0:1] * lf[1:2], axis=1, keepdims=True))
                   - jnp.exp(jnp.sum(lf[2:3] * lf[3:4], axis=1, keepdims=True)) + lam_init)
            od = o[:, 0:tq] - lam * o[:, tq:2 * tq]
            y = _rms_rows(od) * g_ref[0]
            o_ref[0, :, cols] = (y * (1.0 - lam_init)).astype(BF16)
        else:
            o_ref[0, :, cols] = o.astype(BF16)


def _attention_call(mode, l, qr, qf, k, v, S, C, lam, g, y_latent):
    B, nmaps, d, Sp = qr.shape
    H = v.shape[1]
    nm = nmaps // H
    ctx_only = y_latent is not None
    nsub = 1 if ctx_only else ATTN_SUBTILES
    tq = C if ctx_only else ATTN_N // nm
    n = nm * tq
    tqb = nsub * tq
    q0 = S // tqb if ctx_only else 0
    lam_init = 0.8 - 0.6 * math.exp(-0.3 * l)
    kern = functools.partial(_attn_kernel, mode=mode, S=S, C=C, lam_init=lam_init, ctx_only=ctx_only, nsub=nsub)
    in_specs = [
        pl.BlockSpec((1, nm, d, tqb), lambda b, h, i: (b, h, 0, q0 + i)),
        pl.BlockSpec((1, nm, d, tqb), lambda b, h, i: (b, h, 0, q0 + i)),
        (pl.BlockSpec((1, C, LANES), lambda b, h, i: (b, S // C, h)) if ctx_only
         else pl.BlockSpec((1, Sp, LANES), lambda b, h, i: (b, 0, h))),
        (pl.BlockSpec((1, 1, V_ROWS, C), lambda b, h, i: (b, h, 0, S // C)) if ctx_only
         else pl.BlockSpec((1, 1, V_ROWS, Sp), lambda b, h, i: (b, h, 0, 0))),
    ]
    args = [qr, qf, k, v]
    if mode == "da":
        in_specs += [pl.BlockSpec((1,) + lam.shape[1:], lambda b, h, i: (l, 0, 0)),
                     pl.BlockSpec((1,) + g.shape[1:], lambda b, h, i: (l, 0, 0))]
        args += [lam, g]
    aliases = {}
    if ctx_only:
        in_specs.append(pl.BlockSpec(memory_space=pl.ANY))
        args.append(y_latent)
        aliases = {len(args) - 1: 0}
    tk = C if ctx_only else TK
    return pl.pallas_call(
        kern,
        grid=(B, H, 1 if ctx_only else S // tqb),
        in_specs=in_specs,
        out_specs=pl.BlockSpec((1, V_DIM, tqb), lambda b, h, i: (b, h, q0 + i)),
        out_shape=jax.ShapeDtypeStruct((B, H * V_DIM, Sp), BF16),
        input_output_aliases=aliases,
        scratch_shapes=(
            [pltpu.VMEM((nsub, 1, n), F32), pltpu.VMEM((nsub, V_ROWS, n), F32), pltpu.VMEM((nsub, LANES, n), BF16),
             pltpu.VMEM((nsub, 1, n), F32)]
            + [pltpu.VMEM((nsub, tk, n), BF16)] * ATTN_NBUF + [pltpu.VMEM((nsub, 1, n), F32)] * ATTN_NBUF),
        compiler_params=pltpu.CompilerParams(
            dimension_semantics=("arbitrary", "arbitrary", "arbitrary"), vmem_limit_bytes=VMEM_LIMIT),
        name="attn_" + mode + ("_ctx" if ctx_only else ""),
    )(*args)


def _attention(mode, l, qr, qf, k, v, S, C, with_ctx, lam=None, g=None):
    y = _attention_call(mode, l, qr, qf, k, v, S, C, lam, g, None)
    return _attention_call(mode, l, qr, qf, k, v, S, C, lam, g, y) if with_ctx else y


def _ret_kernel(q_ref, k_ref, v_ref, g_ref, dec_ref, o_ref, sb_sc, *, S, C):
    CH = RET_CHUNK
    nl, nc = S // CH, C // CH

    def log_sigmoid(x):
        return jnp.minimum(x, 0.0) - jnp.log1p(jnp.exp(-jnp.abs(x)))

    lgf = log_sigmoid(dec_ref[0, 0][0:1, :])
    lgb = log_sigmoid(dec_ref[1, 0][0:1, :])
    ii = lax.broadcasted_iota(jnp.int32, (CH, CH), 1).astype(F32)
    jj = lax.broadcasted_iota(jnp.int32, (CH, CH), 0).astype(F32)
    dd = ii - jj
    fwd = dd >= 0
    mt = (jnp.where(fwd, jnp.exp(lgf * jnp.where(fwd, dd, 0.0)), 0.0)
          + jnp.where(fwd, 0.0, jnp.exp(lgb * jnp.where(fwd, 0.0, -dd))))
    xi_f = jnp.exp(lgf * (ii + 1.0))
    xi_b = jnp.exp(lgb * (CH - ii))
    zeta_f = jnp.exp(lgf * (CH - 1.0 - jj))
    zeta_b = jnp.exp(lgb * jj)
    cdf = jnp.exp(lgf * CH)
    cdb = jnp.exp(lgb * CH)
    zq = jnp.zeros((RET_DIM, CH), BF16)

    def sl(c):
        return pl.ds(pl.multiple_of(c * CH, CH), CH)

    def state_update(c, zeta):
        kz = (k_ref[0, sl(c), :].astype(F32) * zeta).astype(BF16)
        return jnp.dot(v_ref[0, 0, :, sl(c)], kz, preferred_element_type=F32)

    def bwd_step(c, sb):
        sb_sc[c] = sb
        return sb * cdb + state_update(c, zeta_b)

    sb = jnp.zeros((RET_DIM, CH), F32)
    for c in range(nl + nc - 1, nl - 1, -1):
        sb = bwd_step(c, sb)
    lax.fori_loop(0, nl, lambda n, s: bwd_step(nl - 1 - n, s), sb, unroll=RET_UNROLL)

    def fwd_step(c, sf):
        qc = q_ref[0, 0, :, sl(c)]
        qp = jnp.concatenate([qc, zq], axis=0)
        kc = k_ref[0, sl(c), :]
        vc = v_ref[0, 0, :, sl(c)]
        at = jnp.dot(kc, qp, preferred_element_type=F32)
        qpf = qp.astype(F32)
        rhs = jnp.concatenate([(at * mt).astype(BF16), (qpf * xi_f).astype(BF16), (qpf * xi_b).astype(BF16)], axis=0)
        lhs = jnp.concatenate([vc, sf.astype(BF16), sb_sc[c].astype(BF16)], axis=1)
        o = jnp.dot(lhs, rhs, preferred_element_type=F32)
        gch = g_ref[0, :, sl(c)]
        o_ref[0, :, sl(c)] = (gch * _sigmoid(gch) * _rms_rows(o)).astype(BF16)
        return sf * cdf + state_update(c, zeta_f)

    sf = jnp.zeros((RET_DIM, CH), F32)
    for c in range(nl, nl + nc):
        sf = fwd_step(c, sf)
    lax.fori_loop(0, nl, fwd_step, sf, unroll=RET_UNROLL)


def _retention(rq, rk, rv, rg, dec, S, C):
    B, H, d, Sp = rq.shape
    kern = functools.partial(_ret_kernel, S=S, C=C)
    return pl.pallas_call(
        kern,
        grid=(B, H),
        in_specs=[
            pl.BlockSpec((1, 1, d, Sp), lambda b, h: (b, h, 0, 0)),
            pl.BlockSpec((1, Sp, LANES), lambda b, h: (b, 0, h)),
            pl.BlockSpec((1, 1, d, Sp), lambda b, h: (b, h, 0, 0)),
            pl.BlockSpec((1, d, Sp), lambda b, h: (b, h, 0)),
            pl.BlockSpec((2, 1, 8, LANES), lambda b, h: (0, h, 0, 0)),
        ],
        out_specs=pl.BlockSpec((1, d, Sp), lambda b, h: (b, h, 0)),
        out_shape=jax.ShapeDtypeStruct((B, H * d, Sp), BF16),
        scratch_shapes=[pltpu.VMEM((Sp // RET_CHUNK, RET_DIM, RET_CHUNK), F32)],
        compiler_params=pltpu.CompilerParams(
            dimension_semantics=("arbitrary", "arbitrary"), vmem_limit_bytes=VMEM_LIMIT),
        name="retention",
    )(rq, rk, rv, rg, dec)


def _layer_norm(t, g, b):
    mu = jnp.mean(t, axis=-1, keepdims=True)
    tc = t - mu
    var = jnp.mean(tc * tc, axis=-1, keepdims=True)
    return tc * lax.rsqrt(var + LN_EPS) * g + b


def _out_kernel(yda_ref, ymla_ref, yret_ref, w_ref, x_ref, gate_ref, g1_ref, b1_ref, sh2_ref, sc2_ref, rw_ref,
                x1_ref, h2_ref, lg_ref, *, alpha):
    proj = (lax.dot_general(yda_ref[0], w_ref[0, 0:DA_VW, :], TN_DIMS, preferred_element_type=F32)
            + lax.dot_general(ymla_ref[0], w_ref[0, DA_VW:DA_VW + MLA_VW, :], TN_DIMS, preferred_element_type=F32)
            + lax.dot_general(yret_ref[0], w_ref[0, DA_VW + MLA_VW:, :], TN_DIMS, preferred_element_type=F32))
    x1 = _layer_norm(alpha * x_ref[0] + gate_ref[0, 0, 0] * proj, g1_ref[0], b1_ref[0])
    x1_ref[0] = x1
    h2 = x1 * (1.0 + sc2_ref[0, 0, 0]) + sh2_ref[0, 0, 0]
    h2_ref[0] = h2.astype(BF16)
    h2_hi = h2.astype(BF16)
    h2_lo = (h2 - h2_hi.astype(F32)).astype(BF16)
    hi = jnp.dot(h2_hi, rw_ref[...], preferred_element_type=F32)
    lo = jnp.dot(h2_lo, rw_ref[:, 0:LANES], preferred_element_type=F32)
    lg_ref[0] = hi[:, 0:LANES] + (hi[:, LANES:] + lo)


def _out_proj(l, yda, ymla, yret, w_out, xc, modr, ln_g, ln_b, rw_pad, S, alpha, latents_only):
    B, Sp, D = xc.shape
    n_lat_tiles = S // TM
    n_rows = S if latents_only else Sp

    def mod_spec(j):
        return pl.BlockSpec((1, 1, 1, 1, D), lambda b, t: (l, jnp.where(t >= n_lat_tiles, B, b), j, 0, 0))

    def fm(a):
        return pl.BlockSpec((1, a.shape[1], TM), lambda b, t: (b, 0, t))

    def wspec(a):
        return pl.BlockSpec((1,) + a.shape[1:], lambda b, t: (l,) + (0,) * (a.ndim - 1))

    row = pl.BlockSpec((1, TM, D), lambda b, t: (b, t, 0))
    sds = jax.ShapeDtypeStruct
    return pl.pallas_call(
        functools.partial(_out_kernel, alpha=alpha),
        grid=(B, n_rows // TM),
        in_specs=[fm(yda), fm(ymla), fm(yret), wspec(w_out), row, mod_spec(2), wspec(ln_g), wspec(ln_b),
                  mod_spec(3), mod_spec(4), pl.BlockSpec(rw_pad.shape, lambda b, t: (0, 0))],
        out_specs=[row, row, pl.BlockSpec((1, TM, LANES), lambda b, t: (b, t, 0))],
        out_shape=[sds((B, n_rows, D), F32), sds((B, n_rows, D), BF16), sds((B, n_rows, LANES), F32)],
        compiler_params=pltpu.CompilerParams(dimension_semantics=("arbitrary", "arbitrary")),
        name="out_proj",
    )(yda, ymla, yret, w_out, xc, modr, ln_g, ln_b, modr, modr, rw_pad)


def _gates_T(logits_T, rb):
    s = _sigmoid(logits_T)
    ch = s + rb
    srow = [s[e:e + 1] for e in range(N_EXPERTS)]
    crow = [ch[e:e + 1] for e in range(N_EXPERTS)]
    per = N_EXPERTS // N_GROUPS
    gs = []
    for g in range(N_GROUPS):
        a, b, c, d = crow[per * g: per * g + per]
        m1, n1, m2, n2 = jnp.maximum(a, b), jnp.minimum(a, b), jnp.maximum(c, d), jnp.minimum(c, d)
        gs.append(jnp.maximum(m1, m2) + jnp.maximum(jnp.minimum(m1, m2), jnp.maximum(n1, n2)))
    gmax = functools.reduce(jnp.maximum, gs)
    taken = jnp.zeros(gmax.shape, jnp.bool_)
    gsel = []
    for g in range(N_GROUPS):
        sg = jnp.logical_and(gs[g] == gmax, jnp.logical_not(taken))
        gsel.append(sg)
        taken = jnp.logical_or(taken, sg)
    neg = jnp.full(gmax.shape, -jnp.inf, F32)
    mc = [jnp.where(gsel[e // per], crow[e], neg) for e in range(N_EXPERTS)]
    sel = [jnp.zeros(gmax.shape, jnp.bool_) for _ in range(N_EXPERTS)]
    for _ in range(2):
        top = functools.reduce(jnp.maximum, mc)
        taken = jnp.zeros(gmax.shape, jnp.bool_)
        for e in range(N_EXPERTS):
            hit = jnp.logical_and(mc[e] == top, jnp.logical_not(taken))
            taken = jnp.logical_or(taken, hit)
            sel[e] = jnp.logical_or(sel[e], hit)
            mc[e] = jnp.where(hit, neg, mc[e])
    w = [jnp.where(sel[e], srow[e], 0.0) for e in range(N_EXPERTS)]
    wsum = functools.reduce(lambda a, b: a + b, w)
    gates = jnp.concatenate([we / wsum * ROUTED_SCALE for we in w], axis=0)
    return gates, [jnp.where(sg, 1.0, 0.0) for sg in gsel]


def _moe_sorted_rows(tm):
    return tm + N_GROUPS * MOE_ALIGN + MOE_BLOCK


def _moe_kernel(h_ref, lg_ref, rb_ref, w13_ref, w2_ref, o_ref,
                off_sc, nblk_sc, perm_sc, hs_sc, gs_sc, accs_sc, earlier_sc, *, nsub):
    e = pl.program_id(2)
    tm = h_ref.shape[1] // nsub
    n_sorted = perm_sc.shape[1]
    per = N_EXPERTS // N_GROUPS
    lane = lax.broadcasted_iota(jnp.int32, (1, LANES), 1)

    def expert(x):
        ab = jnp.dot(x, w13_ref[0, 0], preferred_element_type=F32)
        a = ab[:, 0:D_EXPERT]
        hid = (a * _sigmoid(a) * ab[:, D_EXPERT:]).astype(BF16)
        return jnp.dot(hid, w2_ref[0, 0], preferred_element_type=F32)

    def gate_col(g):
        return jnp.sum(jnp.where(lane == e, g, 0.0), axis=1, keepdims=True)

    def sub(j):
        return slice(j * tm, (j + 1) * tm)

    @pl.when(jnp.logical_and(jnp.logical_and(pl.program_id(0) == 0, pl.program_id(1) == 0), e == 0))
    def _():
        before = lax.broadcasted_iota(jnp.int32, (tm, tm), 0) < lax.broadcasted_iota(jnp.int32, (tm, tm), 1)
        earlier_sc[...] = jnp.where(before, 1.0, 0.0).astype(BF16)

    @pl.when(e == 0)
    def _():
        for j in range(nsub):
            gT, gsel = _gates_T(lg_ref[0, sub(j), :].T[0:N_EXPERTS, :], rb_ref[...])
            gates = jnp.concatenate([gT, jnp.zeros((LANES - N_EXPERTS, tm), F32)], axis=0).T
            member = jnp.concatenate(gsel + [jnp.zeros((BF16_ROWS - N_GROUPS, tm), F32)], axis=0)
            rank = jnp.dot(member.astype(BF16), earlier_sc[...], preferred_element_type=F32)
            count = jnp.sum(member, axis=1, keepdims=True)
            off = jnp.int32(0)
            slot = jnp.zeros((1, tm), F32)
            for g in range(N_GROUPS):
                n_g = jnp.max(count[g:g + 1]).astype(jnp.int32)
                off_sc[j * N_GROUPS + g] = off
                nblk_sc[j * N_GROUPS + g] = (n_g + (MOE_BLOCK - 1)) // MOE_BLOCK
                slot = slot + gsel[g] * (rank[g:g + 1] + off.astype(F32))
                off = off + ((n_g + (MOE_ALIGN - 1)) // MOE_ALIGN) * MOE_ALIGN
            rows_i = lax.broadcasted_iota(jnp.int32, (n_sorted, tm), 0)
            perm = jnp.where(rows_i == slot.astype(jnp.int32), 1.0, 0.0).astype(BF16)
            perm_sc[j] = perm
            hs_sc[j] = jnp.dot(perm, h_ref[0, sub(j), :], preferred_element_type=F32).astype(BF16)
            g_hi = gates.astype(BF16)
            g_lo = (gates - g_hi.astype(F32)).astype(BF16)
            gs_sc[j] = (jnp.dot(perm, g_hi, preferred_element_type=F32)
                        + jnp.dot(perm, g_lo, preferred_element_type=F32))
            accs_sc[j] = jnp.zeros(accs_sc.shape[1:], F32)

    start = [off_sc[j * N_GROUPS + e // per] for j in range(nsub)]
    n_blocks = [nblk_sc[j * N_GROUPS + e // per] for j in range(nsub)]
    n_common = functools.reduce(jnp.minimum, n_blocks)

    def rows_of(j, k):
        return pl.ds(pl.multiple_of(start[j] + k * MOE_BLOCK, MOE_ALIGN), MOE_BLOCK)

    def together(k, carry):
        y = expert(jnp.concatenate([hs_sc[j, rows_of(j, k), :] for j in range(nsub)], axis=0))
        for j in range(nsub):
            rows = rows_of(j, k)
            accs_sc[j, rows, :] += gate_col(gs_sc[j, rows, :]) * y[j * MOE_BLOCK:(j + 1) * MOE_BLOCK]
        return carry

    lax.fori_loop(0, n_common, together, 0)
    for j in range(nsub):
        def alone(k, carry, j=j):
            rows = rows_of(j, k)
            accs_sc[j, rows, :] += gate_col(gs_sc[j, rows, :]) * expert(hs_sc[j, rows, :])
            return carry

        lax.fori_loop(n_common, n_blocks[j], alone, 0)

    @pl.when(e == N_EXPERTS - 1)
    def _():
        for j in range(nsub):
            o_ref[0, sub(j), :] = lax.dot_general(perm_sc[j], accs_sc[j].astype(BF16), TN_DIMS,
                                                  preferred_element_type=F32).astype(BF16)


def _moe_ln_kernel(x1_ref, f_ref, gb_ref, gc_ref, g2_ref, b2_ref, o_ref, *, S, alpha):
    tm = x1_ref.shape[1]
    n_lat = S - pl.program_id(1) * tm
    rowi = lax.broadcasted_iota(jnp.int32, (tm, 1), 0)
    gate = jnp.where(rowi < n_lat, gb_ref[0, 0, 0], gc_ref[0, 0, 0])
    o_ref[0] = _layer_norm(alpha * x1_ref[0] + gate * f_ref[0].astype(F32), g2_ref[0], b2_ref[0])


def _moe(l, h2, logits, rb, w13, w2, x1, modr, ln_g, ln_b, S, alpha, latents_only):
    B, n_rows, D = x1.shape
    sub = TM_MOE_LAST if latents_only else TM_MOE
    tm = MOE_SUBTILES * sub
    n_sorted = _moe_sorted_rows(sub)

    def wspec(a):
        return pl.BlockSpec((1,) + a.shape[1:], lambda b, t: (l,) + (0,) * (a.ndim - 1))

    row = lambda b, t, e: (b, t, 0)
    f = pl.pallas_call(
        functools.partial(_moe_kernel, nsub=MOE_SUBTILES),
        grid=(B, n_rows // tm, N_EXPERTS),
        in_specs=[
            pl.BlockSpec((1, tm, D), row),
            pl.BlockSpec((1, tm, LANES), row),
            pl.BlockSpec(rb.shape, lambda b, t, e: (0, 0)),
            pl.BlockSpec((1, 1, D, 2 * D_EXPERT), lambda b, t, e: (l, e, 0, 0)),
            pl.BlockSpec((1, 1, D_EXPERT, D), lambda b, t, e: (l, e, 0, 0)),
        ],
        out_specs=pl.BlockSpec((1, tm, D), row),
        out_shape=jax.ShapeDtypeStruct((B, n_rows, D), BF16),
        scratch_shapes=[
            pltpu.SMEM((MOE_SUBTILES * N_GROUPS,), jnp.int32), pltpu.SMEM((MOE_SUBTILES * N_GROUPS,), jnp.int32),
            pltpu.VMEM((MOE_SUBTILES, n_sorted, sub), BF16),
            pltpu.VMEM((MOE_SUBTILES, n_sorted, D), BF16),
            pltpu.VMEM((MOE_SUBTILES, n_sorted, LANES), F32),
            pltpu.VMEM((MOE_SUBTILES, n_sorted, D), F32),
            pltpu.VMEM((sub, sub), BF16)],
        compiler_params=pltpu.CompilerParams(
            dimension_semantics=("arbitrary", "arbitrary", "arbitrary"), vmem_limit_bytes=VMEM_LIMIT),
        name="moe",
    )(h2, logits, rb, w13, w2)
    if not latents_only:
        return f
    rows = pl.BlockSpec((1, tm, D), lambda b, t: (b, t, 0))
    return pl.pallas_call(
        functools.partial(_moe_ln_kernel, S=S, alpha=alpha),
        grid=(B, n_rows // tm),
        in_specs=[rows, rows,
                  pl.BlockSpec((1, 1, 1, 1, D), lambda b, t: (l, b, 5, 0, 0)),
                  pl.BlockSpec((1, 1, 1, 1, D), lambda b, t: (l, B, 5, 0, 0)),
                  wspec(ln_g), wspec(ln_b)],
        out_specs=rows,
        out_shape=jax.ShapeDtypeStruct((B, n_rows, D), F32),
        compiler_params=pltpu.CompilerParams(
            dimension_semantics=("arbitrary", "arbitrary"), vmem_limit_bytes=VMEM_LIMIT),
        name="moe_ln",
    )(x1, f, modr, modr, ln_g, ln_b)


def _rope_tables(S, C, rot_dim):
    rows = S // GRID_W
    r = jnp.repeat(jnp.arange(rows, dtype=F32), GRID_W)
    col = jnp.tile(jnp.arange(GRID_W, dtype=F32), rows)
    n_freq = rot_dim // 4
    freqs = ROPE_BASE ** (-jnp.arange(n_freq, dtype=F32) / n_freq)
    ang = jnp.concatenate([r[:, None] * freqs, col[:, None] * freqs], -1)
    cos, sin = jnp.cos(ang), jnp.sin(ang)
    cos_rep = jnp.repeat(cos, 2, axis=-1)
    sin_alt = jnp.stack([-sin, sin], -1).reshape(S, rot_dim)
    cos_rep = jnp.concatenate([cos_rep, jnp.ones((C, rot_dim), F32)], 0)
    sin_alt = jnp.concatenate([sin_alt, jnp.zeros((C, rot_dim), F32)], 0)
    return cos_rep.T, sin_alt.T


def kernel(x, c, ctx, c_ctx, w_ada, b_ada, w_in, da_lambda, da_subln, mla_q_norm, mla_w_uq, mla_kv_norm, mla_w_ukv, ret_decay_f, ret_decay_b, w_out, ln1_g, ln1_b, router_w, router_b, exp_w1, exp_w3, exp_w2, ln2_g, ln2_b):
    B, S, D = x.shape
    C = ctx.shape[1]
    depth = w_in.shape[0]
    alpha = float((2 * depth) ** 0.25)

    cvec = jnp.concatenate([c, c_ctx[None, :], jnp.zeros((8 - B - 1, D), F32)], 0)
    mod = _modulation(cvec, w_ada, b_ada)
    modr = mod.reshape(depth, 8, N_MOD, 1, D)

    wT_all = jnp.swapaxes(w_in, 1, 2).astype(BF16)
    wuqT = jnp.swapaxes(mla_w_uq, 1, 2).astype(BF16)
    ukv = mla_w_ukv.reshape(depth, MLA_KV_RANK, MLA_HEADS, MLA_NOPE + MLA_VDIM)
    wvT = jnp.swapaxes(ukv[..., MLA_NOPE:].reshape(depth, MLA_KV_RANK, MLA_VW), 1, 2).astype(BF16)
    nopeT = jnp.transpose(ukv[..., :MLA_NOPE], (0, 2, 3, 1))
    top = jnp.concatenate([nopeT, jnp.zeros((depth, MLA_HEADS, MLA_NOPE, MLA_ROPE), F32)], -1)
    mid = jnp.concatenate([jnp.zeros((MLA_ROPE, MLA_KV_RANK), F32), jnp.eye(MLA_ROPE, dtype=F32)], -1)
    mid = jnp.broadcast_to(mid, (depth, MLA_HEADS, MLA_ROPE, MLA_KV_RANK + MLA_ROPE))
    bot = jnp.zeros((depth, MLA_HEADS, LANES - MLA_NOPE - MLA_ROPE, MLA_KV_RANK + MLA_ROPE), F32)
    wkT = jnp.concatenate([top, mid, bot], 2).reshape(depth, MLA_HEADS * LANES, MLA_KV_RANK + MLA_ROPE).astype(BF16)
    gq = mla_q_norm[:, :, None]
    gkv = mla_kv_norm[:, :, None]
    subln = da_subln[:, :, None]
    w_out_b = w_out.astype(BF16)
    w13 = jnp.concatenate([exp_w1, exp_w3], -1).astype(BF16)
    w2 = exp_w2.astype(BF16)
    rw_f = jnp.concatenate([router_w, jnp.zeros((D, LANES - N_EXPERTS), F32)], -1)
    rw_hi = rw_f.astype(BF16)
    rw_pad = jnp.concatenate([rw_hi, (rw_f - rw_hi.astype(F32)).astype(BF16)], -1)
    rb = router_b[:, None]
    dec = jnp.broadcast_to(jnp.stack([ret_decay_f, ret_decay_b], 1)[:, :, :, None, None],
                           (depth, 2, RET_HEADS, 8, LANES))
    tabs = _rope_tables(S, C, DA_DIM) + _rope_tables(S, C, RET_DIM)
    ln1g, ln1b, ln2g, ln2b = (a[:, None, :] for a in (ln1_g, ln1_b, ln2_g, ln2_b))

    xc = jnp.concatenate([x, ctx], 1)
    prev = None
    for l in range(depth):
        last = l == depth - 1
        outs = _in_proj(l, xc, modr, wT_all, tabs, gq, wuqT, gkv, wvT, wkT, S, alpha, prev)
        (daqr, daqf, dak, dav, mqr, mqf, mk, mv, rq, rk, rv, rg) = outs[:12]
        if prev is not None:
            xc = outs[12]
        yda = _attention("da", l, daqr, daqf, dak, dav, S, C, not last, lam=da_lambda, g=subln)
        ymla = _attention("mla", l, mqr, mqf, mk, mv, S, C, not last)
        yret = _retention(rq, rk, rv, rg, dec[l], S, C)
        x1, h2, logits = _out_proj(l, yda, ymla, yret, w_out_b, xc, modr, ln1g, ln1b, rw_pad, S, alpha, last)
        f = _moe(l, h2, logits, rb, w13, w2, x1, modr, ln2g, ln2b, S, alpha, last)
        prev = (x1, f, ln2g, ln2b)
    return f
```

```python
import functools
import math

import jax
import jax.numpy as jnp
import numpy as np
from jax import lax
from jax.experimental import pallas as pl
from jax.experimental.pallas import tpu as pltpu

F32 = jnp.float32
BF16 = jnp.bfloat16

GRID_W = 64
DA_HEADS, DA_DIM, DA_VDIM = 6, 32, 64
MLA_HEADS, MLA_NOPE, MLA_ROPE, MLA_VDIM = 6, 64, 32, 64
MLA_Q_RANK, MLA_KV_RANK = 256, 128
RET_HEADS, RET_DIM, RET_CHUNK = 4, 64, 128
ROPE_BASE = 10000.0
N_EXPERTS, N_GROUPS, D_EXPERT = 16, 4, 512
ROUTED_SCALE = 1.0
N_MOD = 6
LN_EPS = 1e-5
RMS_EPS = 1e-6
LOG2E = math.log2(math.e)

DA_W = DA_HEADS * 2 * DA_DIM
DA_VW = DA_HEADS * DA_VDIM
RET_W = RET_HEADS * RET_DIM
MLA_QW = MLA_HEADS * (MLA_NOPE + MLA_ROPE)
MLA_VW = MLA_HEADS * MLA_VDIM
MLA_QD = MLA_NOPE + MLA_ROPE
DA_MAPS = 2 * DA_HEADS
V_DIM = DA_VDIM
assert MLA_VDIM == V_DIM
IN_SPLIT = (DA_W, DA_W, DA_VW, MLA_Q_RANK, MLA_KV_RANK, MLA_ROPE, RET_W, RET_W, RET_W, RET_W)

LANES = 128
BF16_ROWS = 16
VMEM_LIMIT = 56 * 1024 * 1024
V_ROWS = V_DIM + BF16_ROWS
TM = 256
ATTN_N = 512
TK = 2048
ATTN_SUBTILES = 4
ATTN_NBUF = 2
ATTN_MAX_JUMP = 32.0
ATTN_UNROLL = 2
TM_MOE = 640
TM_MOE_LAST = 512
MOE_SUBTILES = 2
MOE_BLOCK = 256
MOE_ALIGN = 32
RET_UNROLL = 8

NT_DIMS = (((1,), (1,)), ((), ()))
TN_DIMS = (((0,), (0,)), ((), ()))


def _sigmoid(x):
    return 1.0 / (1.0 + jnp.exp(-x))


def _mod_kernel(c_ref, w_ref, b_ref, o_ref):
    c = c_ref[...]
    sc = (c * _sigmoid(c)).astype(BF16)
    o_ref[0] = jnp.dot(sc, w_ref[0].astype(BF16), preferred_element_type=F32) + b_ref[0]


def _modulation(cvec, w_ada, b_ada):
    depth, d, n = w_ada.shape
    tn = n // 4
    return pl.pallas_call(
        _mod_kernel,
        grid=(depth, n // tn),
        in_specs=[
            pl.BlockSpec((8, d), lambda l, j: (0, 0)),
            pl.BlockSpec((1, d, tn), lambda l, j: (l, 0, j)),
            pl.BlockSpec((1, 1, tn), lambda l, j: (l, 0, j)),
        ],
        out_specs=pl.BlockSpec((1, 8, tn), lambda l, j: (l, 0, j)),
        out_shape=jax.ShapeDtypeStruct((depth, 8, n), F32),
        name="adaln_mod",
    )(cvec, w_ada, b_ada.reshape(depth, 1, n))


_O_DAQ, _O_DAK, _O_DAV, _O_CQ, _O_CKV, _O_KPE, _O_RQ, _O_RK, _O_RV, _O_RG = (
    int(v) for v in np.cumsum((0,) + IN_SPLIT)[:-1])
N_ZROWS = sum(IN_SPLIT)


def _rms_rows(x):
    return x * lax.rsqrt(jnp.mean(x * x, axis=0, keepdims=True) + RMS_EPS)


def _pairswap_rows(x):
    n = x.shape[0]
    even = (lax.broadcasted_iota(jnp.int32, x.shape, 0) & 1) == 0
    return jnp.where(even, pltpu.roll(x, n - 1, 0), pltpu.roll(x, 1, 0))


def _in_kernel(x_ref, sh_ref, sc_ref, wT_ref, cosA_ref, sinA_ref, cosR_ref, sinR_ref,
               gq_ref, wuqT_ref, gkv_ref, wvT_ref, wkT_ref,
               daqr, daqf, dak, dav, mqr, mqf, mk, mv, rq, rk, rv, rg, z_ref, *, cq_da, cq_mla, kscale):
    tm = x_ref.shape[1]
    x = x_ref[0]
    h = (x * (1.0 + sc_ref[0, 0, 0]) + sh_ref[0, 0, 0]).astype(BF16)
    z_ref[...] = lax.dot_general(wT_ref[0], h, NT_DIMS, preferred_element_type=F32)
    cosA = cosA_ref[...]
    sinA = sinA_ref[...]
    cosR = cosR_ref[...]
    sinR = sinR_ref[...]
    ones_rows = (lax.broadcasted_iota(jnp.int32, (DA_HEADS, V_ROWS - V_DIM, tm), 1) == 0).astype(BF16)

    def rows(off, n):
        return z_ref[off:off + n, :]

    q2 = rows(_O_DAQ, DA_W)
    q = q2.reshape(DA_MAPS, DA_DIM, tm)
    qs = _pairswap_rows(q2).reshape(DA_MAPS, DA_DIM, tm)
    daqr[0] = ((q * cosA + qs * sinA) * cq_da).astype(BF16)
    daqf[0] = (q * cq_da).astype(BF16)
    k2 = rows(_O_DAK, DA_W)
    k = k2.reshape(DA_MAPS, DA_DIM, tm)
    ks = _pairswap_rows(k2).reshape(DA_MAPS, DA_DIM, tm)
    kr = (k * cosA + ks * sinA).reshape(DA_HEADS, 2 * DA_DIM, tm)
    kp = jnp.concatenate([kr, jnp.zeros_like(kr)], axis=1).reshape(DA_HEADS * LANES, tm)
    dak[0] = kp.T.astype(BF16)
    dav[0, :, 0:V_DIM, :] = rows(_O_DAV, DA_VW).reshape(DA_HEADS, DA_VDIM, tm).astype(BF16)
    dav[0, :, V_DIM:V_ROWS, :] = ones_rows

    cqn = (_rms_rows(rows(_O_CQ, MLA_Q_RANK)) * gq_ref[0]).astype(BF16)
    qm = jnp.dot(wuqT_ref[0], cqn, preferred_element_type=F32)
    qh = qm.reshape(MLA_HEADS, MLA_NOPE + MLA_ROPE, tm)
    qsw = _pairswap_rows(qm).reshape(MLA_HEADS, MLA_NOPE + MLA_ROPE, tm)[:, MLA_NOPE:, :]
    rot = qh[:, MLA_NOPE:, :] * cosA + qsw * sinA
    mqr[0, :, 0:MLA_NOPE, :] = (qh[:, 0:MLA_NOPE, :] * cq_mla).astype(BF16)
    mqr[0, :, MLA_NOPE:, :] = (rot * cq_mla).astype(BF16)
    mqf[0] = (qh * cq_mla).astype(BF16)
    ckvn = (_rms_rows(rows(_O_CKV, MLA_KV_RANK)) * gkv_ref[0]).astype(BF16)
    vm = jnp.dot(wvT_ref[0], ckvn, preferred_element_type=F32)
    mv[0, :, 0:V_DIM, :] = vm.reshape(MLA_HEADS, MLA_VDIM, tm).astype(BF16)
    mv[0, :, V_DIM:V_ROWS, :] = ones_rows
    kpe = rows(_O_KPE, MLA_ROPE)
    kper = kpe * cosA + _pairswap_rows(kpe) * sinA
    kin = jnp.concatenate([ckvn, kper.astype(BF16)], axis=0)
    kmT = jnp.dot(wkT_ref[0], kin, preferred_element_type=F32)
    mk[0] = kmT.T.astype(BF16)

    r_q2 = rows(_O_RQ, RET_W)
    r_q = r_q2.reshape(RET_HEADS, RET_DIM, tm)
    r_qs = _pairswap_rows(r_q2).reshape(RET_HEADS, RET_DIM, tm)
    rq[0] = (r_q * cosR + r_qs * sinR).astype(BF16)
    r_k2 = rows(_O_RK, RET_W)
    r_k = r_k2.reshape(RET_HEADS, RET_DIM, tm)
    r_ks = _pairswap_rows(r_k2).reshape(RET_HEADS, RET_DIM, tm)
    rkr = (r_k * cosR + r_ks * sinR) * kscale
    rkp = jnp.concatenate([rkr, jnp.zeros_like(rkr)], axis=1).reshape(RET_HEADS * LANES, tm)
    rk[0] = rkp.T.astype(BF16)
    rv[0] = rows(_O_RV, RET_W).reshape(RET_HEADS, RET_DIM, tm).astype(BF16)
    rg[0] = rows(_O_RG, RET_W)


def _in_proj(l, xc, modr, wT_all, tabs, gq, wuqT, gkv, wvT, wkT, S):
    B, Sp, D = xc.shape
    nt = Sp // TM
    n_lat_tiles = S // TM
    cosA, sinA, cosR, sinR = tabs

    def mod_spec(j):
        return pl.BlockSpec((1, 1, 1, 1, D), lambda b, t: (l, jnp.where(t >= n_lat_tiles, B, b), j, 0, 0))

    def wspec(a):
        return pl.BlockSpec((1,) + a.shape[1:], lambda b, t: (l,) + (0,) * (a.ndim - 1))

    def fm4(h, d):
        return pl.BlockSpec((1, h, d, TM), lambda b, t: (b, 0, 0, t))

    def tmaj(w):
        return pl.BlockSpec((1, TM, w), lambda b, t: (b, t, 0))

    kern = functools.partial(
        _in_kernel,
        cq_da=float(DA_DIM ** -0.5 * LOG2E),
        cq_mla=float((MLA_NOPE + MLA_ROPE) ** -0.5 * LOG2E),
        kscale=float(RET_DIM ** -0.5),
    )
    sds = jax.ShapeDtypeStruct
    return pl.pallas_call(
        kern,
        grid=(B, nt),
        in_specs=[
            pl.BlockSpec((1, TM, D), lambda b, t: (b, t, 0)),
            mod_spec(0), mod_spec(1),
            wspec(wT_all),
            pl.BlockSpec((DA_DIM, TM), lambda b, t: (0, t)),
            pl.BlockSpec((DA_DIM, TM), lambda b, t: (0, t)),
            pl.BlockSpec((RET_DIM, TM), lambda b, t: (0, t)),
            pl.BlockSpec((RET_DIM, TM), lambda b, t: (0, t)),
            wspec(gq), wspec(wuqT), wspec(gkv), wspec(wvT), wspec(wkT),
        ],
        out_specs=[
            fm4(DA_MAPS, DA_DIM), fm4(DA_MAPS, DA_DIM), tmaj(DA_HEADS * LANES), fm4(DA_HEADS, V_ROWS),
            fm4(MLA_HEADS, MLA_QD), fm4(MLA_HEADS, MLA_QD), tmaj(MLA_HEADS * LANES), fm4(MLA_HEADS, V_ROWS),
            fm4(RET_HEADS, RET_DIM), tmaj(RET_HEADS * LANES), fm4(RET_HEADS, RET_DIM),
            pl.BlockSpec((1, RET_W, TM), lambda b, t: (b, 0, t)),
        ],
        out_shape=[
            sds((B, DA_MAPS, DA_DIM, Sp), BF16), sds((B, DA_MAPS, DA_DIM, Sp), BF16),
            sds((B, Sp, DA_HEADS * LANES), BF16), sds((B, DA_HEADS, V_ROWS, Sp), BF16),
            sds((B, MLA_HEADS, MLA_QD, Sp), BF16), sds((B, MLA_HEADS, MLA_QD, Sp), BF16),
            sds((B, Sp, MLA_HEADS * LANES), BF16), sds((B, MLA_HEADS, V_ROWS, Sp), BF16),
            sds((B, RET_HEADS, RET_DIM, Sp), BF16), sds((B, Sp, RET_HEADS * LANES), BF16),
            sds((B, RET_HEADS, RET_DIM, Sp), BF16), sds((B, RET_W, Sp), F32),
        ],
        scratch_shapes=[pltpu.VMEM((N_ZROWS, TM), F32)],
        compiler_params=pltpu.CompilerParams(
            dimension_semantics=("arbitrary", "arbitrary"), vmem_limit_bytes=VMEM_LIMIT),
        name="in_proj",
    )(xc, modr, modr, wT_all, cosA, sinA, cosR, sinR, gq, wuqT, gkv, wvT, wkT)


def _attn_kernel(*refs, mode, S, C, lam_init, ctx_only, nsub):
    refs = list(refs)
    qr_ref, qf_ref, k_ref, v_ref = refs[:4]
    del refs[:4]
    if mode == "da":
        lam_ref, g_ref = refs[:2]
        del refs[:2]
    if ctx_only:
        del refs[:1]
    o_ref, m_sc, acc_sc, q_sc, jump_sc = refs[:5]
    nb = ATTN_NBUF
    pbuf, albuf = (refs[5 + i * nb: 5 + (i + 1) * nb] for i in range(2))
    tq = o_ref.shape[2] // nsub
    n_lat = S // TK

    def qpad(ref, t):
        cols = slice(t * tq, (t + 1) * tq)
        if mode == "da":
            z = jnp.zeros((DA_DIM, tq), BF16)
            c1 = jnp.concatenate([ref[0, 0, :, cols], z, z, z], axis=0)
            c2 = jnp.concatenate([z, ref[0, 1, :, cols], z, z], axis=0)
            return jnp.concatenate([c1, c2], axis=1)
        return jnp.concatenate([ref[0, 0, :, cols], jnp.zeros((LANES - MLA_QD, tq), BF16)], axis=0)

    def chunk(j):
        return pl.ds(j * TK if isinstance(j, int) else pl.multiple_of(j * TK, TK), TK)

    def context_keys(t):
        c0 = 0 if ctx_only else S
        s = jnp.dot(k_ref[0, c0:c0 + C, :], qpad(qf_ref, t), preferred_element_type=F32)
        m_c = jnp.max(s, axis=0, keepdims=True)
        m_sc[t] = m_c
        acc_sc[t] = jnp.dot(v_ref[0, 0, :, c0:c0 + C], jnp.exp2(s - m_c).astype(BF16), preferred_element_type=F32)

    for t in range(nsub):
        context_keys(t)

    if not ctx_only:
        for t in range(nsub):
            q_sc[t] = qpad(qr_ref, t)
        jump_sc[...] = jnp.full(jump_sc.shape, -jnp.inf, F32)

        def probs(t, j, r):
            s = jnp.dot(k_ref[0, chunk(j), :], q_sc[t], preferred_element_type=F32)
            m_prev = m_sc[t]
            pbuf[r][t] = jnp.exp2(s - m_prev).astype(BF16)
            m_chunk = jnp.max(s, axis=0, keepdims=True)
            m_new = jnp.maximum(m_prev, m_chunk)
            jump_sc[t] = jnp.maximum(jump_sc[t], m_chunk - m_prev)
            albuf[r][t] = jnp.exp2(m_prev - m_new)
            m_sc[t] = m_new

        def accumulate(t, j, r):
            pv = jnp.dot(v_ref[0, 0, :, chunk(j)], pbuf[r][t], preferred_element_type=F32)
            acc_sc[t] = (acc_sc[t] + pv) * albuf[r][t]

        def step(j, r):
            for t in range(nsub):
                accumulate(t, j, r)
                probs(t, j + 1, (r + 1) % nb)

        for t in range(nsub):
            probs(t, 0, 0)
        n_steps = n_lat - 1
        trips = n_steps // ATTN_UNROLL

        def body(i, carry):
            for u in range(ATTN_UNROLL):
                step(i * ATTN_UNROLL + u, u % nb)
            return carry

        lax.fori_loop(0, trips, body, 0)
        for j in range(trips * ATTN_UNROLL, n_steps):
            step(j, j % nb)
        for t in range(nsub):
            accumulate(t, n_lat - 1, (n_lat - 1) % nb)

        @pl.when(jnp.max(jump_sc[...]) > ATTN_MAX_JUMP)
        def _():
            for t in range(nsub):
                context_keys(t)

                def exact_step(j, carry, t=t):
                    s = jnp.dot(k_ref[0, chunk(j), :], q_sc[t], preferred_element_type=F32)
                    m_old = m_sc[t]
                    m_new = jnp.maximum(m_old, jnp.max(s, axis=0, keepdims=True))
                    pv = jnp.dot(v_ref[0, 0, :, chunk(j)], jnp.exp2(s - m_new).astype(BF16),
                                 preferred_element_type=F32)
                    acc_sc[t] = acc_sc[t] * jnp.exp2(m_old - m_new) + pv
                    m_sc[t] = m_new
                    return carry

                lax.fori_loop(0, n_lat, exact_step, 0)

    for t in range(nsub):
        acc = acc_sc[t]
        o = acc[0:V_DIM, :] / acc[V_DIM:V_DIM + 1, :]
        cols = slice(t * tq, (t + 1) * tq)
        if mode == "da":
            lf = lam_ref[0]
            lam = (jnp.exp(jnp.sum(lf[0:1] * lf[1:2], axis=1, keepdims=True))
                   - jnp.exp(jnp.sum(lf[2:3] * lf[3:4], axis=1, keepdims=True)) + lam_init)
            od = o[:, 0:tq] - lam * o[:, tq:2 * tq]
            y = _rms_rows(od) * g_ref[0]
            o_ref[0, :, cols] = (y * (1.0 - lam_init)).astype(BF16)
        else:
            o_ref[0, :, cols] = o.astype(BF16)


def _attention_call(mode, l, qr, qf, k, v, S, C, lam, g, y_latent):
    B, nmaps, d, Sp = qr.shape
    H = v.shape[1]
    nm = nmaps // H
    ctx_only = y_latent is not None
    nsub = 1 if ctx_only else ATTN_SUBTILES
    tq = C if ctx_only else ATTN_N // nm
    n = nm * tq
    tqb = nsub * tq
    q0 = S // tqb if ctx_only else 0
    lam_init = 0.8 - 0.6 * math.exp(-0.3 * l)
    kern = functools.partial(_attn_kernel, mode=mode, S=S, C=C, lam_init=lam_init, ctx_only=ctx_only, nsub=nsub)
    in_specs = [
        pl.BlockSpec((1, nm, d, tqb), lambda b, h, i: (b, h, 0, q0 + i)),
        pl.BlockSpec((1, nm, d, tqb), lambda b, h, i: (b, h, 0, q0 + i)),
        (pl.BlockSpec((1, C, LANES), lambda b, h, i: (b, S // C, h)) if ctx_only
         else pl.BlockSpec((1, Sp, LANES), lambda b, h, i: (b, 0, h))),
        (pl.BlockSpec((1, 1, V_ROWS, C), lambda b, h, i: (b, h, 0, S // C)) if ctx_only
         else pl.BlockSpec((1, 1, V_ROWS, Sp), lambda b, h, i: (b, h, 0, 0))),
    ]
    args = [qr, qf, k, v]
    if mode == "da":
        in_specs += [pl.BlockSpec((1,) + lam.shape[1:], lambda b, h, i: (l, 0, 0)),
                     pl.BlockSpec((1,) + g.shape[1:], lambda b, h, i: (l, 0, 0))]
        args += [lam, g]
    aliases = {}
    if ctx_only:
        in_specs.append(pl.BlockSpec(memory_space=pl.ANY))
        args.append(y_latent)
        aliases = {len(args) - 1: 0}
    tk = C if ctx_only else TK
    return pl.pallas_call(
        kern,
        grid=(B, H, 1 if ctx_only else S // tqb),
        in_specs=in_specs,
        out_specs=pl.BlockSpec((1, V_DIM, tqb), lambda b, h, i: (b, h, q0 + i)),
        out_shape=jax.ShapeDtypeStruct((B, H * V_DIM, Sp), BF16),
        input_output_aliases=aliases,
        scratch_shapes=(
            [pltpu.VMEM((nsub, 1, n), F32), pltpu.VMEM((nsub, V_ROWS, n), F32), pltpu.VMEM((nsub, LANES, n), BF16),
             pltpu.VMEM((nsub, 1, n), F32)]
            + [pltpu.VMEM((nsub, tk, n), BF16)] * ATTN_NBUF + [pltpu.VMEM((nsub, 1, n), F32)] * ATTN_NBUF),
        compiler_params=pltpu.CompilerParams(
            dimension_semantics=("arbitrary", "arbitrary", "arbitrary"), vmem_limit_bytes=VMEM_LIMIT),
        name="attn_" + mode + ("_ctx" if ctx_only else ""),
    )(*args)


def _attention(mode, l, qr, qf, k, v, S, C, with_ctx, lam=None, g=None):
    y = _attention_call(mode, l, qr, qf, k, v, S, C, lam, g, None)
    return _attention_call(mode, l, qr, qf, k, v, S, C, lam, g, y) if with_ctx else y


def _ret_kernel(q_ref, k_ref, v_ref, g_ref, dec_ref, o_ref, sb_sc, *, S, C):
    CH = RET_CHUNK
    nl, nc = S // CH, C // CH

    def log_sigmoid(x):
        return jnp.minimum(x, 0.0) - jnp.log1p(jnp.exp(-jnp.abs(x)))

    lgf = log_sigmoid(dec_ref[0, 0][0:1, :])
    lgb = log_sigmoid(dec_ref[1, 0][0:1, :])
    ii = lax.broadcasted_iota(jnp.int32, (CH, CH), 1).astype(F32)
    jj = lax.broadcasted_iota(jnp.int32, (CH, CH), 0).astype(F32)
    dd = ii - jj
    fwd = dd >= 0
    mt = (jnp.where(fwd, jnp.exp(lgf * jnp.where(fwd, dd, 0.0)), 0.0)
          + jnp.where(fwd, 0.0, jnp.exp(lgb * jnp.where(fwd, 0.0, -dd))))
    xi_f = jnp.exp(lgf * (ii + 1.0))
    xi_b = jnp.exp(lgb * (CH - ii))
    zeta_f = jnp.exp(lgf * (CH - 1.0 - jj))
    zeta_b = jnp.exp(lgb * jj)
    cdf = jnp.exp(lgf * CH)
    cdb = jnp.exp(lgb * CH)
    zq = jnp.zeros((RET_DIM, CH), BF16)

    def sl(c):
        return pl.ds(pl.multiple_of(c * CH, CH), CH)

    def state_update(c, zeta):
        kz = (k_ref[0, sl(c), :].astype(F32) * zeta).astype(BF16)
        return jnp.dot(v_ref[0, 0, :, sl(c)], kz, preferred_element_type=F32)

    def bwd_step(c, sb):
        sb_sc[c] = sb
        return sb * cdb + state_update(c, zeta_b)

    sb = jnp.zeros((RET_DIM, CH), F32)
    for c in range(nl + nc - 1, nl - 1, -1):
        sb = bwd_step(c, sb)
    lax.fori_loop(0, nl, lambda n, s: bwd_step(nl - 1 - n, s), sb, unroll=RET_UNROLL)

    def fwd_step(c, sf):
        qc = q_ref[0, 0, :, sl(c)]
        qp = jnp.concatenate([qc, zq], axis=0)
        kc = k_ref[0, sl(c), :]
        vc = v_ref[0, 0, :, sl(c)]
        at = jnp.dot(kc, qp, preferred_element_type=F32)
        qpf = qp.astype(F32)
        rhs = jnp.concatenate([(at * mt).astype(BF16), (qpf * xi_f).astype(BF16), (qpf * xi_b).astype(BF16)], axis=0)
        lhs = jnp.concatenate([vc, sf.astype(BF16), sb_sc[c].astype(BF16)], axis=1)
        o = jnp.dot(lhs, rhs, preferred_element_type=F32)
        gch = g_ref[0, :, sl(c)]
        o_ref[0, :, sl(c)] = (gch * _sigmoid(gch) * _rms_rows(o)).astype(BF16)
        return sf * cdf + state_update(c, zeta_f)

    sf = jnp.zeros((RET_DIM, CH), F32)
    for c in range(nl, nl + nc):
        sf = fwd_step(c, sf)
    lax.fori_loop(0, nl, fwd_step, sf, unroll=RET_UNROLL)


def _retention(rq, rk, rv, rg, dec, S, C):
    B, H, d, Sp = rq.shape
    kern = functools.partial(_ret_kernel, S=S, C=C)
    return pl.pallas_call(
        kern,
        grid=(B, H),
        in_specs=[
            pl.BlockSpec((1, 1, d, Sp), lambda b, h: (b, h, 0, 0)),
            pl.BlockSpec((1, Sp, LANES), lambda b, h: (b, 0, h)),
            pl.BlockSpec((1, 1, d, Sp), lambda b, h: (b, h, 0, 0)),
            pl.BlockSpec((1, d, Sp), lambda b, h: (b, h, 0)),
            pl.BlockSpec((2, 1, 8, LANES), lambda b, h: (0, h, 0, 0)),
        ],
        out_specs=pl.BlockSpec((1, d, Sp), lambda b, h: (b, h, 0)),
        out_shape=jax.ShapeDtypeStruct((B, H * d, Sp), BF16),
        scratch_shapes=[pltpu.VMEM((Sp // RET_CHUNK, RET_DIM, RET_CHUNK), F32)],
        compiler_params=pltpu.CompilerParams(
            dimension_semantics=("arbitrary", "arbitrary"), vmem_limit_bytes=VMEM_LIMIT),
        name="retention",
    )(rq, rk, rv, rg, dec)


def _layer_norm(t, g, b):
    mu = jnp.mean(t, axis=-1, keepdims=True)
    tc = t - mu
    var = jnp.mean(tc * tc, axis=-1, keepdims=True)
    return tc * lax.rsqrt(var + LN_EPS) * g + b


def _out_kernel(yda_ref, ymla_ref, yret_ref, w_ref, x_ref, gate_ref, g1_ref, b1_ref, sh2_ref, sc2_ref, rw_ref,
                x1_ref, h2_ref, lg_ref, *, alpha):
    proj = (lax.dot_general(yda_ref[0], w_ref[0, 0:DA_VW, :], TN_DIMS, preferred_element_type=F32)
            + lax.dot_general(ymla_ref[0], w_ref[0, DA_VW:DA_VW + MLA_VW, :], TN_DIMS, preferred_element_type=F32)
            + lax.dot_general(yret_ref[0], w_ref[0, DA_VW + MLA_VW:, :], TN_DIMS, preferred_element_type=F32))
    x1 = _layer_norm(alpha * x_ref[0] + gate_ref[0, 0, 0] * proj, g1_ref[0], b1_ref[0])
    x1_ref[0] = x1
    h2 = x1 * (1.0 + sc2_ref[0, 0, 0]) + sh2_ref[0, 0, 0]
    h2_ref[0] = h2.astype(BF16)
    h2_hi = h2.astype(BF16)
    h2_lo = (h2 - h2_hi.astype(F32)).astype(BF16)
    hi = jnp.dot(h2_hi, rw_ref[...], preferred_element_type=F32)
    lo = jnp.dot(h2_lo, rw_ref[:, 0:LANES], preferred_element_type=F32)
    lg_ref[0] = hi[:, 0:LANES] + (hi[:, LANES:] + lo)


def _out_proj(l, yda, ymla, yret, w_out, xc, modr, ln_g, ln_b, rw_pad, S, alpha, latents_only):
    B, Sp, D = xc.shape
    n_lat_tiles = S // TM
    n_rows = S if latents_only else Sp

    def mod_spec(j):
        return pl.BlockSpec((1, 1, 1, 1, D), lambda b, t: (l, jnp.where(t >= n_lat_tiles, B, b), j, 0, 0))

    def fm(a):
        return pl.BlockSpec((1, a.shape[1], TM), lambda b, t: (b, 0, t))

    def wspec(a):
        return pl.BlockSpec((1,) + a.shape[1:], lambda b, t: (l,) + (0,) * (a.ndim - 1))

    row = pl.BlockSpec((1, TM, D), lambda b, t: (b, t, 0))
    sds = jax.ShapeDtypeStruct
    return pl.pallas_call(
        functools.partial(_out_kernel, alpha=alpha),
        grid=(B, n_rows // TM),
        in_specs=[fm(yda), fm(ymla), fm(yret), wspec(w_out), row, mod_spec(2), wspec(ln_g), wspec(ln_b),
                  mod_spec(3), mod_spec(4), pl.BlockSpec(rw_pad.shape, lambda b, t: (0, 0))],
        out_specs=[row, row, pl.BlockSpec((1, TM, LANES), lambda b, t: (b, t, 0))],
        out_shape=[sds((B, n_rows, D), F32), sds((B, n_rows, D), BF16), sds((B, n_rows, LANES), F32)],
        compiler_params=pltpu.CompilerParams(dimension_semantics=("arbitrary", "arbitrary")),
        name="out_proj",
    )(yda, ymla, yret, w_out, xc, modr, ln_g, ln_b, modr, modr, rw_pad)


def _gates_T(logits_T, rb):
    s = _sigmoid(logits_T)
    ch = s + rb
    srow = [s[e:e + 1] for e in range(N_EXPERTS)]
    crow = [ch[e:e + 1] for e in range(N_EXPERTS)]
    per = N_EXPERTS // N_GROUPS
    gs = []
    for g in range(N_GROUPS):
        a, b, c, d = crow[per * g: per * g + per]
        m1, n1, m2, n2 = jnp.maximum(a, b), jnp.minimum(a, b), jnp.maximum(c, d), jnp.minimum(c, d)
        gs.append(jnp.maximum(m1, m2) + jnp.maximum(jnp.minimum(m1, m2), jnp.maximum(n1, n2)))
    gmax = functools.reduce(jnp.maximum, gs)
    taken = jnp.zeros(gmax.shape, jnp.bool_)
    gsel = []
    for g in range(N_GROUPS):
        sg = jnp.logical_and(gs[g] == gmax, jnp.logical_not(taken))
        gsel.append(sg)
        taken = jnp.logical_or(taken, sg)
    neg = jnp.full(gmax.shape, -jnp.inf, F32)
    mc = [jnp.where(gsel[e // per], crow[e], neg) for e in range(N_EXPERTS)]
    sel = [jnp.zeros(gmax.shape, jnp.bool_) for _ in range(N_EXPERTS)]
    for _ in range(2):
        top = functools.reduce(jnp.maximum, mc)
        taken = jnp.zeros(gmax.shape, jnp.bool_)
        for e in range(N_EXPERTS):
            hit = jnp.logical_and(mc[e] == top, jnp.logical_not(taken))
            taken = jnp.logical_or(taken, hit)
            sel[e] = jnp.logical_or(sel[e], hit)
            mc[e] = jnp.where(hit, neg, mc[e])
    w = [jnp.where(sel[e], srow[e], 0.0) for e in range(N_EXPERTS)]
    wsum = functools.reduce(lambda a, b: a + b, w)
    gates = jnp.concatenate([we / wsum * ROUTED_SCALE for we in w], axis=0)
    return gates, [jnp.where(sg, 1.0, 0.0) for sg in gsel]


def _moe_sorted_rows(tm):
    return tm + N_GROUPS * MOE_ALIGN + MOE_BLOCK


def _moe_kernel(h_ref, lg_ref, rb_ref, w13_ref, w2_ref, o_ref,
                off_sc, nblk_sc, perm_sc, hs_sc, gs_sc, accs_sc, earlier_sc, *, nsub):
    e = pl.program_id(2)
    tm = h_ref.shape[1] // nsub
    n_live = perm_sc.shape[1]
    per = N_EXPERTS // N_GROUPS
    lane = lax.broadcasted_iota(jnp.int32, (1, LANES), 1)

    def expert(x):
        ab = jnp.dot(x, w13_ref[0, 0], preferred_element_type=F32)
        a = ab[:, 0:D_EXPERT]
        hid = (a * _sigmoid(a) * ab[:, D_EXPERT:]).astype(BF16)
        return jnp.dot(hid, w2_ref[0, 0], preferred_element_type=F32)

    def gate_col(g):
        return jnp.sum(jnp.where(lane == e, g, 0.0), axis=1, keepdims=True)

    def sub(j):
        return slice(j * tm, (j + 1) * tm)

    @pl.when(jnp.logical_and(jnp.logical_and(pl.program_id(0) == 0, pl.program_id(1) == 0), e == 0))
    def _():
        before = lax.broadcasted_iota(jnp.int32, (tm, tm), 0) < lax.broadcasted_iota(jnp.int32, (tm, tm), 1)
        earlier_sc[...] = jnp.where(before, 1.0, 0.0).astype(BF16)

    @pl.when(e == 0)
    def _():
        for j in range(nsub):
            gT, gsel = _gates_T(lg_ref[0, sub(j), :].T[0:N_EXPERTS, :], rb_ref[...])
            gates = jnp.concatenate([gT, jnp.zeros((LANES - N_EXPERTS, tm), F32)], axis=0).T
            member = jnp.concatenate(gsel + [jnp.zeros((BF16_ROWS - N_GROUPS, tm), F32)], axis=0)
            rank = jnp.dot(member.astype(BF16), earlier_sc[...], preferred_element_type=F32)
            count = jnp.sum(member, axis=1, keepdims=True)
            off = jnp.int32(0)
            slot = jnp.zeros((1, tm), F32)
            for g in range(N_GROUPS):
                n_g = jnp.max(count[g:g + 1]).astype(jnp.int32)
                off_sc[j * N_GROUPS + g] = off
                nblk_sc[j * N_GROUPS + g] = (n_g + (MOE_BLOCK - 1)) // MOE_BLOCK
                slot = slot + gsel[g] * (rank[g:g + 1] + off.astype(F32))
                off = off + ((n_g + (MOE_ALIGN - 1)) // MOE_ALIGN) * MOE_ALIGN
            rows_i = lax.broadcasted_iota(jnp.int32, (n_live, tm), 0)
            perm = jnp.where(rows_i == slot.astype(jnp.int32), 1.0, 0.0).astype(BF16)
            perm_sc[j] = perm
            hs_sc[j, 0:n_live, :] = jnp.dot(perm, h_ref[0, sub(j), :], preferred_element_type=F32).astype(BF16)
            hs_sc[j, n_live:, :] = jnp.zeros((MOE_BLOCK, hs_sc.shape[2]), BF16)
            g_hi = gates.astype(BF16)
            g_lo = (gates - g_hi.astype(F32)).astype(BF16)
            gs_sc[j, 0:n_live, :] = (jnp.dot(perm, g_hi, preferred_element_type=F32)
                                     + jnp.dot(perm, g_lo, preferred_element_type=F32))
            gs_sc[j, n_live:, :] = jnp.zeros((MOE_BLOCK, LANES), F32)
            accs_sc[j] = jnp.zeros(accs_sc.shape[1:], F32)

    start = [off_sc[j * N_GROUPS + e // per] for j in range(nsub)]
    n_blocks = [nblk_sc[j * N_GROUPS + e // per] for j in range(nsub)]
    n_common = functools.reduce(jnp.minimum, n_blocks)

    def rows_of(j, k):
        return pl.ds(pl.multiple_of(start[j] + k * MOE_BLOCK, MOE_ALIGN), MOE_BLOCK)

    def together(k, carry):
        y = expert(jnp.concatenate([hs_sc[j, rows_of(j, k), :] for j in range(nsub)], axis=0))
        for j in range(nsub):
            rows = rows_of(j, k)
            accs_sc[j, rows, :] += gate_col(gs_sc[j, rows, :]) * y[j * MOE_BLOCK:(j + 1) * MOE_BLOCK]
        return carry

    lax.fori_loop(0, n_common, together, 0)
    for j in range(nsub):
        def alone(k, carry, j=j):
            rows = rows_of(j, k)
            accs_sc[j, rows, :] += gate_col(gs_sc[j, rows, :]) * expert(hs_sc[j, rows, :])
            return carry

        lax.fori_loop(n_common, n_blocks[j], alone, 0)

    @pl.when(e == N_EXPERTS - 1)
    def _():
        for j in range(nsub):
            o_ref[0, sub(j), :] = lax.dot_general(perm_sc[j], accs_sc[j, 0:n_live, :].astype(BF16), TN_DIMS,
                                                  preferred_element_type=F32).astype(BF16)


def _moe_ln_kernel(x1_ref, f_ref, gb_ref, gc_ref, g2_ref, b2_ref, o_ref, *, S, alpha):
    tm = x1_ref.shape[1]
    n_lat = S - pl.program_id(1) * tm
    rowi = lax.broadcasted_iota(jnp.int32, (tm, 1), 0)
    gate = jnp.where(rowi < n_lat, gb_ref[0, 0, 0], gc_ref[0, 0, 0])
    o_ref[0] = _layer_norm(alpha * x1_ref[0] + gate * f_ref[0].astype(F32), g2_ref[0], b2_ref[0])


def _moe(l, h2, logits, rb, w13, w2, x1, modr, ln_g, ln_b, S, alpha, latents_only):
    B, n_rows, D = x1.shape
    sub = TM_MOE_LAST if latents_only else TM_MOE
    tm = MOE_SUBTILES * sub
    n_sorted = _moe_sorted_rows(sub)

    def wspec(a):
        return pl.BlockSpec((1,) + a.shape[1:], lambda b, t: (l,) + (0,) * (a.ndim - 1))

    row = lambda b, t, e: (b, t, 0)
    f = pl.pallas_call(
        functools.partial(_moe_kernel, nsub=MOE_SUBTILES),
        grid=(B, n_rows // tm, N_EXPERTS),
        in_specs=[
            pl.BlockSpec((1, tm, D), row),
            pl.BlockSpec((1, tm, LANES), row),
            pl.BlockSpec(rb.shape, lambda b, t, e: (0, 0)),
            pl.BlockSpec((1, 1, D, 2 * D_EXPERT), lambda b, t, e: (l, e, 0, 0)),
            pl.BlockSpec((1, 1, D_EXPERT, D), lambda b, t, e: (l, e, 0, 0)),
        ],
        out_specs=pl.BlockSpec((1, tm, D), row),
        out_shape=jax.ShapeDtypeStruct((B, n_rows, D), BF16),
        scratch_shapes=[
            pltpu.SMEM((MOE_SUBTILES * N_GROUPS,), jnp.int32), pltpu.SMEM((MOE_SUBTILES * N_GROUPS,), jnp.int32),
            pltpu.VMEM((MOE_SUBTILES, n_sorted - MOE_BLOCK, sub), BF16),
            pltpu.VMEM((MOE_SUBTILES, n_sorted, D), BF16),
            pltpu.VMEM((MOE_SUBTILES, n_sorted, LANES), F32),
            pltpu.VMEM((MOE_SUBTILES, n_sorted, D), F32),
            pltpu.VMEM((sub, sub), BF16)],
        compiler_params=pltpu.CompilerParams(
            dimension_semantics=("arbitrary", "arbitrary", "arbitrary"), vmem_limit_bytes=VMEM_LIMIT),
        name="moe",
    )(h2, logits, rb, w13, w2)
    rows = pl.BlockSpec((1, tm, D), lambda b, t: (b, t, 0))
    return pl.pallas_call(
        functools.partial(_moe_ln_kernel, S=S, alpha=alpha),
        grid=(B, n_rows // tm),
        in_specs=[rows, rows,
                  pl.BlockSpec((1, 1, 1, 1, D), lambda b, t: (l, b, 5, 0, 0)),
                  pl.BlockSpec((1, 1, 1, 1, D), lambda b, t: (l, B, 5, 0, 0)),
                  wspec(ln_g), wspec(ln_b)],
        out_specs=rows,
        out_shape=jax.ShapeDtypeStruct((B, n_rows, D), F32),
        compiler_params=pltpu.CompilerParams(
            dimension_semantics=("arbitrary", "arbitrary"), vmem_limit_bytes=VMEM_LIMIT),
        name="moe_ln",
    )(x1, f, modr, modr, ln_g, ln_b)


def _rope_tables(S, C, rot_dim):
    rows = S // GRID_W
    r = jnp.repeat(jnp.arange(rows, dtype=F32), GRID_W)
    col = jnp.tile(jnp.arange(GRID_W, dtype=F32), rows)
    n_freq = rot_dim // 4
    freqs = ROPE_BASE ** (-jnp.arange(n_freq, dtype=F32) / n_freq)
    ang = jnp.concatenate([r[:, None] * freqs, col[:, None] * freqs], -1)
    cos, sin = jnp.cos(ang), jnp.sin(ang)
    cos_rep = jnp.repeat(cos, 2, axis=-1)
    sin_alt = jnp.stack([-sin, sin], -1).reshape(S, rot_dim)
    cos_rep = jnp.concatenate([cos_rep, jnp.ones((C, rot_dim), F32)], 0)
    sin_alt = jnp.concatenate([sin_alt, jnp.zeros((C, rot_dim), F32)], 0)
    return cos_rep.T, sin_alt.T


def kernel(x, c, ctx, c_ctx, w_ada, b_ada, w_in, da_lambda, da_subln, mla_q_norm, mla_w_uq, mla_kv_norm, mla_w_ukv, ret_decay_f, ret_decay_b, w_out, ln1_g, ln1_b, router_w, router_b, exp_w1, exp_w3, exp_w2, ln2_g, ln2_b):
    B, S, D = x.shape
    C = ctx.shape[1]
    depth = w_in.shape[0]
    alpha = float((2 * depth) ** 0.25)

    cvec = jnp.concatenate([c, c_ctx[None, :], jnp.zeros((8 - B - 1, D), F32)], 0)
    mod = _modulation(cvec, w_ada, b_ada)
    modr = mod.reshape(depth, 8, N_MOD, 1, D)

    wT_all = jnp.swapaxes(w_in, 1, 2).astype(BF16)
    wuqT = jnp.swapaxes(mla_w_uq, 1, 2).astype(BF16)
    ukv = mla_w_ukv.reshape(depth, MLA_KV_RANK, MLA_HEADS, MLA_NOPE + MLA_VDIM)
    wvT = jnp.swapaxes(ukv[..., MLA_NOPE:].reshape(depth, MLA_KV_RANK, MLA_VW), 1, 2).astype(BF16)
    nopeT = jnp.transpose(ukv[..., :MLA_NOPE], (0, 2, 3, 1))
    top = jnp.concatenate([nopeT, jnp.zeros((depth, MLA_HEADS, MLA_NOPE, MLA_ROPE), F32)], -1)
    mid = jnp.concatenate([jnp.zeros((MLA_ROPE, MLA_KV_RANK), F32), jnp.eye(MLA_ROPE, dtype=F32)], -1)
    mid = jnp.broadcast_to(mid, (depth, MLA_HEADS, MLA_ROPE, MLA_KV_RANK + MLA_ROPE))
    bot = jnp.zeros((depth, MLA_HEADS, LANES - MLA_NOPE - MLA_ROPE, MLA_KV_RANK + MLA_ROPE), F32)
    wkT = jnp.concatenate([top, mid, bot], 2).reshape(depth, MLA_HEADS * LANES, MLA_KV_RANK + MLA_ROPE).astype(BF16)
    gq = mla_q_norm[:, :, None]
    gkv = mla_kv_norm[:, :, None]
    subln = da_subln[:, :, None]
    w_out_b = w_out.astype(BF16)
    w13 = jnp.concatenate([exp_w1, exp_w3], -1).astype(BF16)
    w2 = exp_w2.astype(BF16)
    rw_f = jnp.concatenate([router_w, jnp.zeros((D, LANES - N_EXPERTS), F32)], -1)
    rw_hi = rw_f.astype(BF16)
    rw_pad = jnp.concatenate([rw_hi, (rw_f - rw_hi.astype(F32)).astype(BF16)], -1)
    rb = router_b[:, None]
    dec = jnp.broadcast_to(jnp.stack([ret_decay_f, ret_decay_b], 1)[:, :, :, None, None],
                           (depth, 2, RET_HEADS, 8, LANES))
    tabs = _rope_tables(S, C, DA_DIM) + _rope_tables(S, C, RET_DIM)
    ln1g, ln1b, ln2g, ln2b = (a[:, None, :] for a in (ln1_g, ln1_b, ln2_g, ln2_b))

    xc = jnp.concatenate([x, ctx], 1)
    for l in range(depth):
        last = l == depth - 1
        (daqr, daqf, dak, dav, mqr, mqf, mk, mv, rq, rk, rv, rg) = _in_proj(
            l, xc, modr, wT_all, tabs, gq, wuqT, gkv, wvT, wkT, S)
        yda = _attention("da", l, daqr, daqf, dak, dav, S, C, not last, lam=da_lambda, g=subln)
        ymla = _attention("mla", l, mqr, mqf, mk, mv, S, C, not last)
        yret = _retention(rq, rk, rv, rg, dec[l], S, C)
        x1, h2, logits = _out_proj(l, yda, ymla, yret, w_out_b, xc, modr, ln1g, ln1b, rw_pad, S, alpha, last)
        xc = _moe(l, h2, logits, rb, w13, w2, x1, modr, ln2g, ln2b, S, alpha, last)
    return xc
```
